```python
import math
import jax
import jax.numpy as jnp
from jax import lax
import numpy as np

D_MODEL = 1024
BATCH = 8
SEQ = 4096
DEPTH = 2
DEC_BATCH = 32
DEC_SEQ = 32
PAST_LEN = 2048

CHUNK = 64
N_EVEN = (DEPTH + 1) // 2
N_ODD = DEPTH // 2
RMS_EPS = 1e-6
ROPE_THETA = 500000.0
NEG_INF = -1e30

SWA_HEADS = 8
SWA_KV_HEADS = 2
SWA_GROUP = SWA_HEADS // SWA_KV_HEADS
HEAD_DIM = 64
ROT_DIM = HEAD_DIM // 4
WINDOW = 128
WINDOW_CHUNKS = WINDOW // CHUNK
BAND = (WINDOW_CHUNKS + 1) * CHUNK
SWA_Q = SWA_HEADS * HEAD_DIM
SWA_KV = SWA_KV_HEADS * HEAD_DIM
SWA_SCALE = HEAD_DIM ** -0.5

S5_WIDTH = 512
S5_GROUP = 16
S5_GROUPS = S5_WIDTH // S5_GROUP
S5_STATE = 64

POOL_WIDTH = 512
POOL_WINDOWS = (2, 4, 8, 16)
POOL_GROUP = POOL_WIDTH // len(POOL_WINDOWS)
POOL_MAX = 16
POOL_BUF = POOL_MAX - 1

MLA_HEADS = 8
Q_LORA = 512
KV_LORA = 256
NOPE_DIM = 64
ROPE_DIM = 32
V_DIM = 64
Q_BLOCK = 128
MLA_SCALE = (NOPE_DIM + ROPE_DIM) ** -0.5

EVEN_IN = SWA_Q + 2 * SWA_KV + S5_WIDTH
EVEN_MIX = SWA_Q + S5_WIDTH
ODD_IN = Q_LORA + KV_LORA + ROPE_DIM + POOL_WIDTH
ODD_MIX = POOL_WIDTH + MLA_HEADS * V_DIM

PEER_HEADS = 8
N_KEYS = 128
N_EXPERTS = N_KEYS * N_KEYS
D_KEY = 128
D_HALF = D_KEY // 2
PEER_TOPK = 16
PEER_BLOCK = 256

kernel_name = 'hybrid_streaming_encoder_step'


def rmsnorm(x, g):
    xf = x.astype(jnp.float32)
    y = xf * lax.rsqrt(jnp.mean(xf * xf, axis=-1, keepdims=True) + RMS_EPS)
    return (y * g.astype(jnp.float32)).astype(x.dtype)


def rope(x, pos):
    r = x.shape[-1]
    inv = ROPE_THETA ** (-jnp.arange(0, r, 2, dtype=jnp.float32) / r)
    ang = pos.astype(jnp.float32)[:, None] * inv[None, :]
    cos = jnp.cos(ang)[:, None, :]
    sin = jnp.sin(ang)[:, None, :]
    xf = x.astype(jnp.float32)
    x1, x2 = xf[..., : r // 2], xf[..., r // 2:]
    return jnp.concatenate([x1 * cos - x2 * sin, x2 * cos + x1 * sin], axis=-1).astype(x.dtype)


def partial_rope(x, pos):
    return jnp.concatenate([rope(x[..., :ROT_DIM], pos), x[..., ROT_DIM:]], axis=-1)


def sink_softmax(s, sink):
    sk = sink.astype(jnp.float32).reshape(SWA_KV_HEADS, SWA_GROUP)[:, :, None, None]
    m = jnp.maximum(jnp.max(s, axis=-1, keepdims=True), sk)
    p = jnp.exp(s - m)
    return p / (jnp.sum(p, axis=-1, keepdims=True) + jnp.exp(sk - m))


def swa_attend_prompt(q, k, v, sink):
    b, s = q.shape[:2]
    nc = s // CHUNK
    qc = q.reshape(b, nc, CHUNK, SWA_KV_HEADS, SWA_GROUP, HEAD_DIM)
    pad = WINDOW_CHUNKS * CHUNK

    def band(t):
        tp = jnp.pad(t, ((0, 0), (pad, 0), (0, 0), (0, 0)))
        tp = tp.reshape(b, nc + WINDOW_CHUNKS, CHUNK, SWA_KV_HEADS, HEAD_DIM)
        return jnp.concatenate([tp[:, i:i + nc] for i in range(WINDOW_CHUNKS + 1)], axis=2)

    kb, vb = band(k), band(v)
    sc = jnp.einsum('bcqhgd,bckhd->bchgqk', qc, kb, preferred_element_type=jnp.float32) * SWA_SCALE
    key_pos = (jnp.arange(nc)[:, None] - WINDOW_CHUNKS) * CHUNK + jnp.arange(BAND)[None, :]
    sc = jnp.where((key_pos >= 0)[None, :, None, None, None, :], sc, NEG_INF)
    p = sink_softmax(sc, sink).astype(v.dtype)
    o = jnp.einsum('bchgqk,bckhd->bcqhgd', p, vb)
    return o.reshape(b, s, SWA_Q)


def swa_attend_sample(q, k_all, v_all, sink):
    b, t = q.shape[:2]
    qg = q.reshape(b, t, SWA_KV_HEADS, SWA_GROUP, HEAD_DIM)
    sc = jnp.einsum('bqhgd,bkhd->bhgqk', qg, k_all, preferred_element_type=jnp.float32) * SWA_SCALE
    p = sink_softmax(sc, sink).astype(v_all.dtype)
    o = jnp.einsum('bhgqk,bkhd->bqhgd', p, v_all)
    return o.reshape(b, t, SWA_Q)


def s5_scan(u, h0_re, h0_im, lam_re, lam_im, log_dt, b_re, b_im, c_re, c_im, d_skip):
    bsz, t = u.shape[:2]
    uf = u.astype(jnp.float32).reshape(bsz, t, S5_GROUPS, S5_GROUP)
    lr = jnp.minimum(lam_re.astype(jnp.float32), -1e-4)
    li = lam_im.astype(jnp.float32)
    dt = jnp.exp(log_dt.astype(jnp.float32))[:, None]
    mag = jnp.exp(lr * dt)
    ang = li * dt
    ab_re, ab_im = mag * jnp.cos(ang), mag * jnp.sin(ang)
    den = lr * lr + li * li
    nr, ni = ab_re - 1.0, ab_im
    f_re = (nr * lr + ni * li) / den
    f_im = (ni * lr - nr * li) / den
    br, bi = b_re.astype(jnp.float32), b_im.astype(jnp.float32)
    bb_re = f_re[..., None] * br - f_im[..., None] * bi
    bb_im = f_re[..., None] * bi + f_im[..., None] * br
    bu_re = jnp.einsum('btgc,gnc->btgn', uf, bb_re)
    bu_im = jnp.einsum('btgc,gnc->btgn', uf, bb_im)
    h0r, h0i = h0_re.astype(jnp.float32), h0_im.astype(jnp.float32)
    bu_re = bu_re.at[:, 0].add(ab_re * h0r - ab_im * h0i)
    bu_im = bu_im.at[:, 0].add(ab_re * h0i + ab_im * h0r)
    a_re = jnp.broadcast_to(ab_re, (1, t, S5_GROUPS, S5_STATE))
    a_im = jnp.broadcast_to(ab_im, (1, t, S5_GROUPS, S5_STATE))

    def combine(e1, e2):
        a1r, a1i, b1r, b1i = e1
        a2r, a2i, b2r, b2i = e2
        return (a2r * a1r - a2i * a1i, a2r * a1i + a2i * a1r,
                a2r * b1r - a2i * b1i + b2r, a2r * b1i + a2i * b1r + b2i)

    _, _, hr, hi = lax.associative_scan(combine, (a_re, a_im, bu_re, bu_im), axis=1)
    y = (jnp.einsum('btgn,gcn->btgc', hr, c_re.astype(jnp.float32))
         - jnp.einsum('btgn,gcn->btgc', hi, c_im.astype(jnp.float32))
         + d_skip.astype(jnp.float32).reshape(S5_GROUPS, S5_GROUP) * uf)
    return y.reshape(bsz, t, S5_WIDTH).astype(u.dtype), hr[:, -1], hi[:, -1]


def pool_mix(u, prev, pos, pool_w, pool_scale):
    t = u.shape[1]
    ext = jnp.concatenate([prev.astype(u.dtype), u], axis=1)
    extf = ext.astype(jnp.float32)
    cs = jnp.pad(jnp.cumsum(extf, axis=1), ((0, 0), (1, 0), (0, 0)))
    uf = u.astype(jnp.float32)
    outs = []
    for gi, w in enumerate(POOL_WINDOWS):
        sl = slice(gi * POOL_GROUP, (gi + 1) * POOL_GROUP)
        tot = cs[:, POOL_MAX:POOL_MAX + t, sl] - cs[:, POOL_MAX - w:POOL_MAX - w + t, sl]
        cnt = jnp.minimum(pos + 1, w).astype(jnp.float32)[None, :, None]
        outs.append(tot / cnt - uf[..., sl])
    m = jnp.stack(outs, axis=2)
    y = jnp.einsum('btgc,gcd->btgd', m, pool_w.astype(jnp.float32)).reshape(u.shape[0], t, POOL_WIDTH)
    y = y * pool_scale.astype(jnp.float32)
    return y.astype(u.dtype), ext[:, -POOL_BUF:]


def mla_block(qa, qp, c, kp, mask=None):
    s = (jnp.einsum('bthc,bsc->bhts', qa, c, preferred_element_type=jnp.float32)
         + jnp.einsum('bthr,bsr->bhts', qp, kp, preferred_element_type=jnp.float32)) * MLA_SCALE
    if mask is not None:
        s = jnp.where(mask[None, None], s, NEG_INF)
    p = jax.nn.softmax(s, axis=-1).astype(c.dtype)
    return jnp.einsum('bhts,bsc->bthc', p, c)


def mla_prompt(qa, qp, c, kp):
    b, s = qa.shape[:2]
    nb = s // Q_BLOCK
    kchunk = jnp.arange(s) // CHUNK
    qa_b = qa.reshape(b, nb, Q_BLOCK, MLA_HEADS, KV_LORA).swapaxes(0, 1)
    qp_b = qp.reshape(b, nb, Q_BLOCK, MLA_HEADS, ROPE_DIM).swapaxes(0, 1)

    def one(args):
        qa_i, qp_i, i = args
        qchunk = (i * Q_BLOCK + jnp.arange(Q_BLOCK)) // CHUNK
        mask = kchunk[None, :] <= qchunk[:, None]
        return mla_block(qa_i, qp_i, c, kp, mask)

    o = lax.map(one, (qa_b, qp_b, jnp.arange(nb)))
    return o.swapaxes(0, 1).reshape(b, s, MLA_HEADS, KV_LORA)


def even_mixer(hn, pos, k_prev, v_prev, hre_prev, him_prev, w_in, w_out, sink, lam_re, lam_im,
               log_dt, b_re, b_im, c_re, c_im, d_skip, w_glu, b_glu):
    b, t = hn.shape[:2]
    proj = hn @ w_in
    q = proj[..., :SWA_Q].reshape(b, t, SWA_HEADS, HEAD_DIM)
    k = proj[..., SWA_Q:SWA_Q + SWA_KV].reshape(b, t, SWA_KV_HEADS, HEAD_DIM)
    v = proj[..., SWA_Q + SWA_KV:SWA_Q + 2 * SWA_KV].reshape(b, t, SWA_KV_HEADS, HEAD_DIM)
    u = proj[..., SWA_Q + 2 * SWA_KV:]
    q = partial_rope(q, pos)
    k = partial_rope(k, pos)
    if k_prev is None:
        att = swa_attend_prompt(q, k, v, sink)
        k_all, v_all = k, v
        hre_prev = jnp.zeros((b, S5_GROUPS, S5_STATE), jnp.float32)
        him_prev = jnp.zeros((b, S5_GROUPS, S5_STATE), jnp.float32)
    else:
        k_all = jnp.concatenate([k_prev.astype(k.dtype), k], axis=1)
        v_all = jnp.concatenate([v_prev.astype(v.dtype), v], axis=1)
        att = swa_attend_sample(q, k_all, v_all, sink)
    y, hre, him = s5_scan(u, hre_prev, him_prev, lam_re, lam_im, log_dt, b_re, b_im, c_re, c_im, d_skip)
    z = jax.nn.gelu(y)
    s5o = z * jax.nn.sigmoid(z @ w_glu + b_glu)
    out = jnp.concatenate([att, s5o.astype(att.dtype)], axis=-1) @ w_out
    return out, k_all[:, -WINDOW:], v_all[:, -WINDOW:], hre, him


def odd_mixer(hn, pos, pool_prev, ckv_prev, kpe_prev, w_in, w_out, pool_w, pool_scale,
              q_norm, kv_norm, w_uq, w_uk, w_uv):
    b, t = hn.shape[:2]
    proj = hn @ w_in
    o0 = Q_LORA
    o1 = o0 + KV_LORA
    o2 = o1 + ROPE_DIM
    cq, ckv, kpe, u = proj[..., :o0], proj[..., o0:o1], proj[..., o1:o2], proj[..., o2:]
    if pool_prev is None:
        pool_prev = jnp.zeros((b, POOL_BUF, POOL_WIDTH), u.dtype)
    pool_out, pool_new = pool_mix(u, pool_prev, pos, pool_w, pool_scale)
    q = (rmsnorm(cq, q_norm) @ w_uq).reshape(b, t, MLA_HEADS, NOPE_DIM + ROPE_DIM)
    q_nope = q[..., :NOPE_DIM]
    q_pe = rope(q[..., NOPE_DIM:], pos)
    c = rmsnorm(ckv, kv_norm)
    kp = rope(kpe[:, :, None, :], pos)[:, :, 0]
    qa = jnp.einsum('bthn,chn->bthc', q_nope, w_uk)
    if ckv_prev is None:
        o_lat = mla_prompt(qa, q_pe, c, kp)
    else:
        c_all = jnp.concatenate([ckv_prev.astype(c.dtype), c], axis=1)
        kp_all = jnp.concatenate([kpe_prev.astype(kp.dtype), kp], axis=1)
        o_lat = mla_block(qa, q_pe, c_all, kp_all)
    mla_out = jnp.einsum('bthc,chv->bthv', o_lat, w_uv).reshape(b, t, MLA_HEADS * V_DIM)
    out = jnp.concatenate([pool_out, mla_out.astype(pool_out.dtype)], axis=-1) @ w_out
    return out, pool_new, c, kp


def peer(x, w_q, sub_keys, u_tab, v_tab):
    shp = x.shape
    xt = x.reshape(-1, D_MODEL)
    n = xt.shape[0]
    blk = min(PEER_BLOCK, n)
    n_pad = -(-n // blk) * blk
    xt = jnp.pad(xt, ((0, n_pad - n), (0, 0)))

    def one(xb):
        q = (xb @ w_q).reshape(blk, PEER_HEADS, 2, D_HALF)
        s = jnp.einsum('nhpd,hpkd->nhpk', q, sub_keys, preferred_element_type=jnp.float32)
        sv, si = lax.top_k(s, PEER_TOPK)
        cand = (sv[:, :, 0, :, None] + sv[:, :, 1, None, :]).reshape(blk, PEER_HEADS, PEER_TOPK * PEER_TOPK)
        cv, ci = lax.top_k(cand, PEER_TOPK)
        i1 = jnp.take_along_axis(si[:, :, 0], ci // PEER_TOPK, axis=-1)
        i2 = jnp.take_along_axis(si[:, :, 1], ci % PEER_TOPK, axis=-1)
        eid = i1 * N_KEYS + i2
        g = jax.nn.softmax(cv, axis=-1)
        act = jax.nn.gelu(jnp.einsum('nd,nhkd->nhk', xb, u_tab[eid], preferred_element_type=jnp.float32),
                          approximate=False)
        return jnp.einsum('nhk,nhkd->nd', (g * act).astype(xb.dtype), v_tab[eid])

    out = lax.map(one, xt.reshape(n_pad // blk, blk, D_MODEL)).reshape(n_pad, D_MODEL)[:n]
    return out.reshape(shp).astype(x.dtype)


def setup_inputs(seed: int = 0) -> dict:
    key = jax.random.key(seed)
    ks = iter(jax.random.split(key, 64))

    def nrm(shape, scale):
        return jax.random.normal(next(ks), shape, jnp.float32) * scale

    E, O, L = N_EVEN, N_ODD, DEPTH
    return {
        'x_prompt': nrm((BATCH, SEQ, D_MODEL), 1.0),
        'x_sample': nrm((DEC_BATCH, DEC_SEQ, D_MODEL), 1.0),
        'cache_swa_k': nrm((E, DEC_BATCH, WINDOW, SWA_KV_HEADS, HEAD_DIM), 1.0),
        'cache_swa_v': nrm((E, DEC_BATCH, WINDOW, SWA_KV_HEADS, HEAD_DIM), 1.0),
        'state_ssm_re': nrm((E, DEC_BATCH, S5_GROUPS, S5_STATE), 0.3),
        'state_ssm_im': nrm((E, DEC_BATCH, S5_GROUPS, S5_STATE), 0.3),
        'state_pool': nrm((O, DEC_BATCH, POOL_BUF, POOL_WIDTH), 1.0),
        'cache_mla_ckv': nrm((O, DEC_BATCH, PAST_LEN, KV_LORA), 1.0),
        'cache_mla_kpe': nrm((O, DEC_BATCH, PAST_LEN, ROPE_DIM), 1.0),
        'norm_mix': 1.0 + nrm((L, D_MODEL), 0.02),
        'norm_ffn': 1.0 + nrm((L, D_MODEL), 0.02),
        'norm_final': 1.0 + nrm((D_MODEL,), 0.02),
        'w_in_even': nrm((E, D_MODEL, EVEN_IN), D_MODEL ** -0.5),
        'w_out_even': nrm((E, EVEN_MIX, D_MODEL), EVEN_MIX ** -0.5),
        'swa_sink': nrm((E, SWA_HEADS), 1.0),
        's5_lam_re': -0.5 + nrm((E, S5_GROUPS, S5_STATE), 0.01),
        's5_lam_im': jnp.pi * jnp.arange(S5_STATE, dtype=jnp.float32) + nrm((E, S5_GROUPS, S5_STATE), 0.01),
        's5_log_dt': jax.random.uniform(next(ks), (E, S5_GROUPS), jnp.float32, math.log(1e-3), math.log(1e-1)),
        's5_b_re': nrm((E, S5_GROUPS, S5_STATE, S5_GROUP), (2 * S5_GROUP) ** -0.5),
        's5_b_im': nrm((E, S5_GROUPS, S5_STATE, S5_GROUP), (2 * S5_GROUP) ** -0.5),
        's5_c_re': nrm((E, S5_GROUPS, S5_GROUP, S5_STATE), S5_STATE ** -0.5),
        's5_c_im': nrm((E, S5_GROUPS, S5_GROUP, S5_STATE), S5_STATE ** -0.5),
        's5_d': nrm((E, S5_WIDTH), 1.0),
        's5_w_glu': nrm((E, S5_WIDTH, S5_WIDTH), S5_WIDTH ** -0.5),
        's5_b_glu': nrm((E, S5_WIDTH), 0.02),
        'w_in_odd': nrm((O, D_MODEL, ODD_IN), D_MODEL ** -0.5),
        'w_out_odd': nrm((O, ODD_MIX, D_MODEL), ODD_MIX ** -0.5),
        'pool_w': nrm((O, len(POOL_WINDOWS), POOL_GROUP, POOL_GROUP), POOL_GROUP ** -0.5),
        'pool_scale': 1.0 + nrm((O, POOL_WIDTH), 0.1),
        'mla_q_norm': 1.0 + nrm((O, Q_LORA), 0.02),
        'mla_kv_norm': 1.0 + nrm((O, KV_LORA), 0.02),
        'mla_w_uq': nrm((O, Q_LORA, MLA_HEADS * (NOPE_DIM + ROPE_DIM)), Q_LORA ** -0.5),
        'mla_w_uk': nrm((O, KV_LORA, MLA_HEADS, NOPE_DIM), KV_LORA ** -0.5),
        'mla_w_uv': nrm((O, KV_LORA, MLA_HEADS, V_DIM), KV_LORA ** -0.5),
        'peer_w_q': nrm((L, D_MODEL, PEER_HEADS * D_KEY), D_MODEL ** -0.5),
        'peer_keys': nrm((L, PEER_HEADS, 2, N_KEYS, D_HALF), D_HALF ** -0.5),
        'peer_u': nrm((L, N_EXPERTS, D_MODEL), D_MODEL ** -0.5),
        'peer_v': nrm((L, N_EXPERTS, D_MODEL), (PEER_HEADS * PEER_TOPK) ** -0.5),
    }


def reference(x_prompt, x_sample, cache_swa_k, cache_swa_v, state_ssm_re, state_ssm_im, state_pool,
              cache_mla_ckv, cache_mla_kpe, norm_mix, norm_ffn, norm_final, w_in_even, w_out_even,
              swa_sink, s5_lam_re, s5_lam_im, s5_log_dt, s5_b_re, s5_b_im, s5_c_re, s5_c_im, s5_d,
              s5_w_glu, s5_b_glu, w_in_odd, w_out_odd, pool_w, pool_scale, mla_q_norm, mla_kv_norm,
              mla_w_uq, mla_w_uk, mla_w_uv, peer_w_q, peer_keys, peer_u, peer_v):
    pos_p = jnp.arange(x_prompt.shape[1])
    pos_s = PAST_LEN + jnp.arange(x_sample.shape[1])
    xp, xs = x_prompt, x_sample
    kp_l, vp_l, rp_l, ip_l, poolp_l, cp_l, ep_l = [], [], [], [], [], [], []
    ks_l, vs_l, rs_l, is_l, pools_l, cs_l, es_l = [], [], [], [], [], [], []
    for layer in range(DEPTH):
        i = layer // 2
        if layer % 2 == 0:
            ew = (w_in_even[i], w_out_even[i], swa_sink[i], s5_lam_re[i], s5_lam_im[i], s5_log_dt[i],
                  s5_b_re[i], s5_b_im[i], s5_c_re[i], s5_c_im[i], s5_d[i], s5_w_glu[i], s5_b_glu[i])
            mp, k1, v1, r1, i1 = even_mixer(rmsnorm(xp, norm_mix[layer]), pos_p, None, None, None, None, *ew)
            ms, k2, v2, r2, i2 = even_mixer(rmsnorm(xs, norm_mix[layer]), pos_s, cache_swa_k[i], cache_swa_v[i],
                                            state_ssm_re[i], state_ssm_im[i], *ew)
            kp_l.append(k1); vp_l.append(v1); rp_l.append(r1); ip_l.append(i1)
            ks_l.append(k2); vs_l.append(v2); rs_l.append(r2); is_l.append(i2)
        else:
            ow = (w_in_odd[i], w_out_odd[i], pool_w[i], pool_scale[i], mla_q_norm[i], mla_kv_norm[i],
                  mla_w_uq[i], mla_w_uk[i], mla_w_uv[i])
            mp, p1, c1, e1 = odd_mixer(rmsnorm(xp, norm_mix[layer]), pos_p, None, None, None, *ow)
            ms, p2, c2, e2 = odd_mixer(rmsnorm(xs, norm_mix[layer]), pos_s, state_pool[i], cache_mla_ckv[i],
                                       cache_mla_kpe[i], *ow)
            poolp_l.append(p1); cp_l.append(c1); ep_l.append(e1)
            pools_l.append(p2); cs_l.append(c2); es_l.append(e2)
        xp = xp + mp
        xs = xs + ms
        xp = xp + peer(rmsnorm(xp, norm_ffn[layer]), peer_w_q[layer], peer_keys[layer], peer_u[layer], peer_v[layer])
        xs = xs + peer(rmsnorm(xs, norm_ffn[layer]), peer_w_q[layer], peer_keys[layer], peer_u[layer], peer_v[layer])
    y_prompt = rmsnorm(xp, norm_final)
    y_sample = rmsnorm(xs, norm_final)
    return (y_prompt, y_sample,
            jnp.stack(kp_l), jnp.stack(vp_l), jnp.stack(rp_l), jnp.stack(ip_l),
            jnp.stack(poolp_l), jnp.stack(cp_l), jnp.stack(ep_l),
            jnp.stack(ks_l), jnp.stack(vs_l), jnp.stack(rs_l), jnp.stack(is_l),
            jnp.stack(pools_l), jnp.stack(cs_l), jnp.stack(es_l))
```

```python
import math
import jax
import jax.numpy as jnp
from jax import lax
import numpy as np
from jax.experimental import pallas as pl
from jax.experimental.pallas import tpu as pltpu

D_MODEL = 1024
BATCH = 8
SEQ = 4096
DEPTH = 2
DEC_BATCH = 32
DEC_SEQ = 32
PAST_LEN = 2048

CHUNK = 64
N_EVEN = (DEPTH + 1) // 2
N_ODD = DEPTH // 2
RMS_EPS = 1e-6
ROPE_THETA = 500000.0
NEG_INF = -1e30

SWA_HEADS = 8
SWA_KV_HEADS = 2
SWA_GROUP = SWA_HEADS // SWA_KV_HEADS
HEAD_DIM = 64
ROT_DIM = HEAD_DIM // 4
WINDOW = 128
WINDOW_CHUNKS = WINDOW // CHUNK
BAND = (WINDOW_CHUNKS + 1) * CHUNK
SWA_Q = SWA_HEADS * HEAD_DIM
SWA_KV = SWA_KV_HEADS * HEAD_DIM
SWA_SCALE = HEAD_DIM ** -0.5

S5_WIDTH = 512
S5_GROUP = 16
S5_GROUPS = S5_WIDTH // S5_GROUP
S5_STATE = 64

POOL_WIDTH = 512
POOL_WINDOWS = (2, 4, 8, 16)
POOL_GROUP = POOL_WIDTH // len(POOL_WINDOWS)
POOL_MAX = 16
POOL_BUF = POOL_MAX - 1

MLA_HEADS = 8
Q_LORA = 512
KV_LORA = 256
NOPE_DIM = 64
ROPE_DIM = 32
V_DIM = 64
Q_BLOCK = 128
MLA_SCALE = (NOPE_DIM + ROPE_DIM) ** -0.5

EVEN_IN = SWA_Q + 2 * SWA_KV + S5_WIDTH
EVEN_MIX = SWA_Q + S5_WIDTH
ODD_IN = Q_LORA + KV_LORA + ROPE_DIM + POOL_WIDTH
ODD_MIX = POOL_WIDTH + MLA_HEADS * V_DIM

PEER_HEADS = 8
N_KEYS = 128
N_EXPERTS = N_KEYS * N_KEYS
D_KEY = 128
D_HALF = D_KEY // 2
PEER_TOPK = 16
PEER_BLOCK = 256


def rmsnorm(x, g):
    xf = x.astype(jnp.float32)
    y = xf * lax.rsqrt(jnp.mean(xf * xf, axis=-1, keepdims=True) + RMS_EPS)
    return (y * g.astype(jnp.float32)).astype(x.dtype)


def rope(x, pos):
    r = x.shape[-1]
    inv = ROPE_THETA ** (-jnp.arange(0, r, 2, dtype=jnp.float32) / r)
    ang = pos.astype(jnp.float32)[:, None] * inv[None, :]
    cos = jnp.cos(ang)[:, None, :]
    sin = jnp.sin(ang)[:, None, :]
    xf = x.astype(jnp.float32)
    x1, x2 = xf[..., : r // 2], xf[..., r // 2:]
    return jnp.concatenate([x1 * cos - x2 * sin, x2 * cos + x1 * sin], axis=-1).astype(x.dtype)


def partial_rope(x, pos):
    return jnp.concatenate([rope(x[..., :ROT_DIM], pos), x[..., ROT_DIM:]], axis=-1)


def sink_softmax(s, sink):
    sk = sink.astype(jnp.float32).reshape(SWA_KV_HEADS, SWA_GROUP)[:, :, None, None]
    m = jnp.maximum(jnp.max(s, axis=-1, keepdims=True), sk)
    p = jnp.exp(s - m)
    return p / (jnp.sum(p, axis=-1, keepdims=True) + jnp.exp(sk - m))


def swa_attend_prompt(q, k, v, sink):
    b, s = q.shape[:2]
    nc = s // CHUNK
    qc = q.reshape(b, nc, CHUNK, SWA_KV_HEADS, SWA_GROUP, HEAD_DIM)
    pad = WINDOW_CHUNKS * CHUNK

    def band(t):
        tp = jnp.pad(t, ((0, 0), (pad, 0), (0, 0), (0, 0)))
        tp = tp.reshape(b, nc + WINDOW_CHUNKS, CHUNK, SWA_KV_HEADS, HEAD_DIM)
        return jnp.concatenate([tp[:, i:i + nc] for i in range(WINDOW_CHUNKS + 1)], axis=2)

    kb, vb = band(k), band(v)
    sc = jnp.einsum('bcqhgd,bckhd->bchgqk', qc, kb, preferred_element_type=jnp.float32) * SWA_SCALE
    key_pos = (jnp.arange(nc)[:, None] - WINDOW_CHUNKS) * CHUNK + jnp.arange(BAND)[None, :]
    sc = jnp.where((key_pos >= 0)[None, :, None, None, None, :], sc, NEG_INF)
    p = sink_softmax(sc, sink).astype(v.dtype)
    o = jnp.einsum('bchgqk,bckhd->bcqhgd', p, vb)
    return o.reshape(b, s, SWA_Q)


def swa_attend_sample(q, k_all, v_all, sink):
    b, t = q.shape[:2]
    qg = q.reshape(b, t, SWA_KV_HEADS, SWA_GROUP, HEAD_DIM)
    sc = jnp.einsum('bqhgd,bkhd->bhgqk', qg, k_all, preferred_element_type=jnp.float32) * SWA_SCALE
    p = sink_softmax(sc, sink).astype(v_all.dtype)
    o = jnp.einsum('bhgqk,bkhd->bqhgd', p, v_all)
    return o.reshape(b, t, SWA_Q)


def s5_scan(u, h0_re, h0_im, lam_re, lam_im, log_dt, b_re, b_im, c_re, c_im, d_skip):
    bsz, t = u.shape[:2]
    uf = u.astype(jnp.float32).reshape(bsz, t, S5_GROUPS, S5_GROUP)
    lr = jnp.minimum(lam_re.astype(jnp.float32), -1e-4)
    li = lam_im.astype(jnp.float32)
    dt = jnp.exp(log_dt.astype(jnp.float32))[:, None]
    mag = jnp.exp(lr * dt)
    ang = li * dt
    ab_re, ab_im = mag * jnp.cos(ang), mag * jnp.sin(ang)
    den = lr * lr + li * li
    nr, ni = ab_re - 1.0, ab_im
    f_re = (nr * lr + ni * li) / den
    f_im = (ni * lr - nr * li) / den
    br, bi = b_re.astype(jnp.float32), b_im.astype(jnp.float32)
    bb_re = f_re[..., None] * br - f_im[..., None] * bi
    bb_im = f_re[..., None] * bi + f_im[..., None] * br
    bu_re = jnp.einsum('btgc,gnc->btgn', uf, bb_re)
    bu_im = jnp.einsum('btgc,gnc->btgn', uf, bb_im)
    h0r, h0i = h0_re.astype(jnp.float32), h0_im.astype(jnp.float32)
    bu_re = bu_re.at[:, 0].add(ab_re * h0r - ab_im * h0i)
    bu_im = bu_im.at[:, 0].add(ab_re * h0i + ab_im * h0r)
    a_re = jnp.broadcast_to(ab_re, (1, t, S5_GROUPS, S5_STATE))
    a_im = jnp.broadcast_to(ab_im, (1, t, S5_GROUPS, S5_STATE))

    def combine(e1, e2):
        a1r, a1i, b1r, b1i = e1
        a2r, a2i, b2r, b2i = e2
        return (a2r * a1r - a2i * a1i, a2r * a1i + a2i * a1r,
                a2r * b1r - a2i * b1i + b2r, a2r * b1i + a2i * b1r + b2i)

    _, _, hr, hi = lax.associative_scan(combine, (a_re, a_im, bu_re, bu_im), axis=1)
    y = (jnp.einsum('btgn,gcn->btgc', hr, c_re.astype(jnp.float32))
         - jnp.einsum('btgn,gcn->btgc', hi, c_im.astype(jnp.float32))
         + d_skip.astype(jnp.float32).reshape(S5_GROUPS, S5_GROUP) * uf)
    return y.reshape(bsz, t, S5_WIDTH).astype(u.dtype), hr[:, -1], hi[:, -1]


def pool_mix(u, prev, pos, pool_w, pool_scale):
    t = u.shape[1]
    ext = jnp.concatenate([prev.astype(u.dtype), u], axis=1)
    extf = ext.astype(jnp.float32)
    cs = jnp.pad(jnp.cumsum(extf, axis=1), ((0, 0), (1, 0), (0, 0)))
    uf = u.astype(jnp.float32)
    outs = []
    for gi, w in enumerate(POOL_WINDOWS):
        sl = slice(gi * POOL_GROUP, (gi + 1) * POOL_GROUP)
        tot = cs[:, POOL_MAX:POOL_MAX + t, sl] - cs[:, POOL_MAX - w:POOL_MAX - w + t, sl]
        cnt = jnp.minimum(pos + 1, w).astype(jnp.float32)[None, :, None]
        outs.append(tot / cnt - uf[..., sl])
    m = jnp.stack(outs, axis=2)
    y = jnp.einsum('btgc,gcd->btgd', m, pool_w.astype(jnp.float32)).reshape(u.shape[0], t, POOL_WIDTH)
    y = y * pool_scale.astype(jnp.float32)
    return y.astype(u.dtype), ext[:, -POOL_BUF:]


def mla_block(qa, qp, c, kp, mask=None):
    s = (jnp.einsum('bthc,bsc->bhts', qa, c, preferred_element_type=jnp.float32)
         + jnp.einsum('bthr,bsr->bhts', qp, kp, preferred_element_type=jnp.float32)) * MLA_SCALE
    if mask is not None:
        s = jnp.where(mask[None, None], s, NEG_INF)
    p = jax.nn.softmax(s, axis=-1).astype(c.dtype)
    return jnp.einsum('bhts,bsc->bthc', p, c)


def mla_prompt(qa, qp, c, kp):
    b, s = qa.shape[:2]
    nb = s // Q_BLOCK
    kchunk = jnp.arange(s) // CHUNK
    qa_b = qa.reshape(b, nb, Q_BLOCK, MLA_HEADS, KV_LORA).swapaxes(0, 1)
    qp_b = qp.reshape(b, nb, Q_BLOCK, MLA_HEADS, ROPE_DIM).swapaxes(0, 1)

    def one(args):
        qa_i, qp_i, i = args
        qchunk = (i * Q_BLOCK + jnp.arange(Q_BLOCK)) // CHUNK
        mask = kchunk[None, :] <= qchunk[:, None]
        return mla_block(qa_i, qp_i, c, kp, mask)

    o = lax.map(one, (qa_b, qp_b, jnp.arange(nb)))
    return o.swapaxes(0, 1).reshape(b, s, MLA_HEADS, KV_LORA)


def even_mixer(hn, pos, k_prev, v_prev, hre_prev, him_prev, w_in, w_out, sink, lam_re, lam_im,
               log_dt, b_re, b_im, c_re, c_im, d_skip, w_glu, b_glu):
    b, t = hn.shape[:2]
    proj = hn @ w_in
    q = proj[..., :SWA_Q].reshape(b, t, SWA_HEADS, HEAD_DIM)
    k = proj[..., SWA_Q:SWA_Q + SWA_KV].reshape(b, t, SWA_KV_HEADS, HEAD_DIM)
    v = proj[..., SWA_Q + SWA_KV:SWA_Q + 2 * SWA_KV].reshape(b, t, SWA_KV_HEADS, HEAD_DIM)
    u = proj[..., SWA_Q + 2 * SWA_KV:]
    q = partial_rope(q, pos)
    k = partial_rope(k, pos)
    if k_prev is None:
        att = swa_attend_prompt(q, k, v, sink)
        k_all, v_all = k, v
        hre_prev = jnp.zeros((b, S5_GROUPS, S5_STATE), jnp.float32)
        him_prev = jnp.zeros((b, S5_GROUPS, S5_STATE), jnp.float32)
    else:
        k_all = jnp.concatenate([k_prev.astype(k.dtype), k], axis=1)
        v_all = jnp.concatenate([v_prev.astype(v.dtype), v], axis=1)
        att = swa_attend_sample(q, k_all, v_all, sink)
    y, hre, him = s5_scan(u, hre_prev, him_prev, lam_re, lam_im, log_dt, b_re, b_im, c_re, c_im, d_skip)
    z = jax.nn.gelu(y)
    s5o = z * jax.nn.sigmoid(z @ w_glu + b_glu)
    out = jnp.concatenate([att, s5o.astype(att.dtype)], axis=-1) @ w_out
    return out, k_all[:, -WINDOW:], v_all[:, -WINDOW:], hre, him


def odd_mixer(hn, pos, pool_prev, ckv_prev, kpe_prev, w_in, w_out, pool_w, pool_scale,
              q_norm, kv_norm, w_uq, w_uk, w_uv):
    b, t = hn.shape[:2]
    proj = hn @ w_in
    o0 = Q_LORA
    o1 = o0 + KV_LORA
    o2 = o1 + ROPE_DIM
    cq, ckv, kpe, u = proj[..., :o0], proj[..., o0:o1], proj[..., o1:o2], proj[..., o2:]
    if pool_prev is None:
        pool_prev = jnp.zeros((b, POOL_BUF, POOL_WIDTH), u.dtype)
    pool_out, pool_new = pool_mix(u, pool_prev, pos, pool_w, pool_scale)
    q = (rmsnorm(cq, q_norm) @ w_uq).reshape(b, t, MLA_HEADS, NOPE_DIM + ROPE_DIM)
    q_nope = q[..., :NOPE_DIM]
    q_pe = rope(q[..., NOPE_DIM:], pos)
    c = rmsnorm(ckv, kv_norm)
    kp = rope(kpe[:, :, None, :], pos)[:, :, 0]
    qa = jnp.einsum('bthn,chn->bthc', q_nope, w_uk)
    if ckv_prev is None:
        o_lat = mla_prompt(qa, q_pe, c, kp)
    else:
        c_all = jnp.concatenate([ckv_prev.astype(c.dtype), c], axis=1)
        kp_all = jnp.concatenate([kpe_prev.astype(kp.dtype), kp], axis=1)
        o_lat = mla_block(qa, q_pe, c_all, kp_all)
    mla_out = jnp.einsum('bthc,chv->bthv', o_lat, w_uv).reshape(b, t, MLA_HEADS * V_DIM)
    out = jnp.concatenate([pool_out, mla_out.astype(pool_out.dtype)], axis=-1) @ w_out
    return out, pool_new, c, kp


def peer(x, w_q, sub_keys, u_tab, v_tab):
    shp = x.shape
    xt = x.reshape(-1, D_MODEL)
    n = xt.shape[0]
    blk = min(PEER_BLOCK, n)
    n_pad = -(-n // blk) * blk
    xt = jnp.pad(xt, ((0, n_pad - n), (0, 0)))

    def one(xb):
        q = (xb @ w_q).reshape(blk, PEER_HEADS, 2, D_HALF)
        s = jnp.einsum('nhpd,hpkd->nhpk', q, sub_keys, preferred_element_type=jnp.float32)
        sv, si = lax.top_k(s, PEER_TOPK)
        cand = (sv[:, :, 0, :, None] + sv[:, :, 1, None, :]).reshape(blk, PEER_HEADS, PEER_TOPK * PEER_TOPK)
        cv, ci = lax.top_k(cand, PEER_TOPK)
        i1 = jnp.take_along_axis(si[:, :, 0], ci // PEER_TOPK, axis=-1)
        i2 = jnp.take_along_axis(si[:, :, 1], ci % PEER_TOPK, axis=-1)
        eid = i1 * N_KEYS + i2
        g = jax.nn.softmax(cv, axis=-1)
        act = jax.nn.gelu(jnp.einsum('nd,nhkd->nhk', xb, u_tab[eid], preferred_element_type=jnp.float32),
                          approximate=False)
        return jnp.einsum('nhk,nhkd->nd', (g * act).astype(xb.dtype), v_tab[eid])

    out = lax.map(one, xt.reshape(n_pad // blk, blk, D_MODEL)).reshape(n_pad, D_MODEL)[:n]
    return out.reshape(shp).astype(x.dtype)


def _rms_kernel(x_ref, g_ref, o_ref):
    x = x_ref[...]
    y = x * lax.rsqrt(jnp.mean(x * x, axis=-1, keepdims=True) + RMS_EPS)
    o_ref[...] = y * g_ref[...]


def rmsnorm_pallas(x, g):
    shp = x.shape
    xt = x.reshape(-1, shp[-1])
    n = xt.shape[0]
    tm = 512
    out = pl.pallas_call(
        _rms_kernel,
        out_shape=jax.ShapeDtypeStruct(xt.shape, xt.dtype),
        grid=(n // tm,),
        in_specs=[pl.BlockSpec((tm, shp[-1]), lambda i: (i, 0)),
                  pl.BlockSpec((1, shp[-1]), lambda i: (0, 0))],
        out_specs=pl.BlockSpec((tm, shp[-1]), lambda i: (i, 0)),
        name="final_rmsnorm",
    )(xt, g.reshape(1, -1))
    return out.reshape(shp)


def kernel(x_prompt, x_sample, cache_swa_k, cache_swa_v, state_ssm_re, state_ssm_im, state_pool,
           cache_mla_ckv, cache_mla_kpe, norm_mix, norm_ffn, norm_final, w_in_even, w_out_even,
           swa_sink, s5_lam_re, s5_lam_im, s5_log_dt, s5_b_re, s5_b_im, s5_c_re, s5_c_im, s5_d,
           s5_w_glu, s5_b_glu, w_in_odd, w_out_odd, pool_w, pool_scale, mla_q_norm, mla_kv_norm,
           mla_w_uq, mla_w_uk, mla_w_uv, peer_w_q, peer_keys, peer_u, peer_v):
    pos_p = jnp.arange(x_prompt.shape[1])
    pos_s = PAST_LEN + jnp.arange(x_sample.shape[1])
    xp, xs = x_prompt, x_sample
    kp_l, vp_l, rp_l, ip_l, poolp_l, cp_l, ep_l = [], [], [], [], [], [], []
    ks_l, vs_l, rs_l, is_l, pools_l, cs_l, es_l = [], [], [], [], [], [], []
    for layer in range(DEPTH):
        i = layer // 2
        if layer % 2 == 0:
            ew = (w_in_even[i], w_out_even[i], swa_sink[i], s5_lam_re[i], s5_lam_im[i], s5_log_dt[i],
                  s5_b_re[i], s5_b_im[i], s5_c_re[i], s5_c_im[i], s5_d[i], s5_w_glu[i], s5_b_glu[i])
            mp, k1, v1, r1, i1 = even_mixer(rmsnorm(xp, norm_mix[layer]), pos_p, None, None, None, None, *ew)
            ms, k2, v2, r2, i2 = even_mixer(rmsnorm(xs, norm_mix[layer]), pos_s, cache_swa_k[i], cache_swa_v[i],
                                            state_ssm_re[i], state_ssm_im[i], *ew)
            kp_l.append(k1); vp_l.append(v1); rp_l.append(r1); ip_l.append(i1)
            ks_l.append(k2); vs_l.append(v2); rs_l.append(r2); is_l.append(i2)
        else:
            ow = (w_in_odd[i], w_out_odd[i], pool_w[i], pool_scale[i], mla_q_norm[i], mla_kv_norm[i],
                  mla_w_uq[i], mla_w_uk[i], mla_w_uv[i])
            mp, p1, c1, e1 = odd_mixer(rmsnorm(xp, norm_mix[layer]), pos_p, None, None, None, *ow)
            ms, p2, c2, e2 = odd_mixer(rmsnorm(xs, norm_mix[layer]), pos_s, state_pool[i], cache_mla_ckv[i],
                                       cache_mla_kpe[i], *ow)
            poolp_l.append(p1); cp_l.append(c1); ep_l.append(e1)
            pools_l.append(p2); cs_l.append(c2); es_l.append(e2)
        xp = xp + mp
        xs = xs + ms
        xp = xp + peer(rmsnorm(xp, norm_ffn[layer]), peer_w_q[layer], peer_keys[layer], peer_u[layer], peer_v[layer])
        xs = xs + peer(rmsnorm(xs, norm_ffn[layer]), peer_w_q[layer], peer_keys[layer], peer_u[layer], peer_v[layer])
    y_prompt = rmsnorm_pallas(xp, norm_final)
    y_sample = rmsnorm_pallas(xs, norm_final)
    return (y_prompt, y_sample,
            jnp.stack(kp_l), jnp.stack(vp_l), jnp.stack(rp_l), jnp.stack(ip_l),
            jnp.stack(poolp_l), jnp.stack(cp_l), jnp.stack(ep_l),
            jnp.stack(ks_l), jnp.stack(vs_l), jnp.stack(rs_l), jnp.stack(is_l),
            jnp.stack(pools_l), jnp.stack(cs_l), jnp.stack(es_l))
```

```python
import functools
import math
import jax
import jax.numpy as jnp
from jax import lax
import numpy as np
from jax.experimental import pallas as pl
from jax.experimental.pallas import tpu as pltpu

D_MODEL = 1024
BATCH = 8
SEQ = 4096
DEPTH = 2
DEC_BATCH = 32
DEC_SEQ = 32
PAST_LEN = 2048

CHUNK = 64
N_EVEN = (DEPTH + 1) // 2
N_ODD = DEPTH // 2
RMS_EPS = 1e-6
ROPE_THETA = 500000.0
NEG_INF = -1e30

SWA_HEADS = 8
SWA_KV_HEADS = 2
SWA_GROUP = SWA_HEADS // SWA_KV_HEADS
HEAD_DIM = 64
ROT_DIM = HEAD_DIM // 4
WINDOW = 128
WINDOW_CHUNKS = WINDOW // CHUNK
BAND = (WINDOW_CHUNKS + 1) * CHUNK
SWA_Q = SWA_HEADS * HEAD_DIM
SWA_KV = SWA_KV_HEADS * HEAD_DIM
SWA_SCALE = HEAD_DIM ** -0.5

S5_WIDTH = 512
S5_GROUP = 16
S5_GROUPS = S5_WIDTH // S5_GROUP
S5_STATE = 64

POOL_WIDTH = 512
POOL_WINDOWS = (2, 4, 8, 16)
POOL_GROUP = POOL_WIDTH // len(POOL_WINDOWS)
POOL_MAX = 16
POOL_BUF = POOL_MAX - 1

MLA_HEADS = 8
Q_LORA = 512
KV_LORA = 256
NOPE_DIM = 64
ROPE_DIM = 32
V_DIM = 64
Q_BLOCK = 128
MLA_SCALE = (NOPE_DIM + ROPE_DIM) ** -0.5

EVEN_IN = SWA_Q + 2 * SWA_KV + S5_WIDTH
EVEN_MIX = SWA_Q + S5_WIDTH
ODD_IN = Q_LORA + KV_LORA + ROPE_DIM + POOL_WIDTH
ODD_MIX = POOL_WIDTH + MLA_HEADS * V_DIM

PEER_HEADS = 8
N_KEYS = 128
N_EXPERTS = N_KEYS * N_KEYS
D_KEY = 128
D_HALF = D_KEY // 2
PEER_TOPK = 16
PEER_BLOCK = 256


def rmsnorm(x, g):
    xf = x.astype(jnp.float32)
    y = xf * lax.rsqrt(jnp.mean(xf * xf, axis=-1, keepdims=True) + RMS_EPS)
    return (y * g.astype(jnp.float32)).astype(x.dtype)


def rope(x, pos):
    r = x.shape[-1]
    inv = ROPE_THETA ** (-jnp.arange(0, r, 2, dtype=jnp.float32) / r)
    ang = pos.astype(jnp.float32)[:, None] * inv[None, :]
    cos = jnp.cos(ang)[:, None, :]
    sin = jnp.sin(ang)[:, None, :]
    xf = x.astype(jnp.float32)
    x1, x2 = xf[..., : r // 2], xf[..., r // 2:]
    return jnp.concatenate([x1 * cos - x2 * sin, x2 * cos + x1 * sin], axis=-1).astype(x.dtype)


def partial_rope(x, pos):
    return jnp.concatenate([rope(x[..., :ROT_DIM], pos), x[..., ROT_DIM:]], axis=-1)


def sink_softmax(s, sink):
    sk = sink.astype(jnp.float32).reshape(SWA_KV_HEADS, SWA_GROUP)[:, :, None, None]
    m = jnp.maximum(jnp.max(s, axis=-1, keepdims=True), sk)
    p = jnp.exp(s - m)
    return p / (jnp.sum(p, axis=-1, keepdims=True) + jnp.exp(sk - m))


def swa_attend_prompt(q, k, v, sink):
    b, s = q.shape[:2]
    nc = s // CHUNK
    qc = q.reshape(b, nc, CHUNK, SWA_KV_HEADS, SWA_GROUP, HEAD_DIM)
    pad = WINDOW_CHUNKS * CHUNK

    def band(t):
        tp = jnp.pad(t, ((0, 0), (pad, 0), (0, 0), (0, 0)))
        tp = tp.reshape(b, nc + WINDOW_CHUNKS, CHUNK, SWA_KV_HEADS, HEAD_DIM)
        return jnp.concatenate([tp[:, i:i + nc] for i in range(WINDOW_CHUNKS + 1)], axis=2)

    kb, vb = band(k), band(v)
    sc = jnp.einsum('bcqhgd,bckhd->bchgqk', qc, kb, preferred_element_type=jnp.float32) * SWA_SCALE
    key_pos = (jnp.arange(nc)[:, None] - WINDOW_CHUNKS) * CHUNK + jnp.arange(BAND)[None, :]
    sc = jnp.where((key_pos >= 0)[None, :, None, None, None, :], sc, NEG_INF)
    p = sink_softmax(sc, sink).astype(v.dtype)
    o = jnp.einsum('bchgqk,bckhd->bcqhgd', p, vb)
    return o.reshape(b, s, SWA_Q)


def swa_attend_sample(q, k_all, v_all, sink):
    b, t = q.shape[:2]
    qg = q.reshape(b, t, SWA_KV_HEADS, SWA_GROUP, HEAD_DIM)
    sc = jnp.einsum('bqhgd,bkhd->bhgqk', qg, k_all, preferred_element_type=jnp.float32) * SWA_SCALE
    p = sink_softmax(sc, sink).astype(v_all.dtype)
    o = jnp.einsum('bhgqk,bkhd->bqhgd', p, v_all)
    return o.reshape(b, t, SWA_Q)


def s5_scan(u, h0_re, h0_im, lam_re, lam_im, log_dt, b_re, b_im, c_re, c_im, d_skip):
    bsz, t = u.shape[:2]
    uf = u.astype(jnp.float32).reshape(bsz, t, S5_GROUPS, S5_GROUP)
    lr = jnp.minimum(lam_re.astype(jnp.float32), -1e-4)
    li = lam_im.astype(jnp.float32)
    dt = jnp.exp(log_dt.astype(jnp.float32))[:, None]
    mag = jnp.exp(lr * dt)
    ang = li * dt
    ab_re, ab_im = mag * jnp.cos(ang), mag * jnp.sin(ang)
    den = lr * lr + li * li
    nr, ni = ab_re - 1.0, ab_im
    f_re = (nr * lr + ni * li) / den
    f_im = (ni * lr - nr * li) / den
    br, bi = b_re.astype(jnp.float32), b_im.astype(jnp.float32)
    bb_re = f_re[..., None] * br - f_im[..., None] * bi
    bb_im = f_re[..., None] * bi + f_im[..., None] * br
    bu_re = jnp.einsum('btgc,gnc->btgn', uf, bb_re)
    bu_im = jnp.einsum('btgc,gnc->btgn', uf, bb_im)
    h0r, h0i = h0_re.astype(jnp.float32), h0_im.astype(jnp.float32)
    bu_re = bu_re.at[:, 0].add(ab_re * h0r - ab_im * h0i)
    bu_im = bu_im.at[:, 0].add(ab_re * h0i + ab_im * h0r)
    a_re = jnp.broadcast_to(ab_re, (1, t, S5_GROUPS, S5_STATE))
    a_im = jnp.broadcast_to(ab_im, (1, t, S5_GROUPS, S5_STATE))

    def combine(e1, e2):
        a1r, a1i, b1r, b1i = e1
        a2r, a2i, b2r, b2i = e2
        return (a2r * a1r - a2i * a1i, a2r * a1i + a2i * a1r,
                a2r * b1r - a2i * b1i + b2r, a2r * b1i + a2i * b1r + b2i)

    _, _, hr, hi = lax.associative_scan(combine, (a_re, a_im, bu_re, bu_im), axis=1)
    y = (jnp.einsum('btgn,gcn->btgc', hr, c_re.astype(jnp.float32))
         - jnp.einsum('btgn,gcn->btgc', hi, c_im.astype(jnp.float32))
         + d_skip.astype(jnp.float32).reshape(S5_GROUPS, S5_GROUP) * uf)
    return y.reshape(bsz, t, S5_WIDTH).astype(u.dtype), hr[:, -1], hi[:, -1]


def pool_mix(u, prev, pos, pool_w, pool_scale):
    t = u.shape[1]
    ext = jnp.concatenate([prev.astype(u.dtype), u], axis=1)
    extf = ext.astype(jnp.float32)
    cs = jnp.pad(jnp.cumsum(extf, axis=1), ((0, 0), (1, 0), (0, 0)))
    uf = u.astype(jnp.float32)
    outs = []
    for gi, w in enumerate(POOL_WINDOWS):
        sl = slice(gi * POOL_GROUP, (gi + 1) * POOL_GROUP)
        tot = cs[:, POOL_MAX:POOL_MAX + t, sl] - cs[:, POOL_MAX - w:POOL_MAX - w + t, sl]
        cnt = jnp.minimum(pos + 1, w).astype(jnp.float32)[None, :, None]
        outs.append(tot / cnt - uf[..., sl])
    m = jnp.stack(outs, axis=2)
    y = jnp.einsum('btgc,gcd->btgd', m, pool_w.astype(jnp.float32)).reshape(u.shape[0], t, POOL_WIDTH)
    y = y * pool_scale.astype(jnp.float32)
    return y.astype(u.dtype), ext[:, -POOL_BUF:]


def mla_block(qa, qp, c, kp, mask=None):
    s = (jnp.einsum('bthc,bsc->bhts', qa, c, preferred_element_type=jnp.float32)
         + jnp.einsum('bthr,bsr->bhts', qp, kp, preferred_element_type=jnp.float32)) * MLA_SCALE
    if mask is not None:
        s = jnp.where(mask[None, None], s, NEG_INF)
    p = jax.nn.softmax(s, axis=-1).astype(c.dtype)
    return jnp.einsum('bhts,bsc->bthc', p, c)


def mla_prompt(qa, qp, c, kp):
    b, s = qa.shape[:2]
    nb = s // Q_BLOCK
    kchunk = jnp.arange(s) // CHUNK
    qa_b = qa.reshape(b, nb, Q_BLOCK, MLA_HEADS, KV_LORA).swapaxes(0, 1)
    qp_b = qp.reshape(b, nb, Q_BLOCK, MLA_HEADS, ROPE_DIM).swapaxes(0, 1)

    def one(args):
        qa_i, qp_i, i = args
        qchunk = (i * Q_BLOCK + jnp.arange(Q_BLOCK)) // CHUNK
        mask = kchunk[None, :] <= qchunk[:, None]
        return mla_block(qa_i, qp_i, c, kp, mask)

    o = lax.map(one, (qa_b, qp_b, jnp.arange(nb)))
    return o.swapaxes(0, 1).reshape(b, s, MLA_HEADS, KV_LORA)


def even_mixer(hn, pos, k_prev, v_prev, hre_prev, him_prev, w_in, w_out, sink, lam_re, lam_im,
               log_dt, b_re, b_im, c_re, c_im, d_skip, w_glu, b_glu):
    b, t = hn.shape[:2]
    proj = hn @ w_in
    q = proj[..., :SWA_Q].reshape(b, t, SWA_HEADS, HEAD_DIM)
    k = proj[..., SWA_Q:SWA_Q + SWA_KV].reshape(b, t, SWA_KV_HEADS, HEAD_DIM)
    v = proj[..., SWA_Q + SWA_KV:SWA_Q + 2 * SWA_KV].reshape(b, t, SWA_KV_HEADS, HEAD_DIM)
    u = proj[..., SWA_Q + 2 * SWA_KV:]
    q = partial_rope(q, pos)
    k = partial_rope(k, pos)
    if k_prev is None:
        att = swa_attend_prompt(q, k, v, sink)
        k_all, v_all = k, v
        hre_prev = jnp.zeros((b, S5_GROUPS, S5_STATE), jnp.float32)
        him_prev = jnp.zeros((b, S5_GROUPS, S5_STATE), jnp.float32)
    else:
        k_all = jnp.concatenate([k_prev.astype(k.dtype), k], axis=1)
        v_all = jnp.concatenate([v_prev.astype(v.dtype), v], axis=1)
        att = swa_attend_sample(q, k_all, v_all, sink)
    y, hre, him = s5_scan(u, hre_prev, him_prev, lam_re, lam_im, log_dt, b_re, b_im, c_re, c_im, d_skip)
    z = jax.nn.gelu(y)
    s5o = z * jax.nn.sigmoid(z @ w_glu + b_glu)
    out = jnp.concatenate([att, s5o.astype(att.dtype)], axis=-1) @ w_out
    return out, k_all[:, -WINDOW:], v_all[:, -WINDOW:], hre, him


def odd_mixer(hn, pos, pool_prev, ckv_prev, kpe_prev, w_in, w_out, pool_w, pool_scale,
              q_norm, kv_norm, w_uq, w_uk, w_uv):
    b, t = hn.shape[:2]
    proj = hn @ w_in
    o0 = Q_LORA
    o1 = o0 + KV_LORA
    o2 = o1 + ROPE_DIM
    cq, ckv, kpe, u = proj[..., :o0], proj[..., o0:o1], proj[..., o1:o2], proj[..., o2:]
    if pool_prev is None:
        pool_prev = jnp.zeros((b, POOL_BUF, POOL_WIDTH), u.dtype)
    pool_out, pool_new = pool_mix(u, pool_prev, pos, pool_w, pool_scale)
    q = (rmsnorm(cq, q_norm) @ w_uq).reshape(b, t, MLA_HEADS, NOPE_DIM + ROPE_DIM)
    q_nope = q[..., :NOPE_DIM]
    q_pe = rope(q[..., NOPE_DIM:], pos)
    c = rmsnorm(ckv, kv_norm)
    kp = rope(kpe[:, :, None, :], pos)[:, :, 0]
    qa = jnp.einsum('bthn,chn->bthc', q_nope, w_uk)
    if ckv_prev is None:
        o_lat = mla_prompt(qa, q_pe, c, kp)
    else:
        c_all = jnp.concatenate([ckv_prev.astype(c.dtype), c], axis=1)
        kp_all = jnp.concatenate([kpe_prev.astype(kp.dtype), kp], axis=1)
        o_lat = mla_block(qa, q_pe, c_all, kp_all)
    mla_out = jnp.einsum('bthc,chv->bthv', o_lat, w_uv).reshape(b, t, MLA_HEADS * V_DIM)
    out = jnp.concatenate([pool_out, mla_out.astype(pool_out.dtype)], axis=-1) @ w_out
    return out, pool_new, c, kp


NSEL = PEER_HEADS * PEER_TOPK
HALF_EXPERTS = N_EXPERTS // 2
SUBLANES = 8
LANES = 128
ROW_TILE = D_MODEL // LANES
VMEM_LIMIT = 56 * 1024 * 1024
PEER_ROUTE_BLOCK = 256
PEER_PASS_BLOCK = 128


def _top16_rows(s, n_rows):
    iota = lax.broadcasted_iota(jnp.int32, s.shape, 0)
    vals, idxs = [], []
    for _ in range(PEER_TOPK):
        m = jnp.max(s, axis=0, keepdims=True)
        idx = jnp.min(jnp.where(s == m, iota, n_rows), axis=0, keepdims=True)
        vals.append(m)
        idxs.append(idx)
        s = jnp.where(iota == idx, -jnp.inf, s)
    return jnp.concatenate(vals, axis=0), jnp.concatenate(idxs, axis=0)


def _route_kernel(x_ref, g_ref, wq_ref, keys_ref, xn_ref, code_ref, gate_ref, code_t, gate_t):
    x = x_ref[...]
    xn = x * lax.rsqrt(jnp.mean(x * x, axis=-1, keepdims=True) + RMS_EPS) * g_ref[...]
    xn_ref[...] = xn
    xb = xn.astype(jnp.bfloat16)

    def head(h, carry):
        qb = jnp.dot(xb, wq_ref[h], preferred_element_type=jnp.float32).astype(jnp.bfloat16)
        sv, si = [], []
        for p in range(2):
            s = lax.dot_general(keys_ref[h * 2 + p], qb[:, p * D_HALF:(p + 1) * D_HALF],
                                (((1,), (1,)), ((), ())), preferred_element_type=jnp.float32)
            v, i = _top16_rows(s, N_KEYS)
            sv.append(v)
            si.append(i)
        cand = jnp.concatenate([sv[0][a:a + 1] + sv[1] for a in range(PEER_TOPK)], axis=0)
        eid = jnp.concatenate([si[0][a:a + 1] * N_KEYS + si[1] for a in range(PEER_TOPK)], axis=0)
        iota = lax.broadcasted_iota(jnp.int32, cand.shape, 0)
        cv, ce = [], []
        for _ in range(PEER_TOPK):
            m = jnp.max(cand, axis=0, keepdims=True)
            idx = jnp.min(jnp.where(cand == m, iota, PEER_TOPK * PEER_TOPK), axis=0, keepdims=True)
            hit = iota == idx
            cv.append(m)
            ce.append(jnp.max(jnp.where(hit, eid, -1), axis=0, keepdims=True))
            cand = jnp.where(hit, -jnp.inf, cand)
        cv = jnp.concatenate(cv, axis=0)
        ce = jnp.concatenate(ce, axis=0)
        e = jnp.exp(cv - cv[0:1])
        rows = pl.ds(pl.multiple_of(h * PEER_TOPK, PEER_TOPK), PEER_TOPK)
        gate_t[rows, :] = e / jnp.sum(e, axis=0, keepdims=True)
        code_t[rows, :] = ((ce & (HALF_EXPERTS - 1)) << 3) | (ce >> 13)
        return carry

    lax.fori_loop(0, PEER_HEADS, head, 0)
    code_ref[...] = code_t[...].T
    gate_ref[...] = gate_t[...].T


def peer_route(x2d, g, wq_heads, keys_bf16, tb):
    n = x2d.shape[0]
    return pl.pallas_call(
        _route_kernel,
        out_shape=(jax.ShapeDtypeStruct((n, D_MODEL), jnp.float32),
                   jax.ShapeDtypeStruct((n, NSEL), jnp.int32),
                   jax.ShapeDtypeStruct((n, NSEL), jnp.float32)),
        grid=(n // tb,),
        in_specs=[pl.BlockSpec((tb, D_MODEL), lambda i: (i, 0)),
                  pl.BlockSpec((1, D_MODEL), lambda i: (0, 0)),
                  pl.BlockSpec((PEER_HEADS, D_MODEL, D_KEY), lambda i: (0, 0, 0)),
                  pl.BlockSpec((PEER_HEADS * 2, N_KEYS, D_HALF), lambda i: (0, 0, 0))],
        out_specs=(pl.BlockSpec((tb, D_MODEL), lambda i: (i, 0)),
                   pl.BlockSpec((tb, NSEL), lambda i: (i, 0)),
                   pl.BlockSpec((tb, NSEL), lambda i: (i, 0))),
        scratch_shapes=[pltpu.VMEM((NSEL, tb), jnp.int32), pltpu.VMEM((NSEL, tb), jnp.float32)],
        compiler_params=pltpu.CompilerParams(dimension_semantics=("arbitrary",), vmem_limit_bytes=VMEM_LIMIT),
        name="peer_route",
    )(x2d, g.reshape(1, D_MODEL), wq_heads, keys_bf16)


def pack_table(tab):
    b = lax.bitcast_convert_type(tab.astype(jnp.bfloat16), jnp.uint16).astype(jnp.uint32)
    w = b[:HALF_EXPERTS] | (b[HALF_EXPERTS:] << 16)
    return lax.bitcast_convert_type(w, jnp.int32).reshape(HALF_EXPERTS * ROW_TILE, LANES)


def _gather_row(tab_ref, code):
    start = pl.multiple_of(code & jnp.int32(-8), SUBLANES)
    word = tab_ref[pl.ds(start, SUBLANES), :]
    half = jnp.full((SUBLANES, LANES), code, jnp.int32) & 1
    bits = jnp.where(half == 1, word & jnp.int32(-65536), word << 16)
    return lax.bitcast_convert_type(bits, jnp.float32)


def _split_bf16(p):
    hi = p.astype(jnp.bfloat16)
    lo = (p - hi.astype(jnp.float32)).astype(jnp.bfloat16)
    return hi, lo


def _upass_kernel(code_ref, gate_ref, xn_ref, tab_ref, w_ref, prod_ref, *, tb):
    ones = jnp.ones((LANES, LANES), jnp.bfloat16)
    rows = lax.broadcasted_iota(jnp.int32, (NSEL * ROW_TILE, LANES), 0)
    lanes = lax.broadcasted_iota(jnp.int32, (NSEL * ROW_TILE, LANES), 1)
    pick = ((rows >> 3) == lanes).astype(jnp.float32)

    def tok(t, carry):
        xt = xn_ref[pl.ds(pl.multiple_of(t * ROW_TILE, ROW_TILE), ROW_TILE), :]
        for j in range(NSEL):
            prod_ref[j * ROW_TILE:(j + 1) * ROW_TILE, :] = _gather_row(tab_ref, code_ref[t, j]) * xt
        hi, lo = _split_bf16(prod_ref[...])
        lane_sum = (jnp.dot(hi, ones, preferred_element_type=jnp.float32)
                    + jnp.dot(lo, ones, preferred_element_type=jnp.float32))
        act = jnp.sum(lane_sum * pick, axis=0, keepdims=True)
        gelu = 0.5 * act * (1.0 + lax.erf(act * (1.0 / math.sqrt(2.0))))
        w_ref[pl.ds(t, 1), :] = gate_ref[pl.ds(t, 1), :] * gelu
        return carry

    lax.fori_loop(0, tb, tok, 0)


def peer_upass(code, gate, xn_rows, tab_packed, tb):
    n = code.shape[0]
    return pl.pallas_call(
        functools.partial(_upass_kernel, tb=tb),
        out_shape=jax.ShapeDtypeStruct((n, NSEL), jnp.float32),
        grid=(n // tb,),
        in_specs=[pl.BlockSpec((tb, NSEL), lambda i: (i, 0), memory_space=pltpu.SMEM),
                  pl.BlockSpec((tb, NSEL), lambda i: (i, 0)),
                  pl.BlockSpec((tb * ROW_TILE, LANES), lambda i: (i, 0)),
                  pl.BlockSpec((HALF_EXPERTS * ROW_TILE, LANES), lambda i: (0, 0), pipeline_mode=pl.Buffered(1))],
        out_specs=pl.BlockSpec((tb, NSEL), lambda i: (i, 0)),
        scratch_shapes=[pltpu.VMEM((NSEL * ROW_TILE, LANES), jnp.float32)],
        compiler_params=pltpu.CompilerParams(dimension_semantics=("arbitrary",), vmem_limit_bytes=VMEM_LIMIT),
        name="peer_upass",
    )(code, gate, xn_rows, tab_packed)


def _vpass_kernel(code_ref, w_ref, x_ref, tab_ref, o_ref, *, tb):
    def tok(t, carry):
        accs = [jnp.zeros((SUBLANES, LANES), jnp.float32) for _ in range(4)]
        for j in range(NSEL):
            accs[j % 4] = accs[j % 4] + w_ref[t, j] * _gather_row(tab_ref, code_ref[t, j])
        sl = pl.ds(pl.multiple_of(t * ROW_TILE, ROW_TILE), ROW_TILE)
        o_ref[sl, :] = x_ref[sl, :] + ((accs[0] + accs[1]) + (accs[2] + accs[3]))
        return carry

    lax.fori_loop(0, tb, tok, 0)


def peer_vpass(code, w, x_rows, tab_packed, tb):
    n = code.shape[0]
    return pl.pallas_call(
        functools.partial(_vpass_kernel, tb=tb),
        out_shape=jax.ShapeDtypeStruct((n * ROW_TILE, LANES), jnp.float32),
        grid=(n // tb,),
        in_specs=[pl.BlockSpec((tb, NSEL), lambda i: (i, 0), memory_space=pltpu.SMEM),
                  pl.BlockSpec((tb, NSEL), lambda i: (i, 0), memory_space=pltpu.SMEM),
                  pl.BlockSpec((tb * ROW_TILE, LANES), lambda i: (i, 0)),
                  pl.BlockSpec((HALF_EXPERTS * ROW_TILE, LANES), lambda i: (0, 0), pipeline_mode=pl.Buffered(1))],
        out_specs=pl.BlockSpec((tb * ROW_TILE, LANES), lambda i: (i, 0)),
        compiler_params=pltpu.CompilerParams(dimension_semantics=("arbitrary",), vmem_limit_bytes=VMEM_LIMIT),
        name="peer_vpass",
    )(code, w, x_rows, tab_packed)


def peer_block(x, g, wq_heads, keys_bf16, u_packed, v_packed):
    shp = x.shape
    x2d = x.reshape(-1, D_MODEL)
    n = x2d.shape[0]
    xn, code, gate = peer_route(x2d, g, wq_heads, keys_bf16, min(PEER_ROUTE_BLOCK, n))
    w = peer_upass(code, gate, xn.reshape(n * ROW_TILE, LANES), u_packed, min(PEER_PASS_BLOCK, n))
    out = peer_vpass(code, w, x2d.reshape(n * ROW_TILE, LANES), v_packed, min(PEER_PASS_BLOCK, n))
    return out.reshape(shp)


def _rms_kernel(x_ref, g_ref, o_ref):
    x = x_ref[...]
    y = x * lax.rsqrt(jnp.mean(x * x, axis=-1, keepdims=True) + RMS_EPS)
    o_ref[...] = y * g_ref[...]


def rmsnorm_pallas(x, g):
    shp = x.shape
    xt = x.reshape(-1, shp[-1])
    n = xt.shape[0]
    tm = 512
    out = pl.pallas_call(
        _rms_kernel,
        out_shape=jax.ShapeDtypeStruct(xt.shape, xt.dtype),
        grid=(n // tm,),
        in_specs=[pl.BlockSpec((tm, shp[-1]), lambda i: (i, 0)),
                  pl.BlockSpec((1, shp[-1]), lambda i: (0, 0))],
        out_specs=pl.BlockSpec((tm, shp[-1]), lambda i: (i, 0)),
        name="final_rmsnorm",
    )(xt, g.reshape(1, -1))
    return out.reshape(shp)


def kernel(x_prompt, x_sample, cache_swa_k, cache_swa_v, state_ssm_re, state_ssm_im, state_pool,
           cache_mla_ckv, cache_mla_kpe, norm_mix, norm_ffn, norm_final, w_in_even, w_out_even,
           swa_sink, s5_lam_re, s5_lam_im, s5_log_dt, s5_b_re, s5_b_im, s5_c_re, s5_c_im, s5_d,
           s5_w_glu, s5_b_glu, w_in_odd, w_out_odd, pool_w, pool_scale, mla_q_norm, mla_kv_norm,
           mla_w_uq, mla_w_uk, mla_w_uv, peer_w_q, peer_keys, peer_u, peer_v):
    pos_p = jnp.arange(x_prompt.shape[1])
    pos_s = PAST_LEN + jnp.arange(x_sample.shape[1])
    xp, xs = x_prompt, x_sample
    kp_l, vp_l, rp_l, ip_l, poolp_l, cp_l, ep_l = [], [], [], [], [], [], []
    ks_l, vs_l, rs_l, is_l, pools_l, cs_l, es_l = [], [], [], [], [], [], []
    for layer in range(DEPTH):
        i = layer // 2
        if layer % 2 == 0:
            ew = (w_in_even[i], w_out_even[i], swa_sink[i], s5_lam_re[i], s5_lam_im[i], s5_log_dt[i],
                  s5_b_re[i], s5_b_im[i], s5_c_re[i], s5_c_im[i], s5_d[i], s5_w_glu[i], s5_b_glu[i])
            mp, k1, v1, r1, i1 = even_mixer(rmsnorm(xp, norm_mix[layer]), pos_p, None, None, None, None, *ew)
            ms, k2, v2, r2, i2 = even_mixer(rmsnorm(xs, norm_mix[layer]), pos_s, cache_swa_k[i], cache_swa_v[i],
                                            state_ssm_re[i], state_ssm_im[i], *ew)
            kp_l.append(k1); vp_l.append(v1); rp_l.append(r1); ip_l.append(i1)
            ks_l.append(k2); vs_l.append(v2); rs_l.append(r2); is_l.append(i2)
        else:
            ow = (w_in_odd[i], w_out_odd[i], pool_w[i], pool_scale[i], mla_q_norm[i], mla_kv_norm[i],
                  mla_w_uq[i], mla_w_uk[i], mla_w_uv[i])
            mp, p1, c1, e1 = odd_mixer(rmsnorm(xp, norm_mix[layer]), pos_p, None, None, None, *ow)
            ms, p2, c2, e2 = odd_mixer(rmsnorm(xs, norm_mix[layer]), pos_s, state_pool[i], cache_mla_ckv[i],
                                       cache_mla_kpe[i], *ow)
            poolp_l.append(p1); cp_l.append(c1); ep_l.append(e1)
            pools_l.append(p2); cs_l.append(c2); es_l.append(e2)
        xp = xp + mp
        xs = xs + ms
        wq_heads = peer_w_q[layer].reshape(D_MODEL, PEER_HEADS, D_KEY).transpose(1, 0, 2).astype(jnp.bfloat16)
        keys_bf16 = peer_keys[layer].reshape(PEER_HEADS * 2, N_KEYS, D_HALF).astype(jnp.bfloat16)
        u_packed, v_packed = pack_table(peer_u[layer]), pack_table(peer_v[layer])
        xp = peer_block(xp, norm_ffn[layer], wq_heads, keys_bf16, u_packed, v_packed)
        xs = peer_block(xs, norm_ffn[layer], wq_heads, keys_bf16, u_packed, v_packed)
    y_prompt = rmsnorm_pallas(xp, norm_final)
    y_sample = rmsnorm_pallas(xs, norm_final)
    return (y_prompt, y_sample,
            jnp.stack(kp_l), jnp.stack(vp_l), jnp.stack(rp_l), jnp.stack(ip_l),
            jnp.stack(poolp_l), jnp.stack(cp_l), jnp.stack(ep_l),
            jnp.stack(ks_l), jnp.stack(vs_l), jnp.stack(rs_l), jnp.stack(is_l),
            jnp.stack(pools_l), jnp.stack(cs_l), jnp.stack(es_l))
```

```python
import functools
import math
import jax
import jax.numpy as jnp
from jax import lax
import numpy as np
from jax.experimental import pallas as pl
from jax.experimental.pallas import tpu as pltpu

D_MODEL = 1024
BATCH = 8
SEQ = 4096
DEPTH = 2
DEC_BATCH = 32
DEC_SEQ = 32
PAST_LEN = 2048

CHUNK = 64
N_EVEN = (DEPTH + 1) // 2
N_ODD = DEPTH // 2
RMS_EPS = 1e-6
ROPE_THETA = 500000.0
NEG_INF = -1e30

SWA_HEADS = 8
SWA_KV_HEADS = 2
SWA_GROUP = SWA_HEADS // SWA_KV_HEADS
HEAD_DIM = 64
ROT_DIM = HEAD_DIM // 4
WINDOW = 128
WINDOW_CHUNKS = WINDOW // CHUNK
BAND = (WINDOW_CHUNKS + 1) * CHUNK
SWA_Q = SWA_HEADS * HEAD_DIM
SWA_KV = SWA_KV_HEADS * HEAD_DIM
SWA_SCALE = HEAD_DIM ** -0.5

S5_WIDTH = 512
S5_GROUP = 16
S5_GROUPS = S5_WIDTH // S5_GROUP
S5_STATE = 64

POOL_WIDTH = 512
POOL_WINDOWS = (2, 4, 8, 16)
POOL_GROUP = POOL_WIDTH // len(POOL_WINDOWS)
POOL_MAX = 16
POOL_BUF = POOL_MAX - 1

MLA_HEADS = 8
Q_LORA = 512
KV_LORA = 256
NOPE_DIM = 64
ROPE_DIM = 32
V_DIM = 64
Q_BLOCK = 128
MLA_SCALE = (NOPE_DIM + ROPE_DIM) ** -0.5

EVEN_IN = SWA_Q + 2 * SWA_KV + S5_WIDTH
EVEN_MIX = SWA_Q + S5_WIDTH
ODD_IN = Q_LORA + KV_LORA + ROPE_DIM + POOL_WIDTH
ODD_MIX = POOL_WIDTH + MLA_HEADS * V_DIM

PEER_HEADS = 8
N_KEYS = 128
N_EXPERTS = N_KEYS * N_KEYS
D_KEY = 128
D_HALF = D_KEY // 2
PEER_TOPK = 16

SUBLANES = 8
LANES = 128
VMEM_LIMIT = 56 * 1024 * 1024


def rmsnorm(x, g):
    xf = x.astype(jnp.float32)
    y = xf * lax.rsqrt(jnp.mean(xf * xf, axis=-1, keepdims=True) + RMS_EPS)
    return (y * g.astype(jnp.float32)).astype(x.dtype)


def rope(x, pos):
    r = x.shape[-1]
    inv = ROPE_THETA ** (-jnp.arange(0, r, 2, dtype=jnp.float32) / r)
    ang = pos.astype(jnp.float32)[:, None] * inv[None, :]
    cos = jnp.cos(ang)[:, None, :]
    sin = jnp.sin(ang)[:, None, :]
    xf = x.astype(jnp.float32)
    x1, x2 = xf[..., : r // 2], xf[..., r // 2:]
    return jnp.concatenate([x1 * cos - x2 * sin, x2 * cos + x1 * sin], axis=-1).astype(x.dtype)


def partial_rope(x, pos):
    return jnp.concatenate([rope(x[..., :ROT_DIM], pos), x[..., ROT_DIM:]], axis=-1)


def sink_softmax(s, sink):
    sk = sink.astype(jnp.float32).reshape(SWA_KV_HEADS, SWA_GROUP)[:, :, None, None]
    m = jnp.maximum(jnp.max(s, axis=-1, keepdims=True), sk)
    p = jnp.exp(s - m)
    return p / (jnp.sum(p, axis=-1, keepdims=True) + jnp.exp(sk - m))


def swa_attend_prompt(q, k, v, sink):
    b, s = q.shape[:2]
    nc = s // CHUNK
    qc = q.reshape(b, nc, CHUNK, SWA_KV_HEADS, SWA_GROUP, HEAD_DIM)
    pad = WINDOW_CHUNKS * CHUNK

    def band(t):
        tp = jnp.pad(t, ((0, 0), (pad, 0), (0, 0), (0, 0)))
        tp = tp.reshape(b, nc + WINDOW_CHUNKS, CHUNK, SWA_KV_HEADS, HEAD_DIM)
        return jnp.concatenate([tp[:, i:i + nc] for i in range(WINDOW_CHUNKS + 1)], axis=2)

    kb, vb = band(k), band(v)
    sc = jnp.einsum('bcqhgd,bckhd->bchgqk', qc, kb, preferred_element_type=jnp.float32) * SWA_SCALE
    key_pos = (jnp.arange(nc)[:, None] - WINDOW_CHUNKS) * CHUNK + jnp.arange(BAND)[None, :]
    sc = jnp.where((key_pos >= 0)[None, :, None, None, None, :], sc, NEG_INF)
    p = sink_softmax(sc, sink).astype(v.dtype)
    o = jnp.einsum('bchgqk,bckhd->bcqhgd', p, vb)
    return o.reshape(b, s, SWA_Q)


def swa_attend_sample(q, k_all, v_all, sink):
    b, t = q.shape[:2]
    qg = q.reshape(b, t, SWA_KV_HEADS, SWA_GROUP, HEAD_DIM)
    sc = jnp.einsum('bqhgd,bkhd->bhgqk', qg, k_all, preferred_element_type=jnp.float32) * SWA_SCALE
    p = sink_softmax(sc, sink).astype(v_all.dtype)
    o = jnp.einsum('bhgqk,bkhd->bqhgd', p, v_all)
    return o.reshape(b, t, SWA_Q)


S5_FLAT = S5_GROUPS * S5_STATE
S5_STEPS = 128
S5_CARRY_VREGS = 16


def s5_discretize(lam_re, lam_im, log_dt, b_re, b_im, c_re, c_im):
    lr = jnp.minimum(lam_re, -1e-4)
    li = lam_im
    dt = jnp.exp(log_dt)[:, None]
    mag = jnp.exp(lr * dt)
    ang = li * dt
    ab_re, ab_im = mag * jnp.cos(ang), mag * jnp.sin(ang)
    den = lr * lr + li * li
    nr, ni = ab_re - 1.0, ab_im
    f_re = (nr * lr + ni * li) / den
    f_im = (ni * lr - nr * li) / den
    bb_re = f_re[..., None] * b_re - f_im[..., None] * b_im
    bb_im = f_re[..., None] * b_im + f_im[..., None] * b_re
    eye = jnp.eye(S5_GROUPS, dtype=jnp.float32)

    def embed_b(bb):
        return jnp.einsum('gnc,gh->gchn', bb, eye).reshape(S5_WIDTH, S5_FLAT)

    def embed_c(c):
        return jnp.einsum('gcn,gh->gnhc', c, eye).reshape(S5_FLAT, S5_WIDTH)

    wb = jnp.concatenate([embed_b(bb_re), embed_b(bb_im)], axis=1)
    wc = jnp.concatenate([embed_c(c_re), -embed_c(c_im)], axis=0)
    lam = jnp.stack([ab_re.reshape(S5_FLAT), ab_im.reshape(S5_FLAT)])
    return lam, wb.astype(jnp.bfloat16), wc.astype(jnp.bfloat16)


def _s5_kernel(u_ref, lam_ref, wb_ref, wc_ref, d_ref, wglu_ref, bglu_ref, h0r_ref, h0i_ref,
               o_ref, hr_ref, hi_ref, hbuf, *, nb, steps, width):
    @pl.when(pl.program_id(0) == 0)
    def _():
        hr_ref[...] = h0r_ref[...]
        hi_ref[...] = h0i_ref[...]

    u = u_ref[...]
    hbuf[...] = jnp.dot(u.astype(jnp.bfloat16), wb_ref[...], preferred_element_type=jnp.float32)
    for c0 in range(0, S5_FLAT, width):
        cre = slice(c0, c0 + width)
        cim = slice(S5_FLAT + c0, S5_FLAT + c0 + width)
        lr = jnp.broadcast_to(lam_ref[0:1, cre], (nb, width))
        li = jnp.broadcast_to(lam_ref[1:2, cre], (nb, width))

        def step(t, carry):
            hr, hi = carry
            rows = pl.ds(pl.multiple_of(t * nb, nb), nb)
            nhr = lr * hr - li * hi + hbuf[rows, cre]
            nhi = lr * hi + li * hr + hbuf[rows, cim]
            hbuf[rows, cre] = nhr
            hbuf[rows, cim] = nhi
            return nhr, nhi

        hr, hi = lax.fori_loop(0, steps, step, (hr_ref[:, cre], hi_ref[:, cre]))
        hr_ref[:, cre] = hr
        hi_ref[:, cre] = hi
    y = jnp.dot(hbuf[...].astype(jnp.bfloat16), wc_ref[...], preferred_element_type=jnp.float32) + d_ref[...] * u
    z = 0.5 * y * (1.0 + jnp.tanh(math.sqrt(2.0 / math.pi) * (y + 0.044715 * (y * y * y))))
    gate = jnp.dot(z.astype(jnp.bfloat16), wglu_ref[...], preferred_element_type=jnp.float32) + bglu_ref[...]
    o_ref[...] = z * (1.0 / (1.0 + jnp.exp(-gate)))


def s5_mixer(u_tm, nb, lam, wb, wc, d_skip, w_glu_bf16, b_glu, h0r, h0i, steps):
    rows = u_tm.shape[0]
    t_total = rows // nb
    width = min(S5_FLAT, max(LANES, (S5_CARRY_VREGS * SUBLANES * LANES) // (2 * nb)))
    blk = steps * nb
    const = lambda i: (0, 0)
    return pl.pallas_call(
        functools.partial(_s5_kernel, nb=nb, steps=steps, width=width),
        out_shape=(jax.ShapeDtypeStruct((rows, S5_WIDTH), jnp.float32),
                   jax.ShapeDtypeStruct((nb, S5_FLAT), jnp.float32),
                   jax.ShapeDtypeStruct((nb, S5_FLAT), jnp.float32)),
        grid=(t_total // steps,),
        in_specs=[pl.BlockSpec((blk, S5_WIDTH), lambda i: (i, 0)),
                  pl.BlockSpec((2, S5_FLAT), const),
                  pl.BlockSpec((S5_WIDTH, 2 * S5_FLAT), const),
                  pl.BlockSpec((2 * S5_FLAT, S5_WIDTH), const),
                  pl.BlockSpec((1, S5_WIDTH), const),
                  pl.BlockSpec((S5_WIDTH, S5_WIDTH), const),
                  pl.BlockSpec((1, S5_WIDTH), const),
                  pl.BlockSpec((nb, S5_FLAT), const),
                  pl.BlockSpec((nb, S5_FLAT), const)],
        out_specs=(pl.BlockSpec((blk, S5_WIDTH), lambda i: (i, 0)),
                   pl.BlockSpec((nb, S5_FLAT), const),
                   pl.BlockSpec((nb, S5_FLAT), const)),
        scratch_shapes=[pltpu.VMEM((blk, 2 * S5_FLAT), jnp.float32)],
        compiler_params=pltpu.CompilerParams(dimension_semantics=("arbitrary",), vmem_limit_bytes=VMEM_LIMIT),
        name="s5_mixer",
    )(u_tm, lam, wb, wc, d_skip.reshape(1, S5_WIDTH), w_glu_bf16, b_glu.reshape(1, S5_WIDTH), h0r, h0i)


def pool_mix(u, prev, pos, pool_w, pool_scale):
    t = u.shape[1]
    ext = jnp.concatenate([prev.astype(u.dtype), u], axis=1)
    extf = ext.astype(jnp.float32)
    cs = jnp.pad(jnp.cumsum(extf, axis=1), ((0, 0), (1, 0), (0, 0)))
    uf = u.astype(jnp.float32)
    outs = []
    for gi, w in enumerate(POOL_WINDOWS):
        sl = slice(gi * POOL_GROUP, (gi + 1) * POOL_GROUP)
        tot = cs[:, POOL_MAX:POOL_MAX + t, sl] - cs[:, POOL_MAX - w:POOL_MAX - w + t, sl]
        cnt = jnp.minimum(pos + 1, w).astype(jnp.float32)[None, :, None]
        outs.append(tot / cnt - uf[..., sl])
    m = jnp.stack(outs, axis=2)
    y = jnp.einsum('btgc,gcd->btgd', m, pool_w.astype(jnp.float32)).reshape(u.shape[0], t, POOL_WIDTH)
    y = y * pool_scale.astype(jnp.float32)
    return y.astype(u.dtype), ext[:, -POOL_BUF:]


def mla_block(qa, qp, c, kp, mask=None):
    s = (jnp.einsum('bthc,bsc->bhts', qa, c, preferred_element_type=jnp.float32)
         + jnp.einsum('bthr,bsr->bhts', qp, kp, preferred_element_type=jnp.float32)) * MLA_SCALE
    if mask is not None:
        s = jnp.where(mask[None, None], s, NEG_INF)
    p = jax.nn.softmax(s, axis=-1).astype(c.dtype)
    return jnp.einsum('bhts,bsc->bthc', p, c)


def mla_prompt(qa, qp, c, kp):
    b, s = qa.shape[:2]
    nb = s // Q_BLOCK
    kchunk = jnp.arange(s) // CHUNK
    qa_b = qa.reshape(b, nb, Q_BLOCK, MLA_HEADS, KV_LORA).swapaxes(0, 1)
    qp_b = qp.reshape(b, nb, Q_BLOCK, MLA_HEADS, ROPE_DIM).swapaxes(0, 1)

    def one(args):
        qa_i, qp_i, i = args
        qchunk = (i * Q_BLOCK + jnp.arange(Q_BLOCK)) // CHUNK
        mask = kchunk[None, :] <= qchunk[:, None]
        return mla_block(qa_i, qp_i, c, kp, mask)

    o = lax.map(one, (qa_b, qp_b, jnp.arange(nb)))
    return o.swapaxes(0, 1).reshape(b, s, MLA_HEADS, KV_LORA)


def even_mixer(hn, pos, k_prev, v_prev, hre_prev, him_prev, w_in, w_out, sink, lam_re, lam_im,
               log_dt, b_re, b_im, c_re, c_im, d_skip, w_glu, b_glu):
    b, t = hn.shape[:2]
    proj = hn @ w_in
    q = proj[..., :SWA_Q].reshape(b, t, SWA_HEADS, HEAD_DIM)
    k = proj[..., SWA_Q:SWA_Q + SWA_KV].reshape(b, t, SWA_KV_HEADS, HEAD_DIM)
    v = proj[..., SWA_Q + SWA_KV:SWA_Q + 2 * SWA_KV].reshape(b, t, SWA_KV_HEADS, HEAD_DIM)
    u = proj[..., SWA_Q + 2 * SWA_KV:]
    q = partial_rope(q, pos)
    k = partial_rope(k, pos)
    if k_prev is None:
        att = swa_attend_prompt(q, k, v, sink)
        k_all, v_all = k, v
        hre_prev = jnp.zeros((b, S5_GROUPS, S5_STATE), jnp.float32)
        him_prev = jnp.zeros((b, S5_GROUPS, S5_STATE), jnp.float32)
    else:
        k_all = jnp.concatenate([k_prev.astype(k.dtype), k], axis=1)
        v_all = jnp.concatenate([v_prev.astype(v.dtype), v], axis=1)
        att = swa_attend_sample(q, k_all, v_all, sink)
    lam, wb, wc = s5_discretize(lam_re, lam_im, log_dt, b_re, b_im, c_re, c_im)
    u_tm = u.transpose(1, 0, 2).reshape(t * b, S5_WIDTH)
    s5o_tm, hre, him = s5_mixer(u_tm, b, lam, wb, wc, d_skip, w_glu.astype(jnp.bfloat16), b_glu,
                                hre_prev.reshape(b, S5_FLAT), him_prev.reshape(b, S5_FLAT), min(t, S5_STEPS))
    s5o = s5o_tm.reshape(t, b, S5_WIDTH).transpose(1, 0, 2)
    out = jnp.concatenate([att, s5o.astype(att.dtype)], axis=-1) @ w_out
    return (out, k_all[:, -WINDOW:], v_all[:, -WINDOW:],
            hre.reshape(b, S5_GROUPS, S5_STATE), him.reshape(b, S5_GROUPS, S5_STATE))


def odd_mixer(hn, pos, pool_prev, ckv_prev, kpe_prev, w_in, w_out, pool_w, pool_scale,
              q_norm, kv_norm, w_uq, w_uk, w_uv):
    b, t = hn.shape[:2]
    proj = hn @ w_in
    o0 = Q_LORA
    o1 = o0 + KV_LORA
    o2 = o1 + ROPE_DIM
    cq, ckv, kpe, u = proj[..., :o0], proj[..., o0:o1], proj[..., o1:o2], proj[..., o2:]
    if pool_prev is None:
        pool_prev = jnp.zeros((b, POOL_BUF, POOL_WIDTH), u.dtype)
    pool_out, pool_new = pool_mix(u, pool_prev, pos, pool_w, pool_scale)
    q = (rmsnorm(cq, q_norm) @ w_uq).reshape(b, t, MLA_HEADS, NOPE_DIM + ROPE_DIM)
    q_nope = q[..., :NOPE_DIM]
    q_pe = rope(q[..., NOPE_DIM:], pos)
    c = rmsnorm(ckv, kv_norm)
    kp = rope(kpe[:, :, None, :], pos)[:, :, 0]
    qa = jnp.einsum('bthn,chn->bthc', q_nope, w_uk)
    if ckv_prev is None:
        o_lat = mla_prompt(qa, q_pe, c, kp)
    else:
        c_all = jnp.concatenate([ckv_prev.astype(c.dtype), c], axis=1)
        kp_all = jnp.concatenate([kpe_prev.astype(kp.dtype), kp], axis=1)
        o_lat = mla_block(qa, q_pe, c_all, kp_all)
    mla_out = jnp.einsum('bthc,chv->bthv', o_lat, w_uv).reshape(b, t, MLA_HEADS * V_DIM)
    out = jnp.concatenate([pool_out, mla_out.astype(pool_out.dtype)], axis=-1) @ w_out
    return out, pool_new, c, kp


NSEL = PEER_HEADS * PEER_TOPK
HALF_EXPERTS = N_EXPERTS // 2
ROW_TILE = D_MODEL // LANES
PEER_ROUTE_BLOCK = 256
PEER_PASS_BLOCK = 128


def _top16_rows(s, n_rows):
    iota = lax.broadcasted_iota(jnp.int32, s.shape, 0)
    vals, idxs = [], []
    for _ in range(PEER_TOPK):
        m = jnp.max(s, axis=0, keepdims=True)
        idx = jnp.min(jnp.where(s == m, iota, n_rows), axis=0, keepdims=True)
        vals.append(m)
        idxs.append(idx)
        s = jnp.where(iota == idx, -jnp.inf, s)
    return jnp.concatenate(vals, axis=0), jnp.concatenate(idxs, axis=0)


def _route_kernel(x_ref, g_ref, wq_ref, keys_ref, xn_ref, code_ref, gate_ref, code_t, gate_t):
    x = x_ref[...]
    xn = x * lax.rsqrt(jnp.mean(x * x, axis=-1, keepdims=True) + RMS_EPS) * g_ref[...]
    xn_ref[...] = xn
    xb = xn.astype(jnp.bfloat16)

    def head(h, carry):
        qb = jnp.dot(xb, wq_ref[h], preferred_element_type=jnp.float32).astype(jnp.bfloat16)
        sv, si = [], []
        for p in range(2):
            s = lax.dot_general(keys_ref[h * 2 + p], qb[:, p * D_HALF:(p + 1) * D_HALF],
                                (((1,), (1,)), ((), ())), preferred_element_type=jnp.float32)
            v, i = _top16_rows(s, N_KEYS)
            sv.append(v)
            si.append(i)
        cand = jnp.concatenate([sv[0][a:a + 1] + sv[1] for a in range(PEER_TOPK)], axis=0)
        eid = jnp.concatenate([si[0][a:a + 1] * N_KEYS + si[1] for a in range(PEER_TOPK)], axis=0)
        iota = lax.broadcasted_iota(jnp.int32, cand.shape, 0)
        cv, ce = [], []
        for _ in range(PEER_TOPK):
            m = jnp.max(cand, axis=0, keepdims=True)
            idx = jnp.min(jnp.where(cand == m, iota, PEER_TOPK * PEER_TOPK), axis=0, keepdims=True)
            hit = iota == idx
            cv.append(m)
            ce.append(jnp.max(jnp.where(hit, eid, -1), axis=0, keepdims=True))
            cand = jnp.where(hit, -jnp.inf, cand)
        cv = jnp.concatenate(cv, axis=0)
        ce = jnp.concatenate(ce, axis=0)
        e = jnp.exp(cv - cv[0:1])
        rows = pl.ds(pl.multiple_of(h * PEER_TOPK, PEER_TOPK), PEER_TOPK)
        gate_t[rows, :] = e / jnp.sum(e, axis=0, keepdims=True)
        code_t[rows, :] = ((ce & (HALF_EXPERTS - 1)) << 3) | (ce >> 13)
        return carry

    lax.fori_loop(0, PEER_HEADS, head, 0)
    code_ref[...] = code_t[...].T
    gate_ref[...] = gate_t[...].T


def peer_route(x2d, g, wq_heads, keys_bf16, tb):
    n = x2d.shape[0]
    return pl.pallas_call(
        _route_kernel,
        out_shape=(jax.ShapeDtypeStruct((n, D_MODEL), jnp.float32),
                   jax.ShapeDtypeStruct((n, NSEL), jnp.int32),
                   jax.ShapeDtypeStruct((n, NSEL), jnp.float32)),
        grid=(n // tb,),
        in_specs=[pl.BlockSpec((tb, D_MODEL), lambda i: (i, 0)),
                  pl.BlockSpec((1, D_MODEL), lambda i: (0, 0)),
                  pl.BlockSpec((PEER_HEADS, D_MODEL, D_KEY), lambda i: (0, 0, 0)),
                  pl.BlockSpec((PEER_HEADS * 2, N_KEYS, D_HALF), lambda i: (0, 0, 0))],
        out_specs=(pl.BlockSpec((tb, D_MODEL), lambda i: (i, 0)),
                   pl.BlockSpec((tb, NSEL), lambda i: (i, 0)),
                   pl.BlockSpec((tb, NSEL), lambda i: (i, 0))),
        scratch_shapes=[pltpu.VMEM((NSEL, tb), jnp.int32), pltpu.VMEM((NSEL, tb), jnp.float32)],
        compiler_params=pltpu.CompilerParams(dimension_semantics=("arbitrary",), vmem_limit_bytes=VMEM_LIMIT),
        name="peer_route",
    )(x2d, g.reshape(1, D_MODEL), wq_heads, keys_bf16)


def pack_table(tab):
    b = lax.bitcast_convert_type(tab.astype(jnp.bfloat16), jnp.uint16).astype(jnp.uint32)
    w = b[:HALF_EXPERTS] | (b[HALF_EXPERTS:] << 16)
    return lax.bitcast_convert_type(w, jnp.int32).reshape(HALF_EXPERTS * ROW_TILE, LANES)


def _gather_row(tab_ref, code):
    start = pl.multiple_of(code & jnp.int32(-8), SUBLANES)
    word = tab_ref[pl.ds(start, SUBLANES), :]
    half = jnp.full((SUBLANES, LANES), code, jnp.int32) & 1
    bits = jnp.where(half == 1, word & jnp.int32(-65536), word << 16)
    return lax.bitcast_convert_type(bits, jnp.float32)


def _split_bf16(p):
    hi = p.astype(jnp.bfloat16)
    lo = (p - hi.astype(jnp.float32)).astype(jnp.bfloat16)
    return hi, lo


def _upass_kernel(code_ref, gate_ref, xn_ref, tab_ref, w_ref, prod_ref, *, tb):
    ones = jnp.ones((LANES, LANES), jnp.bfloat16)
    rows = lax.broadcasted_iota(jnp.int32, (NSEL * ROW_TILE, LANES), 0)
    lanes = lax.broadcasted_iota(jnp.int32, (NSEL * ROW_TILE, LANES), 1)
    pick = ((rows >> 3) == lanes).astype(jnp.float32)

    def tok(t, carry):
        xt = xn_ref[pl.ds(pl.multiple_of(t * ROW_TILE, ROW_TILE), ROW_TILE), :]
        for j in range(NSEL):
            prod_ref[j * ROW_TILE:(j + 1) * ROW_TILE, :] = _gather_row(tab_ref, code_ref[t, j]) * xt
        hi, lo = _split_bf16(prod_ref[...])
        lane_sum = (jnp.dot(hi, ones, preferred_element_type=jnp.float32)
                    + jnp.dot(lo, ones, preferred_element_type=jnp.float32))
        act = jnp.sum(lane_sum * pick, axis=0, keepdims=True)
        gelu = 0.5 * act * (1.0 + lax.erf(act * (1.0 / math.sqrt(2.0))))
        w_ref[pl.ds(t, 1), :] = gate_ref[pl.ds(t, 1), :] * gelu
        return carry

    lax.fori_loop(0, tb, tok, 0)


def peer_upass(code, gate, xn_rows, tab_packed, tb):
    n = code.shape[0]
    return pl.pallas_call(
        functools.partial(_upass_kernel, tb=tb),
        out_shape=jax.ShapeDtypeStruct((n, NSEL), jnp.float32),
        grid=(n // tb,),
        in_specs=[pl.BlockSpec((tb, NSEL), lambda i: (i, 0), memory_space=pltpu.SMEM),
                  pl.BlockSpec((tb, NSEL), lambda i: (i, 0)),
                  pl.BlockSpec((tb * ROW_TILE, LANES), lambda i: (i, 0)),
                  pl.BlockSpec((HALF_EXPERTS * ROW_TILE, LANES), lambda i: (0, 0), pipeline_mode=pl.Buffered(1))],
        out_specs=pl.BlockSpec((tb, NSEL), lambda i: (i, 0)),
        scratch_shapes=[pltpu.VMEM((NSEL * ROW_TILE, LANES), jnp.float32)],
        compiler_params=pltpu.CompilerParams(dimension_semantics=("arbitrary",), vmem_limit_bytes=VMEM_LIMIT),
        name="peer_upass",
    )(code, gate, xn_rows, tab_packed)


def _vpass_kernel(code_ref, w_ref, x_ref, tab_ref, o_ref, *, tb):
    def tok(t, carry):
        accs = [jnp.zeros((SUBLANES, LANES), jnp.float32) for _ in range(4)]
        for j in range(NSEL):
            accs[j % 4] = accs[j % 4] + w_ref[t, j] * _gather_row(tab_ref, code_ref[t, j])
        sl = pl.ds(pl.multiple_of(t * ROW_TILE, ROW_TILE), ROW_TILE)
        o_ref[sl, :] = x_ref[sl, :] + ((accs[0] + accs[1]) + (accs[2] + accs[3]))
        return carry

    lax.fori_loop(0, tb, tok, 0)


def peer_vpass(code, w, x_rows, tab_packed, tb):
    n = code.shape[0]
    return pl.pallas_call(
        functools.partial(_vpass_kernel, tb=tb),
        out_shape=jax.ShapeDtypeStruct((n * ROW_TILE, LANES), jnp.float32),
        grid=(n // tb,),
        in_specs=[pl.BlockSpec((tb, NSEL), lambda i: (i, 0), memory_space=pltpu.SMEM),
                  pl.BlockSpec((tb, NSEL), lambda i: (i, 0), memory_space=pltpu.SMEM),
                  pl.BlockSpec((tb * ROW_TILE, LANES), lambda i: (i, 0)),
                  pl.BlockSpec((HALF_EXPERTS * ROW_TILE, LANES), lambda i: (0, 0), pipeline_mode=pl.Buffered(1))],
        out_specs=pl.BlockSpec((tb * ROW_TILE, LANES), lambda i: (i, 0)),
        compiler_params=pltpu.CompilerParams(dimension_semantics=("arbitrary",), vmem_limit_bytes=VMEM_LIMIT),
        name="peer_vpass",
    )(code, w, x_rows, tab_packed)


def peer_block(x, g, wq_heads, keys_bf16, u_packed, v_packed):
    shp = x.shape
    x2d = x.reshape(-1, D_MODEL)
    n = x2d.shape[0]
    xn, code, gate = peer_route(x2d, g, wq_heads, keys_bf16, min(PEER_ROUTE_BLOCK, n))
    w = peer_upass(code, gate, xn.reshape(n * ROW_TILE, LANES), u_packed, min(PEER_PASS_BLOCK, n))
    out = peer_vpass(code, w, x2d.reshape(n * ROW_TILE, LANES), v_packed, min(PEER_PASS_BLOCK, n))
    return out.reshape(shp)


def _rms_kernel(x_ref, g_ref, o_ref):
    x = x_ref[...]
    y = x * lax.rsqrt(jnp.mean(x * x, axis=-1, keepdims=True) + RMS_EPS)
    o_ref[...] = y * g_ref[...]


def rmsnorm_pallas(x, g):
    shp = x.shape
    xt = x.reshape(-1, shp[-1])
    n = xt.shape[0]
    tm = 512
    out = pl.pallas_call(
        _rms_kernel,
        out_shape=jax.ShapeDtypeStruct(xt.shape, xt.dtype),
        grid=(n // tm,),
        in_specs=[pl.BlockSpec((tm, shp[-1]), lambda i: (i, 0)),
                  pl.BlockSpec((1, shp[-1]), lambda i: (0, 0))],
        out_specs=pl.BlockSpec((tm, shp[-1]), lambda i: (i, 0)),
        name="final_rmsnorm",
    )(xt, g.reshape(1, -1))
    return out.reshape(shp)


def kernel(x_prompt, x_sample, cache_swa_k, cache_swa_v, state_ssm_re, state_ssm_im, state_pool,
           cache_mla_ckv, cache_mla_kpe, norm_mix, norm_ffn, norm_final, w_in_even, w_out_even,
           swa_sink, s5_lam_re, s5_lam_im, s5_log_dt, s5_b_re, s5_b_im, s5_c_re, s5_c_im, s5_d,
           s5_w_glu, s5_b_glu, w_in_odd, w_out_odd, pool_w, pool_scale, mla_q_norm, mla_kv_norm,
           mla_w_uq, mla_w_uk, mla_w_uv, peer_w_q, peer_keys, peer_u, peer_v):
    pos_p = jnp.arange(x_prompt.shape[1])
    pos_s = PAST_LEN + jnp.arange(x_sample.shape[1])
    xp, xs = x_prompt, x_sample
    kp_l, vp_l, rp_l, ip_l, poolp_l, cp_l, ep_l = [], [], [], [], [], [], []
    ks_l, vs_l, rs_l, is_l, pools_l, cs_l, es_l = [], [], [], [], [], [], []
    for layer in range(DEPTH):
        i = layer // 2
        if layer % 2 == 0:
            ew = (w_in_even[i], w_out_even[i], swa_sink[i], s5_lam_re[i], s5_lam_im[i], s5_log_dt[i],
                  s5_b_re[i], s5_b_im[i], s5_c_re[i], s5_c_im[i], s5_d[i], s5_w_glu[i], s5_b_glu[i])
            mp, k1, v1, r1, i1 = even_mixer(rmsnorm(xp, norm_mix[layer]), pos_p, None, None, None, None, *ew)
            ms, k2, v2, r2, i2 = even_mixer(rmsnorm(xs, norm_mix[layer]), pos_s, cache_swa_k[i], cache_swa_v[i],
                                            state_ssm_re[i], state_ssm_im[i], *ew)
            kp_l.append(k1); vp_l.append(v1); rp_l.append(r1); ip_l.append(i1)
            ks_l.append(k2); vs_l.append(v2); rs_l.append(r2); is_l.append(i2)
        else:
            ow = (w_in_odd[i], w_out_odd[i], pool_w[i], pool_scale[i], mla_q_norm[i], mla_kv_norm[i],
                  mla_w_uq[i], mla_w_uk[i], mla_w_uv[i])
            mp, p1, c1, e1 = odd_mixer(rmsnorm(xp, norm_mix[layer]), pos_p, None, None, None, *ow)
            ms, p2, c2, e2 = odd_mixer(rmsnorm(xs, norm_mix[layer]), pos_s, state_pool[i], cache_mla_ckv[i],
                                       cache_mla_kpe[i], *ow)
            poolp_l.append(p1); cp_l.append(c1); ep_l.append(e1)
            pools_l.append(p2); cs_l.append(c2); es_l.append(e2)
        xp = xp + mp
        xs = xs + ms
        wq_heads = peer_w_q[layer].reshape(D_MODEL, PEER_HEADS, D_KEY).transpose(1, 0, 2).astype(jnp.bfloat16)
        keys_bf16 = peer_keys[layer].reshape(PEER_HEADS * 2, N_KEYS, D_HALF).astype(jnp.bfloat16)
        u_packed, v_packed = pack_table(peer_u[layer]), pack_table(peer_v[layer])
        xp = peer_block(xp, norm_ffn[layer], wq_heads, keys_bf16, u_packed, v_packed)
        xs = peer_block(xs, norm_ffn[layer], wq_heads, keys_bf16, u_packed, v_packed)
    y_prompt = rmsnorm_pallas(xp, norm_final)
    y_sample = rmsnorm_pallas(xs, norm_final)
    return (y_prompt, y_sample,
            jnp.stack(kp_l), jnp.stack(vp_l), jnp.stack(rp_l), jnp.stack(ip_l),
            jnp.stack(poolp_l), jnp.stack(cp_l), jnp.stack(ep_l),
            jnp.stack(ks_l), jnp.stack(vs_l), jnp.stack(rs_l), jnp.stack(is_l),
            jnp.stack(pools_l), jnp.stack(cs_l), jnp.stack(es_l))
```

```python
import functools
import math
import jax
import jax.numpy as jnp
from jax import lax
from jax.experimental import pallas as pl
from jax.experimental.pallas import tpu as pltpu

D_MODEL = 1024
DEPTH = 2
PAST_LEN = 2048

CHUNK = 64
RMS_EPS = 1e-6
ROPE_THETA = 500000.0
NEG_INF = -1e30

SWA_HEADS = 8
SWA_KV_HEADS = 2
SWA_GROUP = SWA_HEADS // SWA_KV_HEADS
HEAD_DIM = 64
ROT_DIM = HEAD_DIM // 4
WINDOW = 128
SWA_Q = SWA_HEADS * HEAD_DIM
SWA_KV = SWA_KV_HEADS * HEAD_DIM
SWA_SCALE = HEAD_DIM ** -0.5

S5_WIDTH = 512
S5_GROUP = 16
S5_GROUPS = S5_WIDTH // S5_GROUP
S5_STATE = 64
S5_FLAT = S5_GROUPS * S5_STATE

POOL_WIDTH = 512
POOL_WINDOWS = (2, 4, 8, 16)
POOL_GROUP = POOL_WIDTH // len(POOL_WINDOWS)
POOL_MAX = 16
POOL_BUF = POOL_MAX - 1

MLA_HEADS = 8
Q_LORA = 512
KV_LORA = 256
NOPE_DIM = 64
ROPE_DIM = 32
V_DIM = 64
MLA_SCALE = (NOPE_DIM + ROPE_DIM) ** -0.5
MLA_QNOPE = MLA_HEADS * NOPE_DIM
MLA_QPE = MLA_HEADS * ROPE_DIM

EVEN_IN = SWA_Q + 2 * SWA_KV + S5_WIDTH

PEER_HEADS = 8
N_KEYS = 128
N_EXPERTS = N_KEYS * N_KEYS
D_KEY = 128
D_HALF = D_KEY // 2
PEER_TOPK = 16
NSEL = PEER_HEADS * PEER_TOPK
HALF_EXPERTS = N_EXPERTS // 2

SUBLANES = 8
LANES = 128
VMEM_LIMIT = 56 * 1024 * 1024
ROW_TILE = D_MODEL // LANES

ROW_BLOCK = 512
SWA_QBLOCK = 256
MLA_QBLOCK = 256
MLA_KBLOCK = 512
KPE_PAD = LANES
S5_STEPS = 128
S5_CARRY_VREGS = 16
PEER_ROUTE_BLOCK = 256
PEER_PASS_BLOCK = 128
PAIR_COLS = tuple(PEER_TOPK // (a + 1) for a in range(PEER_TOPK))
PAIR_ROWS = -(-sum(PAIR_COLS) // SUBLANES) * SUBLANES
PAIR_PAD = PAIR_ROWS - sum(PAIR_COLS)

ODD_U0 = Q_LORA + KV_LORA
ODD_KPE0 = ODD_U0 + POOL_WIDTH
ODD_IN_PAD = ODD_KPE0 + KPE_PAD


def _cparams(n_axes=1):
    return pltpu.CompilerParams(dimension_semantics=("arbitrary",) * n_axes, vmem_limit_bytes=VMEM_LIMIT)


def _rms(x, g):
    return x * lax.rsqrt(jnp.mean(x * x, axis=-1, keepdims=True) + RMS_EPS) * g


def _rope_lanes(x, cos, sin_lo, sin_hi, half):
    n = x.shape[-1]
    return x * cos + pltpu.roll(x, n - half, 1) * sin_lo + pltpu.roll(x, half, 1) * sin_hi


def rope_tables(pos, rot, period, width, reps):
    inv = ROPE_THETA ** (-jnp.arange(0, rot, 2, dtype=jnp.float32) / rot)
    ang = pos.astype(jnp.float32)[:, None] * inv[None, :]
    cos, sin = jnp.cos(ang), jnp.sin(ang)
    lane = jnp.arange(width) % period
    idx = lane % (rot // 2)
    in_lo = lane < rot // 2
    in_hi = (lane >= rot // 2) & (lane < rot)
    c = jnp.where((in_lo | in_hi)[None, :], cos[:, idx], 1.0)
    s_lo = jnp.where(in_lo[None, :], -sin[:, idx], 0.0)
    s_hi = jnp.where(in_hi[None, :], sin[:, idx], 0.0)
    return tuple(jnp.tile(t, (reps, 1)) for t in (c, s_lo, s_hi))


def _in_even_kernel(x_ref, g_ref, w_ref, cos_ref, slo_ref, shi_ref, q_ref, k_ref, v_ref, u_ref):
    xn = _rms(x_ref[...], g_ref[...])
    proj = jnp.dot(xn.astype(jnp.bfloat16), w_ref[...], preferred_element_type=jnp.float32)
    cos, slo, shi = cos_ref[...], slo_ref[...], shi_ref[...]
    q_ref[...] = _rope_lanes(proj[:, :SWA_Q], cos, slo, shi, ROT_DIM // 2)
    k_ref[...] = _rope_lanes(proj[:, SWA_Q:SWA_Q + SWA_KV], cos[:, :SWA_KV], slo[:, :SWA_KV], shi[:, :SWA_KV],
                             ROT_DIM // 2)
    v_ref[...] = proj[:, SWA_Q + SWA_KV:SWA_Q + 2 * SWA_KV]
    u_ref[...] = proj[:, SWA_Q + 2 * SWA_KV:]


def in_even(x2d, g, w_bf16, tabs):
    n = x2d.shape[0]
    tm = min(ROW_BLOCK, n)
    nt = tabs[0].shape[0] // tm
    row = lambda i: (i, 0)
    const = lambda i: (0, 0)
    tab = lambda i: (i % nt, 0)
    return pl.pallas_call(
        _in_even_kernel,
        out_shape=(jax.ShapeDtypeStruct((n, SWA_Q), jnp.float32), jax.ShapeDtypeStruct((n, SWA_KV), jnp.float32),
                   jax.ShapeDtypeStruct((n, SWA_KV), jnp.float32), jax.ShapeDtypeStruct((n, S5_WIDTH), jnp.float32)),
        grid=(n // tm,),
        in_specs=[pl.BlockSpec((tm, D_MODEL), row), pl.BlockSpec((1, D_MODEL), const),
                  pl.BlockSpec((D_MODEL, EVEN_IN), const),
                  pl.BlockSpec((tm, SWA_Q), tab), pl.BlockSpec((tm, SWA_Q), tab), pl.BlockSpec((tm, SWA_Q), tab)],
        out_specs=(pl.BlockSpec((tm, SWA_Q), row), pl.BlockSpec((tm, SWA_KV), row),
                   pl.BlockSpec((tm, SWA_KV), row), pl.BlockSpec((tm, S5_WIDTH), row)),
        compiler_params=_cparams(), name="in_even",
    )(x2d, g.reshape(1, D_MODEL), w_bf16, *tabs)


def _swa_kernel(sink_ref, q_ref, kp_ref, kc_ref, vp_ref, vc_ref, o_ref, *, banded, qb):
    i = pl.program_id(1)
    q = q_ref[0]
    k = jnp.concatenate([kp_ref[0], kc_ref[0]], axis=0).astype(jnp.bfloat16)
    v = jnp.concatenate([vp_ref[0], vc_ref[0]], axis=0).astype(jnp.bfloat16)
    nk = WINDOW + qb
    wc = WINDOW // CHUNK
    if banded:
        qc = lax.broadcasted_iota(jnp.int32, (qb, nk), 0) // CHUNK + wc
        kc = lax.broadcasted_iota(jnp.int32, (qb, nk), 1) // CHUNK
        visible = (kc <= qc) & (kc >= qc - wc) & ((kc >= wc) | (i > 0))
    outs = []
    for h in range(SWA_HEADS):
        hk = h // SWA_GROUP
        qh = q[:, h * HEAD_DIM:(h + 1) * HEAD_DIM].astype(jnp.bfloat16)
        kh = k[:, hk * HEAD_DIM:(hk + 1) * HEAD_DIM]
        s = lax.dot_general(qh, kh, (((1,), (1,)), ((), ())), preferred_element_type=jnp.float32) * SWA_SCALE
        if banded:
            s = jnp.where(visible, s, NEG_INF)
        sk = sink_ref[h]
        m = jnp.maximum(jnp.max(s, axis=-1, keepdims=True), sk)
        p = jnp.exp(s - m)
        den = jnp.sum(p, axis=-1, keepdims=True) + jnp.exp(sk - m)
        o = jnp.dot(p.astype(jnp.bfloat16), v[:, hk * HEAD_DIM:(hk + 1) * HEAD_DIM], preferred_element_type=jnp.float32)
        outs.append(o / den)
    o_ref[0] = jnp.concatenate(outs, axis=-1)


def swa_attention(q, k_prev, k_cur, v_prev, v_cur, sink, banded):
    b, t, _ = q.shape
    qb = min(SWA_QBLOCK, t)
    per = qb // WINDOW
    prev_map = (lambda bi, i: (bi, jnp.maximum(i * per - 1, 0), 0)) if banded else (lambda bi, i: (bi, 0, 0))
    cur = lambda bi, i: (bi, i, 0)
    return pl.pallas_call(
        functools.partial(_swa_kernel, banded=banded, qb=qb),
        out_shape=jax.ShapeDtypeStruct((b, t, SWA_Q), jnp.float32),
        grid=(b, t // qb),
        in_specs=[pl.BlockSpec(memory_space=pltpu.SMEM),
                  pl.BlockSpec((1, qb, SWA_Q), cur),
                  pl.BlockSpec((1, WINDOW, SWA_KV), prev_map), pl.BlockSpec((1, qb, SWA_KV), cur),
                  pl.BlockSpec((1, WINDOW, SWA_KV), prev_map), pl.BlockSpec((1, qb, SWA_KV), cur)],
        out_specs=pl.BlockSpec((1, qb, SWA_Q), cur),
        compiler_params=_cparams(2), name="swa_attention",
    )(sink, q, k_prev, k_cur, v_prev, v_cur)


def _out_kernel(x_ref, a_ref, b_ref, w_ref, o_ref):
    ka = a_ref.shape[-1]
    o_ref[...] = (x_ref[...]
                  + jnp.dot(a_ref[...].astype(jnp.bfloat16), w_ref[:ka, :], preferred_element_type=jnp.float32)
                  + jnp.dot(b_ref[...].astype(jnp.bfloat16), w_ref[ka:, :], preferred_element_type=jnp.float32))


def out_proj(x2d, a, b, w_bf16):
    n = x2d.shape[0]
    tm = min(ROW_BLOCK, n)
    row = lambda i: (i, 0)
    return pl.pallas_call(
        _out_kernel,
        out_shape=jax.ShapeDtypeStruct((n, D_MODEL), jnp.float32),
        grid=(n // tm,),
        in_specs=[pl.BlockSpec((tm, D_MODEL), row), pl.BlockSpec((tm, a.shape[1]), row),
                  pl.BlockSpec((tm, b.shape[1]), row), pl.BlockSpec(w_bf16.shape, lambda i: (0, 0))],
        out_specs=pl.BlockSpec((tm, D_MODEL), row),
        compiler_params=_cparams(), name="out_proj",
    )(x2d, a, b, w_bf16)


def s5_discretize(lam_re, lam_im, log_dt, b_re, b_im, c_re, c_im):
    lr = jnp.minimum(lam_re, -1e-4)
    li = lam_im
    dt = jnp.exp(log_dt)[:, None]
    mag = jnp.exp(lr * dt)
    ang = li * dt
    ab_re, ab_im = mag * jnp.cos(ang), mag * jnp.sin(ang)
    den = lr * lr + li * li
    nr, ni = ab_re - 1.0, ab_im
    f_re = (nr * lr + ni * li) / den
    f_im = (ni * lr - nr * li) / den
    bb_re = f_re[..., None] * b_re - f_im[..., None] * b_im
    bb_im = f_re[..., None] * b_im + f_im[..., None] * b_re
    eye = jnp.eye(S5_GROUPS, dtype=jnp.float32)

    def embed_b(bb):
        return jnp.einsum('gnc,gh->gchn', bb, eye).reshape(S5_WIDTH, S5_FLAT)

    def embed_c(c):
        return jnp.einsum('gcn,gh->gnhc', c, eye).reshape(S5_FLAT, S5_WIDTH)

    wb = jnp.concatenate([embed_b(bb_re), embed_b(bb_im)], axis=1)
    wc = jnp.concatenate([embed_c(c_re), -embed_c(c_im)], axis=0)
    lam = jnp.stack([ab_re.reshape(S5_FLAT), ab_im.reshape(S5_FLAT)])
    return lam, wb.astype(jnp.bfloat16), wc.astype(jnp.bfloat16)


def _s5_kernel(u_ref, lam_ref, wb_ref, wc_ref, d_ref, wglu_ref, bglu_ref, h0r_ref, h0i_ref,
               o_ref, hr_ref, hi_ref, hbuf, *, nb, steps, width):
    @pl.when(pl.program_id(0) == 0)
    def _():
        hr_ref[...] = h0r_ref[...]
        hi_ref[...] = h0i_ref[...]

    u = u_ref[...]
    hbuf[...] = jnp.dot(u.astype(jnp.bfloat16), wb_ref[...], preferred_element_type=jnp.float32)
    for c0 in range(0, S5_FLAT, width):
        cre = slice(c0, c0 + width)
        cim = slice(S5_FLAT + c0, S5_FLAT + c0 + width)
        lr = jnp.broadcast_to(lam_ref[0:1, cre], (nb, width))
        li = jnp.broadcast_to(lam_ref[1:2, cre], (nb, width))

        def step(t, carry):
            hr, hi = carry
            rows = pl.ds(pl.multiple_of(t * nb, nb), nb)
            nhr = lr * hr - li * hi + hbuf[rows, cre]
            nhi = lr * hi + li * hr + hbuf[rows, cim]
            hbuf[rows, cre] = nhr
            hbuf[rows, cim] = nhi
            return nhr, nhi

        hr, hi = lax.fori_loop(0, steps, step, (hr_ref[:, cre], hi_ref[:, cre]))
        hr_ref[:, cre] = hr
        hi_ref[:, cre] = hi
    y = jnp.dot(hbuf[...].astype(jnp.bfloat16), wc_ref[...], preferred_element_type=jnp.float32) + d_ref[...] * u
    z = 0.5 * y * (1.0 + jnp.tanh(math.sqrt(2.0 / math.pi) * (y + 0.044715 * (y * y * y))))
    gate = jnp.dot(z.astype(jnp.bfloat16), wglu_ref[...], preferred_element_type=jnp.float32) + bglu_ref[...]
    o_ref[...] = z * (1.0 / (1.0 + jnp.exp(-gate)))


def s5_mixer(u_tm, nb, lam, wb, wc, d_skip, w_glu_bf16, b_glu, h0r, h0i, steps):
    rows = u_tm.shape[0]
    t_total = rows // nb
    width = min(S5_FLAT, max(LANES, (S5_CARRY_VREGS * SUBLANES * LANES) // (2 * nb)))
    blk = steps * nb
    const = lambda i: (0, 0)
    return pl.pallas_call(
        functools.partial(_s5_kernel, nb=nb, steps=steps, width=width),
        out_shape=(jax.ShapeDtypeStruct((rows, S5_WIDTH), jnp.float32),
                   jax.ShapeDtypeStruct((nb, S5_FLAT), jnp.float32),
                   jax.ShapeDtypeStruct((nb, S5_FLAT), jnp.float32)),
        grid=(t_total // steps,),
        in_specs=[pl.BlockSpec((blk, S5_WIDTH), lambda i: (i, 0)),
                  pl.BlockSpec((2, S5_FLAT), const),
                  pl.BlockSpec((S5_WIDTH, 2 * S5_FLAT), const),
                  pl.BlockSpec((2 * S5_FLAT, S5_WIDTH), const),
                  pl.BlockSpec((1, S5_WIDTH), const),
                  pl.BlockSpec((S5_WIDTH, S5_WIDTH), const),
                  pl.BlockSpec((1, S5_WIDTH), const),
                  pl.BlockSpec((nb, S5_FLAT), const),
                  pl.BlockSpec((nb, S5_FLAT), const)],
        out_specs=(pl.BlockSpec((blk, S5_WIDTH), lambda i: (i, 0)),
                   pl.BlockSpec((nb, S5_FLAT), const),
                   pl.BlockSpec((nb, S5_FLAT), const)),
        scratch_shapes=[pltpu.VMEM((blk, 2 * S5_FLAT), jnp.float32)],
        compiler_params=_cparams(), name="s5_mixer",
    )(u_tm, lam, wb, wc, d_skip.reshape(1, S5_WIDTH), w_glu_bf16, b_glu.reshape(1, S5_WIDTH), h0r, h0i)


def _in_odd_kernel(x_ref, g_ref, w_ref, qn_ref, kvn_ref, wuq_ref, kcos_ref, kslo_ref, kshi_ref,
                   qcos_ref, qslo_ref, qshi_ref, qnope_ref, qpe_ref, c_ref, kp_ref, u_ref):
    xn = _rms(x_ref[...], g_ref[...])
    proj = jnp.dot(xn.astype(jnp.bfloat16), w_ref[...], preferred_element_type=jnp.float32)
    cqn = _rms(proj[:, :Q_LORA], qn_ref[...])
    q = jnp.dot(cqn.astype(jnp.bfloat16), wuq_ref[...], preferred_element_type=jnp.float32)
    qnope_ref[...] = q[:, :MLA_QNOPE]
    qpe_ref[...] = _rope_lanes(q[:, MLA_QNOPE:], qcos_ref[...], qslo_ref[...], qshi_ref[...], ROPE_DIM // 2)
    c_ref[...] = _rms(proj[:, Q_LORA:ODD_U0], kvn_ref[...])
    kp = _rope_lanes(proj[:, ODD_KPE0:], kcos_ref[...], kslo_ref[...], kshi_ref[...], ROPE_DIM // 2)
    kp_ref[...] = kp[:, :ROPE_DIM]
    u_ref[...] = proj[:, ODD_U0:ODD_KPE0]


def in_odd(x2d, g, w_perm_bf16, q_norm, kv_norm, wuq_perm_bf16, ktabs, qtabs):
    n = x2d.shape[0]
    tm = min(ROW_BLOCK, n)
    nt = ktabs[0].shape[0] // tm
    row = lambda i: (i, 0)
    const = lambda i: (0, 0)
    tab = lambda i: (i % nt, 0)
    return pl.pallas_call(
        _in_odd_kernel,
        out_shape=(jax.ShapeDtypeStruct((n, MLA_QNOPE), jnp.float32), jax.ShapeDtypeStruct((n, MLA_QPE), jnp.float32),
                   jax.ShapeDtypeStruct((n, KV_LORA), jnp.float32), jax.ShapeDtypeStruct((n, ROPE_DIM), jnp.float32),
                   jax.ShapeDtypeStruct((n, POOL_WIDTH), jnp.float32)),
        grid=(n // tm,),
        in_specs=[pl.BlockSpec((tm, D_MODEL), row), pl.BlockSpec((1, D_MODEL), const),
                  pl.BlockSpec((D_MODEL, ODD_IN_PAD), const),
                  pl.BlockSpec((1, Q_LORA), const), pl.BlockSpec((1, KV_LORA), const),
                  pl.BlockSpec((Q_LORA, MLA_QNOPE + MLA_QPE), const),
                  pl.BlockSpec((tm, KPE_PAD), tab), pl.BlockSpec((tm, KPE_PAD), tab), pl.BlockSpec((tm, KPE_PAD), tab),
                  pl.BlockSpec((tm, MLA_QPE), tab), pl.BlockSpec((tm, MLA_QPE), tab), pl.BlockSpec((tm, MLA_QPE), tab)],
        out_specs=(pl.BlockSpec((tm, MLA_QNOPE), row), pl.BlockSpec((tm, MLA_QPE), row), pl.BlockSpec((tm, KV_LORA), row),
                   pl.BlockSpec((tm, ROPE_DIM), row), pl.BlockSpec((tm, POOL_WIDTH), row)),
        compiler_params=_cparams(), name="in_odd",
    )(x2d, g.reshape(1, D_MODEL), w_perm_bf16, q_norm.reshape(1, Q_LORA), kv_norm.reshape(1, KV_LORA), wuq_perm_bf16,
      *ktabs, *qtabs)


def permute_odd_weights(w_in, w_uq):
    o1, o2 = Q_LORA + KV_LORA, Q_LORA + KV_LORA + ROPE_DIM
    kpe = jnp.pad(w_in[:, o1:o2], ((0, 0), (0, KPE_PAD - ROPE_DIM)))
    w_perm = jnp.concatenate([w_in[:, :o1], w_in[:, o2:], kpe], axis=1)
    wq = w_uq.reshape(Q_LORA, MLA_HEADS, NOPE_DIM + ROPE_DIM)
    wq_perm = jnp.concatenate([wq[:, :, :NOPE_DIM].reshape(Q_LORA, MLA_QNOPE),
                               wq[:, :, NOPE_DIM:].reshape(Q_LORA, MLA_QPE)], axis=1)
    return w_perm.astype(jnp.bfloat16), wq_perm.astype(jnp.bfloat16)


def _mla_kernel(qn_ref, qp_ref, c_ref, kp_ref, wuk_ref, wuv_ref, o_ref, *scratch, causal, qb, kb, n_keys):
    i = pl.program_id(1)
    qa_s, m_s, l_s, acc_s = (scratch[k * MLA_HEADS:(k + 1) * MLA_HEADS] for k in range(4))
    qn = qn_ref[0].astype(jnp.bfloat16)
    qp = qp_ref[0].astype(jnp.bfloat16)
    for h in range(MLA_HEADS):
        qa_s[h][...] = jnp.dot(qn[:, h * NOPE_DIM:(h + 1) * NOPE_DIM], wuk_ref[h],
                               preferred_element_type=jnp.float32).astype(jnp.bfloat16)
        m_s[h][...] = jnp.full(m_s[h].shape, NEG_INF, jnp.float32)
        l_s[h][...] = jnp.zeros(l_s[h].shape, jnp.float32)
        acc_s[h][...] = jnp.zeros(acc_s[h].shape, jnp.float32)
    if causal:
        qpos = i * qb + lax.broadcasted_iota(jnp.int32, (qb, kb), 0)
        limit = (qpos // CHUNK + 1) * CHUNK
        nblk = ((i + 1) * qb + kb - 1) // kb
    else:
        limit = n_keys
        nblk = (n_keys + kb - 1) // kb
    kidx0 = lax.broadcasted_iota(jnp.int32, (qb, kb), 1)

    def body(j, carry):
        rows = pl.ds(pl.multiple_of(j * kb, kb), kb)
        cb = c_ref[0, rows, :]
        kpb = kp_ref[0, rows, :]
        visible = kidx0 + j * kb < limit
        for h in range(MLA_HEADS):
            s = (lax.dot_general(qa_s[h][...], cb, (((1,), (1,)), ((), ())), preferred_element_type=jnp.float32)
                 + lax.dot_general(qp[:, h * ROPE_DIM:(h + 1) * ROPE_DIM], kpb, (((1,), (1,)), ((), ())),
                                   preferred_element_type=jnp.float32)) * MLA_SCALE
            s = jnp.where(visible, s, NEG_INF)
            m_old = m_s[h][...]
            m_new = jnp.maximum(m_old, jnp.max(s, axis=-1, keepdims=True))
            alpha = jnp.exp(m_old - m_new)
            p = jnp.exp(s - m_new)
            m_s[h][...] = m_new
            l_s[h][...] = alpha * l_s[h][...] + jnp.sum(p, axis=-1, keepdims=True)
            acc_s[h][...] = alpha * acc_s[h][...] + jnp.dot(p.astype(jnp.bfloat16), cb, preferred_element_type=jnp.float32)
        return carry

    lax.fori_loop(0, nblk, body, 0)
    outs = []
    for h in range(MLA_HEADS):
        o_lat = (acc_s[h][...] / l_s[h][...]).astype(jnp.bfloat16)
        outs.append(jnp.dot(o_lat, wuv_ref[h], preferred_element_type=jnp.float32))
    o_ref[0] = jnp.concatenate(outs, axis=-1)


def mla_attention(q_nope, q_pe, c_keys_bf16, kp_keys_bf16, wuk_h, wuv_h, causal, n_keys):
    b, t, _ = q_nope.shape
    tk = c_keys_bf16.shape[1]
    qb = min(MLA_QBLOCK, t)
    kb = min(MLA_KBLOCK, tk)
    cur = lambda bi, i: (bi, i, 0)
    whole = lambda bi, i: (bi, 0, 0)
    const3 = lambda bi, i: (0, 0, 0)
    return pl.pallas_call(
        functools.partial(_mla_kernel, causal=causal, qb=qb, kb=kb, n_keys=n_keys),
        out_shape=jax.ShapeDtypeStruct((b, t, MLA_HEADS * V_DIM), jnp.float32),
        grid=(b, t // qb),
        in_specs=[pl.BlockSpec((1, qb, MLA_QNOPE), cur), pl.BlockSpec((1, qb, MLA_QPE), cur),
                  pl.BlockSpec((1, tk, KV_LORA), whole), pl.BlockSpec((1, tk, ROPE_DIM), whole),
                  pl.BlockSpec((MLA_HEADS, NOPE_DIM, KV_LORA), const3), pl.BlockSpec((MLA_HEADS, KV_LORA, V_DIM), const3)],
        out_specs=pl.BlockSpec((1, qb, MLA_HEADS * V_DIM), cur),
        scratch_shapes=([pltpu.VMEM((qb, KV_LORA), jnp.bfloat16)] * MLA_HEADS
                        + [pltpu.VMEM((qb, 1), jnp.float32)] * (2 * MLA_HEADS)
                        + [pltpu.VMEM((qb, KV_LORA), jnp.float32)] * MLA_HEADS),
        compiler_params=_cparams(2), name="mla_attention",
    )(q_nope, q_pe, c_keys_bf16, kp_keys_bf16, wuk_h, wuv_h)


def _pool_kernel(u_ref, prev_ref, w_ref, scale_ref, o_ref, new_ref, ext, *, tm, pos0):
    j = pl.program_id(1)

    @pl.when(j == 0)
    def _():
        ext[0:1, :] = jnp.zeros((1, POOL_WIDTH), jnp.float32)
        ext[1:POOL_MAX, :] = prev_ref[0]

    @pl.when(j > 0)
    def _():
        ext[0:POOL_MAX, :] = ext[tm:tm + POOL_MAX, :]

    u = u_ref[0]
    ext[POOL_MAX:POOL_MAX + tm, :] = u
    pos = pos0 + j * tm + lax.broadcasted_iota(jnp.int32, (tm, POOL_GROUP), 0)
    outs = []
    for gi, w in enumerate(POOL_WINDOWS):
        cols = slice(gi * POOL_GROUP, (gi + 1) * POOL_GROUP)
        tot = u[:, cols]
        for d in range(1, w):
            tot = tot + ext[POOL_MAX - d:POOL_MAX - d + tm, cols]
        cnt = jnp.minimum(pos + 1, w).astype(jnp.float32)
        m = tot / cnt - u[:, cols]
        outs.append(jnp.dot(m.astype(jnp.bfloat16), w_ref[gi], preferred_element_type=jnp.float32))
    o_ref[0] = jnp.concatenate(outs, axis=-1) * scale_ref[...]
    new_ref[0] = ext[tm + 1:tm + POOL_MAX, :]


def pool_mixer(u, prev, pool_w_bf16, pool_scale, pos0):
    b, t, _ = u.shape
    tm = min(ROW_BLOCK, t)
    cur = lambda bi, j: (bi, j, 0)
    per_b = lambda bi, j: (bi, 0, 0)
    return pl.pallas_call(
        functools.partial(_pool_kernel, tm=tm, pos0=pos0),
        out_shape=(jax.ShapeDtypeStruct((b, t, POOL_WIDTH), jnp.float32),
                   jax.ShapeDtypeStruct((b, POOL_BUF, POOL_WIDTH), jnp.float32)),
        grid=(b, t // tm),
        in_specs=[pl.BlockSpec((1, tm, POOL_WIDTH), cur), pl.BlockSpec((1, POOL_BUF, POOL_WIDTH), per_b),
                  pl.BlockSpec((len(POOL_WINDOWS), POOL_GROUP, POOL_GROUP), lambda bi, j: (0, 0, 0)),
                  pl.BlockSpec((1, POOL_WIDTH), lambda bi, j: (0, 0))],
        out_specs=(pl.BlockSpec((1, tm, POOL_WIDTH), cur), pl.BlockSpec((1, POOL_BUF, POOL_WIDTH), per_b)),
        scratch_shapes=[pltpu.VMEM((tm + POOL_MAX, POOL_WIDTH), jnp.float32)],
        compiler_params=_cparams(2), name="pool_mixer",
    )(u, prev, pool_w_bf16, pool_scale.reshape(1, POOL_WIDTH))


def _top16_rows(s, n_rows):
    iota = lax.broadcasted_iota(jnp.int32, s.shape, 0)
    vals, idxs = [], []
    for _ in range(PEER_TOPK):
        m = jnp.max(s, axis=0, keepdims=True)
        idx = jnp.min(jnp.where(s == m, iota, n_rows), axis=0, keepdims=True)
        vals.append(m)
        idxs.append(idx)
        s = jnp.where(iota == idx, -jnp.inf, s)
    return jnp.concatenate(vals, axis=0), jnp.concatenate(idxs, axis=0)


def _route_kernel(x_ref, g_ref, wq_ref, keys_ref, xn_ref, code_ref, gate_ref, code_t, gate_t):
    xn = _rms(x_ref[...], g_ref[...])
    xn_ref[...] = xn
    xb = xn.astype(jnp.bfloat16)

    def head(h, carry):
        qb = jnp.dot(xb, wq_ref[h], preferred_element_type=jnp.float32).astype(jnp.bfloat16)
        sv, si = [], []
        for p in range(2):
            s = lax.dot_general(keys_ref[h * 2 + p], qb[:, p * D_HALF:(p + 1) * D_HALF],
                                (((1,), (1,)), ((), ())), preferred_element_type=jnp.float32)
            v, i = _top16_rows(s, N_KEYS)
            sv.append(v)
            si.append(i)
        cand = jnp.concatenate([sv[0][a:a + 1] + sv[1][:nb] for a, nb in enumerate(PAIR_COLS)]
                               + [jnp.full((PAIR_PAD, s.shape[1]), -jnp.inf, jnp.float32)], axis=0)
        eid = jnp.concatenate([si[0][a:a + 1] * N_KEYS + si[1][:nb] for a, nb in enumerate(PAIR_COLS)]
                              + [jnp.zeros((PAIR_PAD, s.shape[1]), jnp.int32)], axis=0)
        iota = lax.broadcasted_iota(jnp.int32, cand.shape, 0)
        cv, ce = [], []
        for _ in range(PEER_TOPK):
            m = jnp.max(cand, axis=0, keepdims=True)
            idx = jnp.min(jnp.where(cand == m, iota, PAIR_ROWS), axis=0, keepdims=True)
            hit = iota == idx
            cv.append(m)
            ce.append(jnp.max(jnp.where(hit, eid, -1), axis=0, keepdims=True))
            cand = jnp.where(hit, -jnp.inf, cand)
        cv = jnp.concatenate(cv, axis=0)
        ce = jnp.concatenate(ce, axis=0)
        e = jnp.exp(cv - cv[0:1])
        rows = pl.ds(pl.multiple_of(h * PEER_TOPK, PEER_TOPK), PEER_TOPK)
        gate_t[rows, :] = e / jnp.sum(e, axis=0, keepdims=True)
        code_t[rows, :] = ((ce & (HALF_EXPERTS - 1)) << 3) | (ce >> 13)
        return carry

    lax.fori_loop(0, PEER_HEADS, head, 0)
    code_ref[...] = code_t[...].T
    gate_ref[...] = gate_t[...].T


def peer_route(x2d, g, wq_heads, keys_bf16, tb):
    n = x2d.shape[0]
    return pl.pallas_call(
        _route_kernel,
        out_shape=(jax.ShapeDtypeStruct((n, D_MODEL), jnp.float32),
                   jax.ShapeDtypeStruct((n, NSEL), jnp.int32),
                   jax.ShapeDtypeStruct((n, NSEL), jnp.float32)),
        grid=(n // tb,),
        in_specs=[pl.BlockSpec((tb, D_MODEL), lambda i: (i, 0)),
                  pl.BlockSpec((1, D_MODEL), lambda i: (0, 0)),
                  pl.BlockSpec((PEER_HEADS, D_MODEL, D_KEY), lambda i: (0, 0, 0)),
                  pl.BlockSpec((PEER_HEADS * 2, N_KEYS, D_HALF), lambda i: (0, 0, 0))],
        out_specs=(pl.BlockSpec((tb, D_MODEL), lambda i: (i, 0)),
                   pl.BlockSpec((tb, NSEL), lambda i: (i, 0)),
                   pl.BlockSpec((tb, NSEL), lambda i: (i, 0))),
        scratch_shapes=[pltpu.VMEM((NSEL, tb), jnp.int32), pltpu.VMEM((NSEL, tb), jnp.float32)],
        compiler_params=_cparams(), name="peer_route",
    )(x2d, g.reshape(1, D_MODEL), wq_heads, keys_bf16)


def pack_table(tab):
    b = lax.bitcast_convert_type(tab.astype(jnp.bfloat16), jnp.uint16).astype(jnp.uint32)
    w = b[:HALF_EXPERTS] | (b[HALF_EXPERTS:] << 16)
    return lax.bitcast_convert_type(w, jnp.int32).reshape(HALF_EXPERTS * ROW_TILE, LANES)


def _gather_row(tab_ref, code):
    start = pl.multiple_of(code & jnp.int32(-8), SUBLANES)
    word = tab_ref[pl.ds(start, SUBLANES), :]
    half = jnp.full((SUBLANES, LANES), code, jnp.int32) & 1
    bits = jnp.where(half == 1, word & jnp.int32(-65536), word << 16)
    return lax.bitcast_convert_type(bits, jnp.float32)


def _split_bf16(p):
    hi = p.astype(jnp.bfloat16)
    lo = (p - hi.astype(jnp.float32)).astype(jnp.bfloat16)
    return hi, lo


def _upass_kernel(code_ref, gate_ref, xn_ref, tab_ref, w_ref, prod_ref, *, tb):
    ones = jnp.ones((LANES, LANES), jnp.bfloat16)
    rows = lax.broadcasted_iota(jnp.int32, (NSEL * ROW_TILE, LANES), 0)
    lanes = lax.broadcasted_iota(jnp.int32, (NSEL * ROW_TILE, LANES), 1)
    pick = ((rows >> 3) == lanes).astype(jnp.float32)

    def tok(t, carry):
        xt = xn_ref[pl.ds(pl.multiple_of(t * ROW_TILE, ROW_TILE), ROW_TILE), :]
        for j in range(NSEL):
            prod_ref[j * ROW_TILE:(j + 1) * ROW_TILE, :] = _gather_row(tab_ref, code_ref[t, j]) * xt
        hi, lo = _split_bf16(prod_ref[...])
        lane_sum = (jnp.dot(hi, ones, preferred_element_type=jnp.float32)
                    + jnp.dot(lo, ones, preferred_element_type=jnp.float32))
        act = jnp.sum(lane_sum * pick, axis=0, keepdims=True)
        gelu = 0.5 * act * (1.0 + lax.erf(act * (1.0 / math.sqrt(2.0))))
        w_ref[pl.ds(t, 1), :] = gate_ref[pl.ds(t, 1), :] * gelu
        return carry

    lax.fori_loop(0, tb, tok, 0)


def peer_upass(code, gate, xn_rows, tab_packed, tb):
    n = code.shape[0]
    return pl.pallas_call(
        functools.partial(_upass_kernel, tb=tb),
        out_shape=jax.ShapeDtypeStruct((n, NSEL), jnp.float32),
        grid=(n // tb,),
        in_specs=[pl.BlockSpec((tb, NSEL), lambda i: (i, 0), memory_space=pltpu.SMEM),
                  pl.BlockSpec((tb, NSEL), lambda i: (i, 0)),
                  pl.BlockSpec((tb * ROW_TILE, LANES), lambda i: (i, 0)),
                  pl.BlockSpec((HALF_EXPERTS * ROW_TILE, LANES), lambda i: (0, 0), pipeline_mode=pl.Buffered(1))],
        out_specs=pl.BlockSpec((tb, NSEL), lambda i: (i, 0)),
        scratch_shapes=[pltpu.VMEM((NSEL * ROW_TILE, LANES), jnp.float32)],
        compiler_params=_cparams(), name="peer_upass",
    )(code, gate, xn_rows, tab_packed)


def _vpass_kernel(code_ref, w_ref, x_ref, tab_ref, o_ref, *, tb):
    def tok(t, carry):
        accs = [jnp.zeros((SUBLANES, LANES), jnp.float32) for _ in range(4)]
        for j in range(NSEL):
            accs[j % 4] = accs[j % 4] + w_ref[t, j] * _gather_row(tab_ref, code_ref[t, j])
        sl = pl.ds(pl.multiple_of(t * ROW_TILE, ROW_TILE), ROW_TILE)
        o_ref[sl, :] = x_ref[sl, :] + ((accs[0] + accs[1]) + (accs[2] + accs[3]))
        return carry

    lax.fori_loop(0, tb, tok, 0)


def peer_vpass(code, w, x_rows, tab_packed, tb):
    n = code.shape[0]
    return pl.pallas_call(
        functools.partial(_vpass_kernel, tb=tb),
        out_shape=jax.ShapeDtypeStruct((n * ROW_TILE, LANES), jnp.float32),
        grid=(n // tb,),
        in_specs=[pl.BlockSpec((tb, NSEL), lambda i: (i, 0), memory_space=pltpu.SMEM),
                  pl.BlockSpec((tb, NSEL), lambda i: (i, 0), memory_space=pltpu.SMEM),
                  pl.BlockSpec((tb * ROW_TILE, LANES), lambda i: (i, 0)),
                  pl.BlockSpec((HALF_EXPERTS * ROW_TILE, LANES), lambda i: (0, 0), pipeline_mode=pl.Buffered(1))],
        out_specs=pl.BlockSpec((tb * ROW_TILE, LANES), lambda i: (i, 0)),
        compiler_params=_cparams(), name="peer_vpass",
    )(code, w, x_rows, tab_packed)


def peer_block(x, g, wq_heads, keys_bf16, u_packed, v_packed):
    shp = x.shape
    x2d = x.reshape(-1, D_MODEL)
    n = x2d.shape[0]
    xn, code, gate = peer_route(x2d, g, wq_heads, keys_bf16, min(PEER_ROUTE_BLOCK, n))
    w = peer_upass(code, gate, xn.reshape(n * ROW_TILE, LANES), u_packed, min(PEER_PASS_BLOCK, n))
    out = peer_vpass(code, w, x2d.reshape(n * ROW_TILE, LANES), v_packed, min(PEER_PASS_BLOCK, n))
    return out.reshape(shp)


def _rms_kernel(x_ref, g_ref, o_ref):
    o_ref[...] = _rms(x_ref[...], g_ref[...])


def rmsnorm_pallas(x, g):
    shp = x.shape
    xt = x.reshape(-1, shp[-1])
    n = xt.shape[0]
    tm = min(ROW_BLOCK, n)
    out = pl.pallas_call(
        _rms_kernel,
        out_shape=jax.ShapeDtypeStruct(xt.shape, xt.dtype),
        grid=(n // tm,),
        in_specs=[pl.BlockSpec((tm, shp[-1]), lambda i: (i, 0)),
                  pl.BlockSpec((1, shp[-1]), lambda i: (0, 0))],
        out_specs=pl.BlockSpec((tm, shp[-1]), lambda i: (i, 0)),
        compiler_params=_cparams(), name="final_rmsnorm",
    )(xt, g.reshape(1, -1))
    return out.reshape(shp)


def even_layer(x, pos0, k_prev, v_prev, h0r, h0i, norm_g, w_in, w_out, sink, s5_params, d_skip, w_glu, b_glu):
    b, t, _ = x.shape
    n = b * t
    x2d = x.reshape(n, D_MODEL)
    tabs = rope_tables(pos0 + jnp.arange(t), ROT_DIM, HEAD_DIM, SWA_Q, max(1, min(ROW_BLOCK, n) // t))
    q, k, v, u = in_even(x2d, norm_g, w_in.astype(jnp.bfloat16), tabs)
    q3, k3, v3 = q.reshape(b, t, SWA_Q), k.reshape(b, t, SWA_KV), v.reshape(b, t, SWA_KV)
    if k_prev is None:
        att = swa_attention(q3, k3, k3, v3, v3, sink, True)
        k_all, v_all = k3, v3
        h0r = jnp.zeros((b, S5_FLAT), jnp.float32)
        h0i = jnp.zeros((b, S5_FLAT), jnp.float32)
    else:
        kp, vp = k_prev.reshape(b, WINDOW, SWA_KV), v_prev.reshape(b, WINDOW, SWA_KV)
        att = swa_attention(q3, kp, k3, vp, v3, sink, False)
        k_all, v_all = jnp.concatenate([kp, k3], axis=1), jnp.concatenate([vp, v3], axis=1)
        h0r, h0i = h0r.reshape(b, S5_FLAT), h0i.reshape(b, S5_FLAT)
    lam, wb, wc = s5_discretize(*s5_params)
    u_tm = u.reshape(b, t, S5_WIDTH).transpose(1, 0, 2).reshape(n, S5_WIDTH)
    s5o_tm, hre, him = s5_mixer(u_tm, b, lam, wb, wc, d_skip, w_glu.astype(jnp.bfloat16), b_glu, h0r, h0i,
                                min(t, S5_STEPS))
    s5o = s5o_tm.reshape(t, b, S5_WIDTH).transpose(1, 0, 2).reshape(n, S5_WIDTH)
    out = out_proj(x2d, att.reshape(n, SWA_Q), s5o, w_out.astype(jnp.bfloat16))
    return (out.reshape(b, t, D_MODEL),
            k_all[:, -WINDOW:].reshape(b, WINDOW, SWA_KV_HEADS, HEAD_DIM),
            v_all[:, -WINDOW:].reshape(b, WINDOW, SWA_KV_HEADS, HEAD_DIM),
            hre.reshape(b, S5_GROUPS, S5_STATE), him.reshape(b, S5_GROUPS, S5_STATE))


def odd_layer(x, pos0, pool_prev, ckv_prev, kpe_prev, norm_g, w_in, w_out, pool_w, pool_scale,
              q_norm, kv_norm, w_uq, w_uk, w_uv):
    b, t, _ = x.shape
    n = b * t
    x2d = x.reshape(n, D_MODEL)
    reps = max(1, min(ROW_BLOCK, n) // t)
    pos = pos0 + jnp.arange(t)
    ktabs = rope_tables(pos, ROPE_DIM, KPE_PAD, KPE_PAD, reps)
    qtabs = rope_tables(pos, ROPE_DIM, ROPE_DIM, MLA_QPE, reps)
    w_perm, wuq_perm = permute_odd_weights(w_in, w_uq)
    qnope, qpe, c, kp, u = in_odd(x2d, norm_g, w_perm, q_norm, kv_norm, wuq_perm, ktabs, qtabs)
    c3, kp3 = c.reshape(b, t, KV_LORA), kp.reshape(b, t, ROPE_DIM)
    wuk_h = w_uk.transpose(1, 2, 0).astype(jnp.bfloat16)
    wuv_h = w_uv.transpose(1, 0, 2).astype(jnp.bfloat16)
    if ckv_prev is None:
        ck, kk, causal, n_keys = c3, kp3, True, t
        pool_prev = jnp.zeros((b, POOL_BUF, POOL_WIDTH), jnp.float32)
    else:
        ck, kk, causal = jnp.concatenate([ckv_prev, c3], axis=1), jnp.concatenate([kpe_prev, kp3], axis=1), False
        n_keys = ck.shape[1]
        pad = -n_keys % min(MLA_KBLOCK, n_keys)
        ck, kk = jnp.pad(ck, ((0, 0), (0, pad), (0, 0))), jnp.pad(kk, ((0, 0), (0, pad), (0, 0)))
    mla = mla_attention(qnope.reshape(b, t, MLA_QNOPE), qpe.reshape(b, t, MLA_QPE), ck.astype(jnp.bfloat16),
                        kk.astype(jnp.bfloat16), wuk_h, wuv_h, causal, n_keys)
    pool_out, pool_new = pool_mixer(u.reshape(b, t, POOL_WIDTH), pool_prev, pool_w.astype(jnp.bfloat16), pool_scale, pos0)
    out = out_proj(x2d, pool_out.reshape(n, POOL_WIDTH), mla.reshape(n, MLA_HEADS * V_DIM), w_out.astype(jnp.bfloat16))
    return out.reshape(b, t, D_MODEL), pool_new, c3, kp3


def kernel(x_prompt, x_sample, cache_swa_k, cache_swa_v, state_ssm_re, state_ssm_im, state_pool,
           cache_mla_ckv, cache_mla_kpe, norm_mix, norm_ffn, norm_final, w_in_even, w_out_even,
           swa_sink, s5_lam_re, s5_lam_im, s5_log_dt, s5_b_re, s5_b_im, s5_c_re, s5_c_im, s5_d,
           s5_w_glu, s5_b_glu, w_in_odd, w_out_odd, pool_w, pool_scale, mla_q_norm, mla_kv_norm,
           mla_w_uq, mla_w_uk, mla_w_uv, peer_w_q, peer_keys, peer_u, peer_v):
    xp, xs = x_prompt, x_sample
    kp_l, vp_l, rp_l, ip_l, poolp_l, cp_l, ep_l = [], [], [], [], [], [], []
    ks_l, vs_l, rs_l, is_l, pools_l, cs_l, es_l = [], [], [], [], [], [], []
    for layer in range(DEPTH):
        i = layer // 2
        if layer % 2 == 0:
            s5_params = (s5_lam_re[i], s5_lam_im[i], s5_log_dt[i], s5_b_re[i], s5_b_im[i], s5_c_re[i], s5_c_im[i])
            ew = (norm_mix[layer], w_in_even[i], w_out_even[i], swa_sink[i], s5_params, s5_d[i], s5_w_glu[i], s5_b_glu[i])
            xp, k1, v1, r1, i1 = even_layer(xp, 0, None, None, None, None, *ew)
            xs, k2, v2, r2, i2 = even_layer(xs, PAST_LEN, cache_swa_k[i], cache_swa_v[i],
                                            state_ssm_re[i], state_ssm_im[i], *ew)
            kp_l.append(k1); vp_l.append(v1); rp_l.append(r1); ip_l.append(i1)
            ks_l.append(k2); vs_l.append(v2); rs_l.append(r2); is_l.append(i2)
        else:
            ow = (norm_mix[layer], w_in_odd[i], w_out_odd[i], pool_w[i], pool_scale[i], mla_q_norm[i], mla_kv_norm[i],
                  mla_w_uq[i], mla_w_uk[i], mla_w_uv[i])
            xp, p1, c1, e1 = odd_layer(xp, 0, None, None, None, *ow)
            xs, p2, c2, e2 = odd_layer(xs, PAST_LEN, state_pool[i], cache_mla_ckv[i], cache_mla_kpe[i], *ow)
            poolp_l.append(p1); cp_l.append(c1); ep_l.append(e1)
            pools_l.append(p2); cs_l.append(c2); es_l.append(e2)
        wq_heads = peer_w_q[layer].reshape(D_MODEL, PEER_HEADS, D_KEY).transpose(1, 0, 2).astype(jnp.bfloat16)
        keys_bf16 = peer_keys[layer].reshape(PEER_HEADS * 2, N_KEYS, D_HALF).astype(jnp.bfloat16)
        u_packed, v_packed = pack_table(peer_u[layer]), pack_table(peer_v[layer])
        xp = peer_block(xp, norm_ffn[layer], wq_heads, keys_bf16, u_packed, v_packed)
        xs = peer_block(xs, norm_ffn[layer], wq_heads, keys_bf16, u_packed, v_packed)
    y_prompt = rmsnorm_pallas(xp, norm_final)
    y_sample = rmsnorm_pallas(xs, norm_final)
    return (y_prompt, y_sample,
            jnp.stack(kp_l), jnp.stack(vp_l), jnp.stack(rp_l), jnp.stack(ip_l),
            jnp.stack(poolp_l), jnp.stack(cp_l), jnp.stack(ep_l),
            jnp.stack(ks_l), jnp.stack(vs_l), jnp.stack(rs_l), jnp.stack(is_l),
            jnp.stack(pools_l), jnp.stack(cs_l), jnp.stack(es_l))
```

```python
import functools
import math
import jax
import jax.numpy as jnp
from jax import lax
from jax.experimental import pallas as pl
from jax.experimental.pallas import tpu as pltpu

D_MODEL = 1024
DEPTH = 2
PAST_LEN = 2048

CHUNK = 64
RMS_EPS = 1e-6
ROPE_THETA = 500000.0
NEG_INF = -1e30

SWA_HEADS = 8
SWA_KV_HEADS = 2
SWA_GROUP = SWA_HEADS // SWA_KV_HEADS
HEAD_DIM = 64
ROT_DIM = HEAD_DIM // 4
WINDOW = 128
SWA_Q = SWA_HEADS * HEAD_DIM
SWA_KV = SWA_KV_HEADS * HEAD_DIM
SWA_SCALE = HEAD_DIM ** -0.5

S5_WIDTH = 512
S5_GROUP = 16
S5_GROUPS = S5_WIDTH // S5_GROUP
S5_STATE = 64
S5_FLAT = S5_GROUPS * S5_STATE

POOL_WIDTH = 512
POOL_WINDOWS = (2, 4, 8, 16)
POOL_GROUP = POOL_WIDTH // len(POOL_WINDOWS)
POOL_MAX = 16
POOL_BUF = POOL_MAX - 1

MLA_HEADS = 8
Q_LORA = 512
KV_LORA = 256
NOPE_DIM = 64
ROPE_DIM = 32
V_DIM = 64
MLA_SCALE = (NOPE_DIM + ROPE_DIM) ** -0.5
MLA_QNOPE = MLA_HEADS * NOPE_DIM
MLA_QPE = MLA_HEADS * ROPE_DIM

EVEN_IN = SWA_Q + 2 * SWA_KV + S5_WIDTH

PEER_HEADS = 8
N_KEYS = 128
N_EXPERTS = N_KEYS * N_KEYS
D_KEY = 128
D_HALF = D_KEY // 2
PEER_TOPK = 16
NSEL = PEER_HEADS * PEER_TOPK
HALF_EXPERTS = N_EXPERTS // 2

SUBLANES = 8
LANES = 128
VMEM_LIMIT = 56 * 1024 * 1024
ROW_TILE = D_MODEL // LANES

ROW_BLOCK = 512
SWA_QBLOCK = 256
MLA_QBLOCK = 256
MLA_KBLOCK = 512
KPE_PAD = LANES
S5_STEPS = 128
S5_CARRY_VREGS = 16
PEER_ROUTE_BLOCK = 256
PEER_PASS_BLOCK = 128
UPASS_GROUP = 8
SLOT_ROWS = NSEL * ROW_TILE
HI16 = -65536
PAIR_COLS = tuple(PEER_TOPK // (a + 1) for a in range(PEER_TOPK))
PAIR_ROWS = -(-sum(PAIR_COLS) // SUBLANES) * SUBLANES
PAIR_PAD = PAIR_ROWS - sum(PAIR_COLS)

ODD_U0 = Q_LORA + KV_LORA
ODD_KPE0 = ODD_U0 + POOL_WIDTH
ODD_IN_PAD = ODD_KPE0 + KPE_PAD


def _cparams(n_axes=1):
    return pltpu.CompilerParams(dimension_semantics=("arbitrary",) * n_axes, vmem_limit_bytes=VMEM_LIMIT)


def _rms(x, g):
    return x * lax.rsqrt(jnp.mean(x * x, axis=-1, keepdims=True) + RMS_EPS) * g


def _rope_lanes(x, cos, sin_lo, sin_hi, half):
    n = x.shape[-1]
    return x * cos + pltpu.roll(x, n - half, 1) * sin_lo + pltpu.roll(x, half, 1) * sin_hi


def rope_tables(pos, rot, period, width, reps):
    inv = ROPE_THETA ** (-jnp.arange(0, rot, 2, dtype=jnp.float32) / rot)
    ang = pos.astype(jnp.float32)[:, None] * inv[None, :]
    cos, sin = jnp.cos(ang), jnp.sin(ang)
    lane = jnp.arange(width) % period
    idx = lane % (rot // 2)
    in_lo = lane < rot // 2
    in_hi = (lane >= rot // 2) & (lane < rot)
    c = jnp.where((in_lo | in_hi)[None, :], cos[:, idx], 1.0)
    s_lo = jnp.where(in_lo[None, :], -sin[:, idx], 0.0)
    s_hi = jnp.where(in_hi[None, :], sin[:, idx], 0.0)
    return tuple(jnp.tile(t, (reps, 1)) for t in (c, s_lo, s_hi))


def _in_even_kernel(x_ref, g_ref, w_ref, cos_ref, slo_ref, shi_ref, q_ref, k_ref, v_ref, u_ref):
    xn = _rms(x_ref[...], g_ref[...])
    proj = jnp.dot(xn.astype(jnp.bfloat16), w_ref[...], preferred_element_type=jnp.float32)
    cos, slo, shi = cos_ref[...], slo_ref[...], shi_ref[...]
    q_ref[...] = _rope_lanes(proj[:, :SWA_Q], cos, slo, shi, ROT_DIM // 2)
    k_ref[...] = _rope_lanes(proj[:, SWA_Q:SWA_Q + SWA_KV], cos[:, :SWA_KV], slo[:, :SWA_KV], shi[:, :SWA_KV],
                             ROT_DIM // 2)
    v_ref[...] = proj[:, SWA_Q + SWA_KV:SWA_Q + 2 * SWA_KV]
    u_ref[...] = proj[:, SWA_Q + 2 * SWA_KV:]


def in_even(x2d, g, w_bf16, tabs):
    n = x2d.shape[0]
    tm = min(ROW_BLOCK, n)
    nt = tabs[0].shape[0] // tm
    row = lambda i: (i, 0)
    const = lambda i: (0, 0)
    tab = lambda i: (i % nt, 0)
    return pl.pallas_call(
        _in_even_kernel,
        out_shape=(jax.ShapeDtypeStruct((n, SWA_Q), jnp.float32), jax.ShapeDtypeStruct((n, SWA_KV), jnp.float32),
                   jax.ShapeDtypeStruct((n, SWA_KV), jnp.float32), jax.ShapeDtypeStruct((n, S5_WIDTH), jnp.float32)),
        grid=(n // tm,),
        in_specs=[pl.BlockSpec((tm, D_MODEL), row), pl.BlockSpec((1, D_MODEL), const),
                  pl.BlockSpec((D_MODEL, EVEN_IN), const),
                  pl.BlockSpec((tm, SWA_Q), tab), pl.BlockSpec((tm, SWA_Q), tab), pl.BlockSpec((tm, SWA_Q), tab)],
        out_specs=(pl.BlockSpec((tm, SWA_Q), row), pl.BlockSpec((tm, SWA_KV), row),
                   pl.BlockSpec((tm, SWA_KV), row), pl.BlockSpec((tm, S5_WIDTH), row)),
        compiler_params=_cparams(), name="in_even",
    )(x2d, g.reshape(1, D_MODEL), w_bf16, *tabs)


def _swa_kernel(sink_ref, q_ref, kp_ref, kc_ref, vp_ref, vc_ref, o_ref, *, banded, qb):
    i = pl.program_id(1)
    q = q_ref[0]
    k = jnp.concatenate([kp_ref[0], kc_ref[0]], axis=0).astype(jnp.bfloat16)
    v = jnp.concatenate([vp_ref[0], vc_ref[0]], axis=0).astype(jnp.bfloat16)
    nk = WINDOW + qb
    wc = WINDOW // CHUNK
    if banded:
        qc = lax.broadcasted_iota(jnp.int32, (qb, nk), 0) // CHUNK + wc
        kc = lax.broadcasted_iota(jnp.int32, (qb, nk), 1) // CHUNK
        visible = (kc <= qc) & (kc >= qc - wc) & ((kc >= wc) | (i > 0))
    outs = []
    for h in range(SWA_HEADS):
        hk = h // SWA_GROUP
        qh = q[:, h * HEAD_DIM:(h + 1) * HEAD_DIM].astype(jnp.bfloat16)
        kh = k[:, hk * HEAD_DIM:(hk + 1) * HEAD_DIM]
        s = lax.dot_general(qh, kh, (((1,), (1,)), ((), ())), preferred_element_type=jnp.float32) * SWA_SCALE
        if banded:
            s = jnp.where(visible, s, NEG_INF)
        sk = sink_ref[h]
        m = jnp.maximum(jnp.max(s, axis=-1, keepdims=True), sk)
        p = jnp.exp(s - m)
        den = jnp.sum(p, axis=-1, keepdims=True) + jnp.exp(sk - m)
        o = jnp.dot(p.astype(jnp.bfloat16), v[:, hk * HEAD_DIM:(hk + 1) * HEAD_DIM], preferred_element_type=jnp.float32)
        outs.append(o / den)
    o_ref[0] = jnp.concatenate(outs, axis=-1)


def swa_attention(q, k_prev, k_cur, v_prev, v_cur, sink, banded):
    b, t, _ = q.shape
    qb = min(SWA_QBLOCK, t)
    per = qb // WINDOW
    prev_map = (lambda bi, i: (bi, jnp.maximum(i * per - 1, 0), 0)) if banded else (lambda bi, i: (bi, 0, 0))
    cur = lambda bi, i: (bi, i, 0)
    return pl.pallas_call(
        functools.partial(_swa_kernel, banded=banded, qb=qb),
        out_shape=jax.ShapeDtypeStruct((b, t, SWA_Q), jnp.float32),
        grid=(b, t // qb),
        in_specs=[pl.BlockSpec(memory_space=pltpu.SMEM),
                  pl.BlockSpec((1, qb, SWA_Q), cur),
                  pl.BlockSpec((1, WINDOW, SWA_KV), prev_map), pl.BlockSpec((1, qb, SWA_KV), cur),
                  pl.BlockSpec((1, WINDOW, SWA_KV), prev_map), pl.BlockSpec((1, qb, SWA_KV), cur)],
        out_specs=pl.BlockSpec((1, qb, SWA_Q), cur),
        compiler_params=_cparams(2), name="swa_attention",
    )(sink, q, k_prev, k_cur, v_prev, v_cur)


def _out_kernel(x_ref, a_ref, b_ref, w_ref, o_ref):
    ka = a_ref.shape[-1]
    o_ref[...] = (x_ref[...]
                  + jnp.dot(a_ref[...].astype(jnp.bfloat16), w_ref[:ka, :], preferred_element_type=jnp.float32)
                  + jnp.dot(b_ref[...].astype(jnp.bfloat16), w_ref[ka:, :], preferred_element_type=jnp.float32))


def out_proj(x2d, a, b, w_bf16):
    n = x2d.shape[0]
    tm = min(ROW_BLOCK, n)
    row = lambda i: (i, 0)
    return pl.pallas_call(
        _out_kernel,
        out_shape=jax.ShapeDtypeStruct((n, D_MODEL), jnp.float32),
        grid=(n // tm,),
        in_specs=[pl.BlockSpec((tm, D_MODEL), row), pl.BlockSpec((tm, a.shape[1]), row),
                  pl.BlockSpec((tm, b.shape[1]), row), pl.BlockSpec(w_bf16.shape, lambda i: (0, 0))],
        out_specs=pl.BlockSpec((tm, D_MODEL), row),
        compiler_params=_cparams(), name="out_proj",
    )(x2d, a, b, w_bf16)


def s5_discretize(lam_re, lam_im, log_dt, b_re, b_im, c_re, c_im):
    lr = jnp.minimum(lam_re, -1e-4)
    li = lam_im
    dt = jnp.exp(log_dt)[:, None]
    mag = jnp.exp(lr * dt)
    ang = li * dt
    ab_re, ab_im = mag * jnp.cos(ang), mag * jnp.sin(ang)
    den = lr * lr + li * li
    nr, ni = ab_re - 1.0, ab_im
    f_re = (nr * lr + ni * li) / den
    f_im = (ni * lr - nr * li) / den
    bb_re = f_re[..., None] * b_re - f_im[..., None] * b_im
    bb_im = f_re[..., None] * b_im + f_im[..., None] * b_re
    eye = jnp.eye(S5_GROUPS, dtype=jnp.float32)

    def embed_b(bb):
        return jnp.einsum('gnc,gh->gchn', bb, eye).reshape(S5_WIDTH, S5_FLAT)

    def embed_c(c):
        return jnp.einsum('gcn,gh->gnhc', c, eye).reshape(S5_FLAT, S5_WIDTH)

    wb = jnp.concatenate([embed_b(bb_re), embed_b(bb_im)], axis=1)
    wc = jnp.concatenate([embed_c(c_re), -embed_c(c_im)], axis=0)
    lam = jnp.stack([ab_re.reshape(S5_FLAT), ab_im.reshape(S5_FLAT)])
    return lam, wb.astype(jnp.bfloat16), wc.astype(jnp.bfloat16)


def _s5_kernel(u_ref, lam_ref, wb_ref, wc_ref, d_ref, wglu_ref, bglu_ref, h0r_ref, h0i_ref,
               o_ref, hr_ref, hi_ref, hbuf, *, nb, steps, width):
    @pl.when(pl.program_id(0) == 0)
    def _():
        hr_ref[...] = h0r_ref[...]
        hi_ref[...] = h0i_ref[...]

    u = u_ref[...]
    hbuf[...] = jnp.dot(u.astype(jnp.bfloat16), wb_ref[...], preferred_element_type=jnp.float32)
    for c0 in range(0, S5_FLAT, width):
        cre = slice(c0, c0 + width)
        cim = slice(S5_FLAT + c0, S5_FLAT + c0 + width)
        lr = jnp.broadcast_to(lam_ref[0:1, cre], (nb, width))
        li = jnp.broadcast_to(lam_ref[1:2, cre], (nb, width))

        def step(t, carry):
            hr, hi = carry
            rows = pl.ds(pl.multiple_of(t * nb, nb), nb)
            nhr = lr * hr - li * hi + hbuf[rows, cre]
            nhi = lr * hi + li * hr + hbuf[rows, cim]
            hbuf[rows, cre] = nhr
            hbuf[rows, cim] = nhi
            return nhr, nhi

        hr, hi = lax.fori_loop(0, steps, step, (hr_ref[:, cre], hi_ref[:, cre]))
        hr_ref[:, cre] = hr
        hi_ref[:, cre] = hi
    y = jnp.dot(hbuf[...].astype(jnp.bfloat16), wc_ref[...], preferred_element_type=jnp.float32) + d_ref[...] * u
    z = 0.5 * y * (1.0 + jnp.tanh(math.sqrt(2.0 / math.pi) * (y + 0.044715 * (y * y * y))))
    gate = jnp.dot(z.astype(jnp.bfloat16), wglu_ref[...], preferred_element_type=jnp.float32) + bglu_ref[...]
    o_ref[...] = z * (1.0 / (1.0 + jnp.exp(-gate)))


def s5_mixer(u_tm, nb, lam, wb, wc, d_skip, w_glu_bf16, b_glu, h0r, h0i, steps):
    rows = u_tm.shape[0]
    t_total = rows // nb
    width = min(S5_FLAT, max(LANES, (S5_CARRY_VREGS * SUBLANES * LANES) // (2 * nb)))
    blk = steps * nb
    const = lambda i: (0, 0)
    return pl.pallas_call(
        functools.partial(_s5_kernel, nb=nb, steps=steps, width=width),
        out_shape=(jax.ShapeDtypeStruct((rows, S5_WIDTH), jnp.float32),
                   jax.ShapeDtypeStruct((nb, S5_FLAT), jnp.float32),
                   jax.ShapeDtypeStruct((nb, S5_FLAT), jnp.float32)),
        grid=(t_total // steps,),
        in_specs=[pl.BlockSpec((blk, S5_WIDTH), lambda i: (i, 0)),
                  pl.BlockSpec((2, S5_FLAT), const),
                  pl.BlockSpec((S5_WIDTH, 2 * S5_FLAT), const),
                  pl.BlockSpec((2 * S5_FLAT, S5_WIDTH), const),
                  pl.BlockSpec((1, S5_WIDTH), const),
                  pl.BlockSpec((S5_WIDTH, S5_WIDTH), const),
                  pl.BlockSpec((1, S5_WIDTH), const),
                  pl.BlockSpec((nb, S5_FLAT), const),
                  pl.BlockSpec((nb, S5_FLAT), const)],
        out_specs=(pl.BlockSpec((blk, S5_WIDTH), lambda i: (i, 0)),
                   pl.BlockSpec((nb, S5_FLAT), const),
                   pl.BlockSpec((nb, S5_FLAT), const)),
        scratch_shapes=[pltpu.VMEM((blk, 2 * S5_FLAT), jnp.float32)],
        compiler_params=_cparams(), name="s5_mixer",
    )(u_tm, lam, wb, wc, d_skip.reshape(1, S5_WIDTH), w_glu_bf16, b_glu.reshape(1, S5_WIDTH), h0r, h0i)


def _in_odd_kernel(x_ref, g_ref, w_ref, qn_ref, kvn_ref, wuq_ref, kcos_ref, kslo_ref, kshi_ref,
                   qcos_ref, qslo_ref, qshi_ref, qnope_ref, qpe_ref, c_ref, kp_ref, u_ref):
    xn = _rms(x_ref[...], g_ref[...])
    proj = jnp.dot(xn.astype(jnp.bfloat16), w_ref[...], preferred_element_type=jnp.float32)
    cqn = _rms(proj[:, :Q_LORA], qn_ref[...])
    q = jnp.dot(cqn.astype(jnp.bfloat16), wuq_ref[...], preferred_element_type=jnp.float32)
    qnope_ref[...] = q[:, :MLA_QNOPE]
    qpe_ref[...] = _rope_lanes(q[:, MLA_QNOPE:], qcos_ref[...], qslo_ref[...], qshi_ref[...], ROPE_DIM // 2)
    c_ref[...] = _rms(proj[:, Q_LORA:ODD_U0], kvn_ref[...])
    kp = _rope_lanes(proj[:, ODD_KPE0:], kcos_ref[...], kslo_ref[...], kshi_ref[...], ROPE_DIM // 2)
    kp_ref[...] = kp[:, :ROPE_DIM]
    u_ref[...] = proj[:, ODD_U0:ODD_KPE0]


def in_odd(x2d, g, w_perm_bf16, q_norm, kv_norm, wuq_perm_bf16, ktabs, qtabs):
    n = x2d.shape[0]
    tm = min(ROW_BLOCK, n)
    nt = ktabs[0].shape[0] // tm
    row = lambda i: (i, 0)
    const = lambda i: (0, 0)
    tab = lambda i: (i % nt, 0)
    return pl.pallas_call(
        _in_odd_kernel,
        out_shape=(jax.ShapeDtypeStruct((n, MLA_QNOPE), jnp.float32), jax.ShapeDtypeStruct((n, MLA_QPE), jnp.float32),
                   jax.ShapeDtypeStruct((n, KV_LORA), jnp.float32), jax.ShapeDtypeStruct((n, ROPE_DIM), jnp.float32),
                   jax.ShapeDtypeStruct((n, POOL_WIDTH), jnp.float32)),
        grid=(n // tm,),
        in_specs=[pl.BlockSpec((tm, D_MODEL), row), pl.BlockSpec((1, D_MODEL), const),
                  pl.BlockSpec((D_MODEL, ODD_IN_PAD), const),
                  pl.BlockSpec((1, Q_LORA), const), pl.BlockSpec((1, KV_LORA), const),
                  pl.BlockSpec((Q_LORA, MLA_QNOPE + MLA_QPE), const),
                  pl.BlockSpec((tm, KPE_PAD), tab), pl.BlockSpec((tm, KPE_PAD), tab), pl.BlockSpec((tm, KPE_PAD), tab),
                  pl.BlockSpec((tm, MLA_QPE), tab), pl.BlockSpec((tm, MLA_QPE), tab), pl.BlockSpec((tm, MLA_QPE), tab)],
        out_specs=(pl.BlockSpec((tm, MLA_QNOPE), row), pl.BlockSpec((tm, MLA_QPE), row), pl.BlockSpec((tm, KV_LORA), row),
                   pl.BlockSpec((tm, ROPE_DIM), row), pl.BlockSpec((tm, POOL_WIDTH), row)),
        compiler_params=_cparams(), name="in_odd",
    )(x2d, g.reshape(1, D_MODEL), w_perm_bf16, q_norm.reshape(1, Q_LORA), kv_norm.reshape(1, KV_LORA), wuq_perm_bf16,
      *ktabs, *qtabs)


def permute_odd_weights(w_in, w_uq):
    o1, o2 = Q_LORA + KV_LORA, Q_LORA + KV_LORA + ROPE_DIM
    kpe = jnp.pad(w_in[:, o1:o2], ((0, 0), (0, KPE_PAD - ROPE_DIM)))
    w_perm = jnp.concatenate([w_in[:, :o1], w_in[:, o2:], kpe], axis=1)
    wq = w_uq.reshape(Q_LORA, MLA_HEADS, NOPE_DIM + ROPE_DIM)
    wq_perm = jnp.concatenate([wq[:, :, :NOPE_DIM].reshape(Q_LORA, MLA_QNOPE),
                               wq[:, :, NOPE_DIM:].reshape(Q_LORA, MLA_QPE)], axis=1)
    return w_perm.astype(jnp.bfloat16), wq_perm.astype(jnp.bfloat16)


def _mla_kernel(qn_ref, qp_ref, c_ref, kp_ref, wuk_ref, wuv_ref, o_ref, *scratch, causal, qb, kb, n_keys):
    i = pl.program_id(1)
    qa_s, m_s, l_s, acc_s = (scratch[k * MLA_HEADS:(k + 1) * MLA_HEADS] for k in range(4))
    qn = qn_ref[0].astype(jnp.bfloat16)
    qp = qp_ref[0].astype(jnp.bfloat16)
    for h in range(MLA_HEADS):
        qa_s[h][...] = jnp.dot(qn[:, h * NOPE_DIM:(h + 1) * NOPE_DIM], wuk_ref[h],
                               preferred_element_type=jnp.float32).astype(jnp.bfloat16)
        m_s[h][...] = jnp.full(m_s[h].shape, NEG_INF, jnp.float32)
        l_s[h][...] = jnp.zeros(l_s[h].shape, jnp.float32)
        acc_s[h][...] = jnp.zeros(acc_s[h].shape, jnp.float32)
    if causal:
        qpos = i * qb + lax.broadcasted_iota(jnp.int32, (qb, kb), 0)
        limit = (qpos // CHUNK + 1) * CHUNK
        nblk = ((i + 1) * qb + kb - 1) // kb
    else:
        limit = n_keys
        nblk = (n_keys + kb - 1) // kb
    kidx0 = lax.broadcasted_iota(jnp.int32, (qb, kb), 1)

    def body(j, carry):
        rows = pl.ds(pl.multiple_of(j * kb, kb), kb)
        cb = c_ref[0, rows, :]
        kpb = kp_ref[0, rows, :]
        visible = kidx0 + j * kb < limit
        for h in range(MLA_HEADS):
            s = (lax.dot_general(qa_s[h][...], cb, (((1,), (1,)), ((), ())), preferred_element_type=jnp.float32)
                 + lax.dot_general(qp[:, h * ROPE_DIM:(h + 1) * ROPE_DIM], kpb, (((1,), (1,)), ((), ())),
                                   preferred_element_type=jnp.float32)) * MLA_SCALE
            s = jnp.where(visible, s, NEG_INF)
            m_old = m_s[h][...]
            m_new = jnp.maximum(m_old, jnp.max(s, axis=-1, keepdims=True))
            alpha = jnp.exp(m_old - m_new)
            p = jnp.exp(s - m_new)
            m_s[h][...] = m_new
            l_s[h][...] = alpha * l_s[h][...] + jnp.sum(p, axis=-1, keepdims=True)
            acc_s[h][...] = alpha * acc_s[h][...] + jnp.dot(p.astype(jnp.bfloat16), cb, preferred_element_type=jnp.float32)
        return carry

    lax.fori_loop(0, nblk, body, 0)
    outs = []
    for h in range(MLA_HEADS):
        o_lat = (acc_s[h][...] / l_s[h][...]).astype(jnp.bfloat16)
        outs.append(jnp.dot(o_lat, wuv_ref[h], preferred_element_type=jnp.float32))
    o_ref[0] = jnp.concatenate(outs, axis=-1)


def mla_attention(q_nope, q_pe, c_keys_bf16, kp_keys_bf16, wuk_h, wuv_h, causal, n_keys):
    b, t, _ = q_nope.shape
    tk = c_keys_bf16.shape[1]
    qb = min(MLA_QBLOCK, t)
    kb = min(MLA_KBLOCK, tk)
    cur = lambda bi, i: (bi, i, 0)
    whole = lambda bi, i: (bi, 0, 0)
    const3 = lambda bi, i: (0, 0, 0)
    return pl.pallas_call(
        functools.partial(_mla_kernel, causal=causal, qb=qb, kb=kb, n_keys=n_keys),
        out_shape=jax.ShapeDtypeStruct((b, t, MLA_HEADS * V_DIM), jnp.float32),
        grid=(b, t // qb),
        in_specs=[pl.BlockSpec((1, qb, MLA_QNOPE), cur), pl.BlockSpec((1, qb, MLA_QPE), cur),
                  pl.BlockSpec((1, tk, KV_LORA), whole), pl.BlockSpec((1, tk, ROPE_DIM), whole),
                  pl.BlockSpec((MLA_HEADS, NOPE_DIM, KV_LORA), const3), pl.BlockSpec((MLA_HEADS, KV_LORA, V_DIM), const3)],
        out_specs=pl.BlockSpec((1, qb, MLA_HEADS * V_DIM), cur),
        scratch_shapes=([pltpu.VMEM((qb, KV_LORA), jnp.bfloat16)] * MLA_HEADS
                        + [pltpu.VMEM((qb, 1), jnp.float32)] * (2 * MLA_HEADS)
                        + [pltpu.VMEM((qb, KV_LORA), jnp.float32)] * MLA_HEADS),
        compiler_params=_cparams(2), name="mla_attention",
    )(q_nope, q_pe, c_keys_bf16, kp_keys_bf16, wuk_h, wuv_h)


def _pool_kernel(u_ref, prev_ref, w_ref, scale_ref, o_ref, new_ref, ext, *, tm, pos0):
    j = pl.program_id(1)

    @pl.when(j == 0)
    def _():
        ext[0:1, :] = jnp.zeros((1, POOL_WIDTH), jnp.float32)
        ext[1:POOL_MAX, :] = prev_ref[0]

    @pl.when(j > 0)
    def _():
        ext[0:POOL_MAX, :] = ext[tm:tm + POOL_MAX, :]

    u = u_ref[0]
    ext[POOL_MAX:POOL_MAX + tm, :] = u
    pos = pos0 + j * tm + lax.broadcasted_iota(jnp.int32, (tm, POOL_GROUP), 0)
    outs = []
    for gi, w in enumerate(POOL_WINDOWS):
        cols = slice(gi * POOL_GROUP, (gi + 1) * POOL_GROUP)
        tot = u[:, cols]
        for d in range(1, w):
            tot = tot + ext[POOL_MAX - d:POOL_MAX - d + tm, cols]
        cnt = jnp.minimum(pos + 1, w).astype(jnp.float32)
        m = tot / cnt - u[:, cols]
        outs.append(jnp.dot(m.astype(jnp.bfloat16), w_ref[gi], preferred_element_type=jnp.float32))
    o_ref[0] = jnp.concatenate(outs, axis=-1) * scale_ref[...]
    new_ref[0] = ext[tm + 1:tm + POOL_MAX, :]


def pool_mixer(u, prev, pool_w_bf16, pool_scale, pos0):
    b, t, _ = u.shape
    tm = min(ROW_BLOCK, t)
    cur = lambda bi, j: (bi, j, 0)
    per_b = lambda bi, j: (bi, 0, 0)
    return pl.pallas_call(
        functools.partial(_pool_kernel, tm=tm, pos0=pos0),
        out_shape=(jax.ShapeDtypeStruct((b, t, POOL_WIDTH), jnp.float32),
                   jax.ShapeDtypeStruct((b, POOL_BUF, POOL_WIDTH), jnp.float32)),
        grid=(b, t // tm),
        in_specs=[pl.BlockSpec((1, tm, POOL_WIDTH), cur), pl.BlockSpec((1, POOL_BUF, POOL_WIDTH), per_b),
                  pl.BlockSpec((len(POOL_WINDOWS), POOL_GROUP, POOL_GROUP), lambda bi, j: (0, 0, 0)),
                  pl.BlockSpec((1, POOL_WIDTH), lambda bi, j: (0, 0))],
        out_specs=(pl.BlockSpec((1, tm, POOL_WIDTH), cur), pl.BlockSpec((1, POOL_BUF, POOL_WIDTH), per_b)),
        scratch_shapes=[pltpu.VMEM((tm + POOL_MAX, POOL_WIDTH), jnp.float32)],
        compiler_params=_cparams(2), name="pool_mixer",
    )(u, prev, pool_w_bf16, pool_scale.reshape(1, POOL_WIDTH))


def _top16_rows(s, n_rows):
    iota = lax.broadcasted_iota(jnp.int32, s.shape, 0)
    vals, idxs = [], []
    for _ in range(PEER_TOPK):
        m = jnp.max(s, axis=0, keepdims=True)
        idx = jnp.min(jnp.where(s == m, iota, n_rows), axis=0, keepdims=True)
        vals.append(m)
        idxs.append(idx)
        s = jnp.where(iota == idx, -jnp.inf, s)
    return jnp.concatenate(vals, axis=0), jnp.concatenate(idxs, axis=0)


def _route_kernel(x_ref, g_ref, wq_ref, keys_ref, xn_ref, code_ref, gate_ref, code_t, gate_t):
    xn = _rms(x_ref[...], g_ref[...])
    xn_ref[...] = xn
    xb = xn.astype(jnp.bfloat16)

    def head(h, carry):
        qb = jnp.dot(xb, wq_ref[h], preferred_element_type=jnp.float32).astype(jnp.bfloat16)
        sv, si = [], []
        for p in range(2):
            s = lax.dot_general(keys_ref[h * 2 + p], qb[:, p * D_HALF:(p + 1) * D_HALF],
                                (((1,), (1,)), ((), ())), preferred_element_type=jnp.float32)
            v, i = _top16_rows(s, N_KEYS)
            sv.append(v)
            si.append(i)
        cand = jnp.concatenate([sv[0][a:a + 1] + sv[1][:nb] for a, nb in enumerate(PAIR_COLS)]
                               + [jnp.full((PAIR_PAD, s.shape[1]), -jnp.inf, jnp.float32)], axis=0)
        eid = jnp.concatenate([si[0][a:a + 1] * N_KEYS + si[1][:nb] for a, nb in enumerate(PAIR_COLS)]
                              + [jnp.zeros((PAIR_PAD, s.shape[1]), jnp.int32)], axis=0)
        iota = lax.broadcasted_iota(jnp.int32, cand.shape, 0)
        cv, ce = [], []
        for _ in range(PEER_TOPK):
            m = jnp.max(cand, axis=0, keepdims=True)
            idx = jnp.min(jnp.where(cand == m, iota, PAIR_ROWS), axis=0, keepdims=True)
            hit = iota == idx
            cv.append(m)
            ce.append(jnp.max(jnp.where(hit, eid, -1), axis=0, keepdims=True))
            cand = jnp.where(hit, -jnp.inf, cand)
        cv = jnp.concatenate(cv, axis=0)
        ce = jnp.concatenate(ce, axis=0)
        e = jnp.exp(cv - cv[0:1])
        rows = pl.ds(pl.multiple_of(h * PEER_TOPK, PEER_TOPK), PEER_TOPK)
        gate_t[rows, :] = e / jnp.sum(e, axis=0, keepdims=True)
        code_t[rows, :] = ((ce & (HALF_EXPERTS - 1)) << 3) | ((ce >> 13) ^ 1)
        return carry

    lax.fori_loop(0, PEER_HEADS, head, 0)
    code_ref[...] = code_t[...].T
    gate_ref[...] = gate_t[...].T


def peer_route(x2d, g, wq_heads, keys_bf16, tb):
    n = x2d.shape[0]
    return pl.pallas_call(
        _route_kernel,
        out_shape=(jax.ShapeDtypeStruct((n, D_MODEL), jnp.float32),
                   jax.ShapeDtypeStruct((n, NSEL), jnp.int32),
                   jax.ShapeDtypeStruct((n, NSEL), jnp.float32)),
        grid=(n // tb,),
        in_specs=[pl.BlockSpec((tb, D_MODEL), lambda i: (i, 0)),
                  pl.BlockSpec((1, D_MODEL), lambda i: (0, 0)),
                  pl.BlockSpec((PEER_HEADS, D_MODEL, D_KEY), lambda i: (0, 0, 0)),
                  pl.BlockSpec((PEER_HEADS * 2, N_KEYS, D_HALF), lambda i: (0, 0, 0))],
        out_specs=(pl.BlockSpec((tb, D_MODEL), lambda i: (i, 0)),
                   pl.BlockSpec((tb, NSEL), lambda i: (i, 0)),
                   pl.BlockSpec((tb, NSEL), lambda i: (i, 0))),
        scratch_shapes=[pltpu.VMEM((NSEL, tb), jnp.int32), pltpu.VMEM((NSEL, tb), jnp.float32)],
        compiler_params=_cparams(), name="peer_route",
    )(x2d, g.reshape(1, D_MODEL), wq_heads, keys_bf16)


def pack_table(tab):
    b = lax.bitcast_convert_type(tab.astype(jnp.bfloat16), jnp.uint16).astype(jnp.uint32)
    w = b[:HALF_EXPERTS] | (b[HALF_EXPERTS:] << 16)
    return lax.bitcast_convert_type(w, jnp.int32).reshape(HALF_EXPERTS * ROW_TILE, LANES)


def _gather_pair(tab_ref, ca, cb):
    wa = tab_ref[pl.ds(pl.multiple_of(ca & -8, SUBLANES), SUBLANES), :]
    wb = tab_ref[pl.ds(pl.multiple_of(cb & -8, SUBLANES), SUBLANES), :]
    sha = ((jnp.full((SUBLANES, LANES), ca, jnp.int32) & 1) ^ 1) << 4
    shb = (jnp.full((SUBLANES, LANES), cb, jnp.int32) & 1) << 4
    return (lax.shift_right_logical(wa, sha) & 0xFFFF) | (lax.shift_left(wb, shb) & HI16)


def _upass_kernel(code_ref, codev_ref, gate_ref, xn_ref, tab_ref, pk_ref, *bufs, tb):
    col = lax.broadcasted_iota(jnp.int32, (2 * ROW_TILE, SLOT_ROWS), 1)
    row = lax.broadcasted_iota(jnp.int32, (2 * ROW_TILE, SLOT_ROWS), 0)
    chunk_mask = (((col & 15) >> 1) == (row & 7)).astype(jnp.float32)
    c2 = lax.broadcasted_iota(jnp.int32, (SLOT_ROWS, NSEL), 0)
    j2 = lax.broadcasted_iota(jnp.int32, (SLOT_ROWS, NSEL), 1)
    fold = (j2 == ((c2 >> 4) << 1) + (c2 & 1)).astype(jnp.bfloat16)
    group_a, group_b = bufs[:UPASS_GROUP], bufs[UPASS_GROUP:]
    for buf in group_b:
        buf[...] = jnp.zeros(buf.shape, jnp.int32)

    def gather(t, buf):
        for i in range(NSEL // 2):
            buf[i * ROW_TILE:(i + 1) * ROW_TILE, :] = _gather_pair(tab_ref, code_ref[t, 2 * i], code_ref[t, 2 * i + 1])

    def finish(t0, group):
        zs = []
        for q, buf in enumerate(group):
            xt = xn_ref[pl.ds(pl.multiple_of((t0 + q) * ROW_TILE, ROW_TILE), ROW_TILE), :]
            xhi = xt.astype(jnp.bfloat16)
            xlo = (xt - xhi.astype(jnp.float32)).astype(jnp.bfloat16)
            x16 = jnp.concatenate([xhi, xlo], axis=0)
            us = pltpu.bitcast(buf[...], jnp.bfloat16)
            r = lax.dot_general(x16, us, (((1,), (1,)), ((), ())), preferred_element_type=jnp.float32)
            zs.append(jnp.sum(r * chunk_mask, axis=0, keepdims=True))
        z = jnp.concatenate(zs, axis=0)
        zhi = z.astype(jnp.bfloat16)
        zlo = (z - zhi.astype(jnp.float32)).astype(jnp.bfloat16)
        act = (jnp.dot(zhi, fold, preferred_element_type=jnp.float32)
               + jnp.dot(zlo, fold, preferred_element_type=jnp.float32))
        gelu = 0.5 * act * (1.0 + lax.erf(act * (1.0 / math.sqrt(2.0))))
        rows = pl.ds(pl.multiple_of(t0, UPASS_GROUP), UPASS_GROUP)
        w = (gate_ref[rows, :] * gelu).astype(jnp.bfloat16).astype(jnp.float32)
        pk_ref[rows, :] = (lax.bitcast_convert_type(w, jnp.int32) & HI16) | codev_ref[rows, :]

    def body(i, carry):
        t0 = 2 * UPASS_GROUP * i
        for q in range(UPASS_GROUP):
            gather(t0 + q, group_a[q])
        finish(jnp.maximum(t0 - UPASS_GROUP, 0), group_b)
        for q in range(UPASS_GROUP):
            gather(t0 + UPASS_GROUP + q, group_b[q])
        finish(t0, group_a)
        return carry

    lax.fori_loop(0, tb // (2 * UPASS_GROUP), body, 0)
    finish(tb - UPASS_GROUP, group_b)


def peer_upass(code, gate, xn_rows, tab_packed, tb):
    n = code.shape[0]
    return pl.pallas_call(
        functools.partial(_upass_kernel, tb=tb),
        out_shape=jax.ShapeDtypeStruct((n, NSEL), jnp.int32),
        grid=(n // tb,),
        in_specs=[pl.BlockSpec((tb, NSEL), lambda i: (i, 0), memory_space=pltpu.SMEM),
                  pl.BlockSpec((tb, NSEL), lambda i: (i, 0)),
                  pl.BlockSpec((tb, NSEL), lambda i: (i, 0)),
                  pl.BlockSpec((tb * ROW_TILE, LANES), lambda i: (i, 0)),
                  pl.BlockSpec((HALF_EXPERTS * ROW_TILE, LANES), lambda i: (0, 0), pipeline_mode=pl.Buffered(1))],
        out_specs=pl.BlockSpec((tb, NSEL), lambda i: (i, 0)),
        scratch_shapes=[pltpu.VMEM((SLOT_ROWS // 2, LANES), jnp.int32)] * (2 * UPASS_GROUP),
        compiler_params=_cparams(), name="peer_upass",
    )(code, code, gate, xn_rows, tab_packed)


def _vpass_kernel(pk_ref, x_ref, tab_ref, o_ref, *, tb):
    def tok(t, carry):
        accs = [jnp.zeros((SUBLANES, LANES), jnp.float32) for _ in range(4)]
        for j in range(NSEL):
            pk = pk_ref[t, j]
            word = tab_ref[pl.ds(pl.multiple_of(pk & 0xFFF8, SUBLANES), SUBLANES), :]
            pkv = jnp.full((SUBLANES, LANES), pk, jnp.int32)
            w = lax.bitcast_convert_type(pkv & HI16, jnp.float32)
            bits = lax.shift_left(word, (pkv & 1) << 4) & HI16
            accs[j % 4] = accs[j % 4] + w * lax.bitcast_convert_type(bits, jnp.float32)
        sl = pl.ds(pl.multiple_of(t * ROW_TILE, ROW_TILE), ROW_TILE)
        o_ref[sl, :] = x_ref[sl, :] + ((accs[0] + accs[1]) + (accs[2] + accs[3]))
        return carry

    lax.fori_loop(0, tb, tok, 0)


def peer_vpass(packed, x_rows, tab_packed, tb):
    n = packed.shape[0]
    return pl.pallas_call(
        functools.partial(_vpass_kernel, tb=tb),
        out_shape=jax.ShapeDtypeStruct((n * ROW_TILE, LANES), jnp.float32),
        grid=(n // tb,),
        in_specs=[pl.BlockSpec((tb, NSEL), lambda i: (i, 0), memory_space=pltpu.SMEM),
                  pl.BlockSpec((tb * ROW_TILE, LANES), lambda i: (i, 0)),
                  pl.BlockSpec((HALF_EXPERTS * ROW_TILE, LANES), lambda i: (0, 0), pipeline_mode=pl.Buffered(1))],
        out_specs=pl.BlockSpec((tb * ROW_TILE, LANES), lambda i: (i, 0)),
        compiler_params=_cparams(), name="peer_vpass",
    )(packed, x_rows, tab_packed)


def peer_block(x, g, wq_heads, keys_bf16, u_packed, v_packed):
    shp = x.shape
    x2d = x.reshape(-1, D_MODEL)
    n = x2d.shape[0]
    xn, code, gate = peer_route(x2d, g, wq_heads, keys_bf16, min(PEER_ROUTE_BLOCK, n))
    packed = peer_upass(code, gate, xn.reshape(n * ROW_TILE, LANES), u_packed, min(PEER_PASS_BLOCK, n))
    out = peer_vpass(packed, x2d.reshape(n * ROW_TILE, LANES), v_packed, min(PEER_PASS_BLOCK, n))
    return out.reshape(shp)


def _rms_kernel(x_ref, g_ref, o_ref):
    o_ref[...] = _rms(x_ref[...], g_ref[...])


def rmsnorm_pallas(x, g):
    shp = x.shape
    xt = x.reshape(-1, shp[-1])
    n = xt.shape[0]
    tm = min(ROW_BLOCK, n)
    out = pl.pallas_call(
        _rms_kernel,
        out_shape=jax.ShapeDtypeStruct(xt.shape, xt.dtype),
        grid=(n // tm,),
        in_specs=[pl.BlockSpec((tm, shp[-1]), lambda i: (i, 0)),
                  pl.BlockSpec((1, shp[-1]), lambda i: (0, 0))],
        out_specs=pl.BlockSpec((tm, shp[-1]), lambda i: (i, 0)),
        compiler_params=_cparams(), name="final_rmsnorm",
    )(xt, g.reshape(1, -1))
    return out.reshape(shp)


def even_layer(x, pos0, k_prev, v_prev, h0r, h0i, norm_g, w_in, w_out, sink, s5_params, d_skip, w_glu, b_glu):
    b, t, _ = x.shape
    n = b * t
    x2d = x.reshape(n, D_MODEL)
    tabs = rope_tables(pos0 + jnp.arange(t), ROT_DIM, HEAD_DIM, SWA_Q, max(1, min(ROW_BLOCK, n) // t))
    q, k, v, u = in_even(x2d, norm_g, w_in.astype(jnp.bfloat16), tabs)
    q3, k3, v3 = q.reshape(b, t, SWA_Q), k.reshape(b, t, SWA_KV), v.reshape(b, t, SWA_KV)
    if k_prev is None:
        att = swa_attention(q3, k3, k3, v3, v3, sink, True)
        k_all, v_all = k3, v3
        h0r = jnp.zeros((b, S5_FLAT), jnp.float32)
        h0i = jnp.zeros((b, S5_FLAT), jnp.float32)
    else:
        kp, vp = k_prev.reshape(b, WINDOW, SWA_KV), v_prev.reshape(b, WINDOW, SWA_KV)
        att = swa_attention(q3, kp, k3, vp, v3, sink, False)
        k_all, v_all = jnp.concatenate([kp, k3], axis=1), jnp.concatenate([vp, v3], axis=1)
        h0r, h0i = h0r.reshape(b, S5_FLAT), h0i.reshape(b, S5_FLAT)
    lam, wb, wc = s5_discretize(*s5_params)
    u_tm = u.reshape(b, t, S5_WIDTH).transpose(1, 0, 2).reshape(n, S5_WIDTH)
    s5o_tm, hre, him = s5_mixer(u_tm, b, lam, wb, wc, d_skip, w_glu.astype(jnp.bfloat16), b_glu, h0r, h0i,
                                min(t, S5_STEPS))
    s5o = s5o_tm.reshape(t, b, S5_WIDTH).transpose(1, 0, 2).reshape(n, S5_WIDTH)
    out = out_proj(x2d, att.reshape(n, SWA_Q), s5o, w_out.astype(jnp.bfloat16))
    return (out.reshape(b, t, D_MODEL),
            k_all[:, -WINDOW:].reshape(b, WINDOW, SWA_KV_HEADS, HEAD_DIM),
            v_all[:, -WINDOW:].reshape(b, WINDOW, SWA_KV_HEADS, HEAD_DIM),
            hre.reshape(b, S5_GROUPS, S5_STATE), him.reshape(b, S5_GROUPS, S5_STATE))


def odd_layer(x, pos0, pool_prev, ckv_prev, kpe_prev, norm_g, w_in, w_out, pool_w, pool_scale,
              q_norm, kv_norm, w_uq, w_uk, w_uv):
    b, t, _ = x.shape
    n = b * t
    x2d = x.reshape(n, D_MODEL)
    reps = max(1, min(ROW_BLOCK, n) // t)
    pos = pos0 + jnp.arange(t)
    ktabs = rope_tables(pos, ROPE_DIM, KPE_PAD, KPE_PAD, reps)
    qtabs = rope_tables(pos, ROPE_DIM, ROPE_DIM, MLA_QPE, reps)
    w_perm, wuq_perm = permute_odd_weights(w_in, w_uq)
    qnope, qpe, c, kp, u = in_odd(x2d, norm_g, w_perm, q_norm, kv_norm, wuq_perm, ktabs, qtabs)
    c3, kp3 = c.reshape(b, t, KV_LORA), kp.reshape(b, t, ROPE_DIM)
    wuk_h = w_uk.transpose(1, 2, 0).astype(jnp.bfloat16)
    wuv_h = w_uv.transpose(1, 0, 2).astype(jnp.bfloat16)
    if ckv_prev is None:
        ck, kk, causal, n_keys = c3, kp3, True, t
        pool_prev = jnp.zeros((b, POOL_BUF, POOL_WIDTH), jnp.float32)
    else:
        ck, kk, causal = jnp.concatenate([ckv_prev, c3], axis=1), jnp.concatenate([kpe_prev, kp3], axis=1), False
        n_keys = ck.shape[1]
        pad = -n_keys % min(MLA_KBLOCK, n_keys)
        ck, kk = jnp.pad(ck, ((0, 0), (0, pad), (0, 0))), jnp.pad(kk, ((0, 0), (0, pad), (0, 0)))
    mla = mla_attention(qnope.reshape(b, t, MLA_QNOPE), qpe.reshape(b, t, MLA_QPE), ck.astype(jnp.bfloat16),
                        kk.astype(jnp.bfloat16), wuk_h, wuv_h, causal, n_keys)
    pool_out, pool_new = pool_mixer(u.reshape(b, t, POOL_WIDTH), pool_prev, pool_w.astype(jnp.bfloat16), pool_scale, pos0)
    out = out_proj(x2d, pool_out.reshape(n, POOL_WIDTH), mla.reshape(n, MLA_HEADS * V_DIM), w_out.astype(jnp.bfloat16))
    return out.reshape(b, t, D_MODEL), pool_new, c3, kp3


def kernel(x_prompt, x_sample, cache_swa_k, cache_swa_v, state_ssm_re, state_ssm_im, state_pool,
           cache_mla_ckv, cache_mla_kpe, norm_mix, norm_ffn, norm_final, w_in_even, w_out_even,
           swa_sink, s5_lam_re, s5_lam_im, s5_log_dt, s5_b_re, s5_b_im, s5_c_re, s5_c_im, s5_d,
           s5_w_glu, s5_b_glu, w_in_odd, w_out_odd, pool_w, pool_scale, mla_q_norm, mla_kv_norm,
           mla_w_uq, mla_w_uk, mla_w_uv, peer_w_q, peer_keys, peer_u, peer_v):
    xp, xs = x_prompt, x_sample
    kp_l, vp_l, rp_l, ip_l, poolp_l, cp_l, ep_l = [], [], [], [], [], [], []
    ks_l, vs_l, rs_l, is_l, pools_l, cs_l, es_l = [], [], [], [], [], [], []
    for layer in range(DEPTH):
        i = layer // 2
        if layer % 2 == 0:
            s5_params = (s5_lam_re[i], s5_lam_im[i], s5_log_dt[i], s5_b_re[i], s5_b_im[i], s5_c_re[i], s5_c_im[i])
            ew = (norm_mix[layer], w_in_even[i], w_out_even[i], swa_sink[i], s5_params, s5_d[i], s5_w_glu[i], s5_b_glu[i])
            xp, k1, v1, r1, i1 = even_layer(xp, 0, None, None, None, None, *ew)
            xs, k2, v2, r2, i2 = even_layer(xs, PAST_LEN, cache_swa_k[i], cache_swa_v[i],
                                            state_ssm_re[i], state_ssm_im[i], *ew)
            kp_l.append(k1); vp_l.append(v1); rp_l.append(r1); ip_l.append(i1)
            ks_l.append(k2); vs_l.append(v2); rs_l.append(r2); is_l.append(i2)
        else:
            ow = (norm_mix[layer], w_in_odd[i], w_out_odd[i], pool_w[i], pool_scale[i], mla_q_norm[i], mla_kv_norm[i],
                  mla_w_uq[i], mla_w_uk[i], mla_w_uv[i])
            xp, p1, c1, e1 = odd_layer(xp, 0, None, None, None, *ow)
            xs, p2, c2, e2 = odd_layer(xs, PAST_LEN, state_pool[i], cache_mla_ckv[i], cache_mla_kpe[i], *ow)
            poolp_l.append(p1); cp_l.append(c1); ep_l.append(e1)
            pools_l.append(p2); cs_l.append(c2); es_l.append(e2)
        wq_heads = peer_w_q[layer].reshape(D_MODEL, PEER_HEADS, D_KEY).transpose(1, 0, 2).astype(jnp.bfloat16)
        keys_bf16 = peer_keys[layer].reshape(PEER_HEADS * 2, N_KEYS, D_HALF).astype(jnp.bfloat16)
        u_packed, v_packed = pack_table(peer_u[layer]), pack_table(peer_v[layer])
        xp = peer_block(xp, norm_ffn[layer], wq_heads, keys_bf16, u_packed, v_packed)
        xs = peer_block(xs, norm_ffn[layer], wq_heads, keys_bf16, u_packed, v_packed)
    y_prompt = rmsnorm_pallas(xp, norm_final)
    y_sample = rmsnorm_pallas(xs, norm_final)
    return (y_prompt, y_sample,
            jnp.stack(kp_l), jnp.stack(vp_l), jnp.stack(rp_l), jnp.stack(ip_l),
            jnp.stack(poolp_l), jnp.stack(cp_l), jnp.stack(ep_l),
            jnp.stack(ks_l), jnp.stack(vs_l), jnp.stack(rs_l), jnp.stack(is_l),
            jnp.stack(pools_l), jnp.stack(cs_l), jnp.stack(es_l))
```

```python
import functools
import math
import jax
import jax.numpy as jnp
from jax import lax
from jax.experimental import pallas as pl
from jax.experimental.pallas import tpu as pltpu

D_MODEL = 1024
DEPTH = 2
PAST_LEN = 2048

CHUNK = 64
RMS_EPS = 1e-6
ROPE_THETA = 500000.0
NEG_INF = -1e30

SWA_HEADS = 8
SWA_KV_HEADS = 2
SWA_GROUP = SWA_HEADS // SWA_KV_HEADS
HEAD_DIM = 64
ROT_DIM = HEAD_DIM // 4
WINDOW = 128
SWA_Q = SWA_HEADS * HEAD_DIM
SWA_KV = SWA_KV_HEADS * HEAD_DIM
SWA_SCALE = HEAD_DIM ** -0.5

S5_WIDTH = 512
S5_GROUP = 16
S5_GROUPS = S5_WIDTH // S5_GROUP
S5_STATE = 64
S5_FLAT = S5_GROUPS * S5_STATE

POOL_WIDTH = 512
POOL_WINDOWS = (2, 4, 8, 16)
POOL_GROUP = POOL_WIDTH // len(POOL_WINDOWS)
POOL_MAX = 16
POOL_BUF = POOL_MAX - 1

MLA_HEADS = 8
Q_LORA = 512
KV_LORA = 256
NOPE_DIM = 64
ROPE_DIM = 32
V_DIM = 64
MLA_SCALE = (NOPE_DIM + ROPE_DIM) ** -0.5
MLA_QNOPE = MLA_HEADS * NOPE_DIM
MLA_QPE = MLA_HEADS * ROPE_DIM

EVEN_IN = SWA_Q + 2 * SWA_KV + S5_WIDTH

PEER_HEADS = 8
N_KEYS = 128
N_EXPERTS = N_KEYS * N_KEYS
D_KEY = 128
D_HALF = D_KEY // 2
PEER_TOPK = 16
NSEL = PEER_HEADS * PEER_TOPK
HALF_EXPERTS = N_EXPERTS // 2

SUBLANES = 8
LANES = 128
VMEM_LIMIT = 56 * 1024 * 1024
ROW_TILE = D_MODEL // LANES

ROW_BLOCK = 512
SWA_QBLOCK = 256
MLA_QBLOCK = 256
MLA_KBLOCK = 512
KPE_PAD = LANES
S5_STEPS = 128
S5_CARRY_VREGS = 16
PEER_ROUTE_BLOCK = 256
ROUTE_HEADS_PER_STEP = 2
PEER_PASS_BLOCK = 128
UPASS_GROUP = 8
SLOT_ROWS = NSEL * ROW_TILE
HI16 = -65536
PAIR_COLS = tuple(PEER_TOPK // (a + 1) for a in range(PEER_TOPK))
PAIR_ROWS = -(-sum(PAIR_COLS) // SUBLANES) * SUBLANES
PAIR_PAD = PAIR_ROWS - sum(PAIR_COLS)

ODD_U0 = Q_LORA + KV_LORA
ODD_KPE0 = ODD_U0 + POOL_WIDTH
ODD_IN_PAD = ODD_KPE0 + KPE_PAD


def _cparams(n_axes=1):
    return pltpu.CompilerParams(dimension_semantics=("arbitrary",) * n_axes, vmem_limit_bytes=VMEM_LIMIT)


def _rms(x, g):
    return x * lax.rsqrt(jnp.mean(x * x, axis=-1, keepdims=True) + RMS_EPS) * g


def _rope_lanes(x, cos, sin_lo, sin_hi, half):
    n = x.shape[-1]
    return x * cos + pltpu.roll(x, n - half, 1) * sin_lo + pltpu.roll(x, half, 1) * sin_hi


def rope_tables(pos, rot, period, width, reps):
    inv = ROPE_THETA ** (-jnp.arange(0, rot, 2, dtype=jnp.float32) / rot)
    ang = pos.astype(jnp.float32)[:, None] * inv[None, :]
    cos, sin = jnp.cos(ang), jnp.sin(ang)
    lane = jnp.arange(width) % period
    idx = lane % (rot // 2)
    in_lo = lane < rot // 2
    in_hi = (lane >= rot // 2) & (lane < rot)
    c = jnp.where((in_lo | in_hi)[None, :], cos[:, idx], 1.0)
    s_lo = jnp.where(in_lo[None, :], -sin[:, idx], 0.0)
    s_hi = jnp.where(in_hi[None, :], sin[:, idx], 0.0)
    return tuple(jnp.tile(t, (reps, 1)) for t in (c, s_lo, s_hi))


def _in_even_kernel(x_ref, g_ref, w_ref, cos_ref, slo_ref, shi_ref, q_ref, k_ref, v_ref, u_ref):
    xn = _rms(x_ref[...], g_ref[...])
    proj = jnp.dot(xn.astype(jnp.bfloat16), w_ref[...], preferred_element_type=jnp.float32)
    cos, slo, shi = cos_ref[...], slo_ref[...], shi_ref[...]
    q_ref[...] = _rope_lanes(proj[:, :SWA_Q], cos, slo, shi, ROT_DIM // 2)
    k_ref[...] = _rope_lanes(proj[:, SWA_Q:SWA_Q + SWA_KV], cos[:, :SWA_KV], slo[:, :SWA_KV], shi[:, :SWA_KV],
                             ROT_DIM // 2)
    v_ref[...] = proj[:, SWA_Q + SWA_KV:SWA_Q + 2 * SWA_KV]
    u_ref[...] = proj[:, SWA_Q + 2 * SWA_KV:]


def in_even(x2d, g, w_bf16, tabs):
    n = x2d.shape[0]
    tm = min(ROW_BLOCK, n)
    nt = tabs[0].shape[0] // tm
    row = lambda i: (i, 0)
    const = lambda i: (0, 0)
    tab = lambda i: (i % nt, 0)
    return pl.pallas_call(
        _in_even_kernel,
        out_shape=(jax.ShapeDtypeStruct((n, SWA_Q), jnp.float32), jax.ShapeDtypeStruct((n, SWA_KV), jnp.float32),
                   jax.ShapeDtypeStruct((n, SWA_KV), jnp.float32), jax.ShapeDtypeStruct((n, S5_WIDTH), jnp.float32)),
        grid=(n // tm,),
        in_specs=[pl.BlockSpec((tm, D_MODEL), row), pl.BlockSpec((1, D_MODEL), const),
                  pl.BlockSpec((D_MODEL, EVEN_IN), const),
                  pl.BlockSpec((tm, SWA_Q), tab), pl.BlockSpec((tm, SWA_Q), tab), pl.BlockSpec((tm, SWA_Q), tab)],
        out_specs=(pl.BlockSpec((tm, SWA_Q), row), pl.BlockSpec((tm, SWA_KV), row),
                   pl.BlockSpec((tm, SWA_KV), row), pl.BlockSpec((tm, S5_WIDTH), row)),
        compiler_params=_cparams(), name="in_even",
    )(x2d, g.reshape(1, D_MODEL), w_bf16, *tabs)


def _swa_kernel(sink_ref, q_ref, kp_ref, kc_ref, vp_ref, vc_ref, o_ref, *, banded, qb):
    i = pl.program_id(1)
    q = q_ref[0]
    k = jnp.concatenate([kp_ref[0], kc_ref[0]], axis=0).astype(jnp.bfloat16)
    v = jnp.concatenate([vp_ref[0], vc_ref[0]], axis=0).astype(jnp.bfloat16)
    nk = WINDOW + qb
    wc = WINDOW // CHUNK
    if banded:
        qc = lax.broadcasted_iota(jnp.int32, (qb, nk), 0) // CHUNK + wc
        kc = lax.broadcasted_iota(jnp.int32, (qb, nk), 1) // CHUNK
        visible = (kc <= qc) & (kc >= qc - wc) & ((kc >= wc) | (i > 0))
    outs = []
    for h in range(SWA_HEADS):
        hk = h // SWA_GROUP
        qh = q[:, h * HEAD_DIM:(h + 1) * HEAD_DIM].astype(jnp.bfloat16)
        kh = k[:, hk * HEAD_DIM:(hk + 1) * HEAD_DIM]
        s = lax.dot_general(qh, kh, (((1,), (1,)), ((), ())), preferred_element_type=jnp.float32) * SWA_SCALE
        if banded:
            s = jnp.where(visible, s, NEG_INF)
        sk = sink_ref[h]
        m = jnp.maximum(jnp.max(s, axis=-1, keepdims=True), sk)
        p = jnp.exp(s - m)
        den = jnp.sum(p, axis=-1, keepdims=True) + jnp.exp(sk - m)
        o = jnp.dot(p.astype(jnp.bfloat16), v[:, hk * HEAD_DIM:(hk + 1) * HEAD_DIM], preferred_element_type=jnp.float32)
        outs.append(o / den)
    o_ref[0] = jnp.concatenate(outs, axis=-1)


def swa_attention(q, k_prev, k_cur, v_prev, v_cur, sink, banded):
    b, t, _ = q.shape
    qb = min(SWA_QBLOCK, t)
    per = qb // WINDOW
    prev_map = (lambda bi, i: (bi, jnp.maximum(i * per - 1, 0), 0)) if banded else (lambda bi, i: (bi, 0, 0))
    cur = lambda bi, i: (bi, i, 0)
    return pl.pallas_call(
        functools.partial(_swa_kernel, banded=banded, qb=qb),
        out_shape=jax.ShapeDtypeStruct((b, t, SWA_Q), jnp.float32),
        grid=(b, t // qb),
        in_specs=[pl.BlockSpec(memory_space=pltpu.SMEM),
                  pl.BlockSpec((1, qb, SWA_Q), cur),
                  pl.BlockSpec((1, WINDOW, SWA_KV), prev_map), pl.BlockSpec((1, qb, SWA_KV), cur),
                  pl.BlockSpec((1, WINDOW, SWA_KV), prev_map), pl.BlockSpec((1, qb, SWA_KV), cur)],
        out_specs=pl.BlockSpec((1, qb, SWA_Q), cur),
        compiler_params=_cparams(2), name="swa_attention",
    )(sink, q, k_prev, k_cur, v_prev, v_cur)


def _out_kernel(x_ref, a_ref, b_ref, w_ref, o_ref):
    ka = a_ref.shape[-1]
    o_ref[...] = (x_ref[...]
                  + jnp.dot(a_ref[...].astype(jnp.bfloat16), w_ref[:ka, :], preferred_element_type=jnp.float32)
                  + jnp.dot(b_ref[...].astype(jnp.bfloat16), w_ref[ka:, :], preferred_element_type=jnp.float32))


def out_proj(x2d, a, b, w_bf16):
    n = x2d.shape[0]
    tm = min(ROW_BLOCK, n)
    row = lambda i: (i, 0)
    return pl.pallas_call(
        _out_kernel,
        out_shape=jax.ShapeDtypeStruct((n, D_MODEL), jnp.float32),
        grid=(n // tm,),
        in_specs=[pl.BlockSpec((tm, D_MODEL), row), pl.BlockSpec((tm, a.shape[1]), row),
                  pl.BlockSpec((tm, b.shape[1]), row), pl.BlockSpec(w_bf16.shape, lambda i: (0, 0))],
        out_specs=pl.BlockSpec((tm, D_MODEL), row),
        compiler_params=_cparams(), name="out_proj",
    )(x2d, a, b, w_bf16)


def s5_discretize(lam_re, lam_im, log_dt, b_re, b_im, c_re, c_im):
    lr = jnp.minimum(lam_re, -1e-4)
    li = lam_im
    dt = jnp.exp(log_dt)[:, None]
    mag = jnp.exp(lr * dt)
    ang = li * dt
    ab_re, ab_im = mag * jnp.cos(ang), mag * jnp.sin(ang)
    den = lr * lr + li * li
    nr, ni = ab_re - 1.0, ab_im
    f_re = (nr * lr + ni * li) / den
    f_im = (ni * lr - nr * li) / den
    bb_re = f_re[..., None] * b_re - f_im[..., None] * b_im
    bb_im = f_re[..., None] * b_im + f_im[..., None] * b_re
    eye = jnp.eye(S5_GROUPS, dtype=jnp.float32)

    def embed_b(bb):
        return jnp.einsum('gnc,gh->gchn', bb, eye).reshape(S5_WIDTH, S5_FLAT)

    def embed_c(c):
        return jnp.einsum('gcn,gh->gnhc', c, eye).reshape(S5_FLAT, S5_WIDTH)

    wb = jnp.concatenate([embed_b(bb_re), embed_b(bb_im)], axis=1)
    wc = jnp.concatenate([embed_c(c_re), -embed_c(c_im)], axis=0)
    lam = jnp.stack([ab_re.reshape(S5_FLAT), ab_im.reshape(S5_FLAT)])
    return lam, wb.astype(jnp.bfloat16), wc.astype(jnp.bfloat16)


def _s5_kernel(u_ref, lam_ref, wb_ref, wc_ref, d_ref, wglu_ref, bglu_ref, h0r_ref, h0i_ref,
               o_ref, hr_ref, hi_ref, hbuf, *, nb, steps, width):
    @pl.when(pl.program_id(0) == 0)
    def _():
        hr_ref[...] = h0r_ref[...]
        hi_ref[...] = h0i_ref[...]

    u = u_ref[...]
    hbuf[...] = jnp.dot(u.astype(jnp.bfloat16), wb_ref[...], preferred_element_type=jnp.float32)
    for c0 in range(0, S5_FLAT, width):
        cre = slice(c0, c0 + width)
        cim = slice(S5_FLAT + c0, S5_FLAT + c0 + width)
        lr = jnp.broadcast_to(lam_ref[0:1, cre], (nb, width))
        li = jnp.broadcast_to(lam_ref[1:2, cre], (nb, width))

        def step(t, carry):
            hr, hi = carry
            rows = pl.ds(pl.multiple_of(t * nb, nb), nb)
            nhr = lr * hr - li * hi + hbuf[rows, cre]
            nhi = lr * hi + li * hr + hbuf[rows, cim]
            hbuf[rows, cre] = nhr
            hbuf[rows, cim] = nhi
            return nhr, nhi

        hr, hi = lax.fori_loop(0, steps, step, (hr_ref[:, cre], hi_ref[:, cre]))
        hr_ref[:, cre] = hr
        hi_ref[:, cre] = hi
    y = jnp.dot(hbuf[...].astype(jnp.bfloat16), wc_ref[...], preferred_element_type=jnp.float32) + d_ref[...] * u
    z = 0.5 * y * (1.0 + jnp.tanh(math.sqrt(2.0 / math.pi) * (y + 0.044715 * (y * y * y))))
    gate = jnp.dot(z.astype(jnp.bfloat16), wglu_ref[...], preferred_element_type=jnp.float32) + bglu_ref[...]
    o_ref[...] = z * (1.0 / (1.0 + jnp.exp(-gate)))


def s5_mixer(u_tm, nb, lam, wb, wc, d_skip, w_glu_bf16, b_glu, h0r, h0i, steps):
    rows = u_tm.shape[0]
    t_total = rows // nb
    width = min(S5_FLAT, max(LANES, (S5_CARRY_VREGS * SUBLANES * LANES) // (2 * nb)))
    blk = steps * nb
    const = lambda i: (0, 0)
    return pl.pallas_call(
        functools.partial(_s5_kernel, nb=nb, steps=steps, width=width),
        out_shape=(jax.ShapeDtypeStruct((rows, S5_WIDTH), jnp.float32),
                   jax.ShapeDtypeStruct((nb, S5_FLAT), jnp.float32),
                   jax.ShapeDtypeStruct((nb, S5_FLAT), jnp.float32)),
        grid=(t_total // steps,),
        in_specs=[pl.BlockSpec((blk, S5_WIDTH), lambda i: (i, 0)),
                  pl.BlockSpec((2, S5_FLAT), const),
                  pl.BlockSpec((S5_WIDTH, 2 * S5_FLAT), const),
                  pl.BlockSpec((2 * S5_FLAT, S5_WIDTH), const),
                  pl.BlockSpec((1, S5_WIDTH), const),
                  pl.BlockSpec((S5_WIDTH, S5_WIDTH), const),
                  pl.BlockSpec((1, S5_WIDTH), const),
                  pl.BlockSpec((nb, S5_FLAT), const),
                  pl.BlockSpec((nb, S5_FLAT), const)],
        out_specs=(pl.BlockSpec((blk, S5_WIDTH), lambda i: (i, 0)),
                   pl.BlockSpec((nb, S5_FLAT), const),
                   pl.BlockSpec((nb, S5_FLAT), const)),
        scratch_shapes=[pltpu.VMEM((blk, 2 * S5_FLAT), jnp.float32)],
        compiler_params=_cparams(), name="s5_mixer",
    )(u_tm, lam, wb, wc, d_skip.reshape(1, S5_WIDTH), w_glu_bf16, b_glu.reshape(1, S5_WIDTH), h0r, h0i)


def _in_odd_kernel(x_ref, g_ref, w_ref, qn_ref, kvn_ref, wuq_ref, kcos_ref, kslo_ref, kshi_ref,
                   qcos_ref, qslo_ref, qshi_ref, qnope_ref, qpe_ref, c_ref, kp_ref, u_ref):
    xn = _rms(x_ref[...], g_ref[...])
    proj = jnp.dot(xn.astype(jnp.bfloat16), w_ref[...], preferred_element_type=jnp.float32)
    cqn = _rms(proj[:, :Q_LORA], qn_ref[...])
    q = jnp.dot(cqn.astype(jnp.bfloat16), wuq_ref[...], preferred_element_type=jnp.float32)
    qnope_ref[...] = q[:, :MLA_QNOPE]
    qpe_ref[...] = _rope_lanes(q[:, MLA_QNOPE:], qcos_ref[...], qslo_ref[...], qshi_ref[...], ROPE_DIM // 2)
    c_ref[...] = _rms(proj[:, Q_LORA:ODD_U0], kvn_ref[...])
    kp = _rope_lanes(proj[:, ODD_KPE0:], kcos_ref[...], kslo_ref[...], kshi_ref[...], ROPE_DIM // 2)
    kp_ref[...] = kp[:, :ROPE_DIM]
    u_ref[...] = proj[:, ODD_U0:ODD_KPE0]


def in_odd(x2d, g, w_perm_bf16, q_norm, kv_norm, wuq_perm_bf16, ktabs, qtabs):
    n = x2d.shape[0]
    tm = min(ROW_BLOCK, n)
    nt = ktabs[0].shape[0] // tm
    row = lambda i: (i, 0)
    const = lambda i: (0, 0)
    tab = lambda i: (i % nt, 0)
    return pl.pallas_call(
        _in_odd_kernel,
        out_shape=(jax.ShapeDtypeStruct((n, MLA_QNOPE), jnp.float32), jax.ShapeDtypeStruct((n, MLA_QPE), jnp.float32),
                   jax.ShapeDtypeStruct((n, KV_LORA), jnp.float32), jax.ShapeDtypeStruct((n, ROPE_DIM), jnp.float32),
                   jax.ShapeDtypeStruct((n, POOL_WIDTH), jnp.float32)),
        grid=(n // tm,),
        in_specs=[pl.BlockSpec((tm, D_MODEL), row), pl.BlockSpec((1, D_MODEL), const),
                  pl.BlockSpec((D_MODEL, ODD_IN_PAD), const),
                  pl.BlockSpec((1, Q_LORA), const), pl.BlockSpec((1, KV_LORA), const),
                  pl.BlockSpec((Q_LORA, MLA_QNOPE + MLA_QPE), const),
                  pl.BlockSpec((tm, KPE_PAD), tab), pl.BlockSpec((tm, KPE_PAD), tab), pl.BlockSpec((tm, KPE_PAD), tab),
                  pl.BlockSpec((tm, MLA_QPE), tab), pl.BlockSpec((tm, MLA_QPE), tab), pl.BlockSpec((tm, MLA_QPE), tab)],
        out_specs=(pl.BlockSpec((tm, MLA_QNOPE), row), pl.BlockSpec((tm, MLA_QPE), row), pl.BlockSpec((tm, KV_LORA), row),
                   pl.BlockSpec((tm, ROPE_DIM), row), pl.BlockSpec((tm, POOL_WIDTH), row)),
        compiler_params=_cparams(), name="in_odd",
    )(x2d, g.reshape(1, D_MODEL), w_perm_bf16, q_norm.reshape(1, Q_LORA), kv_norm.reshape(1, KV_LORA), wuq_perm_bf16,
      *ktabs, *qtabs)


def permute_odd_weights(w_in, w_uq):
    o1, o2 = Q_LORA + KV_LORA, Q_LORA + KV_LORA + ROPE_DIM
    kpe = jnp.pad(w_in[:, o1:o2], ((0, 0), (0, KPE_PAD - ROPE_DIM)))
    w_perm = jnp.concatenate([w_in[:, :o1], w_in[:, o2:], kpe], axis=1)
    wq = w_uq.reshape(Q_LORA, MLA_HEADS, NOPE_DIM + ROPE_DIM)
    wq_perm = jnp.concatenate([wq[:, :, :NOPE_DIM].reshape(Q_LORA, MLA_QNOPE),
                               wq[:, :, NOPE_DIM:].reshape(Q_LORA, MLA_QPE)], axis=1)
    return w_perm.astype(jnp.bfloat16), wq_perm.astype(jnp.bfloat16)


def _mla_kernel(qn_ref, qp_ref, c_ref, kp_ref, wuk_ref, wuv_ref, o_ref, *scratch, causal, qb, kb, n_keys):
    i = pl.program_id(1)
    qa_s, m_s, l_s, acc_s = (scratch[k * MLA_HEADS:(k + 1) * MLA_HEADS] for k in range(4))
    qn = qn_ref[0].astype(jnp.bfloat16)
    qp = qp_ref[0].astype(jnp.bfloat16)
    for h in range(MLA_HEADS):
        qa_s[h][...] = jnp.dot(qn[:, h * NOPE_DIM:(h + 1) * NOPE_DIM], wuk_ref[h],
                               preferred_element_type=jnp.float32).astype(jnp.bfloat16)
        m_s[h][...] = jnp.full(m_s[h].shape, NEG_INF, jnp.float32)
        l_s[h][...] = jnp.zeros(l_s[h].shape, jnp.float32)
        acc_s[h][...] = jnp.zeros(acc_s[h].shape, jnp.float32)
    if causal:
        qpos = i * qb + lax.broadcasted_iota(jnp.int32, (qb, kb), 0)
        limit = (qpos // CHUNK + 1) * CHUNK
        nblk = ((i + 1) * qb + kb - 1) // kb
    else:
        limit = n_keys
        nblk = (n_keys + kb - 1) // kb
    kidx0 = lax.broadcasted_iota(jnp.int32, (qb, kb), 1)

    def body(j, carry):
        rows = pl.ds(pl.multiple_of(j * kb, kb), kb)
        cb = c_ref[0, rows, :]
        kpb = kp_ref[0, rows, :]
        visible = kidx0 + j * kb < limit
        for h in range(MLA_HEADS):
            s = (lax.dot_general(qa_s[h][...], cb, (((1,), (1,)), ((), ())), preferred_element_type=jnp.float32)
                 + lax.dot_general(qp[:, h * ROPE_DIM:(h + 1) * ROPE_DIM], kpb, (((1,), (1,)), ((), ())),
                                   preferred_element_type=jnp.float32)) * MLA_SCALE
            s = jnp.where(visible, s, NEG_INF)
            m_old = m_s[h][...]
            m_new = jnp.maximum(m_old, jnp.max(s, axis=-1, keepdims=True))
            alpha = jnp.exp(m_old - m_new)
            p = jnp.exp(s - m_new)
            m_s[h][...] = m_new
            l_s[h][...] = alpha * l_s[h][...] + jnp.sum(p, axis=-1, keepdims=True)
            acc_s[h][...] = alpha * acc_s[h][...] + jnp.dot(p.astype(jnp.bfloat16), cb, preferred_element_type=jnp.float32)
        return carry

    lax.fori_loop(0, nblk, body, 0)
    outs = []
    for h in range(MLA_HEADS):
        o_lat = (acc_s[h][...] / l_s[h][...]).astype(jnp.bfloat16)
        outs.append(jnp.dot(o_lat, wuv_ref[h], preferred_element_type=jnp.float32))
    o_ref[0] = jnp.concatenate(outs, axis=-1)


def mla_attention(q_nope, q_pe, c_keys_bf16, kp_keys_bf16, wuk_h, wuv_h, causal, n_keys):
    b, t, _ = q_nope.shape
    tk = c_keys_bf16.shape[1]
    qb = min(MLA_QBLOCK, t)
    kb = min(MLA_KBLOCK, tk)
    cur = lambda bi, i: (bi, i, 0)
    whole = lambda bi, i: (bi, 0, 0)
    const3 = lambda bi, i: (0, 0, 0)
    return pl.pallas_call(
        functools.partial(_mla_kernel, causal=causal, qb=qb, kb=kb, n_keys=n_keys),
        out_shape=jax.ShapeDtypeStruct((b, t, MLA_HEADS * V_DIM), jnp.float32),
        grid=(b, t // qb),
        in_specs=[pl.BlockSpec((1, qb, MLA_QNOPE), cur), pl.BlockSpec((1, qb, MLA_QPE), cur),
                  pl.BlockSpec((1, tk, KV_LORA), whole), pl.BlockSpec((1, tk, ROPE_DIM), whole),
                  pl.BlockSpec((MLA_HEADS, NOPE_DIM, KV_LORA), const3), pl.BlockSpec((MLA_HEADS, KV_LORA, V_DIM), const3)],
        out_specs=pl.BlockSpec((1, qb, MLA_HEADS * V_DIM), cur),
        scratch_shapes=([pltpu.VMEM((qb, KV_LORA), jnp.bfloat16)] * MLA_HEADS
                        + [pltpu.VMEM((qb, 1), jnp.float32)] * (2 * MLA_HEADS)
                        + [pltpu.VMEM((qb, KV_LORA), jnp.float32)] * MLA_HEADS),
        compiler_params=_cparams(2), name="mla_attention",
    )(q_nope, q_pe, c_keys_bf16, kp_keys_bf16, wuk_h, wuv_h)


def _pool_kernel(u_ref, prev_ref, w_ref, scale_ref, o_ref, new_ref, ext, *, tm, pos0):
    j = pl.program_id(1)

    @pl.when(j == 0)
    def _():
        ext[0:1, :] = jnp.zeros((1, POOL_WIDTH), jnp.float32)
        ext[1:POOL_MAX, :] = prev_ref[0]

    @pl.when(j > 0)
    def _():
        ext[0:POOL_MAX, :] = ext[tm:tm + POOL_MAX, :]

    u = u_ref[0]
    ext[POOL_MAX:POOL_MAX + tm, :] = u
    pos = pos0 + j * tm + lax.broadcasted_iota(jnp.int32, (tm, POOL_GROUP), 0)
    outs = []
    for gi, w in enumerate(POOL_WINDOWS):
        cols = slice(gi * POOL_GROUP, (gi + 1) * POOL_GROUP)
        tot = u[:, cols]
        for d in range(1, w):
            tot = tot + ext[POOL_MAX - d:POOL_MAX - d + tm, cols]
        cnt = jnp.minimum(pos + 1, w).astype(jnp.float32)
        m = tot / cnt - u[:, cols]
        outs.append(jnp.dot(m.astype(jnp.bfloat16), w_ref[gi], preferred_element_type=jnp.float32))
    o_ref[0] = jnp.concatenate(outs, axis=-1) * scale_ref[...]
    new_ref[0] = ext[tm + 1:tm + POOL_MAX, :]


def pool_mixer(u, prev, pool_w_bf16, pool_scale, pos0):
    b, t, _ = u.shape
    tm = min(ROW_BLOCK, t)
    cur = lambda bi, j: (bi, j, 0)
    per_b = lambda bi, j: (bi, 0, 0)
    return pl.pallas_call(
        functools.partial(_pool_kernel, tm=tm, pos0=pos0),
        out_shape=(jax.ShapeDtypeStruct((b, t, POOL_WIDTH), jnp.float32),
                   jax.ShapeDtypeStruct((b, POOL_BUF, POOL_WIDTH), jnp.float32)),
        grid=(b, t // tm),
        in_specs=[pl.BlockSpec((1, tm, POOL_WIDTH), cur), pl.BlockSpec((1, POOL_BUF, POOL_WIDTH), per_b),
                  pl.BlockSpec((len(POOL_WINDOWS), POOL_GROUP, POOL_GROUP), lambda bi, j: (0, 0, 0)),
                  pl.BlockSpec((1, POOL_WIDTH), lambda bi, j: (0, 0))],
        out_specs=(pl.BlockSpec((1, tm, POOL_WIDTH), cur), pl.BlockSpec((1, POOL_BUF, POOL_WIDTH), per_b)),
        scratch_shapes=[pltpu.VMEM((tm + POOL_MAX, POOL_WIDTH), jnp.float32)],
        compiler_params=_cparams(2), name="pool_mixer",
    )(u, prev, pool_w_bf16, pool_scale.reshape(1, POOL_WIDTH))


def _top16_rows(s, n_rows):
    iota = lax.broadcasted_iota(jnp.int32, s.shape, 0)
    vals, idxs = [], []
    for _ in range(PEER_TOPK):
        m = jnp.max(s, axis=0, keepdims=True)
        idx = jnp.min(jnp.where(s == m, iota, n_rows), axis=0, keepdims=True)
        vals.append(m)
        idxs.append(idx)
        s = jnp.where(iota == idx, -jnp.inf, s)
    return jnp.concatenate(vals, axis=0), jnp.concatenate(idxs, axis=0)


def _route_kernel(x_ref, g_ref, wq_ref, keys_ref, xn_ref, pair_ref, gate_ref, code_t, gate_t):
    xn = _rms(x_ref[...], g_ref[...])
    xn_ref[...] = xn
    xb = xn.astype(jnp.bfloat16)

    def head_step(hs, carry):
        for hh in range(ROUTE_HEADS_PER_STEP):
            one_head(hs * ROUTE_HEADS_PER_STEP + hh)
        return carry

    def one_head(h):
        qb = jnp.dot(xb, wq_ref[h], preferred_element_type=jnp.float32).astype(jnp.bfloat16)
        sv, si = [], []
        for p in range(2):
            s = lax.dot_general(keys_ref[h * 2 + p], qb[:, p * D_HALF:(p + 1) * D_HALF],
                                (((1,), (1,)), ((), ())), preferred_element_type=jnp.float32)
            v, i = _top16_rows(s, N_KEYS)
            sv.append(v)
            si.append(i)
        cand = jnp.concatenate([sv[0][a:a + 1] + sv[1][:nb] for a, nb in enumerate(PAIR_COLS)]
                               + [jnp.full((PAIR_PAD, s.shape[1]), -jnp.inf, jnp.float32)], axis=0)
        eid = jnp.concatenate([si[0][a:a + 1] * N_KEYS + si[1][:nb] for a, nb in enumerate(PAIR_COLS)]
                              + [jnp.zeros((PAIR_PAD, s.shape[1]), jnp.int32)], axis=0)
        iota = lax.broadcasted_iota(jnp.int32, cand.shape, 0)
        cv, ce = [], []
        for _ in range(PEER_TOPK):
            m = jnp.max(cand, axis=0, keepdims=True)
            idx = jnp.min(jnp.where(cand == m, iota, PAIR_ROWS), axis=0, keepdims=True)
            hit = iota == idx
            cv.append(m)
            ce.append(jnp.max(jnp.where(hit, eid, -1), axis=0, keepdims=True))
            cand = jnp.where(hit, -jnp.inf, cand)
        cv = jnp.concatenate(cv, axis=0)
        ce = jnp.concatenate(ce, axis=0)
        e = jnp.exp(cv - cv[0:1])
        rows = pl.ds(pl.multiple_of(h * PEER_TOPK, PEER_TOPK), PEER_TOPK)
        gate_t[rows, :] = e / jnp.sum(e, axis=0, keepdims=True)
        code_t[rows, :] = ((ce & (HALF_EXPERTS - 1)) << 3) | (ce >> 13)

    lax.fori_loop(0, PEER_HEADS // ROUTE_HEADS_PER_STEP, head_step, 0)
    ca, cb = code_t[:NSEL // 2, :], code_t[NSEL // 2:, :]
    pair_ref[...] = (ca | (((cb & 1) ^ 1) << 1) | ((cb & -8) << 16)).T
    gate_ref[...] = gate_t[...].T


def peer_route(x2d, g, wq_heads, keys_bf16, tb):
    n = x2d.shape[0]
    return pl.pallas_call(
        _route_kernel,
        out_shape=(jax.ShapeDtypeStruct((n, D_MODEL), jnp.float32),
                   jax.ShapeDtypeStruct((n, NSEL // 2), jnp.int32),
                   jax.ShapeDtypeStruct((n, NSEL), jnp.float32)),
        grid=(n // tb,),
        in_specs=[pl.BlockSpec((tb, D_MODEL), lambda i: (i, 0)),
                  pl.BlockSpec((1, D_MODEL), lambda i: (0, 0)),
                  pl.BlockSpec((PEER_HEADS, D_MODEL, D_KEY), lambda i: (0, 0, 0)),
                  pl.BlockSpec((PEER_HEADS * 2, N_KEYS, D_HALF), lambda i: (0, 0, 0))],
        out_specs=(pl.BlockSpec((tb, D_MODEL), lambda i: (i, 0)),
                   pl.BlockSpec((tb, NSEL // 2), lambda i: (i, 0)),
                   pl.BlockSpec((tb, NSEL), lambda i: (i, 0))),
        scratch_shapes=[pltpu.VMEM((NSEL, tb), jnp.int32), pltpu.VMEM((NSEL, tb), jnp.float32)],
        compiler_params=_cparams(), name="peer_route",
    )(x2d, g.reshape(1, D_MODEL), wq_heads, keys_bf16)


def pack_table(tab):
    b = lax.bitcast_convert_type(tab.astype(jnp.bfloat16), jnp.uint16).astype(jnp.uint32)
    w = b[:HALF_EXPERTS] | (b[HALF_EXPERTS:] << 16)
    return lax.bitcast_convert_type(w, jnp.int32).reshape(HALF_EXPERTS * ROW_TILE, LANES)


def _gather_pair(tab_ref, cab):
    wa = tab_ref[pl.ds(pl.multiple_of(cab & 0xFFF8, SUBLANES), SUBLANES), :]
    wb = tab_ref[pl.ds(pl.multiple_of(lax.shift_right_logical(cab, 16), SUBLANES), SUBLANES), :]
    cv = jnp.full((SUBLANES, LANES), cab, jnp.int32)
    sha = (cv & 1) << 4
    shb = (cv & 2) << 3
    return (lax.shift_right_logical(wa, sha) & 0xFFFF) | (lax.shift_left(wb, shb) & HI16)


def _gather_token(tab_ref, pair_ref, t, buf):
    for i in range(NSEL // 2):
        buf[i * ROW_TILE:(i + 1) * ROW_TILE, :] = _gather_pair(tab_ref, pair_ref[t, i])


def _grouped_token_loop(tb, gather, finish, group_a, group_b):
    for buf in group_b:
        buf[...] = jnp.zeros(buf.shape, jnp.int32)

    def body(i, carry):
        t0 = 2 * UPASS_GROUP * i
        for q in range(UPASS_GROUP):
            gather(t0 + q, group_a[q])
        finish(jnp.maximum(t0 - UPASS_GROUP, 0), group_b)
        for q in range(UPASS_GROUP):
            gather(t0 + UPASS_GROUP + q, group_b[q])
        finish(t0, group_a)
        return carry

    lax.fori_loop(0, tb // (2 * UPASS_GROUP), body, 0)
    finish(tb - UPASS_GROUP, group_b)


def _upass_kernel(pair_ref, gate_ref, xn_ref, tab_ref, w_ref, *bufs, tb):
    col = lax.broadcasted_iota(jnp.int32, (2 * ROW_TILE, SLOT_ROWS), 1)
    row = lax.broadcasted_iota(jnp.int32, (2 * ROW_TILE, SLOT_ROWS), 0)
    chunk_mask = (((col & 15) >> 1) == (row & 7)).astype(jnp.float32)
    c2 = lax.broadcasted_iota(jnp.int32, (SLOT_ROWS, NSEL), 0)
    j2 = lax.broadcasted_iota(jnp.int32, (SLOT_ROWS, NSEL), 1)
    fold = (j2 == (c2 >> 4) + ((c2 & 1) << 6)).astype(jnp.bfloat16)

    def gather(t, buf):
        _gather_token(tab_ref, pair_ref, t, buf)

    def finish(t0, group):
        zs = []
        for q, buf in enumerate(group):
            xt = xn_ref[pl.ds(pl.multiple_of((t0 + q) * ROW_TILE, ROW_TILE), ROW_TILE), :]
            xhi = xt.astype(jnp.bfloat16)
            xlo = (xt - xhi.astype(jnp.float32)).astype(jnp.bfloat16)
            x16 = jnp.concatenate([xhi, xlo], axis=0)
            us = pltpu.bitcast(buf[...], jnp.bfloat16)
            r = lax.dot_general(x16, us, (((1,), (1,)), ((), ())), preferred_element_type=jnp.float32)
            zs.append(jnp.sum(r * chunk_mask, axis=0, keepdims=True))
        z = jnp.concatenate(zs, axis=0)
        zhi = z.astype(jnp.bfloat16)
        zlo = (z - zhi.astype(jnp.float32)).astype(jnp.bfloat16)
        act = (jnp.dot(zhi, fold, preferred_element_type=jnp.float32)
               + jnp.dot(zlo, fold, preferred_element_type=jnp.float32))
        gelu = 0.5 * act * (1.0 + lax.erf(act * (1.0 / math.sqrt(2.0))))
        rows = pl.ds(pl.multiple_of(t0, UPASS_GROUP), UPASS_GROUP)
        w_ref[rows, :] = (gate_ref[rows, :] * gelu).astype(jnp.bfloat16)

    _grouped_token_loop(tb, gather, finish, bufs[:UPASS_GROUP], bufs[UPASS_GROUP:])


def _peer_pass_call(kernel_fn, out_shape, out_block, pairs, per_token, rows, tab_packed, tb, name):
    n = pairs.shape[0]
    return pl.pallas_call(
        functools.partial(kernel_fn, tb=tb),
        out_shape=out_shape,
        grid=(n // tb,),
        in_specs=[pl.BlockSpec((tb, NSEL // 2), lambda i: (i, 0), memory_space=pltpu.SMEM),
                  pl.BlockSpec((tb, NSEL), lambda i: (i, 0)),
                  pl.BlockSpec((tb * ROW_TILE, LANES), lambda i: (i, 0)),
                  pl.BlockSpec((HALF_EXPERTS * ROW_TILE, LANES), lambda i: (0, 0), pipeline_mode=pl.Buffered(1))],
        out_specs=pl.BlockSpec(out_block, lambda i: (i, 0)),
        scratch_shapes=[pltpu.VMEM((SLOT_ROWS // 2, LANES), jnp.int32)] * (2 * UPASS_GROUP),
        compiler_params=_cparams(), name=name,
    )(pairs, per_token, rows, tab_packed)


def peer_upass(pairs, gate, xn_rows, tab_packed, tb):
    n = pairs.shape[0]
    return _peer_pass_call(_upass_kernel, jax.ShapeDtypeStruct((n, NSEL), jnp.bfloat16), (tb, NSEL),
                           pairs, gate, xn_rows, tab_packed, tb, "peer_upass")


def _vpass_kernel(pair_ref, w_ref, x_ref, tab_ref, o_ref, *bufs, tb):
    col = lax.broadcasted_iota(jnp.int32, (ROW_TILE, SLOT_ROWS), 1)
    row = lax.broadcasted_iota(jnp.int32, (ROW_TILE, SLOT_ROWS), 0)
    chunk_mask = (((col & 15) >> 1) == row).astype(jnp.float32)
    j2 = lax.broadcasted_iota(jnp.int32, (NSEL, SLOT_ROWS), 0)
    c2 = lax.broadcasted_iota(jnp.int32, (NSEL, SLOT_ROWS), 1)
    spread = (j2 == (c2 >> 4) + ((c2 & 1) << 6)).astype(jnp.bfloat16)

    def gather(t, buf):
        _gather_token(tab_ref, pair_ref, t, buf)

    def finish(t0, group):
        rows = pl.ds(pl.multiple_of(t0, UPASS_GROUP), UPASS_GROUP)
        wexp = jnp.dot(w_ref[rows, :], spread, preferred_element_type=jnp.float32)
        for q, buf in enumerate(group):
            wsel = (jnp.broadcast_to(wexp[q:q + 1], (ROW_TILE, SLOT_ROWS)) * chunk_mask).astype(jnp.bfloat16)
            vs = pltpu.bitcast(buf[...], jnp.bfloat16)
            sl = pl.ds(pl.multiple_of((t0 + q) * ROW_TILE, ROW_TILE), ROW_TILE)
            o_ref[sl, :] = x_ref[sl, :] + jnp.dot(wsel, vs, preferred_element_type=jnp.float32)

    _grouped_token_loop(tb, gather, finish, bufs[:UPASS_GROUP], bufs[UPASS_GROUP:])


def peer_vpass(pairs, w_bf16, x_rows, tab_packed, tb):
    n = pairs.shape[0]
    return _peer_pass_call(_vpass_kernel, jax.ShapeDtypeStruct((n * ROW_TILE, LANES), jnp.float32),
                           (tb * ROW_TILE, LANES), pairs, w_bf16, x_rows, tab_packed, tb, "peer_vpass")


def peer_block(x, g, wq_heads, keys_bf16, u_packed, v_packed):
    shp = x.shape
    x2d = x.reshape(-1, D_MODEL)
    n = x2d.shape[0]
    xn, pairs, gate = peer_route(x2d, g, wq_heads, keys_bf16, min(PEER_ROUTE_BLOCK, n))
    w = peer_upass(pairs, gate, xn.reshape(n * ROW_TILE, LANES), u_packed, min(PEER_PASS_BLOCK, n))
    out = peer_vpass(pairs, w, x2d.reshape(n * ROW_TILE, LANES), v_packed, min(PEER_PASS_BLOCK, n))
    return out.reshape(shp)


def _rms_kernel(x_ref, g_ref, o_ref):
    o_ref[...] = _rms(x_ref[...], g_ref[...])


def rmsnorm_pallas(x, g):
    shp = x.shape
    xt = x.reshape(-1, shp[-1])
    n = xt.shape[0]
    tm = min(ROW_BLOCK, n)
    out = pl.pallas_call(
        _rms_kernel,
        out_shape=jax.ShapeDtypeStruct(xt.shape, xt.dtype),
        grid=(n // tm,),
        in_specs=[pl.BlockSpec((tm, shp[-1]), lambda i: (i, 0)),
                  pl.BlockSpec((1, shp[-1]), lambda i: (0, 0))],
        out_specs=pl.BlockSpec((tm, shp[-1]), lambda i: (i, 0)),
        compiler_params=_cparams(), name="final_rmsnorm",
    )(xt, g.reshape(1, -1))
    return out.reshape(shp)


def even_layer(x, pos0, k_prev, v_prev, h0r, h0i, norm_g, w_in, w_out, sink, s5_params, d_skip, w_glu, b_glu):
    b, t, _ = x.shape
    n = b * t
    x2d = x.reshape(n, D_MODEL)
    tabs = rope_tables(pos0 + jnp.arange(t), ROT_DIM, HEAD_DIM, SWA_Q, max(1, min(ROW_BLOCK, n) // t))
    q, k, v, u = in_even(x2d, norm_g, w_in.astype(jnp.bfloat16), tabs)
    q3, k3, v3 = q.reshape(b, t, SWA_Q), k.reshape(b, t, SWA_KV), v.reshape(b, t, SWA_KV)
    if k_prev is None:
        att = swa_attention(q3, k3, k3, v3, v3, sink, True)
        k_all, v_all = k3, v3
        h0r = jnp.zeros((b, S5_FLAT), jnp.float32)
        h0i = jnp.zeros((b, S5_FLAT), jnp.float32)
    else:
        kp, vp = k_prev.reshape(b, WINDOW, SWA_KV), v_prev.reshape(b, WINDOW, SWA_KV)
        att = swa_attention(q3, kp, k3, vp, v3, sink, False)
        k_all, v_all = jnp.concatenate([kp, k3], axis=1), jnp.concatenate([vp, v3], axis=1)
        h0r, h0i = h0r.reshape(b, S5_FLAT), h0i.reshape(b, S5_FLAT)
    lam, wb, wc = s5_discretize(*s5_params)
    u_tm = u.reshape(b, t, S5_WIDTH).transpose(1, 0, 2).reshape(n, S5_WIDTH)
    s5o_tm, hre, him = s5_mixer(u_tm, b, lam, wb, wc, d_skip, w_glu.astype(jnp.bfloat16), b_glu, h0r, h0i,
                                min(t, S5_STEPS))
    s5o = s5o_tm.reshape(t, b, S5_WIDTH).transpose(1, 0, 2).reshape(n, S5_WIDTH)
    out = out_proj(x2d, att.reshape(n, SWA_Q), s5o, w_out.astype(jnp.bfloat16))
    return (out.reshape(b, t, D_MODEL),
            k_all[:, -WINDOW:].reshape(b, WINDOW, SWA_KV_HEADS, HEAD_DIM),
            v_all[:, -WINDOW:].reshape(b, WINDOW, SWA_KV_HEADS, HEAD_DIM),
            hre.reshape(b, S5_GROUPS, S5_STATE), him.reshape(b, S5_GROUPS, S5_STATE))


def odd_layer(x, pos0, pool_prev, ckv_prev, kpe_prev, norm_g, w_in, w_out, pool_w, pool_scale,
              q_norm, kv_norm, w_uq, w_uk, w_uv):
    b, t, _ = x.shape
    n = b * t
    x2d = x.reshape(n, D_MODEL)
    reps = max(1, min(ROW_BLOCK, n) // t)
    pos = pos0 + jnp.arange(t)
    ktabs = rope_tables(pos, ROPE_DIM, KPE_PAD, KPE_PAD, reps)
    qtabs = rope_tables(pos, ROPE_DIM, ROPE_DIM, MLA_QPE, reps)
    w_perm, wuq_perm = permute_odd_weights(w_in, w_uq)
    qnope, qpe, c, kp, u = in_odd(x2d, norm_g, w_perm, q_norm, kv_norm, wuq_perm, ktabs, qtabs)
    c3, kp3 = c.reshape(b, t, KV_LORA), kp.reshape(b, t, ROPE_DIM)
    wuk_h = w_uk.transpose(1, 2, 0).astype(jnp.bfloat16)
    wuv_h = w_uv.transpose(1, 0, 2).astype(jnp.bfloat16)
    if ckv_prev is None:
        ck, kk, causal, n_keys = c3, kp3, True, t
        pool_prev = jnp.zeros((b, POOL_BUF, POOL_WIDTH), jnp.float32)
    else:
        ck, kk, causal = jnp.concatenate([ckv_prev, c3], axis=1), jnp.concatenate([kpe_prev, kp3], axis=1), False
        n_keys = ck.shape[1]
        pad = -n_keys % min(MLA_KBLOCK, n_keys)
        ck, kk = jnp.pad(ck, ((0, 0), (0, pad), (0, 0))), jnp.pad(kk, ((0, 0), (0, pad), (0, 0)))
    mla = mla_attention(qnope.reshape(b, t, MLA_QNOPE), qpe.reshape(b, t, MLA_QPE), ck.astype(jnp.bfloat16),
                        kk.astype(jnp.bfloat16), wuk_h, wuv_h, causal, n_keys)
    pool_out, pool_new = pool_mixer(u.reshape(b, t, POOL_WIDTH), pool_prev, pool_w.astype(jnp.bfloat16), pool_scale, pos0)
    out = out_proj(x2d, pool_out.reshape(n, POOL_WIDTH), mla.reshape(n, MLA_HEADS * V_DIM), w_out.astype(jnp.bfloat16))
    return out.reshape(b, t, D_MODEL), pool_new, c3, kp3


def kernel(x_prompt, x_sample, cache_swa_k, cache_swa_v, state_ssm_re, state_ssm_im, state_pool,
           cache_mla_ckv, cache_mla_kpe, norm_mix, norm_ffn, norm_final, w_in_even, w_out_even,
           swa_sink, s5_lam_re, s5_lam_im, s5_log_dt, s5_b_re, s5_b_im, s5_c_re, s5_c_im, s5_d,
           s5_w_glu, s5_b_glu, w_in_odd, w_out_odd, pool_w, pool_scale, mla_q_norm, mla_kv_norm,
           mla_w_uq, mla_w_uk, mla_w_uv, peer_w_q, peer_keys, peer_u, peer_v):
    xp, xs = x_prompt, x_sample
    kp_l, vp_l, rp_l, ip_l, poolp_l, cp_l, ep_l = [], [], [], [], [], [], []
    ks_l, vs_l, rs_l, is_l, pools_l, cs_l, es_l = [], [], [], [], [], [], []
    for layer in range(DEPTH):
        i = layer // 2
        if layer % 2 == 0:
            s5_params = (s5_lam_re[i], s5_lam_im[i], s5_log_dt[i], s5_b_re[i], s5_b_im[i], s5_c_re[i], s5_c_im[i])
            ew = (norm_mix[layer], w_in_even[i], w_out_even[i], swa_sink[i], s5_params, s5_d[i], s5_w_glu[i], s5_b_glu[i])
            xp, k1, v1, r1, i1 = even_layer(xp, 0, None, None, None, None, *ew)
            xs, k2, v2, r2, i2 = even_layer(xs, PAST_LEN, cache_swa_k[i], cache_swa_v[i],
                                            state_ssm_re[i], state_ssm_im[i], *ew)
            kp_l.append(k1); vp_l.append(v1); rp_l.append(r1); ip_l.append(i1)
            ks_l.append(k2); vs_l.append(v2); rs_l.append(r2); is_l.append(i2)
        else:
            ow = (norm_mix[layer], w_in_odd[i], w_out_odd[i], pool_w[i], pool_scale[i], mla_q_norm[i], mla_kv_norm[i],
                  mla_w_uq[i], mla_w_uk[i], mla_w_uv[i])
            xp, p1, c1, e1 = odd_layer(xp, 0, None, None, None, *ow)
            xs, p2, c2, e2 = odd_layer(xs, PAST_LEN, state_pool[i], cache_mla_ckv[i], cache_mla_kpe[i], *ow)
            poolp_l.append(p1); cp_l.append(c1); ep_l.append(e1)
            pools_l.append(p2); cs_l.append(c2); es_l.append(e2)
        wq_heads = peer_w_q[layer].reshape(D_MODEL, PEER_HEADS, D_KEY).transpose(1, 0, 2).astype(jnp.bfloat16)
        keys_bf16 = peer_keys[layer].reshape(PEER_HEADS * 2, N_KEYS, D_HALF).astype(jnp.bfloat16)
        u_packed, v_packed = pack_table(peer_u[layer]), pack_table(peer_v[layer])
        xp = peer_block(xp, norm_ffn[layer], wq_heads, keys_bf16, u_packed, v_packed)
        xs = peer_block(xs, norm_ffn[layer], wq_heads, keys_bf16, u_packed, v_packed)
    y_prompt = rmsnorm_pallas(xp, norm_final)
    y_sample = rmsnorm_pallas(xs, norm_final)
    return (y_prompt, y_sample,
            jnp.stack(kp_l), jnp.stack(vp_l), jnp.stack(rp_l), jnp.stack(ip_l),
            jnp.stack(poolp_l), jnp.stack(cp_l), jnp.stack(ep_l),
            jnp.stack(ks_l), jnp.stack(vs_l), jnp.stack(rs_l), jnp.stack(is_l),
            jnp.stack(pools_l), jnp.stack(cs_l), jnp.stack(es_l))
```

```python
import functools
import math
import jax
import jax.numpy as jnp
from jax import lax
from jax.experimental import pallas as pl
from jax.experimental.pallas import tpu as pltpu

D_MODEL = 1024
DEPTH = 2
PAST_LEN = 2048

CHUNK = 64
RMS_EPS = 1e-6
ROPE_THETA = 500000.0
NEG_INF = -1e30

SWA_HEADS = 8
SWA_KV_HEADS = 2
SWA_GROUP = SWA_HEADS // SWA_KV_HEADS
HEAD_DIM = 64
ROT_DIM = HEAD_DIM // 4
WINDOW = 128
SWA_Q = SWA_HEADS * HEAD_DIM
SWA_KV = SWA_KV_HEADS * HEAD_DIM
SWA_SCALE = HEAD_DIM ** -0.5

S5_WIDTH = 512
S5_GROUP = 16
S5_GROUPS = S5_WIDTH // S5_GROUP
S5_STATE = 64
S5_FLAT = S5_GROUPS * S5_STATE

POOL_WIDTH = 512
POOL_WINDOWS = (2, 4, 8, 16)
POOL_GROUP = POOL_WIDTH // len(POOL_WINDOWS)
POOL_MAX = 16
POOL_BUF = POOL_MAX - 1

MLA_HEADS = 8
Q_LORA = 512
KV_LORA = 256
NOPE_DIM = 64
ROPE_DIM = 32
V_DIM = 64
MLA_SCALE = (NOPE_DIM + ROPE_DIM) ** -0.5
MLA_QNOPE = MLA_HEADS * NOPE_DIM
MLA_QPE = MLA_HEADS * ROPE_DIM

EVEN_IN = SWA_Q + 2 * SWA_KV + S5_WIDTH

PEER_HEADS = 8
N_KEYS = 128
N_EXPERTS = N_KEYS * N_KEYS
D_KEY = 128
D_HALF = D_KEY // 2
PEER_TOPK = 16
NSEL = PEER_HEADS * PEER_TOPK
HALF_EXPERTS = N_EXPERTS // 2

SUBLANES = 8
LANES = 128
VMEM_LIMIT = 56 * 1024 * 1024
ROW_TILE = D_MODEL // LANES

ROW_BLOCK = 512
SWA_QBLOCK = 256
MLA_QBLOCK = 256
MLA_KBLOCK = 512
MLA_CHAIN_ROWS = 256
KPE_PAD = LANES
S5_STEPS = 128
S5_CARRY_VREGS = 16
PEER_ROUTE_BLOCK = 512
ROUTE_HEADS_PER_STEP = 2
PEER_PASS_BLOCK = 256
UPASS_GROUP = 16
SLOT_ROWS = NSEL * ROW_TILE
HI16 = -65536
PAIR_COLS = tuple(PEER_TOPK // (a + 1) for a in range(PEER_TOPK))
PAIR_ROWS = -(-sum(PAIR_COLS) // SUBLANES) * SUBLANES
PAIR_PAD = PAIR_ROWS - sum(PAIR_COLS)

ODD_U0 = Q_LORA + KV_LORA
ODD_KPE0 = ODD_U0 + POOL_WIDTH
ODD_IN_PAD = ODD_KPE0 + KPE_PAD


def _cparams(n_axes=1):
    return pltpu.CompilerParams(dimension_semantics=("arbitrary",) * n_axes, vmem_limit_bytes=VMEM_LIMIT)


def _rms(x, g):
    return x * lax.rsqrt(jnp.mean(x * x, axis=-1, keepdims=True) + RMS_EPS) * g


def _rope_lanes(x, cos, sin_lo, sin_hi, half):
    n = x.shape[-1]
    return x * cos + pltpu.roll(x, n - half, 1) * sin_lo + pltpu.roll(x, half, 1) * sin_hi


def rope_tables(pos, rot, period, width, reps):
    inv = ROPE_THETA ** (-jnp.arange(0, rot, 2, dtype=jnp.float32) / rot)
    ang = pos.astype(jnp.float32)[:, None] * inv[None, :]
    cos, sin = jnp.cos(ang), jnp.sin(ang)
    lane = jnp.arange(width) % period
    idx = lane % (rot // 2)
    in_lo = lane < rot // 2
    in_hi = (lane >= rot // 2) & (lane < rot)
    c = jnp.where((in_lo | in_hi)[None, :], cos[:, idx], 1.0)
    s_lo = jnp.where(in_lo[None, :], -sin[:, idx], 0.0)
    s_hi = jnp.where(in_hi[None, :], sin[:, idx], 0.0)
    return tuple(jnp.tile(t, (reps, 1)) for t in (c, s_lo, s_hi))


def _in_even_kernel(x_ref, g_ref, w_ref, cos_ref, slo_ref, shi_ref, q_ref, k_ref, v_ref, u_ref):
    xn = _rms(x_ref[...], g_ref[...])
    proj = jnp.dot(xn.astype(jnp.bfloat16), w_ref[...], preferred_element_type=jnp.float32)
    cos, slo, shi = cos_ref[...], slo_ref[...], shi_ref[...]
    q_ref[...] = _rope_lanes(proj[:, :SWA_Q], cos, slo, shi, ROT_DIM // 2)
    k_ref[...] = _rope_lanes(proj[:, SWA_Q:SWA_Q + SWA_KV], cos[:, :SWA_KV], slo[:, :SWA_KV], shi[:, :SWA_KV],
                             ROT_DIM // 2)
    v_ref[...] = proj[:, SWA_Q + SWA_KV:SWA_Q + 2 * SWA_KV]
    u_ref[...] = proj[:, SWA_Q + 2 * SWA_KV:]


def in_even(x2d, g, w_bf16, tabs):
    n = x2d.shape[0]
    tm = min(ROW_BLOCK, n)
    nt = tabs[0].shape[0] // tm
    row = lambda i: (i, 0)
    const = lambda i: (0, 0)
    tab = lambda i: (i % nt, 0)
    return pl.pallas_call(
        _in_even_kernel,
        out_shape=(jax.ShapeDtypeStruct((n, SWA_Q), jnp.float32), jax.ShapeDtypeStruct((n, SWA_KV), jnp.float32),
                   jax.ShapeDtypeStruct((n, SWA_KV), jnp.float32), jax.ShapeDtypeStruct((n, S5_WIDTH), jnp.float32)),
        grid=(n // tm,),
        in_specs=[pl.BlockSpec((tm, D_MODEL), row), pl.BlockSpec((1, D_MODEL), const),
                  pl.BlockSpec((D_MODEL, EVEN_IN), const),
                  pl.BlockSpec((tm, SWA_Q), tab), pl.BlockSpec((tm, SWA_Q), tab), pl.BlockSpec((tm, SWA_Q), tab)],
        out_specs=(pl.BlockSpec((tm, SWA_Q), row), pl.BlockSpec((tm, SWA_KV), row),
                   pl.BlockSpec((tm, SWA_KV), row), pl.BlockSpec((tm, S5_WIDTH), row)),
        compiler_params=_cparams(), name="in_even",
    )(x2d, g.reshape(1, D_MODEL), w_bf16, *tabs)


def _swa_kernel(sink_ref, q_ref, kp_ref, kc_ref, vp_ref, vc_ref, o_ref, *, banded, qb):
    i = pl.program_id(1)
    q = q_ref[0]
    k = jnp.concatenate([kp_ref[0], kc_ref[0]], axis=0).astype(jnp.bfloat16)
    v = jnp.concatenate([vp_ref[0], vc_ref[0]], axis=0).astype(jnp.bfloat16)
    nk = WINDOW + qb
    wc = WINDOW // CHUNK
    if banded:
        qc = lax.broadcasted_iota(jnp.int32, (qb, nk), 0) // CHUNK + wc
        kc = lax.broadcasted_iota(jnp.int32, (qb, nk), 1) // CHUNK
        visible = (kc <= qc) & (kc >= qc - wc) & ((kc >= wc) | (i > 0))
    outs = []
    for h in range(SWA_HEADS):
        hk = h // SWA_GROUP
        qh = q[:, h * HEAD_DIM:(h + 1) * HEAD_DIM].astype(jnp.bfloat16)
        kh = k[:, hk * HEAD_DIM:(hk + 1) * HEAD_DIM]
        s = lax.dot_general(qh, kh, (((1,), (1,)), ((), ())), preferred_element_type=jnp.float32) * SWA_SCALE
        if banded:
            s = jnp.where(visible, s, NEG_INF)
        sk = sink_ref[h]
        m = jnp.maximum(jnp.max(s, axis=-1, keepdims=True), sk)
        p = jnp.exp(s - m)
        den = jnp.sum(p, axis=-1, keepdims=True) + jnp.exp(sk - m)
        o = jnp.dot(p.astype(jnp.bfloat16), v[:, hk * HEAD_DIM:(hk + 1) * HEAD_DIM], preferred_element_type=jnp.float32)
        outs.append(o / den)
    o_ref[0] = jnp.concatenate(outs, axis=-1)


def swa_attention(q, k_prev, k_cur, v_prev, v_cur, sink, banded):
    b, t, _ = q.shape
    qb = min(SWA_QBLOCK, t)
    per = qb // WINDOW
    prev_map = (lambda bi, i: (bi, jnp.maximum(i * per - 1, 0), 0)) if banded else (lambda bi, i: (bi, 0, 0))
    cur = lambda bi, i: (bi, i, 0)
    return pl.pallas_call(
        functools.partial(_swa_kernel, banded=banded, qb=qb),
        out_shape=jax.ShapeDtypeStruct((b, t, SWA_Q), jnp.float32),
        grid=(b, t // qb),
        in_specs=[pl.BlockSpec(memory_space=pltpu.SMEM),
                  pl.BlockSpec((1, qb, SWA_Q), cur),
                  pl.BlockSpec((1, WINDOW, SWA_KV), prev_map), pl.BlockSpec((1, qb, SWA_KV), cur),
                  pl.BlockSpec((1, WINDOW, SWA_KV), prev_map), pl.BlockSpec((1, qb, SWA_KV), cur)],
        out_specs=pl.BlockSpec((1, qb, SWA_Q), cur),
        compiler_params=_cparams(2), name="swa_attention",
    )(sink, q, k_prev, k_cur, v_prev, v_cur)


def _out_kernel(x_ref, a_ref, b_ref, w_ref, o_ref):
    ka = a_ref.shape[-1]
    o_ref[...] = (x_ref[...]
                  + jnp.dot(a_ref[...].astype(jnp.bfloat16), w_ref[:ka, :], preferred_element_type=jnp.float32)
                  + jnp.dot(b_ref[...].astype(jnp.bfloat16), w_ref[ka:, :], preferred_element_type=jnp.float32))


def out_proj(x2d, a, b, w_bf16):
    n = x2d.shape[0]
    tm = min(ROW_BLOCK, n)
    row = lambda i: (i, 0)
    return pl.pallas_call(
        _out_kernel,
        out_shape=jax.ShapeDtypeStruct((n, D_MODEL), jnp.float32),
        grid=(n // tm,),
        in_specs=[pl.BlockSpec((tm, D_MODEL), row), pl.BlockSpec((tm, a.shape[1]), row),
                  pl.BlockSpec((tm, b.shape[1]), row), pl.BlockSpec(w_bf16.shape, lambda i: (0, 0))],
        out_specs=pl.BlockSpec((tm, D_MODEL), row),
        compiler_params=_cparams(), name="out_proj",
    )(x2d, a, b, w_bf16)


def s5_discretize(lam_re, lam_im, log_dt, b_re, b_im, c_re, c_im):
    lr = jnp.minimum(lam_re, -1e-4)
    li = lam_im
    dt = jnp.exp(log_dt)[:, None]
    mag = jnp.exp(lr * dt)
    ang = li * dt
    ab_re, ab_im = mag * jnp.cos(ang), mag * jnp.sin(ang)
    den = lr * lr + li * li
    nr, ni = ab_re - 1.0, ab_im
    f_re = (nr * lr + ni * li) / den
    f_im = (ni * lr - nr * li) / den
    bb_re = f_re[..., None] * b_re - f_im[..., None] * b_im
    bb_im = f_re[..., None] * b_im + f_im[..., None] * b_re
    eye = jnp.eye(S5_GROUPS, dtype=jnp.float32)

    def embed_b(bb):
        return jnp.einsum('gnc,gh->gchn', bb, eye).reshape(S5_WIDTH, S5_FLAT)

    def embed_c(c):
        return jnp.einsum('gcn,gh->gnhc', c, eye).reshape(S5_FLAT, S5_WIDTH)

    wb = jnp.concatenate([embed_b(bb_re), embed_b(bb_im)], axis=1)
    wc = jnp.concatenate([embed_c(c_re), -embed_c(c_im)], axis=0)
    lam = jnp.stack([ab_re.reshape(S5_FLAT), ab_im.reshape(S5_FLAT)])
    return lam, wb.astype(jnp.bfloat16), wc.astype(jnp.bfloat16)


def _s5_kernel(u_ref, lam_ref, wb_ref, wc_ref, d_ref, wglu_ref, bglu_ref, h0r_ref, h0i_ref,
               o_ref, hr_ref, hi_ref, hbuf, *, nb, steps, width):
    @pl.when(pl.program_id(0) == 0)
    def _():
        hr_ref[...] = h0r_ref[...]
        hi_ref[...] = h0i_ref[...]

    u = u_ref[...]
    hbuf[...] = jnp.dot(u.astype(jnp.bfloat16), wb_ref[...], preferred_element_type=jnp.float32)
    for c0 in range(0, S5_FLAT, width):
        cre = slice(c0, c0 + width)
        cim = slice(S5_FLAT + c0, S5_FLAT + c0 + width)
        lr = jnp.broadcast_to(lam_ref[0:1, cre], (nb, width))
        li = jnp.broadcast_to(lam_ref[1:2, cre], (nb, width))

        def step(t, carry):
            hr, hi = carry
            rows = pl.ds(pl.multiple_of(t * nb, nb), nb)
            nhr = lr * hr - li * hi + hbuf[rows, cre]
            nhi = lr * hi + li * hr + hbuf[rows, cim]
            hbuf[rows, cre] = nhr
            hbuf[rows, cim] = nhi
            return nhr, nhi

        hr, hi = lax.fori_loop(0, steps, step, (hr_ref[:, cre], hi_ref[:, cre]))
        hr_ref[:, cre] = hr
        hi_ref[:, cre] = hi
    y = jnp.dot(hbuf[...].astype(jnp.bfloat16), wc_ref[...], preferred_element_type=jnp.float32) + d_ref[...] * u
    z = 0.5 * y * (1.0 + jnp.tanh(math.sqrt(2.0 / math.pi) * (y + 0.044715 * (y * y * y))))
    gate = jnp.dot(z.astype(jnp.bfloat16), wglu_ref[...], preferred_element_type=jnp.float32) + bglu_ref[...]
    o_ref[...] = z * (1.0 / (1.0 + jnp.exp(-gate)))


def s5_mixer(u_tm, nb, lam, wb, wc, d_skip, w_glu_bf16, b_glu, h0r, h0i, steps):
    rows = u_tm.shape[0]
    t_total = rows // nb
    width = min(S5_FLAT, max(LANES, (S5_CARRY_VREGS * SUBLANES * LANES) // (2 * nb)))
    blk = steps * nb
    const = lambda i: (0, 0)
    return pl.pallas_call(
        functools.partial(_s5_kernel, nb=nb, steps=steps, width=width),
        out_shape=(jax.ShapeDtypeStruct((rows, S5_WIDTH), jnp.float32),
                   jax.ShapeDtypeStruct((nb, S5_FLAT), jnp.float32),
                   jax.ShapeDtypeStruct((nb, S5_FLAT), jnp.float32)),
        grid=(t_total // steps,),
        in_specs=[pl.BlockSpec((blk, S5_WIDTH), lambda i: (i, 0)),
                  pl.BlockSpec((2, S5_FLAT), const),
                  pl.BlockSpec((S5_WIDTH, 2 * S5_FLAT), const),
                  pl.BlockSpec((2 * S5_FLAT, S5_WIDTH), const),
                  pl.BlockSpec((1, S5_WIDTH), const),
                  pl.BlockSpec((S5_WIDTH, S5_WIDTH), const),
                  pl.BlockSpec((1, S5_WIDTH), const),
                  pl.BlockSpec((nb, S5_FLAT), const),
                  pl.BlockSpec((nb, S5_FLAT), const)],
        out_specs=(pl.BlockSpec((blk, S5_WIDTH), lambda i: (i, 0)),
                   pl.BlockSpec((nb, S5_FLAT), const),
                   pl.BlockSpec((nb, S5_FLAT), const)),
        scratch_shapes=[pltpu.VMEM((blk, 2 * S5_FLAT), jnp.float32)],
        compiler_params=_cparams(), name="s5_mixer",
    )(u_tm, lam, wb, wc, d_skip.reshape(1, S5_WIDTH), w_glu_bf16, b_glu.reshape(1, S5_WIDTH), h0r, h0i)


def _in_odd_kernel(x_ref, g_ref, w_ref, qn_ref, kvn_ref, wuq_ref, kcos_ref, kslo_ref, kshi_ref,
                   qcos_ref, qslo_ref, qshi_ref, qnope_ref, qpe_ref, c_ref, kp_ref, u_ref):
    xn = _rms(x_ref[...], g_ref[...])
    proj = jnp.dot(xn.astype(jnp.bfloat16), w_ref[...], preferred_element_type=jnp.float32)
    cqn = _rms(proj[:, :Q_LORA], qn_ref[...])
    q = jnp.dot(cqn.astype(jnp.bfloat16), wuq_ref[...], preferred_element_type=jnp.float32)
    qnope_ref[...] = q[:, :MLA_QNOPE]
    qpe_ref[...] = _rope_lanes(q[:, MLA_QNOPE:], qcos_ref[...], qslo_ref[...], qshi_ref[...], ROPE_DIM // 2)
    c_ref[...] = _rms(proj[:, Q_LORA:ODD_U0], kvn_ref[...])
    kp = _rope_lanes(proj[:, ODD_KPE0:], kcos_ref[...], kslo_ref[...], kshi_ref[...], ROPE_DIM // 2)
    kp_ref[...] = kp[:, :ROPE_DIM]
    u_ref[...] = proj[:, ODD_U0:ODD_KPE0]


def in_odd(x2d, g, w_perm_bf16, q_norm, kv_norm, wuq_perm_bf16, ktabs, qtabs):
    n = x2d.shape[0]
    tm = min(ROW_BLOCK, n)
    nt = ktabs[0].shape[0] // tm
    row = lambda i: (i, 0)
    const = lambda i: (0, 0)
    tab = lambda i: (i % nt, 0)
    return pl.pallas_call(
        _in_odd_kernel,
        out_shape=(jax.ShapeDtypeStruct((n, MLA_QNOPE), jnp.float32), jax.ShapeDtypeStruct((n, MLA_QPE), jnp.float32),
                   jax.ShapeDtypeStruct((n, KV_LORA), jnp.float32), jax.ShapeDtypeStruct((n, ROPE_DIM), jnp.float32),
                   jax.ShapeDtypeStruct((n, POOL_WIDTH), jnp.float32)),
        grid=(n // tm,),
        in_specs=[pl.BlockSpec((tm, D_MODEL), row), pl.BlockSpec((1, D_MODEL), const),
                  pl.BlockSpec((D_MODEL, ODD_IN_PAD), const),
                  pl.BlockSpec((1, Q_LORA), const), pl.BlockSpec((1, KV_LORA), const),
                  pl.BlockSpec((Q_LORA, MLA_QNOPE + MLA_QPE), const),
                  pl.BlockSpec((tm, KPE_PAD), tab), pl.BlockSpec((tm, KPE_PAD), tab), pl.BlockSpec((tm, KPE_PAD), tab),
                  pl.BlockSpec((tm, MLA_QPE), tab), pl.BlockSpec((tm, MLA_QPE), tab), pl.BlockSpec((tm, MLA_QPE), tab)],
        out_specs=(pl.BlockSpec((tm, MLA_QNOPE), row), pl.BlockSpec((tm, MLA_QPE), row), pl.BlockSpec((tm, KV_LORA), row),
                   pl.BlockSpec((tm, ROPE_DIM), row), pl.BlockSpec((tm, POOL_WIDTH), row)),
        compiler_params=_cparams(), name="in_odd",
    )(x2d, g.reshape(1, D_MODEL), w_perm_bf16, q_norm.reshape(1, Q_LORA), kv_norm.reshape(1, KV_LORA), wuq_perm_bf16,
      *ktabs, *qtabs)


def permute_odd_weights(w_in, w_uq):
    o1, o2 = Q_LORA + KV_LORA, Q_LORA + KV_LORA + ROPE_DIM
    kpe = jnp.pad(w_in[:, o1:o2], ((0, 0), (0, KPE_PAD - ROPE_DIM)))
    w_perm = jnp.concatenate([w_in[:, :o1], w_in[:, o2:], kpe], axis=1)
    wq = w_uq.reshape(Q_LORA, MLA_HEADS, NOPE_DIM + ROPE_DIM)
    wq_perm = jnp.concatenate([wq[:, :, :NOPE_DIM].reshape(Q_LORA, MLA_QNOPE),
                               wq[:, :, NOPE_DIM:].reshape(Q_LORA, MLA_QPE)], axis=1)
    return w_perm.astype(jnp.bfloat16), wq_perm.astype(jnp.bfloat16)


def _mla_kernel(qn_ref, qp_ref, c_ref, kp_ref, wuk_ref, wuv_ref, o_ref, *scratch, causal, qb, kb, n_keys, hpc):
    i = pl.program_id(1)
    n_chains = MLA_HEADS // hpc
    qa_s, qp_s, m_s, l_s, acc_s = (scratch[k * n_chains:(k + 1) * n_chains] for k in range(5))
    qn = qn_ref[0].astype(jnp.bfloat16)
    qp = qp_ref[0].astype(jnp.bfloat16)
    for g in range(n_chains):
        for hh in range(hpc):
            h = g * hpc + hh
            rows = slice(hh * qb, (hh + 1) * qb)
            qa_s[g][rows, :] = jnp.dot(qn[:, h * NOPE_DIM:(h + 1) * NOPE_DIM], wuk_ref[h],
                                       preferred_element_type=jnp.float32).astype(jnp.bfloat16)
            qp_s[g][rows, :] = qp[:, h * ROPE_DIM:(h + 1) * ROPE_DIM]
        m_s[g][...] = jnp.full(m_s[g].shape, NEG_INF, jnp.float32)
        l_s[g][...] = jnp.zeros(l_s[g].shape, jnp.float32)
        acc_s[g][...] = jnp.zeros(acc_s[g].shape, jnp.float32)
    if causal:
        qpos = i * qb + lax.broadcasted_iota(jnp.int32, (hpc * qb, kb), 0) % qb
        limit = (qpos // CHUNK + 1) * CHUNK
        nblk = ((i + 1) * qb + kb - 1) // kb
    else:
        limit = n_keys
        nblk = (n_keys + kb - 1) // kb
    kidx0 = lax.broadcasted_iota(jnp.int32, (hpc * qb, kb), 1)

    def body(j, carry):
        rows = pl.ds(pl.multiple_of(j * kb, kb), kb)
        cb = c_ref[0, rows, :]
        kpb = kp_ref[0, rows, :]
        visible = kidx0 + j * kb < limit
        for g in range(n_chains):
            s = (lax.dot_general(qa_s[g][...], cb, (((1,), (1,)), ((), ())), preferred_element_type=jnp.float32)
                 + lax.dot_general(qp_s[g][...], kpb, (((1,), (1,)), ((), ())),
                                   preferred_element_type=jnp.float32)) * MLA_SCALE
            s = jnp.where(visible, s, NEG_INF)
            m_old = m_s[g][...]
            m_new = jnp.maximum(m_old, jnp.max(s, axis=-1, keepdims=True))
            alpha = jnp.exp(m_old - m_new)
            p = jnp.exp(s - m_new)
            m_s[g][...] = m_new
            l_s[g][...] = alpha * l_s[g][...] + jnp.sum(p, axis=-1, keepdims=True)
            acc_s[g][...] = alpha * acc_s[g][...] + jnp.dot(p.astype(jnp.bfloat16), cb, preferred_element_type=jnp.float32)
        return carry

    lax.fori_loop(0, nblk, body, 0)
    outs = []
    for g in range(n_chains):
        o_lat = (acc_s[g][...] / l_s[g][...]).astype(jnp.bfloat16)
        for hh in range(hpc):
            outs.append(jnp.dot(o_lat[hh * qb:(hh + 1) * qb, :], wuv_ref[g * hpc + hh],
                                preferred_element_type=jnp.float32))
    o_ref[0] = jnp.concatenate(outs, axis=-1)


def mla_attention(q_nope, q_pe, c_keys_bf16, kp_keys_bf16, wuk_h, wuv_h, causal, n_keys):
    b, t, _ = q_nope.shape
    tk = c_keys_bf16.shape[1]
    qb = min(MLA_QBLOCK, t)
    kb = min(MLA_KBLOCK, tk)
    hpc = max(1, min(MLA_HEADS, MLA_CHAIN_ROWS // qb))
    n_chains, rows = MLA_HEADS // hpc, hpc * qb
    cur = lambda bi, i: (bi, i, 0)
    whole = lambda bi, i: (bi, 0, 0)
    const3 = lambda bi, i: (0, 0, 0)
    return pl.pallas_call(
        functools.partial(_mla_kernel, causal=causal, qb=qb, kb=kb, n_keys=n_keys, hpc=hpc),
        out_shape=jax.ShapeDtypeStruct((b, t, MLA_HEADS * V_DIM), jnp.float32),
        grid=(b, t // qb),
        in_specs=[pl.BlockSpec((1, qb, MLA_QNOPE), cur), pl.BlockSpec((1, qb, MLA_QPE), cur),
                  pl.BlockSpec((1, tk, KV_LORA), whole), pl.BlockSpec((1, tk, ROPE_DIM), whole),
                  pl.BlockSpec((MLA_HEADS, NOPE_DIM, KV_LORA), const3), pl.BlockSpec((MLA_HEADS, KV_LORA, V_DIM), const3)],
        out_specs=pl.BlockSpec((1, qb, MLA_HEADS * V_DIM), cur),
        scratch_shapes=([pltpu.VMEM((rows, KV_LORA), jnp.bfloat16)] * n_chains
                        + [pltpu.VMEM((rows, ROPE_DIM), jnp.bfloat16)] * n_chains
                        + [pltpu.VMEM((rows, 1), jnp.float32)] * (2 * n_chains)
                        + [pltpu.VMEM((rows, KV_LORA), jnp.float32)] * n_chains),
        compiler_params=_cparams(2), name="mla_attention",
    )(q_nope, q_pe, c_keys_bf16, kp_keys_bf16, wuk_h, wuv_h)


def _pool_kernel(u_ref, prev_ref, w_ref, scale_ref, o_ref, new_ref, ext, *, tm, pos0):
    j = pl.program_id(1)

    @pl.when(j == 0)
    def _():
        ext[0:1, :] = jnp.zeros((1, POOL_WIDTH), jnp.float32)
        ext[1:POOL_MAX, :] = prev_ref[0]

    @pl.when(j > 0)
    def _():
        ext[0:POOL_MAX, :] = ext[tm:tm + POOL_MAX, :]

    u = u_ref[0]
    ext[POOL_MAX:POOL_MAX + tm, :] = u
    pos = pos0 + j * tm + lax.broadcasted_iota(jnp.int32, (tm, POOL_GROUP), 0)
    outs = []
    for gi, w in enumerate(POOL_WINDOWS):
        cols = slice(gi * POOL_GROUP, (gi + 1) * POOL_GROUP)
        tot = u[:, cols]
        for d in range(1, w):
            tot = tot + ext[POOL_MAX - d:POOL_MAX - d + tm, cols]
        cnt = jnp.minimum(pos + 1, w).astype(jnp.float32)
        m = tot / cnt - u[:, cols]
        outs.append(jnp.dot(m.astype(jnp.bfloat16), w_ref[gi], preferred_element_type=jnp.float32))
    o_ref[0] = jnp.concatenate(outs, axis=-1) * scale_ref[...]
    new_ref[0] = ext[tm + 1:tm + POOL_MAX, :]


def pool_mixer(u, prev, pool_w_bf16, pool_scale, pos0):
    b, t, _ = u.shape
    tm = min(ROW_BLOCK, t)
    cur = lambda bi, j: (bi, j, 0)
    per_b = lambda bi, j: (bi, 0, 0)
    return pl.pallas_call(
        functools.partial(_pool_kernel, tm=tm, pos0=pos0),
        out_shape=(jax.ShapeDtypeStruct((b, t, POOL_WIDTH), jnp.float32),
                   jax.ShapeDtypeStruct((b, POOL_BUF, POOL_WIDTH), jnp.float32)),
        grid=(b, t // tm),
        in_specs=[pl.BlockSpec((1, tm, POOL_WIDTH), cur), pl.BlockSpec((1, POOL_BUF, POOL_WIDTH), per_b),
                  pl.BlockSpec((len(POOL_WINDOWS), POOL_GROUP, POOL_GROUP), lambda bi, j: (0, 0, 0)),
                  pl.BlockSpec((1, POOL_WIDTH), lambda bi, j: (0, 0))],
        out_specs=(pl.BlockSpec((1, tm, POOL_WIDTH), cur), pl.BlockSpec((1, POOL_BUF, POOL_WIDTH), per_b)),
        scratch_shapes=[pltpu.VMEM((tm + POOL_MAX, POOL_WIDTH), jnp.float32)],
        compiler_params=_cparams(2), name="pool_mixer",
    )(u, prev, pool_w_bf16, pool_scale.reshape(1, POOL_WIDTH))


def _top16_rows(s, n_rows):
    iota = lax.broadcasted_iota(jnp.int32, s.shape, 0)
    vals, idxs = [], []
    for _ in range(PEER_TOPK):
        m = jnp.max(s, axis=0, keepdims=True)
        idx = jnp.min(jnp.where(s == m, iota, n_rows), axis=0, keepdims=True)
        vals.append(m)
        idxs.append(idx)
        s = jnp.where(iota == idx, -jnp.inf, s)
    return jnp.concatenate(vals, axis=0), jnp.concatenate(idxs, axis=0)


def _route_kernel(x_ref, g_ref, wq_ref, keys_ref, xn_ref, pair_ref, gate_ref, code_t, gate_t):
    xn = _rms(x_ref[...], g_ref[...])
    xn_ref[...] = xn
    xb = xn.astype(jnp.bfloat16)

    def head_step(hs, carry):
        for hh in range(ROUTE_HEADS_PER_STEP):
            one_head(hs * ROUTE_HEADS_PER_STEP + hh)
        return carry

    def one_head(h):
        qb = jnp.dot(xb, wq_ref[h], preferred_element_type=jnp.float32).astype(jnp.bfloat16)
        sv, si = [], []
        for p in range(2):
            s = lax.dot_general(keys_ref[h * 2 + p], qb[:, p * D_HALF:(p + 1) * D_HALF],
                                (((1,), (1,)), ((), ())), preferred_element_type=jnp.float32)
            v, i = _top16_rows(s, N_KEYS)
            sv.append(v)
            si.append(i)
        cand = jnp.concatenate([sv[0][a:a + 1] + sv[1][:nb] for a, nb in enumerate(PAIR_COLS)]
                               + [jnp.full((PAIR_PAD, s.shape[1]), -jnp.inf, jnp.float32)], axis=0)
        eid = jnp.concatenate([si[0][a:a + 1] * N_KEYS + si[1][:nb] for a, nb in enumerate(PAIR_COLS)]
                              + [jnp.zeros((PAIR_PAD, s.shape[1]), jnp.int32)], axis=0)
        iota = lax.broadcasted_iota(jnp.int32, cand.shape, 0)
        cv, ce = [], []
        for _ in range(PEER_TOPK):
            m = jnp.max(cand, axis=0, keepdims=True)
            idx = jnp.min(jnp.where(cand == m, iota, PAIR_ROWS), axis=0, keepdims=True)
            hit = iota == idx
            cv.append(m)
            ce.append(jnp.max(jnp.where(hit, eid, -1), axis=0, keepdims=True))
            cand = jnp.where(hit, -jnp.inf, cand)
        cv = jnp.concatenate(cv, axis=0)
        ce = jnp.concatenate(ce, axis=0)
        e = jnp.exp(cv - cv[0:1])
        rows = pl.ds(pl.multiple_of(h * PEER_TOPK, PEER_TOPK), PEER_TOPK)
        gate_t[rows, :] = e / jnp.sum(e, axis=0, keepdims=True)
        code_t[rows, :] = ((ce & (HALF_EXPERTS - 1)) << 3) | (ce >> 13)

    lax.fori_loop(0, PEER_HEADS // ROUTE_HEADS_PER_STEP, head_step, 0)
    ca, cb = code_t[:NSEL // 2, :], code_t[NSEL // 2:, :]
    pair_ref[...] = (ca | (((cb & 1) ^ 1) << 1) | ((cb & -8) << 16)).T
    gate_ref[...] = gate_t[...].T


def peer_route(x2d, g, wq_heads, keys_bf16, tb):
    n = x2d.shape[0]
    return pl.pallas_call(
        _route_kernel,
        out_shape=(jax.ShapeDtypeStruct((n, D_MODEL), jnp.float32),
                   jax.ShapeDtypeStruct((n, NSEL // 2), jnp.int32),
                   jax.ShapeDtypeStruct((n, NSEL), jnp.float32)),
        grid=(n // tb,),
        in_specs=[pl.BlockSpec((tb, D_MODEL), lambda i: (i, 0)),
                  pl.BlockSpec((1, D_MODEL), lambda i: (0, 0)),
                  pl.BlockSpec((PEER_HEADS, D_MODEL, D_KEY), lambda i: (0, 0, 0)),
                  pl.BlockSpec((PEER_HEADS * 2, N_KEYS, D_HALF), lambda i: (0, 0, 0))],
        out_specs=(pl.BlockSpec((tb, D_MODEL), lambda i: (i, 0)),
                   pl.BlockSpec((tb, NSEL // 2), lambda i: (i, 0)),
                   pl.BlockSpec((tb, NSEL), lambda i: (i, 0))),
        scratch_shapes=[pltpu.VMEM((NSEL, tb), jnp.int32), pltpu.VMEM((NSEL, tb), jnp.float32)],
        compiler_params=_cparams(), name="peer_route",
    )(x2d, g.reshape(1, D_MODEL), wq_heads, keys_bf16)


def pack_table(tab):
    b = lax.bitcast_convert_type(tab.astype(jnp.bfloat16), jnp.uint16).astype(jnp.uint32)
    w = b[:HALF_EXPERTS] | (b[HALF_EXPERTS:] << 16)
    return lax.bitcast_convert_type(w, jnp.int32).reshape(HALF_EXPERTS * ROW_TILE, LANES)


def _gather_pair(tab_ref, cab):
    wa = tab_ref[pl.ds(pl.multiple_of(cab & 0xFFF8, SUBLANES), SUBLANES), :]
    wb = tab_ref[pl.ds(pl.multiple_of(lax.shift_right_logical(cab, 16), SUBLANES), SUBLANES), :]
    cv = jnp.full((SUBLANES, LANES), cab, jnp.int32)
    sha = (cv & 1) << 4
    shb = (cv & 2) << 3
    return (lax.shift_right_logical(wa, sha) & 0xFFFF) | (lax.shift_left(wb, shb) & HI16)


def _gather_token(tab_ref, pair_ref, t, buf):
    for i in range(NSEL // 2):
        buf[i * ROW_TILE:(i + 1) * ROW_TILE, :] = _gather_pair(tab_ref, pair_ref[t, i])


def _grouped_token_loop(tb, gather, finish, group_a, group_b):
    for buf in group_b:
        buf[...] = jnp.zeros(buf.shape, jnp.int32)

    def body(i, carry):
        t0 = 2 * UPASS_GROUP * i
        for q in range(UPASS_GROUP):
            gather(t0 + q, group_a[q])
        finish(jnp.maximum(t0 - UPASS_GROUP, 0), group_b)
        for q in range(UPASS_GROUP):
            gather(t0 + UPASS_GROUP + q, group_b[q])
        finish(t0, group_a)
        return carry

    lax.fori_loop(0, tb // (2 * UPASS_GROUP), body, 0)
    finish(tb - UPASS_GROUP, group_b)


def _upass_kernel(pair_ref, gate_ref, xn_ref, tab_ref, w_ref, *bufs, tb):
    col = lax.broadcasted_iota(jnp.int32, (2 * ROW_TILE, SLOT_ROWS), 1)
    row = lax.broadcasted_iota(jnp.int32, (2 * ROW_TILE, SLOT_ROWS), 0)
    chunk_mask = (((col & 15) >> 1) == (row & 7)).astype(jnp.float32)
    c2 = lax.broadcasted_iota(jnp.int32, (SLOT_ROWS, NSEL), 0)
    j2 = lax.broadcasted_iota(jnp.int32, (SLOT_ROWS, NSEL), 1)
    fold = (j2 == (c2 >> 4) + ((c2 & 1) << 6)).astype(jnp.bfloat16)

    def gather(t, buf):
        _gather_token(tab_ref, pair_ref, t, buf)

    def finish(t0, group):
        zs = []
        for q, buf in enumerate(group):
            xt = xn_ref[pl.ds(pl.multiple_of((t0 + q) * ROW_TILE, ROW_TILE), ROW_TILE), :]
            xhi = xt.astype(jnp.bfloat16)
            xlo = (xt - xhi.astype(jnp.float32)).astype(jnp.bfloat16)
            x16 = jnp.concatenate([xhi, xlo], axis=0)
            us = pltpu.bitcast(buf[...], jnp.bfloat16)
            r = lax.dot_general(x16, us, (((1,), (1,)), ((), ())), preferred_element_type=jnp.float32)
            zs.append(jnp.sum(r * chunk_mask, axis=0, keepdims=True))
        z = jnp.concatenate(zs, axis=0)
        zhi = z.astype(jnp.bfloat16)
        zlo = (z - zhi.astype(jnp.float32)).astype(jnp.bfloat16)
        act = (jnp.dot(zhi, fold, preferred_element_type=jnp.float32)
               + jnp.dot(zlo, fold, preferred_element_type=jnp.float32))
        gelu = 0.5 * act * (1.0 + lax.erf(act * (1.0 / math.sqrt(2.0))))
        rows = pl.ds(pl.multiple_of(t0, UPASS_GROUP), UPASS_GROUP)
        w_ref[rows, :] = (gate_ref[rows, :] * gelu).astype(jnp.bfloat16)

    _grouped_token_loop(tb, gather, finish, bufs[:UPASS_GROUP], bufs[UPASS_GROUP:])


def _peer_pass_call(kernel_fn, out_shape, out_block, pairs, per_token, rows, tab_packed, tb, name):
    n = pairs.shape[0]
    return pl.pallas_call(
        functools.partial(kernel_fn, tb=tb),
        out_shape=out_shape,
        grid=(n // tb,),
        in_specs=[pl.BlockSpec((tb, NSEL // 2), lambda i: (i, 0), memory_space=pltpu.SMEM),
                  pl.BlockSpec((tb, NSEL), lambda i: (i, 0)),
                  pl.BlockSpec((tb * ROW_TILE, LANES), lambda i: (i, 0)),
                  pl.BlockSpec((HALF_EXPERTS * ROW_TILE, LANES), lambda i: (0, 0), pipeline_mode=pl.Buffered(1))],
        out_specs=pl.BlockSpec(out_block, lambda i: (i, 0)),
        scratch_shapes=[pltpu.VMEM((SLOT_ROWS // 2, LANES), jnp.int32)] * (2 * UPASS_GROUP),
        compiler_params=_cparams(), name=name,
    )(pairs, per_token, rows, tab_packed)


def peer_upass(pairs, gate, xn_rows, tab_packed, tb):
    n = pairs.shape[0]
    return _peer_pass_call(_upass_kernel, jax.ShapeDtypeStruct((n, NSEL), jnp.bfloat16), (tb, NSEL),
                           pairs, gate, xn_rows, tab_packed, tb, "peer_upass")


def _vpass_kernel(pair_ref, w_ref, x_ref, tab_ref, o_ref, *bufs, tb):
    col = lax.broadcasted_iota(jnp.int32, (ROW_TILE, SLOT_ROWS), 1)
    row = lax.broadcasted_iota(jnp.int32, (ROW_TILE, SLOT_ROWS), 0)
    chunk_mask = (((col & 15) >> 1) == row).astype(jnp.float32)
    j2 = lax.broadcasted_iota(jnp.int32, (NSEL, SLOT_ROWS), 0)
    c2 = lax.broadcasted_iota(jnp.int32, (NSEL, SLOT_ROWS), 1)
    spread = (j2 == (c2 >> 4) + ((c2 & 1) << 6)).astype(jnp.bfloat16)

    def gather(t, buf):
        _gather_token(tab_ref, pair_ref, t, buf)

    def finish(t0, group):
        rows = pl.ds(pl.multiple_of(t0, UPASS_GROUP), UPASS_GROUP)
        wexp = jnp.dot(w_ref[rows, :], spread, preferred_element_type=jnp.float32)
        for q, buf in enumerate(group):
            wsel = (jnp.broadcast_to(wexp[q:q + 1], (ROW_TILE, SLOT_ROWS)) * chunk_mask).astype(jnp.bfloat16)
            vs = pltpu.bitcast(buf[...], jnp.bfloat16)
            sl = pl.ds(pl.multiple_of((t0 + q) * ROW_TILE, ROW_TILE), ROW_TILE)
            o_ref[sl, :] = x_ref[sl, :] + jnp.dot(wsel, vs, preferred_element_type=jnp.float32)

    _grouped_token_loop(tb, gather, finish, bufs[:UPASS_GROUP], bufs[UPASS_GROUP:])


def peer_vpass(pairs, w_bf16, x_rows, tab_packed, tb):
    n = pairs.shape[0]
    return _peer_pass_call(_vpass_kernel, jax.ShapeDtypeStruct((n * ROW_TILE, LANES), jnp.float32),
                           (tb * ROW_TILE, LANES), pairs, w_bf16, x_rows, tab_packed, tb, "peer_vpass")


def peer_block(x, g, wq_heads, keys_bf16, u_packed, v_packed):
    shp = x.shape
    x2d = x.reshape(-1, D_MODEL)
    n = x2d.shape[0]
    xn, pairs, gate = peer_route(x2d, g, wq_heads, keys_bf16, min(PEER_ROUTE_BLOCK, n))
    w = peer_upass(pairs, gate, xn.reshape(n * ROW_TILE, LANES), u_packed, min(PEER_PASS_BLOCK, n))
    out = peer_vpass(pairs, w, x2d.reshape(n * ROW_TILE, LANES), v_packed, min(PEER_PASS_BLOCK, n))
    return out.reshape(shp)


def _rms_kernel(x_ref, g_ref, o_ref):
    o_ref[...] = _rms(x_ref[...], g_ref[...])


def rmsnorm_pallas(x, g):
    shp = x.shape
    xt = x.reshape(-1, shp[-1])
    n = xt.shape[0]
    tm = min(ROW_BLOCK, n)
    out = pl.pallas_call(
        _rms_kernel,
        out_shape=jax.ShapeDtypeStruct(xt.shape, xt.dtype),
        grid=(n // tm,),
        in_specs=[pl.BlockSpec((tm, shp[-1]), lambda i: (i, 0)),
                  pl.BlockSpec((1, shp[-1]), lambda i: (0, 0))],
        out_specs=pl.BlockSpec((tm, shp[-1]), lambda i: (i, 0)),
        compiler_params=_cparams(), name="final_rmsnorm",
    )(xt, g.reshape(1, -1))
    return out.reshape(shp)


def even_layer(x, pos0, k_prev, v_prev, h0r, h0i, norm_g, w_in, w_out, sink, s5_params, d_skip, w_glu, b_glu):
    b, t, _ = x.shape
    n = b * t
    x2d = x.reshape(n, D_MODEL)
    tabs = rope_tables(pos0 + jnp.arange(t), ROT_DIM, HEAD_DIM, SWA_Q, max(1, min(ROW_BLOCK, n) // t))
    q, k, v, u = in_even(x2d, norm_g, w_in.astype(jnp.bfloat16), tabs)
    q3, k3, v3 = q.reshape(b, t, SWA_Q), k.reshape(b, t, SWA_KV), v.reshape(b, t, SWA_KV)
    if k_prev is None:
        att = swa_attention(q3, k3, k3, v3, v3, sink, True)
        k_all, v_all = k3, v3
        h0r = jnp.zeros((b, S5_FLAT), jnp.float32)
        h0i = jnp.zeros((b, S5_FLAT), jnp.float32)
    else:
        kp, vp = k_prev.reshape(b, WINDOW, SWA_KV), v_prev.reshape(b, WINDOW, SWA_KV)
        att = swa_attention(q3, kp, k3, vp, v3, sink, False)
        k_all, v_all = jnp.concatenate([kp, k3], axis=1), jnp.concatenate([vp, v3], axis=1)
        h0r, h0i = h0r.reshape(b, S5_FLAT), h0i.reshape(b, S5_FLAT)
    lam, wb, wc = s5_discretize(*s5_params)
    u_tm = u.reshape(b, t, S5_WIDTH).transpose(1, 0, 2).reshape(n, S5_WIDTH)
    s5o_tm, hre, him = s5_mixer(u_tm, b, lam, wb, wc, d_skip, w_glu.astype(jnp.bfloat16), b_glu, h0r, h0i,
                                min(t, S5_STEPS))
    s5o = s5o_tm.reshape(t, b, S5_WIDTH).transpose(1, 0, 2).reshape(n, S5_WIDTH)
    out = out_proj(x2d, att.reshape(n, SWA_Q), s5o, w_out.astype(jnp.bfloat16))
    return (out.reshape(b, t, D_MODEL),
            k_all[:, -WINDOW:].reshape(b, WINDOW, SWA_KV_HEADS, HEAD_DIM),
            v_all[:, -WINDOW:].reshape(b, WINDOW, SWA_KV_HEADS, HEAD_DIM),
            hre.reshape(b, S5_GROUPS, S5_STATE), him.reshape(b, S5_GROUPS, S5_STATE))


def odd_layer(x, pos0, pool_prev, ckv_prev, kpe_prev, norm_g, w_in, w_out, pool_w, pool_scale,
              q_norm, kv_norm, w_uq, w_uk, w_uv):
    b, t, _ = x.shape
    n = b * t
    x2d = x.reshape(n, D_MODEL)
    reps = max(1, min(ROW_BLOCK, n) // t)
    pos = pos0 + jnp.arange(t)
    ktabs = rope_tables(pos, ROPE_DIM, KPE_PAD, KPE_PAD, reps)
    qtabs = rope_tables(pos, ROPE_DIM, ROPE_DIM, MLA_QPE, reps)
    w_perm, wuq_perm = permute_odd_weights(w_in, w_uq)
    qnope, qpe, c, kp, u = in_odd(x2d, norm_g, w_perm, q_norm, kv_norm, wuq_perm, ktabs, qtabs)
    c3, kp3 = c.reshape(b, t, KV_LORA), kp.reshape(b, t, ROPE_DIM)
    wuk_h = w_uk.transpose(1, 2, 0).astype(jnp.bfloat16)
    wuv_h = w_uv.transpose(1, 0, 2).astype(jnp.bfloat16)
    if ckv_prev is None:
        ck, kk, causal, n_keys = c3, kp3, True, t
        pool_prev = jnp.zeros((b, POOL_BUF, POOL_WIDTH), jnp.float32)
    else:
        ck, kk, causal = jnp.concatenate([ckv_prev, c3], axis=1), jnp.concatenate([kpe_prev, kp3], axis=1), False
        n_keys = ck.shape[1]
        pad = -n_keys % min(MLA_KBLOCK, n_keys)
        ck, kk = jnp.pad(ck, ((0, 0), (0, pad), (0, 0))), jnp.pad(kk, ((0, 0), (0, pad), (0, 0)))
    mla = mla_attention(qnope.reshape(b, t, MLA_QNOPE), qpe.reshape(b, t, MLA_QPE), ck.astype(jnp.bfloat16),
                        kk.astype(jnp.bfloat16), wuk_h, wuv_h, causal, n_keys)
    pool_out, pool_new = pool_mixer(u.reshape(b, t, POOL_WIDTH), pool_prev, pool_w.astype(jnp.bfloat16), pool_scale, pos0)
    out = out_proj(x2d, pool_out.reshape(n, POOL_WIDTH), mla.reshape(n, MLA_HEADS * V_DIM), w_out.astype(jnp.bfloat16))
    return out.reshape(b, t, D_MODEL), pool_new, c3, kp3


def kernel(x_prompt, x_sample, cache_swa_k, cache_swa_v, state_ssm_re, state_ssm_im, state_pool,
           cache_mla_ckv, cache_mla_kpe, norm_mix, norm_ffn, norm_final, w_in_even, w_out_even,
           swa_sink, s5_lam_re, s5_lam_im, s5_log_dt, s5_b_re, s5_b_im, s5_c_re, s5_c_im, s5_d,
           s5_w_glu, s5_b_glu, w_in_odd, w_out_odd, pool_w, pool_scale, mla_q_norm, mla_kv_norm,
           mla_w_uq, mla_w_uk, mla_w_uv, peer_w_q, peer_keys, peer_u, peer_v):
    xp, xs = x_prompt, x_sample
    kp_l, vp_l, rp_l, ip_l, poolp_l, cp_l, ep_l = [], [], [], [], [], [], []
    ks_l, vs_l, rs_l, is_l, pools_l, cs_l, es_l = [], [], [], [], [], [], []
    for layer in range(DEPTH):
        i = layer // 2
        if layer % 2 == 0:
            s5_params = (s5_lam_re[i], s5_lam_im[i], s5_log_dt[i], s5_b_re[i], s5_b_im[i], s5_c_re[i], s5_c_im[i])
            ew = (norm_mix[layer], w_in_even[i], w_out_even[i], swa_sink[i], s5_params, s5_d[i], s5_w_glu[i], s5_b_glu[i])
            xp, k1, v1, r1, i1 = even_layer(xp, 0, None, None, None, None, *ew)
            xs, k2, v2, r2, i2 = even_layer(xs, PAST_LEN, cache_swa_k[i], cache_swa_v[i],
                                            state_ssm_re[i], state_ssm_im[i], *ew)
            kp_l.append(k1); vp_l.append(v1); rp_l.append(r1); ip_l.append(i1)
            ks_l.append(k2); vs_l.append(v2); rs_l.append(r2); is_l.append(i2)
        else:
            ow = (norm_mix[layer], w_in_odd[i], w_out_odd[i], pool_w[i], pool_scale[i], mla_q_norm[i], mla_kv_norm[i],
                  mla_w_uq[i], mla_w_uk[i], mla_w_uv[i])
            xp, p1, c1, e1 = odd_layer(xp, 0, None, None, None, *ow)
            xs, p2, c2, e2 = odd_layer(xs, PAST_LEN, state_pool[i], cache_mla_ckv[i], cache_mla_kpe[i], *ow)
            poolp_l.append(p1); cp_l.append(c1); ep_l.append(e1)
            pools_l.append(p2); cs_l.append(c2); es_l.append(e2)
        wq_heads = peer_w_q[layer].reshape(D_MODEL, PEER_HEADS, D_KEY).transpose(1, 0, 2).astype(jnp.bfloat16)
        keys_bf16 = peer_keys[layer].reshape(PEER_HEADS * 2, N_KEYS, D_HALF).astype(jnp.bfloat16)
        u_packed, v_packed = pack_table(peer_u[layer]), pack_table(peer_v[layer])
        xp = peer_block(xp, norm_ffn[layer], wq_heads, keys_bf16, u_packed, v_packed)
        xs = peer_block(xs, norm_ffn[layer], wq_heads, keys_bf16, u_packed, v_packed)
    y_prompt = rmsnorm_pallas(xp, norm_final)
    y_sample = rmsnorm_pallas(xs, norm_final)
    return (y_prompt, y_sample,
            jnp.stack(kp_l), jnp.stack(vp_l), jnp.stack(rp_l), jnp.stack(ip_l),
            jnp.stack(poolp_l), jnp.stack(cp_l), jnp.stack(ep_l),
            jnp.stack(ks_l), jnp.stack(vs_l), jnp.stack(rs_l), jnp.stack(is_l),
            jnp.stack(pools_l), jnp.stack(cs_l), jnp.stack(es_l))
```

```python
import functools
import math
import jax
import jax.numpy as jnp
from jax import lax
from jax.experimental import pallas as pl
from jax.experimental.pallas import tpu as pltpu

D_MODEL = 1024
DEPTH = 2
PAST_LEN = 2048

CHUNK = 64
RMS_EPS = 1e-6
ROPE_THETA = 500000.0
NEG_INF = -1e30

SWA_HEADS = 8
SWA_KV_HEADS = 2
SWA_GROUP = SWA_HEADS // SWA_KV_HEADS
HEAD_DIM = 64
ROT_DIM = HEAD_DIM // 4
WINDOW = 128
SWA_Q = SWA_HEADS * HEAD_DIM
SWA_KV = SWA_KV_HEADS * HEAD_DIM
SWA_SCALE = HEAD_DIM ** -0.5

S5_WIDTH = 512
S5_GROUP = 16
S5_GROUPS = S5_WIDTH // S5_GROUP
S5_STATE = 64
S5_FLAT = S5_GROUPS * S5_STATE

POOL_WIDTH = 512
POOL_WINDOWS = (2, 4, 8, 16)
POOL_GROUP = POOL_WIDTH // len(POOL_WINDOWS)
POOL_MAX = 16
POOL_BUF = POOL_MAX - 1

MLA_HEADS = 8
Q_LORA = 512
KV_LORA = 256
NOPE_DIM = 64
ROPE_DIM = 32
V_DIM = 64
MLA_SCALE = (NOPE_DIM + ROPE_DIM) ** -0.5
MLA_QNOPE = MLA_HEADS * NOPE_DIM
MLA_QPE = MLA_HEADS * ROPE_DIM

EVEN_IN = SWA_Q + 2 * SWA_KV + S5_WIDTH

PEER_HEADS = 8
N_KEYS = 128
N_EXPERTS = N_KEYS * N_KEYS
D_KEY = 128
D_HALF = D_KEY // 2
PEER_TOPK = 16
NSEL = PEER_HEADS * PEER_TOPK
HALF_EXPERTS = N_EXPERTS // 2

SUBLANES = 8
LANES = 128
VMEM_LIMIT = 56 * 1024 * 1024
ROW_TILE = D_MODEL // LANES

ROW_BLOCK = 512
SWA_QBLOCK = 256
MLA_QBLOCK = 256
MLA_KBLOCK = 512
MLA_CHAIN_ROWS = 256
KPE_PAD = LANES
S5_STEPS = 128
S5_CARRY_VREGS = 16
PEER_ROUTE_BLOCK = 256
ROUTE_HEADS_PER_STEP = 4
PEER_PASS_BLOCK = 256
UPASS_GROUP = 16
SLOT_ROWS = NSEL * ROW_TILE
HI16 = -65536
PAIR_COLS = tuple(PEER_TOPK // (a + 1) for a in range(PEER_TOPK))
PAIR_ROWS = -(-sum(PAIR_COLS) // SUBLANES) * SUBLANES
PAIR_PAD = PAIR_ROWS - sum(PAIR_COLS)

ODD_U0 = Q_LORA + KV_LORA
ODD_KPE0 = ODD_U0 + POOL_WIDTH
ODD_IN_PAD = ODD_KPE0 + KPE_PAD


def _cparams(n_axes=1):
    return pltpu.CompilerParams(dimension_semantics=("arbitrary",) * n_axes, vmem_limit_bytes=VMEM_LIMIT)


def _rms(x, g):
    return x * lax.rsqrt(jnp.mean(x * x, axis=-1, keepdims=True) + RMS_EPS) * g


def _rope_lanes(x, cos, sin_lo, sin_hi, half):
    n = x.shape[-1]
    return x * cos + pltpu.roll(x, n - half, 1) * sin_lo + pltpu.roll(x, half, 1) * sin_hi


def rope_tables(pos, rot, period, width, reps):
    inv = ROPE_THETA ** (-jnp.arange(0, rot, 2, dtype=jnp.float32) / rot)
    ang = pos.astype(jnp.float32)[:, None] * inv[None, :]
    cos, sin = jnp.cos(ang), jnp.sin(ang)
    lane = jnp.arange(width) % period
    idx = lane % (rot // 2)
    in_lo = lane < rot // 2
    in_hi = (lane >= rot // 2) & (lane < rot)
    c = jnp.where((in_lo | in_hi)[None, :], cos[:, idx], 1.0)
    s_lo = jnp.where(in_lo[None, :], -sin[:, idx], 0.0)
    s_hi = jnp.where(in_hi[None, :], sin[:, idx], 0.0)
    return tuple(jnp.tile(t, (reps, 1)) for t in (c, s_lo, s_hi))


def _in_even_kernel(x_ref, g_ref, w_ref, cos_ref, slo_ref, shi_ref, q_ref, k_ref, v_ref, u_ref):
    xn = _rms(x_ref[...], g_ref[...])
    proj = jnp.dot(xn.astype(jnp.bfloat16), w_ref[...], preferred_element_type=jnp.float32)
    cos, slo, shi = cos_ref[...], slo_ref[...], shi_ref[...]
    q_ref[...] = _rope_lanes(proj[:, :SWA_Q], cos, slo, shi, ROT_DIM // 2)
    k_ref[...] = _rope_lanes(proj[:, SWA_Q:SWA_Q + SWA_KV], cos[:, :SWA_KV], slo[:, :SWA_KV], shi[:, :SWA_KV],
                             ROT_DIM // 2)
    v_ref[...] = proj[:, SWA_Q + SWA_KV:SWA_Q + 2 * SWA_KV]
    u_ref[...] = proj[:, SWA_Q + 2 * SWA_KV:]


def in_even(x2d, g, w_bf16, tabs):
    n = x2d.shape[0]
    tm = min(ROW_BLOCK, n)
    nt = tabs[0].shape[0] // tm
    row = lambda i: (i, 0)
    const = lambda i: (0, 0)
    tab = lambda i: (i % nt, 0)
    return pl.pallas_call(
        _in_even_kernel,
        out_shape=(jax.ShapeDtypeStruct((n, SWA_Q), jnp.float32), jax.ShapeDtypeStruct((n, SWA_KV), jnp.float32),
                   jax.ShapeDtypeStruct((n, SWA_KV), jnp.float32), jax.ShapeDtypeStruct((n, S5_WIDTH), jnp.float32)),
        grid=(n // tm,),
        in_specs=[pl.BlockSpec((tm, D_MODEL), row), pl.BlockSpec((1, D_MODEL), const),
                  pl.BlockSpec((D_MODEL, EVEN_IN), const),
                  pl.BlockSpec((tm, SWA_Q), tab), pl.BlockSpec((tm, SWA_Q), tab), pl.BlockSpec((tm, SWA_Q), tab)],
        out_specs=(pl.BlockSpec((tm, SWA_Q), row), pl.BlockSpec((tm, SWA_KV), row),
                   pl.BlockSpec((tm, SWA_KV), row), pl.BlockSpec((tm, S5_WIDTH), row)),
        compiler_params=_cparams(), name="in_even",
    )(x2d, g.reshape(1, D_MODEL), w_bf16, *tabs)


def _swa_kernel(sink_ref, q_ref, kp_ref, kc_ref, vp_ref, vc_ref, o_ref, *, banded, qb):
    i = pl.program_id(1)
    q = q_ref[0]
    k = jnp.concatenate([kp_ref[0], kc_ref[0]], axis=0).astype(jnp.bfloat16)
    v = jnp.concatenate([vp_ref[0], vc_ref[0]], axis=0).astype(jnp.bfloat16)
    nk = WINDOW + qb
    wc = WINDOW // CHUNK
    if banded:
        qc = lax.broadcasted_iota(jnp.int32, (qb, nk), 0) // CHUNK + wc
        kc = lax.broadcasted_iota(jnp.int32, (qb, nk), 1) // CHUNK
        visible = (kc <= qc) & (kc >= qc - wc) & ((kc >= wc) | (i > 0))
    outs = []
    for h in range(SWA_HEADS):
        hk = h // SWA_GROUP
        qh = q[:, h * HEAD_DIM:(h + 1) * HEAD_DIM].astype(jnp.bfloat16)
        kh = k[:, hk * HEAD_DIM:(hk + 1) * HEAD_DIM]
        s = lax.dot_general(qh, kh, (((1,), (1,)), ((), ())), preferred_element_type=jnp.float32) * SWA_SCALE
        if banded:
            s = jnp.where(visible, s, NEG_INF)
        sk = sink_ref[h]
        m = jnp.maximum(jnp.max(s, axis=-1, keepdims=True), sk)
        p = jnp.exp(s - m)
        den = jnp.sum(p, axis=-1, keepdims=True) + jnp.exp(sk - m)
        o = jnp.dot(p.astype(jnp.bfloat16), v[:, hk * HEAD_DIM:(hk + 1) * HEAD_DIM], preferred_element_type=jnp.float32)
        outs.append(o / den)
    o_ref[0] = jnp.concatenate(outs, axis=-1)


def swa_attention(q, k_prev, k_cur, v_prev, v_cur, sink, banded):
    b, t, _ = q.shape
    qb = min(SWA_QBLOCK, t)
    per = qb // WINDOW
    prev_map = (lambda bi, i: (bi, jnp.maximum(i * per - 1, 0), 0)) if banded else (lambda bi, i: (bi, 0, 0))
    cur = lambda bi, i: (bi, i, 0)
    return pl.pallas_call(
        functools.partial(_swa_kernel, banded=banded, qb=qb),
        out_shape=jax.ShapeDtypeStruct((b, t, SWA_Q), jnp.float32),
        grid=(b, t // qb),
        in_specs=[pl.BlockSpec(memory_space=pltpu.SMEM),
                  pl.BlockSpec((1, qb, SWA_Q), cur),
                  pl.BlockSpec((1, WINDOW, SWA_KV), prev_map), pl.BlockSpec((1, qb, SWA_KV), cur),
                  pl.BlockSpec((1, WINDOW, SWA_KV), prev_map), pl.BlockSpec((1, qb, SWA_KV), cur)],
        out_specs=pl.BlockSpec((1, qb, SWA_Q), cur),
        compiler_params=_cparams(2), name="swa_attention",
    )(sink, q, k_prev, k_cur, v_prev, v_cur)


def _out_kernel(x_ref, a_ref, b_ref, w_ref, o_ref):
    ka = a_ref.shape[-1]
    o_ref[...] = (x_ref[...]
                  + jnp.dot(a_ref[...].astype(jnp.bfloat16), w_ref[:ka, :], preferred_element_type=jnp.float32)
                  + jnp.dot(b_ref[...].astype(jnp.bfloat16), w_ref[ka:, :], preferred_element_type=jnp.float32))


def out_proj(x2d, a, b, w_bf16):
    n = x2d.shape[0]
    tm = min(ROW_BLOCK, n)
    row = lambda i: (i, 0)
    return pl.pallas_call(
        _out_kernel,
        out_shape=jax.ShapeDtypeStruct((n, D_MODEL), jnp.float32),
        grid=(n // tm,),
        in_specs=[pl.BlockSpec((tm, D_MODEL), row), pl.BlockSpec((tm, a.shape[1]), row),
                  pl.BlockSpec((tm, b.shape[1]), row), pl.BlockSpec(w_bf16.shape, lambda i: (0, 0))],
        out_specs=pl.BlockSpec((tm, D_MODEL), row),
        compiler_params=_cparams(), name="out_proj",
    )(x2d, a, b, w_bf16)


def s5_discretize(lam_re, lam_im, log_dt, b_re, b_im, c_re, c_im):
    lr = jnp.minimum(lam_re, -1e-4)
    li = lam_im
    dt = jnp.exp(log_dt)[:, None]
    mag = jnp.exp(lr * dt)
    ang = li * dt
    ab_re, ab_im = mag * jnp.cos(ang), mag * jnp.sin(ang)
    den = lr * lr + li * li
    nr, ni = ab_re - 1.0, ab_im
    f_re = (nr * lr + ni * li) / den
    f_im = (ni * lr - nr * li) / den
    bb_re = f_re[..., None] * b_re - f_im[..., None] * b_im
    bb_im = f_re[..., None] * b_im + f_im[..., None] * b_re
    eye = jnp.eye(S5_GROUPS, dtype=jnp.float32)

    def embed_b(bb):
        return jnp.einsum('gnc,gh->gchn', bb, eye).reshape(S5_WIDTH, S5_FLAT)

    def embed_c(c):
        return jnp.einsum('gcn,gh->gnhc', c, eye).reshape(S5_FLAT, S5_WIDTH)

    wb = jnp.concatenate([embed_b(bb_re), embed_b(bb_im)], axis=1)
    wc = jnp.concatenate([embed_c(c_re), -embed_c(c_im)], axis=0)
    lam = jnp.stack([ab_re.reshape(S5_FLAT), ab_im.reshape(S5_FLAT)])
    return lam, wb.astype(jnp.bfloat16), wc.astype(jnp.bfloat16)


def _s5_kernel(u_ref, lam_ref, wb_ref, wc_ref, d_ref, wglu_ref, bglu_ref, h0r_ref, h0i_ref,
               o_ref, hr_ref, hi_ref, hbuf, *, nb, steps, width):
    @pl.when(pl.program_id(0) == 0)
    def _():
        hr_ref[...] = h0r_ref[...]
        hi_ref[...] = h0i_ref[...]

    u = u_ref[...]
    hbuf[...] = jnp.dot(u.astype(jnp.bfloat16), wb_ref[...], preferred_element_type=jnp.float32)
    for c0 in range(0, S5_FLAT, width):
        cre = slice(c0, c0 + width)
        cim = slice(S5_FLAT + c0, S5_FLAT + c0 + width)
        lr = jnp.broadcast_to(lam_ref[0:1, cre], (nb, width))
        li = jnp.broadcast_to(lam_ref[1:2, cre], (nb, width))

        def step(t, carry):
            hr, hi = carry
            rows = pl.ds(pl.multiple_of(t * nb, nb), nb)
            nhr = lr * hr - li * hi + hbuf[rows, cre]
            nhi = lr * hi + li * hr + hbuf[rows, cim]
            hbuf[rows, cre] = nhr
            hbuf[rows, cim] = nhi
            return nhr, nhi

        hr, hi = lax.fori_loop(0, steps, step, (hr_ref[:, cre], hi_ref[:, cre]))
        hr_ref[:, cre] = hr
        hi_ref[:, cre] = hi
    y = jnp.dot(hbuf[...].astype(jnp.bfloat16), wc_ref[...], preferred_element_type=jnp.float32) + d_ref[...] * u
    z = 0.5 * y * (1.0 + jnp.tanh(math.sqrt(2.0 / math.pi) * (y + 0.044715 * (y * y * y))))
    gate = jnp.dot(z.astype(jnp.bfloat16), wglu_ref[...], preferred_element_type=jnp.float32) + bglu_ref[...]
    o_ref[...] = z * (1.0 / (1.0 + jnp.exp(-gate)))


def s5_mixer(u_tm, nb, lam, wb, wc, d_skip, w_glu_bf16, b_glu, h0r, h0i, steps):
    rows = u_tm.shape[0]
    t_total = rows // nb
    width = min(S5_FLAT, max(LANES, (S5_CARRY_VREGS * SUBLANES * LANES) // (2 * nb)))
    blk = steps * nb
    const = lambda i: (0, 0)
    return pl.pallas_call(
        functools.partial(_s5_kernel, nb=nb, steps=steps, width=width),
        out_shape=(jax.ShapeDtypeStruct((rows, S5_WIDTH), jnp.float32),
                   jax.ShapeDtypeStruct((nb, S5_FLAT), jnp.float32),
                   jax.ShapeDtypeStruct((nb, S5_FLAT), jnp.float32)),
        grid=(t_total // steps,),
        in_specs=[pl.BlockSpec((blk, S5_WIDTH), lambda i: (i, 0)),
                  pl.BlockSpec((2, S5_FLAT), const),
                  pl.BlockSpec((S5_WIDTH, 2 * S5_FLAT), const),
                  pl.BlockSpec((2 * S5_FLAT, S5_WIDTH), const),
                  pl.BlockSpec((1, S5_WIDTH), const),
                  pl.BlockSpec((S5_WIDTH, S5_WIDTH), const),
                  pl.BlockSpec((1, S5_WIDTH), const),
                  pl.BlockSpec((nb, S5_FLAT), const),
                  pl.BlockSpec((nb, S5_FLAT), const)],
        out_specs=(pl.BlockSpec((blk, S5_WIDTH), lambda i: (i, 0)),
                   pl.BlockSpec((nb, S5_FLAT), const),
                   pl.BlockSpec((nb, S5_FLAT), const)),
        scratch_shapes=[pltpu.VMEM((blk, 2 * S5_FLAT), jnp.float32)],
        compiler_params=_cparams(), name="s5_mixer",
    )(u_tm, lam, wb, wc, d_skip.reshape(1, S5_WIDTH), w_glu_bf16, b_glu.reshape(1, S5_WIDTH), h0r, h0i)


def _in_odd_kernel(x_ref, g_ref, w_ref, qn_ref, kvn_ref, wuq_ref, kcos_ref, kslo_ref, kshi_ref,
                   qcos_ref, qslo_ref, qshi_ref, qnope_ref, qpe_ref, c_ref, kp_ref, u_ref):
    xn = _rms(x_ref[...], g_ref[...])
    proj = jnp.dot(xn.astype(jnp.bfloat16), w_ref[...], preferred_element_type=jnp.float32)
    cqn = _rms(proj[:, :Q_LORA], qn_ref[...])
    q = jnp.dot(cqn.astype(jnp.bfloat16), wuq_ref[...], preferred_element_type=jnp.float32)
    qnope_ref[...] = q[:, :MLA_QNOPE]
    qpe_ref[...] = _rope_lanes(q[:, MLA_QNOPE:], qcos_ref[...], qslo_ref[...], qshi_ref[...], ROPE_DIM // 2)
    c_ref[...] = _rms(proj[:, Q_LORA:ODD_U0], kvn_ref[...])
    kp = _rope_lanes(proj[:, ODD_KPE0:], kcos_ref[...], kslo_ref[...], kshi_ref[...], ROPE_DIM // 2)
    kp_ref[...] = kp[:, :ROPE_DIM]
    u_ref[...] = proj[:, ODD_U0:ODD_KPE0]


def in_odd(x2d, g, w_perm_bf16, q_norm, kv_norm, wuq_perm_bf16, ktabs, qtabs):
    n = x2d.shape[0]
    tm = min(ROW_BLOCK, n)
    nt = ktabs[0].shape[0] // tm
    row = lambda i: (i, 0)
    const = lambda i: (0, 0)
    tab = lambda i: (i % nt, 0)
    return pl.pallas_call(
        _in_odd_kernel,
        out_shape=(jax.ShapeDtypeStruct((n, MLA_QNOPE), jnp.float32), jax.ShapeDtypeStruct((n, MLA_QPE), jnp.float32),
                   jax.ShapeDtypeStruct((n, KV_LORA), jnp.float32), jax.ShapeDtypeStruct((n, ROPE_DIM), jnp.float32),
                   jax.ShapeDtypeStruct((n, POOL_WIDTH), jnp.float32)),
        grid=(n // tm,),
        in_specs=[pl.BlockSpec((tm, D_MODEL), row), pl.BlockSpec((1, D_MODEL), const),
                  pl.BlockSpec((D_MODEL, ODD_IN_PAD), const),
                  pl.BlockSpec((1, Q_LORA), const), pl.BlockSpec((1, KV_LORA), const),
                  pl.BlockSpec((Q_LORA, MLA_QNOPE + MLA_QPE), const),
                  pl.BlockSpec((tm, KPE_PAD), tab), pl.BlockSpec((tm, KPE_PAD), tab), pl.BlockSpec((tm, KPE_PAD), tab),
                  pl.BlockSpec((tm, MLA_QPE), tab), pl.BlockSpec((tm, MLA_QPE), tab), pl.BlockSpec((tm, MLA_QPE), tab)],
        out_specs=(pl.BlockSpec((tm, MLA_QNOPE), row), pl.BlockSpec((tm, MLA_QPE), row), pl.BlockSpec((tm, KV_LORA), row),
                   pl.BlockSpec((tm, ROPE_DIM), row), pl.BlockSpec((tm, POOL_WIDTH), row)),
        compiler_params=_cparams(), name="in_odd",
    )(x2d, g.reshape(1, D_MODEL), w_perm_bf16, q_norm.reshape(1, Q_LORA), kv_norm.reshape(1, KV_LORA), wuq_perm_bf16,
      *ktabs, *qtabs)


def permute_odd_weights(w_in, w_uq):
    o1, o2 = Q_LORA + KV_LORA, Q_LORA + KV_LORA + ROPE_DIM
    kpe = jnp.pad(w_in[:, o1:o2], ((0, 0), (0, KPE_PAD - ROPE_DIM)))
    w_perm = jnp.concatenate([w_in[:, :o1], w_in[:, o2:], kpe], axis=1)
    wq = w_uq.reshape(Q_LORA, MLA_HEADS, NOPE_DIM + ROPE_DIM)
    wq_perm = jnp.concatenate([wq[:, :, :NOPE_DIM].reshape(Q_LORA, MLA_QNOPE),
                               wq[:, :, NOPE_DIM:].reshape(Q_LORA, MLA_QPE)], axis=1)
    return w_perm.astype(jnp.bfloat16), wq_perm.astype(jnp.bfloat16)


def _mla_kernel(qn_ref, qp_ref, c_ref, kp_ref, wuk_ref, wuv_ref, o_ref, *scratch, causal, qb, kb, n_keys, hpc):
    i = pl.program_id(1)
    n_chains = MLA_HEADS // hpc
    qa_s, qp_s, m_s, l_s, acc_s = (scratch[k * n_chains:(k + 1) * n_chains] for k in range(5))
    qn = qn_ref[0].astype(jnp.bfloat16)
    qp = qp_ref[0].astype(jnp.bfloat16)
    for g in range(n_chains):
        for hh in range(hpc):
            h = g * hpc + hh
            rows = slice(hh * qb, (hh + 1) * qb)
            qa_s[g][rows, :] = jnp.dot(qn[:, h * NOPE_DIM:(h + 1) * NOPE_DIM], wuk_ref[h],
                                       preferred_element_type=jnp.float32).astype(jnp.bfloat16)
            qp_s[g][rows, :] = qp[:, h * ROPE_DIM:(h + 1) * ROPE_DIM]
        m_s[g][...] = jnp.full(m_s[g].shape, NEG_INF, jnp.float32)
        l_s[g][...] = jnp.zeros(l_s[g].shape, jnp.float32)
        acc_s[g][...] = jnp.zeros(acc_s[g].shape, jnp.float32)
    if causal:
        qpos = i * qb + lax.broadcasted_iota(jnp.int32, (hpc * qb, kb), 0) % qb
        limit = (qpos // CHUNK + 1) * CHUNK
        nblk = ((i + 1) * qb + kb - 1) // kb
    else:
        limit = n_keys
        nblk = (n_keys + kb - 1) // kb
    kidx0 = lax.broadcasted_iota(jnp.int32, (hpc * qb, kb), 1)

    def body(j, carry):
        rows = pl.ds(pl.multiple_of(j * kb, kb), kb)
        cb = c_ref[0, rows, :]
        kpb = kp_ref[0, rows, :]
        visible = kidx0 + j * kb < limit
        for g in range(n_chains):
            s = (lax.dot_general(qa_s[g][...], cb, (((1,), (1,)), ((), ())), preferred_element_type=jnp.float32)
                 + lax.dot_general(qp_s[g][...], kpb, (((1,), (1,)), ((), ())),
                                   preferred_element_type=jnp.float32)) * MLA_SCALE
            s = jnp.where(visible, s, NEG_INF)
            m_old = m_s[g][...]
            m_new = jnp.maximum(m_old, jnp.max(s, axis=-1, keepdims=True))
            alpha = jnp.exp(m_old - m_new)
            p = jnp.exp(s - m_new)
            m_s[g][...] = m_new
            l_s[g][...] = alpha * l_s[g][...] + jnp.sum(p, axis=-1, keepdims=True)
            acc_s[g][...] = alpha * acc_s[g][...] + jnp.dot(p.astype(jnp.bfloat16), cb, preferred_element_type=jnp.float32)
        return carry

    lax.fori_loop(0, nblk, body, 0)
    outs = []
    for g in range(n_chains):
        o_lat = (acc_s[g][...] / l_s[g][...]).astype(jnp.bfloat16)
        for hh in range(hpc):
            outs.append(jnp.dot(o_lat[hh * qb:(hh + 1) * qb, :], wuv_ref[g * hpc + hh],
                                preferred_element_type=jnp.float32))
    o_ref[0] = jnp.concatenate(outs, axis=-1)


def mla_attention(q_nope, q_pe, c_keys_bf16, kp_keys_bf16, wuk_h, wuv_h, causal, n_keys):
    b, t, _ = q_nope.shape
    tk = c_keys_bf16.shape[1]
    qb = min(MLA_QBLOCK, t)
    kb = min(MLA_KBLOCK, tk)
    hpc = max(1, min(MLA_HEADS, MLA_CHAIN_ROWS // qb))
    n_chains, rows = MLA_HEADS // hpc, hpc * qb
    cur = lambda bi, i: (bi, i, 0)
    whole = lambda bi, i: (bi, 0, 0)
    const3 = lambda bi, i: (0, 0, 0)
    return pl.pallas_call(
        functools.partial(_mla_kernel, causal=causal, qb=qb, kb=kb, n_keys=n_keys, hpc=hpc),
        out_shape=jax.ShapeDtypeStruct((b, t, MLA_HEADS * V_DIM), jnp.float32),
        grid=(b, t // qb),
        in_specs=[pl.BlockSpec((1, qb, MLA_QNOPE), cur), pl.BlockSpec((1, qb, MLA_QPE), cur),
                  pl.BlockSpec((1, tk, KV_LORA), whole), pl.BlockSpec((1, tk, ROPE_DIM), whole),
                  pl.BlockSpec((MLA_HEADS, NOPE_DIM, KV_LORA), const3), pl.BlockSpec((MLA_HEADS, KV_LORA, V_DIM), const3)],
        out_specs=pl.BlockSpec((1, qb, MLA_HEADS * V_DIM), cur),
        scratch_shapes=([pltpu.VMEM((rows, KV_LORA), jnp.bfloat16)] * n_chains
                        + [pltpu.VMEM((rows, ROPE_DIM), jnp.bfloat16)] * n_chains
                        + [pltpu.VMEM((rows, 1), jnp.float32)] * (2 * n_chains)
                        + [pltpu.VMEM((rows, KV_LORA), jnp.float32)] * n_chains),
        compiler_params=_cparams(2), name="mla_attention",
    )(q_nope, q_pe, c_keys_bf16, kp_keys_bf16, wuk_h, wuv_h)


def _pool_kernel(u_ref, prev_ref, w_ref, scale_ref, o_ref, new_ref, ext, *, tm, pos0):
    j = pl.program_id(1)

    @pl.when(j == 0)
    def _():
        ext[0:1, :] = jnp.zeros((1, POOL_WIDTH), jnp.float32)
        ext[1:POOL_MAX, :] = prev_ref[0]

    @pl.when(j > 0)
    def _():
        ext[0:POOL_MAX, :] = ext[tm:tm + POOL_MAX, :]

    u = u_ref[0]
    ext[POOL_MAX:POOL_MAX + tm, :] = u
    pos = pos0 + j * tm + lax.broadcasted_iota(jnp.int32, (tm, POOL_GROUP), 0)
    outs = []
    for gi, w in enumerate(POOL_WINDOWS):
        cols = slice(gi * POOL_GROUP, (gi + 1) * POOL_GROUP)
        tot = u[:, cols]
        for d in range(1, w):
            tot = tot + ext[POOL_MAX - d:POOL_MAX - d + tm, cols]
        cnt = jnp.minimum(pos + 1, w).astype(jnp.float32)
        m = tot / cnt - u[:, cols]
        outs.append(jnp.dot(m.astype(jnp.bfloat16), w_ref[gi], preferred_element_type=jnp.float32))
    o_ref[0] = jnp.concatenate(outs, axis=-1) * scale_ref[...]
    new_ref[0] = ext[tm + 1:tm + POOL_MAX, :]


def pool_mixer(u, prev, pool_w_bf16, pool_scale, pos0):
    b, t, _ = u.shape
    tm = min(ROW_BLOCK, t)
    cur = lambda bi, j: (bi, j, 0)
    per_b = lambda bi, j: (bi, 0, 0)
    return pl.pallas_call(
        functools.partial(_pool_kernel, tm=tm, pos0=pos0),
        out_shape=(jax.ShapeDtypeStruct((b, t, POOL_WIDTH), jnp.float32),
                   jax.ShapeDtypeStruct((b, POOL_BUF, POOL_WIDTH), jnp.float32)),
        grid=(b, t // tm),
        in_specs=[pl.BlockSpec((1, tm, POOL_WIDTH), cur), pl.BlockSpec((1, POOL_BUF, POOL_WIDTH), per_b),
                  pl.BlockSpec((len(POOL_WINDOWS), POOL_GROUP, POOL_GROUP), lambda bi, j: (0, 0, 0)),
                  pl.BlockSpec((1, POOL_WIDTH), lambda bi, j: (0, 0))],
        out_specs=(pl.BlockSpec((1, tm, POOL_WIDTH), cur), pl.BlockSpec((1, POOL_BUF, POOL_WIDTH), per_b)),
        scratch_shapes=[pltpu.VMEM((tm + POOL_MAX, POOL_WIDTH), jnp.float32)],
        compiler_params=_cparams(2), name="pool_mixer",
    )(u, prev, pool_w_bf16, pool_scale.reshape(1, POOL_WIDTH))


def _top16_rows(s, n_rows):
    iota = lax.broadcasted_iota(jnp.int32, s.shape, 0).astype(jnp.float32)
    vals, idxs = [], []
    for _ in range(PEER_TOPK):
        m = jnp.max(s, axis=0, keepdims=True)
        idx = jnp.min(jnp.where(s == m, iota, float(n_rows)), axis=0, keepdims=True)
        vals.append(m)
        idxs.append(idx)
        s = jnp.where(iota == idx, -jnp.inf, s)
    return jnp.concatenate(vals, axis=0), jnp.concatenate(idxs, axis=0)


def _route_kernel(x_ref, g_ref, wq_ref, keys_ref, xn_ref, pair_ref, gate_ref, code_t, gate_t):
    xn = _rms(x_ref[...], g_ref[...])
    xn_ref[...] = xn
    xb = xn.astype(jnp.bfloat16)

    def head_step(hs, carry):
        for hh in range(ROUTE_HEADS_PER_STEP):
            one_head(hs * ROUTE_HEADS_PER_STEP + hh)
        return carry

    def one_head(h):
        qb = jnp.dot(xb, wq_ref[h], preferred_element_type=jnp.float32).astype(jnp.bfloat16)
        sv, si = [], []
        for p in range(2):
            s = lax.dot_general(keys_ref[h * 2 + p], qb[:, p * D_HALF:(p + 1) * D_HALF],
                                (((1,), (1,)), ((), ())), preferred_element_type=jnp.float32)
            v, i = _top16_rows(s, N_KEYS)
            sv.append(v)
            si.append(i)
        cand = jnp.concatenate([sv[0][a:a + 1] + sv[1][:nb] for a, nb in enumerate(PAIR_COLS)]
                               + [jnp.full((PAIR_PAD, s.shape[1]), -jnp.inf, jnp.float32)], axis=0)
        eid = jnp.concatenate([si[0][a:a + 1] * float(N_KEYS) + si[1][:nb] for a, nb in enumerate(PAIR_COLS)]
                              + [jnp.zeros((PAIR_PAD, s.shape[1]), jnp.float32)], axis=0)
        iota = lax.broadcasted_iota(jnp.int32, cand.shape, 0).astype(jnp.float32)
        cv, ce = [], []
        for _ in range(PEER_TOPK):
            m = jnp.max(cand, axis=0, keepdims=True)
            idx = jnp.min(jnp.where(cand == m, iota, float(PAIR_ROWS)), axis=0, keepdims=True)
            hit = iota == idx
            cv.append(m)
            ce.append(jnp.max(jnp.where(hit, eid, -1.0), axis=0, keepdims=True))
            cand = jnp.where(hit, -jnp.inf, cand)
        cv = jnp.concatenate(cv, axis=0)
        ce = jnp.concatenate(ce, axis=0).astype(jnp.int32)
        e = jnp.exp(cv - cv[0:1])
        rows = pl.ds(pl.multiple_of(h * PEER_TOPK, PEER_TOPK), PEER_TOPK)
        gate_t[rows, :] = e / jnp.sum(e, axis=0, keepdims=True)
        code_t[rows, :] = ((ce & (HALF_EXPERTS - 1)) << 3) | (ce >> 13)

    lax.fori_loop(0, PEER_HEADS // ROUTE_HEADS_PER_STEP, head_step, 0)
    ca, cb = code_t[:NSEL // 2, :], code_t[NSEL // 2:, :]
    pair_ref[...] = (ca | (((cb & 1) ^ 1) << 1) | ((cb & -8) << 16)).T
    gate_ref[...] = gate_t[...].T


def peer_route(x2d, g, wq_heads, keys_bf16, tb):
    n = x2d.shape[0]
    return pl.pallas_call(
        _route_kernel,
        out_shape=(jax.ShapeDtypeStruct((n, D_MODEL), jnp.float32),
                   jax.ShapeDtypeStruct((n, NSEL // 2), jnp.int32),
                   jax.ShapeDtypeStruct((n, NSEL), jnp.float32)),
        grid=(n // tb,),
        in_specs=[pl.BlockSpec((tb, D_MODEL), lambda i: (i, 0)),
                  pl.BlockSpec((1, D_MODEL), lambda i: (0, 0)),
                  pl.BlockSpec((PEER_HEADS, D_MODEL, D_KEY), lambda i: (0, 0, 0)),
                  pl.BlockSpec((PEER_HEADS * 2, N_KEYS, D_HALF), lambda i: (0, 0, 0))],
        out_specs=(pl.BlockSpec((tb, D_MODEL), lambda i: (i, 0)),
                   pl.BlockSpec((tb, NSEL // 2), lambda i: (i, 0)),
                   pl.BlockSpec((tb, NSEL), lambda i: (i, 0))),
        scratch_shapes=[pltpu.VMEM((NSEL, tb), jnp.int32), pltpu.VMEM((NSEL, tb), jnp.float32)],
        compiler_params=_cparams(), name="peer_route",
    )(x2d, g.reshape(1, D_MODEL), wq_heads, keys_bf16)


def pack_table(tab):
    b = lax.bitcast_convert_type(tab.astype(jnp.bfloat16), jnp.uint16).astype(jnp.uint32)
    w = b[:HALF_EXPERTS] | (b[HALF_EXPERTS:] << 16)
    return lax.bitcast_convert_type(w, jnp.int32).reshape(HALF_EXPERTS * ROW_TILE, LANES)


def _gather_pair(tab_ref, cab):
    wa = tab_ref[pl.ds(pl.multiple_of(cab & 0xFFF8, SUBLANES), SUBLANES), :]
    wb = tab_ref[pl.ds(pl.multiple_of(lax.shift_right_logical(cab, 16), SUBLANES), SUBLANES), :]
    cv = jnp.full((SUBLANES, LANES), cab, jnp.int32)
    sha = (cv & 1) << 4
    shb = (cv & 2) << 3
    return (lax.shift_right_logical(wa, sha) & 0xFFFF) | (lax.shift_left(wb, shb) & HI16)


def _gather_token(tab_ref, pair_ref, t, buf):
    for i in range(NSEL // 2):
        buf[i * ROW_TILE:(i + 1) * ROW_TILE, :] = _gather_pair(tab_ref, pair_ref[t, i])


def _grouped_token_loop(tb, gather, finish, group_a, group_b):
    for buf in group_b:
        buf[...] = jnp.zeros(buf.shape, jnp.int32)

    def body(i, carry):
        t0 = 2 * UPASS_GROUP * i
        for q in range(UPASS_GROUP):
            gather(t0 + q, group_a[q])
        finish(jnp.maximum(t0 - UPASS_GROUP, 0), group_b)
        for q in range(UPASS_GROUP):
            gather(t0 + UPASS_GROUP + q, group_b[q])
        finish(t0, group_a)
        return carry

    lax.fori_loop(0, tb // (2 * UPASS_GROUP), body, 0)
    finish(tb - UPASS_GROUP, group_b)


def _upass_kernel(pair_ref, gate_ref, xn_ref, tab_ref, w_ref, *bufs, tb):
    col = lax.broadcasted_iota(jnp.int32, (2 * ROW_TILE, SLOT_ROWS), 1)
    row = lax.broadcasted_iota(jnp.int32, (2 * ROW_TILE, SLOT_ROWS), 0)
    chunk_mask = (((col & 15) >> 1) == (row & 7)).astype(jnp.float32)
    c2 = lax.broadcasted_iota(jnp.int32, (SLOT_ROWS, NSEL), 0)
    j2 = lax.broadcasted_iota(jnp.int32, (SLOT_ROWS, NSEL), 1)
    fold = (j2 == (c2 >> 4) + ((c2 & 1) << 6)).astype(jnp.bfloat16)

    def gather(t, buf):
        _gather_token(tab_ref, pair_ref, t, buf)

    def finish(t0, group):
        zs = []
        for q, buf in enumerate(group):
            xt = xn_ref[pl.ds(pl.multiple_of((t0 + q) * ROW_TILE, ROW_TILE), ROW_TILE), :]
            xhi = xt.astype(jnp.bfloat16)
            xlo = (xt - xhi.astype(jnp.float32)).astype(jnp.bfloat16)
            x16 = jnp.concatenate([xhi, xlo], axis=0)
            us = pltpu.bitcast(buf[...], jnp.bfloat16)
            r = lax.dot_general(x16, us, (((1,), (1,)), ((), ())), preferred_element_type=jnp.float32)
            zs.append(jnp.sum(r * chunk_mask, axis=0, keepdims=True))
        z = jnp.concatenate(zs, axis=0)
        zhi = z.astype(jnp.bfloat16)
        zlo = (z - zhi.astype(jnp.float32)).astype(jnp.bfloat16)
        act = (jnp.dot(zhi, fold, preferred_element_type=jnp.float32)
               + jnp.dot(zlo, fold, preferred_element_type=jnp.float32))
        gelu = 0.5 * act * (1.0 + lax.erf(act * (1.0 / math.sqrt(2.0))))
        rows = pl.ds(pl.multiple_of(t0, UPASS_GROUP), UPASS_GROUP)
        w_ref[rows, :] = (gate_ref[rows, :] * gelu).astype(jnp.bfloat16)

    _grouped_token_loop(tb, gather, finish, bufs[:UPASS_GROUP], bufs[UPASS_GROUP:])


def _peer_pass_call(kernel_fn, out_shape, out_block, pairs, per_token, rows, tab_packed, tb, name):
    n = pairs.shape[0]
    return pl.pallas_call(
        functools.partial(kernel_fn, tb=tb),
        out_shape=out_shape,
        grid=(n // tb,),
        in_specs=[pl.BlockSpec((tb, NSEL // 2), lambda i: (i, 0), memory_space=pltpu.SMEM),
                  pl.BlockSpec((tb, NSEL), lambda i: (i, 0)),
                  pl.BlockSpec((tb * ROW_TILE, LANES), lambda i: (i, 0)),
                  pl.BlockSpec((HALF_EXPERTS * ROW_TILE, LANES), lambda i: (0, 0), pipeline_mode=pl.Buffered(1))],
        out_specs=pl.BlockSpec(out_block, lambda i: (i, 0)),
        scratch_shapes=[pltpu.VMEM((SLOT_ROWS // 2, LANES), jnp.int32)] * (2 * UPASS_GROUP),
        compiler_params=_cparams(), name=name,
    )(pairs, per_token, rows, tab_packed)


def peer_upass(pairs, gate, xn_rows, tab_packed, tb):
    n = pairs.shape[0]
    return _peer_pass_call(_upass_kernel, jax.ShapeDtypeStruct((n, NSEL), jnp.bfloat16), (tb, NSEL),
                           pairs, gate, xn_rows, tab_packed, tb, "peer_upass")


def _vpass_kernel(pair_ref, w_ref, x_ref, tab_ref, o_ref, *bufs, tb):
    col = lax.broadcasted_iota(jnp.int32, (ROW_TILE, SLOT_ROWS), 1)
    row = lax.broadcasted_iota(jnp.int32, (ROW_TILE, SLOT_ROWS), 0)
    chunk_mask = (((col & 15) >> 1) == row).astype(jnp.float32)
    j2 = lax.broadcasted_iota(jnp.int32, (NSEL, SLOT_ROWS), 0)
    c2 = lax.broadcasted_iota(jnp.int32, (NSEL, SLOT_ROWS), 1)
    spread = (j2 == (c2 >> 4) + ((c2 & 1) << 6)).astype(jnp.bfloat16)

    def gather(t, buf):
        _gather_token(tab_ref, pair_ref, t, buf)

    def finish(t0, group):
        rows = pl.ds(pl.multiple_of(t0, UPASS_GROUP), UPASS_GROUP)
        wexp = jnp.dot(w_ref[rows, :], spread, preferred_element_type=jnp.float32)
        for q, buf in enumerate(group):
            wsel = (jnp.broadcast_to(wexp[q:q + 1], (ROW_TILE, SLOT_ROWS)) * chunk_mask).astype(jnp.bfloat16)
            vs = pltpu.bitcast(buf[...], jnp.bfloat16)
            sl = pl.ds(pl.multiple_of((t0 + q) * ROW_TILE, ROW_TILE), ROW_TILE)
            o_ref[sl, :] = x_ref[sl, :] + jnp.dot(wsel, vs, preferred_element_type=jnp.float32)

    _grouped_token_loop(tb, gather, finish, bufs[:UPASS_GROUP], bufs[UPASS_GROUP:])


def peer_vpass(pairs, w_bf16, x_rows, tab_packed, tb):
    n = pairs.shape[0]
    return _peer_pass_call(_vpass_kernel, jax.ShapeDtypeStruct((n * ROW_TILE, LANES), jnp.float32),
                           (tb * ROW_TILE, LANES), pairs, w_bf16, x_rows, tab_packed, tb, "peer_vpass")


def peer_block(x, g, wq_heads, keys_bf16, u_packed, v_packed):
    shp = x.shape
    x2d = x.reshape(-1, D_MODEL)
    n = x2d.shape[0]
    xn, pairs, gate = peer_route(x2d, g, wq_heads, keys_bf16, min(PEER_ROUTE_BLOCK, n))
    w = peer_upass(pairs, gate, xn.reshape(n * ROW_TILE, LANES), u_packed, min(PEER_PASS_BLOCK, n))
    out = peer_vpass(pairs, w, x2d.reshape(n * ROW_TILE, LANES), v_packed, min(PEER_PASS_BLOCK, n))
    return out.reshape(shp)


def _rms_kernel(x_ref, g_ref, o_ref):
    o_ref[...] = _rms(x_ref[...], g_ref[...])


def rmsnorm_pallas(x, g):
    shp = x.shape
    xt = x.reshape(-1, shp[-1])
    n = xt.shape[0]
    tm = min(ROW_BLOCK, n)
    out = pl.pallas_call(
        _rms_kernel,
        out_shape=jax.ShapeDtypeStruct(xt.shape, xt.dtype),
        grid=(n // tm,),
        in_specs=[pl.BlockSpec((tm, shp[-1]), lambda i: (i, 0)),
                  pl.BlockSpec((1, shp[-1]), lambda i: (0, 0))],
        out_specs=pl.BlockSpec((tm, shp[-1]), lambda i: (i, 0)),
        compiler_params=_cparams(), name="final_rmsnorm",
    )(xt, g.reshape(1, -1))
    return out.reshape(shp)


def even_layer(x, pos0, k_prev, v_prev, h0r, h0i, norm_g, w_in, w_out, sink, s5_params, d_skip, w_glu, b_glu):
    b, t, _ = x.shape
    n = b * t
    x2d = x.reshape(n, D_MODEL)
    tabs = rope_tables(pos0 + jnp.arange(t), ROT_DIM, HEAD_DIM, SWA_Q, max(1, min(ROW_BLOCK, n) // t))
    q, k, v, u = in_even(x2d, norm_g, w_in.astype(jnp.bfloat16), tabs)
    q3, k3, v3 = q.reshape(b, t, SWA_Q), k.reshape(b, t, SWA_KV), v.reshape(b, t, SWA_KV)
    if k_prev is None:
        att = swa_attention(q3, k3, k3, v3, v3, sink, True)
        k_all, v_all = k3, v3
        h0r = jnp.zeros((b, S5_FLAT), jnp.float32)
        h0i = jnp.zeros((b, S5_FLAT), jnp.float32)
    else:
        kp, vp = k_prev.reshape(b, WINDOW, SWA_KV), v_prev.reshape(b, WINDOW, SWA_KV)
        att = swa_attention(q3, kp, k3, vp, v3, sink, False)
        k_all, v_all = jnp.concatenate([kp, k3], axis=1), jnp.concatenate([vp, v3], axis=1)
        h0r, h0i = h0r.reshape(b, S5_FLAT), h0i.reshape(b, S5_FLAT)
    lam, wb, wc = s5_discretize(*s5_params)
    u_tm = u.reshape(b, t, S5_WIDTH).transpose(1, 0, 2).reshape(n, S5_WIDTH)
    s5o_tm, hre, him = s5_mixer(u_tm, b, lam, wb, wc, d_skip, w_glu.astype(jnp.bfloat16), b_glu, h0r, h0i,
                                min(t, S5_STEPS))
    s5o = s5o_tm.reshape(t, b, S5_WIDTH).transpose(1, 0, 2).reshape(n, S5_WIDTH)
    out = out_proj(x2d, att.reshape(n, SWA_Q), s5o, w_out.astype(jnp.bfloat16))
    return (out.reshape(b, t, D_MODEL),
            k_all[:, -WINDOW:].reshape(b, WINDOW, SWA_KV_HEADS, HEAD_DIM),
            v_all[:, -WINDOW:].reshape(b, WINDOW, SWA_KV_HEADS, HEAD_DIM),
            hre.reshape(b, S5_GROUPS, S5_STATE), him.reshape(b, S5_GROUPS, S5_STATE))


def odd_layer(x, pos0, pool_prev, ckv_prev, kpe_prev, norm_g, w_in, w_out, pool_w, pool_scale,
              q_norm, kv_norm, w_uq, w_uk, w_uv):
    b, t, _ = x.shape
    n = b * t
    x2d = x.reshape(n, D_MODEL)
    reps = max(1, min(ROW_BLOCK, n) // t)
    pos = pos0 + jnp.arange(t)
    ktabs = rope_tables(pos, ROPE_DIM, KPE_PAD, KPE_PAD, reps)
    qtabs = rope_tables(pos, ROPE_DIM, ROPE_DIM, MLA_QPE, reps)
    w_perm, wuq_perm = permute_odd_weights(w_in, w_uq)
    qnope, qpe, c, kp, u = in_odd(x2d, norm_g, w_perm, q_norm, kv_norm, wuq_perm, ktabs, qtabs)
    c3, kp3 = c.reshape(b, t, KV_LORA), kp.reshape(b, t, ROPE_DIM)
    wuk_h = w_uk.transpose(1, 2, 0).astype(jnp.bfloat16)
    wuv_h = w_uv.transpose(1, 0, 2).astype(jnp.bfloat16)
    if ckv_prev is None:
        ck, kk, causal, n_keys = c3, kp3, True, t
        pool_prev = jnp.zeros((b, POOL_BUF, POOL_WIDTH), jnp.float32)
    else:
        ck, kk, causal = jnp.concatenate([ckv_prev, c3], axis=1), jnp.concatenate([kpe_prev, kp3], axis=1), False
        n_keys = ck.shape[1]
        pad = -n_keys % min(MLA_KBLOCK, n_keys)
        ck, kk = jnp.pad(ck, ((0, 0), (0, pad), (0, 0))), jnp.pad(kk, ((0, 0), (0, pad), (0, 0)))
    mla = mla_attention(qnope.reshape(b, t, MLA_QNOPE), qpe.reshape(b, t, MLA_QPE), ck.astype(jnp.bfloat16),
                        kk.astype(jnp.bfloat16), wuk_h, wuv_h, causal, n_keys)
    pool_out, pool_new = pool_mixer(u.reshape(b, t, POOL_WIDTH), pool_prev, pool_w.astype(jnp.bfloat16), pool_scale, pos0)
    out = out_proj(x2d, pool_out.reshape(n, POOL_WIDTH), mla.reshape(n, MLA_HEADS * V_DIM), w_out.astype(jnp.bfloat16))
    return out.reshape(b, t, D_MODEL), pool_new, c3, kp3


def kernel(x_prompt, x_sample, cache_swa_k, cache_swa_v, state_ssm_re, state_ssm_im, state_pool,
           cache_mla_ckv, cache_mla_kpe, norm_mix, norm_ffn, norm_final, w_in_even, w_out_even,
           swa_sink, s5_lam_re, s5_lam_im, s5_log_dt, s5_b_re, s5_b_im, s5_c_re, s5_c_im, s5_d,
           s5_w_glu, s5_b_glu, w_in_odd, w_out_odd, pool_w, pool_scale, mla_q_norm, mla_kv_norm,
           mla_w_uq, mla_w_uk, mla_w_uv, peer_w_q, peer_keys, peer_u, peer_v):
    xp, xs = x_prompt, x_sample
    kp_l, vp_l, rp_l, ip_l, poolp_l, cp_l, ep_l = [], [], [], [], [], [], []
    ks_l, vs_l, rs_l, is_l, pools_l, cs_l, es_l = [], [], [], [], [], [], []
    for layer in range(DEPTH):
        i = layer // 2
        if layer % 2 == 0:
            s5_params = (s5_lam_re[i], s5_lam_im[i], s5_log_dt[i], s5_b_re[i], s5_b_im[i], s5_c_re[i], s5_c_im[i])
            ew = (norm_mix[layer], w_in_even[i], w_out_even[i], swa_sink[i], s5_params, s5_d[i], s5_w_glu[i], s5_b_glu[i])
            xp, k1, v1, r1, i1 = even_layer(xp, 0, None, None, None, None, *ew)
            xs, k2, v2, r2, i2 = even_layer(xs, PAST_LEN, cache_swa_k[i], cache_swa_v[i],
                                            state_ssm_re[i], state_ssm_im[i], *ew)
            kp_l.append(k1); vp_l.append(v1); rp_l.append(r1); ip_l.append(i1)
            ks_l.append(k2); vs_l.append(v2); rs_l.append(r2); is_l.append(i2)
        else:
            ow = (norm_mix[layer], w_in_odd[i], w_out_odd[i], pool_w[i], pool_scale[i], mla_q_norm[i], mla_kv_norm[i],
                  mla_w_uq[i], mla_w_uk[i], mla_w_uv[i])
            xp, p1, c1, e1 = odd_layer(xp, 0, None, None, None, *ow)
            xs, p2, c2, e2 = odd_layer(xs, PAST_LEN, state_pool[i], cache_mla_ckv[i], cache_mla_kpe[i], *ow)
            poolp_l.append(p1); cp_l.append(c1); ep_l.append(e1)
            pools_l.append(p2); cs_l.append(c2); es_l.append(e2)
        wq_heads = peer_w_q[layer].reshape(D_MODEL, PEER_HEADS, D_KEY).transpose(1, 0, 2).astype(jnp.bfloat16)
        keys_bf16 = peer_keys[layer].reshape(PEER_HEADS * 2, N_KEYS, D_HALF).astype(jnp.bfloat16)
        u_packed, v_packed = pack_table(peer_u[layer]), pack_table(peer_v[layer])
        xp = peer_block(xp, norm_ffn[layer], wq_heads, keys_bf16, u_packed, v_packed)
        xs = peer_block(xs, norm_ffn[layer], wq_heads, keys_bf16, u_packed, v_packed)
    y_prompt = rmsnorm_pallas(xp, norm_final)
    y_sample = rmsnorm_pallas(xs, norm_final)
    return (y_prompt, y_sample,
            jnp.stack(kp_l), jnp.stack(vp_l), jnp.stack(rp_l), jnp.stack(ip_l),
            jnp.stack(poolp_l), jnp.stack(cp_l), jnp.stack(ep_l),
            jnp.stack(ks_l), jnp.stack(vs_l), jnp.stack(rs_l), jnp.stack(is_l),
            jnp.stack(pools_l), jnp.stack(cs_l), jnp.stack(es_l))
```

```python
import functools
import math
import jax
import jax.numpy as jnp
from jax import lax
from jax.experimental import pallas as pl
from jax.experimental.pallas import tpu as pltpu

D_MODEL = 1024
DEPTH = 2
PAST_LEN = 2048

CHUNK = 64
RMS_EPS = 1e-6
ROPE_THETA = 500000.0
NEG_INF = -1e30

SWA_HEADS = 8
SWA_KV_HEADS = 2
SWA_GROUP = SWA_HEADS // SWA_KV_HEADS
HEAD_DIM = 64
ROT_DIM = HEAD_DIM // 4
WINDOW = 128
SWA_Q = SWA_HEADS * HEAD_DIM
SWA_KV = SWA_KV_HEADS * HEAD_DIM
SWA_SCALE = HEAD_DIM ** -0.5

S5_WIDTH = 512
S5_GROUP = 16
S5_GROUPS = S5_WIDTH // S5_GROUP
S5_STATE = 64
S5_FLAT = S5_GROUPS * S5_STATE

POOL_WIDTH = 512
POOL_WINDOWS = (2, 4, 8, 16)
POOL_GROUP = POOL_WIDTH // len(POOL_WINDOWS)
POOL_MAX = 16
POOL_BUF = POOL_MAX - 1

MLA_HEADS = 8
Q_LORA = 512
KV_LORA = 256
NOPE_DIM = 64
ROPE_DIM = 32
V_DIM = 64
MLA_SCALE = (NOPE_DIM + ROPE_DIM) ** -0.5
MLA_QNOPE = MLA_HEADS * NOPE_DIM
MLA_QPE = MLA_HEADS * ROPE_DIM

EVEN_IN = SWA_Q + 2 * SWA_KV + S5_WIDTH

PEER_HEADS = 8
N_KEYS = 128
N_EXPERTS = N_KEYS * N_KEYS
D_KEY = 128
D_HALF = D_KEY // 2
PEER_TOPK = 16
NSEL = PEER_HEADS * PEER_TOPK
HALF_EXPERTS = N_EXPERTS // 2

SUBLANES = 8
LANES = 128
VMEM_LIMIT = 56 * 1024 * 1024
ROW_TILE = D_MODEL // LANES

ROW_BLOCK = 512
SWA_QBLOCK = 256
MLA_QBLOCK = 256
MLA_KBLOCK = 512
MLA_CHAIN_ROWS = 256
KPE_PAD = LANES
S5_STEPS = 128
S5_CARRY_VREGS = 16
PEER_ROUTE_BLOCK = 256
ROUTE_HEADS_PER_STEP = 4
PEER_PASS_BLOCK = 256
UPASS_GROUP = 16
SLOT_ROWS = NSEL * ROW_TILE
HI16 = -65536
PAIR_COLS = tuple(PEER_TOPK // (a + 1) for a in range(PEER_TOPK))
PAIR_ROWS = -(-sum(PAIR_COLS) // SUBLANES) * SUBLANES
PAIR_PAD = PAIR_ROWS - sum(PAIR_COLS)

ODD_U0 = Q_LORA + KV_LORA
ODD_KPE0 = ODD_U0 + POOL_WIDTH
ODD_IN_PAD = ODD_KPE0 + KPE_PAD


def _cparams(n_axes=1):
    return pltpu.CompilerParams(dimension_semantics=("arbitrary",) * n_axes, vmem_limit_bytes=VMEM_LIMIT)


def _rms(x, g):
    return x * lax.rsqrt(jnp.mean(x * x, axis=-1, keepdims=True) + RMS_EPS) * g


def _rope_lanes(x, cos, sin_lo, sin_hi, half):
    n = x.shape[-1]
    return x * cos + pltpu.roll(x, n - half, 1) * sin_lo + pltpu.roll(x, half, 1) * sin_hi


def rope_tables(pos, rot, period, width, reps):
    inv = ROPE_THETA ** (-jnp.arange(0, rot, 2, dtype=jnp.float32) / rot)
    ang = pos.astype(jnp.float32)[:, None] * inv[None, :]
    cos, sin = jnp.cos(ang), jnp.sin(ang)
    lane = jnp.arange(width) % period
    idx = lane % (rot // 2)
    in_lo = lane < rot // 2
    in_hi = (lane >= rot // 2) & (lane < rot)
    c = jnp.where((in_lo | in_hi)[None, :], cos[:, idx], 1.0)
    s_lo = jnp.where(in_lo[None, :], -sin[:, idx], 0.0)
    s_hi = jnp.where(in_hi[None, :], sin[:, idx], 0.0)
    return tuple(jnp.tile(t, (reps, 1)) for t in (c, s_lo, s_hi))


def _in_even_kernel(x_ref, g_ref, w_ref, cos_ref, slo_ref, shi_ref, q_ref, k_ref, v_ref, u_ref):
    xn = _rms(x_ref[...], g_ref[...])
    proj = jnp.dot(xn.astype(jnp.bfloat16), w_ref[...], preferred_element_type=jnp.float32)
    cos, slo, shi = cos_ref[...], slo_ref[...], shi_ref[...]
    q_ref[...] = _rope_lanes(proj[:, :SWA_Q], cos, slo, shi, ROT_DIM // 2)
    k_ref[...] = _rope_lanes(proj[:, SWA_Q:SWA_Q + SWA_KV], cos[:, :SWA_KV], slo[:, :SWA_KV], shi[:, :SWA_KV],
                             ROT_DIM // 2)
    v_ref[...] = proj[:, SWA_Q + SWA_KV:SWA_Q + 2 * SWA_KV]
    u_ref[...] = proj[:, SWA_Q + 2 * SWA_KV:]


def in_even(x2d, g, w_bf16, tabs):
    n = x2d.shape[0]
    tm = min(ROW_BLOCK, n)
    nt = tabs[0].shape[0] // tm
    row = lambda i: (i, 0)
    const = lambda i: (0, 0)
    tab = lambda i: (i % nt, 0)
    return pl.pallas_call(
        _in_even_kernel,
        out_shape=(jax.ShapeDtypeStruct((n, SWA_Q), jnp.float32), jax.ShapeDtypeStruct((n, SWA_KV), jnp.float32),
                   jax.ShapeDtypeStruct((n, SWA_KV), jnp.float32), jax.ShapeDtypeStruct((n, S5_WIDTH), jnp.float32)),
        grid=(n // tm,),
        in_specs=[pl.BlockSpec((tm, D_MODEL), row), pl.BlockSpec((1, D_MODEL), const),
                  pl.BlockSpec((D_MODEL, EVEN_IN), const),
                  pl.BlockSpec((tm, SWA_Q), tab), pl.BlockSpec((tm, SWA_Q), tab), pl.BlockSpec((tm, SWA_Q), tab)],
        out_specs=(pl.BlockSpec((tm, SWA_Q), row), pl.BlockSpec((tm, SWA_KV), row),
                   pl.BlockSpec((tm, SWA_KV), row), pl.BlockSpec((tm, S5_WIDTH), row)),
        compiler_params=_cparams(), name="in_even",
    )(x2d, g.reshape(1, D_MODEL), w_bf16, *tabs)


def _swa_kernel(sink_ref, q_ref, kp_ref, kc_ref, vp_ref, vc_ref, o_ref, *, banded, qb):
    i = pl.program_id(1)
    q = q_ref[0]
    k = jnp.concatenate([kp_ref[0], kc_ref[0]], axis=0).astype(jnp.bfloat16)
    v = jnp.concatenate([vp_ref[0], vc_ref[0]], axis=0).astype(jnp.bfloat16)
    nk = WINDOW + qb
    wc = WINDOW // CHUNK
    if banded:
        qc = lax.broadcasted_iota(jnp.int32, (qb, nk), 0) // CHUNK + wc
        kc = lax.broadcasted_iota(jnp.int32, (qb, nk), 1) // CHUNK
        visible = (kc <= qc) & (kc >= qc - wc) & ((kc >= wc) | (i > 0))
    outs = []
    for h in range(SWA_HEADS):
        hk = h // SWA_GROUP
        qh = q[:, h * HEAD_DIM:(h + 1) * HEAD_DIM].astype(jnp.bfloat16)
        kh = k[:, hk * HEAD_DIM:(hk + 1) * HEAD_DIM]
        s = lax.dot_general(qh, kh, (((1,), (1,)), ((), ())), preferred_element_type=jnp.float32) * SWA_SCALE
        if banded:
            s = jnp.where(visible, s, NEG_INF)
        sk = sink_ref[h]
        m = jnp.maximum(jnp.max(s, axis=-1, keepdims=True), sk)
        p = jnp.exp(s - m)
        den = jnp.sum(p, axis=-1, keepdims=True) + jnp.exp(sk - m)
        o = jnp.dot(p.astype(jnp.bfloat16), v[:, hk * HEAD_DIM:(hk + 1) * HEAD_DIM], preferred_element_type=jnp.float32)
        outs.append(o / den)
    o_ref[0] = jnp.concatenate(outs, axis=-1)


def swa_attention(q, k_prev, k_cur, v_prev, v_cur, sink, banded):
    b, t, _ = q.shape
    qb = min(SWA_QBLOCK, t)
    per = qb // WINDOW
    prev_map = (lambda bi, i: (bi, jnp.maximum(i * per - 1, 0), 0)) if banded else (lambda bi, i: (bi, 0, 0))
    cur = lambda bi, i: (bi, i, 0)
    return pl.pallas_call(
        functools.partial(_swa_kernel, banded=banded, qb=qb),
        out_shape=jax.ShapeDtypeStruct((b, t, SWA_Q), jnp.float32),
        grid=(b, t // qb),
        in_specs=[pl.BlockSpec(memory_space=pltpu.SMEM),
                  pl.BlockSpec((1, qb, SWA_Q), cur),
                  pl.BlockSpec((1, WINDOW, SWA_KV), prev_map), pl.BlockSpec((1, qb, SWA_KV), cur),
                  pl.BlockSpec((1, WINDOW, SWA_KV), prev_map), pl.BlockSpec((1, qb, SWA_KV), cur)],
        out_specs=pl.BlockSpec((1, qb, SWA_Q), cur),
        compiler_params=_cparams(2), name="swa_attention",
    )(sink, q, k_prev, k_cur, v_prev, v_cur)


def _out_kernel(x_ref, a_ref, b_ref, w_ref, o_ref):
    ka = a_ref.shape[-1]
    o_ref[...] = (x_ref[...]
                  + jnp.dot(a_ref[...].astype(jnp.bfloat16), w_ref[:ka, :], preferred_element_type=jnp.float32)
                  + jnp.dot(b_ref[...].astype(jnp.bfloat16), w_ref[ka:, :], preferred_element_type=jnp.float32))


def out_proj(x2d, a, b, w_bf16):
    n = x2d.shape[0]
    tm = min(ROW_BLOCK, n)
    row = lambda i: (i, 0)
    return pl.pallas_call(
        _out_kernel,
        out_shape=jax.ShapeDtypeStruct((n, D_MODEL), jnp.float32),
        grid=(n // tm,),
        in_specs=[pl.BlockSpec((tm, D_MODEL), row), pl.BlockSpec((tm, a.shape[1]), row),
                  pl.BlockSpec((tm, b.shape[1]), row), pl.BlockSpec(w_bf16.shape, lambda i: (0, 0))],
        out_specs=pl.BlockSpec((tm, D_MODEL), row),
        compiler_params=_cparams(), name="out_proj",
    )(x2d, a, b, w_bf16)


def s5_discretize(lam_re, lam_im, log_dt, b_re, b_im, c_re, c_im):
    lr = jnp.minimum(lam_re, -1e-4)
    li = lam_im
    dt = jnp.exp(log_dt)[:, None]
    mag = jnp.exp(lr * dt)
    ang = li * dt
    ab_re, ab_im = mag * jnp.cos(ang), mag * jnp.sin(ang)
    den = lr * lr + li * li
    nr, ni = ab_re - 1.0, ab_im
    f_re = (nr * lr + ni * li) / den
    f_im = (ni * lr - nr * li) / den
    bb_re = f_re[..., None] * b_re - f_im[..., None] * b_im
    bb_im = f_re[..., None] * b_im + f_im[..., None] * b_re
    eye = jnp.eye(S5_GROUPS, dtype=jnp.float32)

    def embed_b(bb):
        return jnp.einsum('gnc,gh->gchn', bb, eye).reshape(S5_WIDTH, S5_FLAT)

    def embed_c(c):
        return jnp.einsum('gcn,gh->gnhc', c, eye).reshape(S5_FLAT, S5_WIDTH)

    wb = jnp.concatenate([embed_b(bb_re), embed_b(bb_im)], axis=1)
    wc = jnp.concatenate([embed_c(c_re), -embed_c(c_im)], axis=0)
    lam = jnp.stack([ab_re.reshape(S5_FLAT), ab_im.reshape(S5_FLAT)])
    return lam, wb.astype(jnp.bfloat16), wc.astype(jnp.bfloat16)


def _s5_kernel(u_ref, lam_ref, wb_ref, wc_ref, d_ref, wglu_ref, bglu_ref, h0r_ref, h0i_ref,
               o_ref, hr_ref, hi_ref, hbuf, *, nb, steps, width):
    @pl.when(pl.program_id(0) == 0)
    def _():
        hr_ref[...] = h0r_ref[...]
        hi_ref[...] = h0i_ref[...]

    u = u_ref[...]
    hbuf[...] = jnp.dot(u.astype(jnp.bfloat16), wb_ref[...], preferred_element_type=jnp.float32)
    for c0 in range(0, S5_FLAT, width):
        cre = slice(c0, c0 + width)
        cim = slice(S5_FLAT + c0, S5_FLAT + c0 + width)
        lr = jnp.broadcast_to(lam_ref[0:1, cre], (nb, width))
        li = jnp.broadcast_to(lam_ref[1:2, cre], (nb, width))

        def step(t, carry):
            hr, hi = carry
            rows = pl.ds(pl.multiple_of(t * nb, nb), nb)
            nhr = lr * hr - li * hi + hbuf[rows, cre]
            nhi = lr * hi + li * hr + hbuf[rows, cim]
            hbuf[rows, cre] = nhr
            hbuf[rows, cim] = nhi
            return nhr, nhi

        hr, hi = lax.fori_loop(0, steps, step, (hr_ref[:, cre], hi_ref[:, cre]))
        hr_ref[:, cre] = hr
        hi_ref[:, cre] = hi
    y = jnp.dot(hbuf[...].astype(jnp.bfloat16), wc_ref[...], preferred_element_type=jnp.float32) + d_ref[...] * u
    z = 0.5 * y * (1.0 + jnp.tanh(math.sqrt(2.0 / math.pi) * (y + 0.044715 * (y * y * y))))
    gate = jnp.dot(z.astype(jnp.bfloat16), wglu_ref[...], preferred_element_type=jnp.float32) + bglu_ref[...]
    o_ref[...] = z * (1.0 / (1.0 + jnp.exp(-gate)))


def s5_mixer(u_tm, nb, lam, wb, wc, d_skip, w_glu_bf16, b_glu, h0r, h0i, steps):
    rows = u_tm.shape[0]
    t_total = rows // nb
    width = min(S5_FLAT, max(LANES, (S5_CARRY_VREGS * SUBLANES * LANES) // (2 * nb)))
    blk = steps * nb
    const = lambda i: (0, 0)
    return pl.pallas_call(
        functools.partial(_s5_kernel, nb=nb, steps=steps, width=width),
        out_shape=(jax.ShapeDtypeStruct((rows, S5_WIDTH), jnp.float32),
                   jax.ShapeDtypeStruct((nb, S5_FLAT), jnp.float32),
                   jax.ShapeDtypeStruct((nb, S5_FLAT), jnp.float32)),
        grid=(t_total // steps,),
        in_specs=[pl.BlockSpec((blk, S5_WIDTH), lambda i: (i, 0)),
                  pl.BlockSpec((2, S5_FLAT), const),
                  pl.BlockSpec((S5_WIDTH, 2 * S5_FLAT), const),
                  pl.BlockSpec((2 * S5_FLAT, S5_WIDTH), const),
                  pl.BlockSpec((1, S5_WIDTH), const),
                  pl.BlockSpec((S5_WIDTH, S5_WIDTH), const),
                  pl.BlockSpec((1, S5_WIDTH), const),
                  pl.BlockSpec((nb, S5_FLAT), const),
                  pl.BlockSpec((nb, S5_FLAT), const)],
        out_specs=(pl.BlockSpec((blk, S5_WIDTH), lambda i: (i, 0)),
                   pl.BlockSpec((nb, S5_FLAT), const),
                   pl.BlockSpec((nb, S5_FLAT), const)),
        scratch_shapes=[pltpu.VMEM((blk, 2 * S5_FLAT), jnp.float32)],
        compiler_params=_cparams(), name="s5_mixer",
    )(u_tm, lam, wb, wc, d_skip.reshape(1, S5_WIDTH), w_glu_bf16, b_glu.reshape(1, S5_WIDTH), h0r, h0i)


def _in_odd_kernel(x_ref, g_ref, w_ref, qn_ref, kvn_ref, wuq_ref, kcos_ref, kslo_ref, kshi_ref,
                   qcos_ref, qslo_ref, qshi_ref, qnope_ref, qpe_ref, c_ref, kp_ref, u_ref):
    xn = _rms(x_ref[...], g_ref[...])
    proj = jnp.dot(xn.astype(jnp.bfloat16), w_ref[...], preferred_element_type=jnp.float32)
    cqn = _rms(proj[:, :Q_LORA], qn_ref[...])
    q = jnp.dot(cqn.astype(jnp.bfloat16), wuq_ref[...], preferred_element_type=jnp.float32)
    qnope_ref[...] = q[:, :MLA_QNOPE]
    qpe_ref[...] = _rope_lanes(q[:, MLA_QNOPE:], qcos_ref[...], qslo_ref[...], qshi_ref[...], ROPE_DIM // 2)
    c_ref[...] = _rms(proj[:, Q_LORA:ODD_U0], kvn_ref[...])
    kp = _rope_lanes(proj[:, ODD_KPE0:], kcos_ref[...], kslo_ref[...], kshi_ref[...], ROPE_DIM // 2)
    kp_ref[...] = kp[:, :ROPE_DIM]
    u_ref[...] = proj[:, ODD_U0:ODD_KPE0]


def in_odd(x2d, g, w_perm_bf16, q_norm, kv_norm, wuq_perm_bf16, ktabs, qtabs):
    n = x2d.shape[0]
    tm = min(ROW_BLOCK, n)
    nt = ktabs[0].shape[0] // tm
    row = lambda i: (i, 0)
    const = lambda i: (0, 0)
    tab = lambda i: (i % nt, 0)
    return pl.pallas_call(
        _in_odd_kernel,
        out_shape=(jax.ShapeDtypeStruct((n, MLA_QNOPE), jnp.float32), jax.ShapeDtypeStruct((n, MLA_QPE), jnp.float32),
                   jax.ShapeDtypeStruct((n, KV_LORA), jnp.float32), jax.ShapeDtypeStruct((n, ROPE_DIM), jnp.float32),
                   jax.ShapeDtypeStruct((n, POOL_WIDTH), jnp.float32)),
        grid=(n // tm,),
        in_specs=[pl.BlockSpec((tm, D_MODEL), row), pl.BlockSpec((1, D_MODEL), const),
                  pl.BlockSpec((D_MODEL, ODD_IN_PAD), const),
                  pl.BlockSpec((1, Q_LORA), const), pl.BlockSpec((1, KV_LORA), const),
                  pl.BlockSpec((Q_LORA, MLA_QNOPE + MLA_QPE), const),
                  pl.BlockSpec((tm, KPE_PAD), tab), pl.BlockSpec((tm, KPE_PAD), tab), pl.BlockSpec((tm, KPE_PAD), tab),
                  pl.BlockSpec((tm, MLA_QPE), tab), pl.BlockSpec((tm, MLA_QPE), tab), pl.BlockSpec((tm, MLA_QPE), tab)],
        out_specs=(pl.BlockSpec((tm, MLA_QNOPE), row), pl.BlockSpec((tm, MLA_QPE), row), pl.BlockSpec((tm, KV_LORA), row),
                   pl.BlockSpec((tm, ROPE_DIM), row), pl.BlockSpec((tm, POOL_WIDTH), row)),
        compiler_params=_cparams(), name="in_odd",
    )(x2d, g.reshape(1, D_MODEL), w_perm_bf16, q_norm.reshape(1, Q_LORA), kv_norm.reshape(1, KV_LORA), wuq_perm_bf16,
      *ktabs, *qtabs)


def permute_odd_weights(w_in, w_uq):
    o1, o2 = Q_LORA + KV_LORA, Q_LORA + KV_LORA + ROPE_DIM
    kpe = jnp.pad(w_in[:, o1:o2], ((0, 0), (0, KPE_PAD - ROPE_DIM)))
    w_perm = jnp.concatenate([w_in[:, :o1], w_in[:, o2:], kpe], axis=1)
    wq = w_uq.reshape(Q_LORA, MLA_HEADS, NOPE_DIM + ROPE_DIM)
    wq_perm = jnp.concatenate([wq[:, :, :NOPE_DIM].reshape(Q_LORA, MLA_QNOPE),
                               wq[:, :, NOPE_DIM:].reshape(Q_LORA, MLA_QPE)], axis=1)
    return w_perm.astype(jnp.bfloat16), wq_perm.astype(jnp.bfloat16)


def _mla_kernel(qn_ref, qp_ref, c_ref, kp_ref, wuk_ref, wuv_ref, o_ref, *scratch, causal, qb, kb, n_keys, hpc):
    i = pl.program_id(1)
    n_chains = MLA_HEADS // hpc
    qa_s, qp_s, m_s, l_s, acc_s = (scratch[k * n_chains:(k + 1) * n_chains] for k in range(5))
    qn = qn_ref[0].astype(jnp.bfloat16)
    qp = qp_ref[0].astype(jnp.bfloat16)
    for g in range(n_chains):
        for hh in range(hpc):
            h = g * hpc + hh
            rows = slice(hh * qb, (hh + 1) * qb)
            qa_s[g][rows, :] = jnp.dot(qn[:, h * NOPE_DIM:(h + 1) * NOPE_DIM], wuk_ref[h],
                                       preferred_element_type=jnp.float32).astype(jnp.bfloat16)
            qp_s[g][rows, :] = qp[:, h * ROPE_DIM:(h + 1) * ROPE_DIM]
        m_s[g][...] = jnp.full(m_s[g].shape, NEG_INF, jnp.float32)
        l_s[g][...] = jnp.zeros(l_s[g].shape, jnp.float32)
        acc_s[g][...] = jnp.zeros(acc_s[g].shape, jnp.float32)
    if causal:
        qpos = i * qb + lax.broadcasted_iota(jnp.int32, (hpc * qb, kb), 0) % qb
        limit = (qpos // CHUNK + 1) * CHUNK
        nblk = ((i + 1) * qb + kb - 1) // kb
    else:
        limit = n_keys
        nblk = (n_keys + kb - 1) // kb
    kidx0 = lax.broadcasted_iota(jnp.int32, (hpc * qb, kb), 1)

    def body(j, carry):
        rows = pl.ds(pl.multiple_of(j * kb, kb), kb)
        cb = c_ref[0, rows, :]
        kpb = kp_ref[0, rows, :]
        visible = kidx0 + j * kb < limit
        for g in range(n_chains):
            s = (lax.dot_general(qa_s[g][...], cb, (((1,), (1,)), ((), ())), preferred_element_type=jnp.float32)
                 + lax.dot_general(qp_s[g][...], kpb, (((1,), (1,)), ((), ())),
                                   preferred_element_type=jnp.float32)) * MLA_SCALE
            s = jnp.where(visible, s, NEG_INF)
            m_old = m_s[g][...]
            m_new = jnp.maximum(m_old, jnp.max(s, axis=-1, keepdims=True))
            alpha = jnp.exp(m_old - m_new)
            p = jnp.exp(s - m_new)
            m_s[g][...] = m_new
            l_s[g][...] = alpha * l_s[g][...] + jnp.sum(p, axis=-1, keepdims=True)
            acc_s[g][...] = alpha * acc_s[g][...] + jnp.dot(p.astype(jnp.bfloat16), cb, preferred_element_type=jnp.float32)
        return carry

    lax.fori_loop(0, nblk, body, 0)
    outs = []
    for g in range(n_chains):
        o_lat = (acc_s[g][...] / l_s[g][...]).astype(jnp.bfloat16)
        for hh in range(hpc):
            outs.append(jnp.dot(o_lat[hh * qb:(hh + 1) * qb, :], wuv_ref[g * hpc + hh],
                                preferred_element_type=jnp.float32))
    o_ref[0] = jnp.concatenate(outs, axis=-1)


def mla_attention(q_nope, q_pe, c_keys_bf16, kp_keys_bf16, wuk_h, wuv_h, causal, n_keys):
    b, t, _ = q_nope.shape
    tk = c_keys_bf16.shape[1]
    qb = min(MLA_QBLOCK, t)
    kb = min(MLA_KBLOCK, tk)
    hpc = max(1, min(MLA_HEADS, MLA_CHAIN_ROWS // qb))
    n_chains, rows = MLA_HEADS // hpc, hpc * qb
    cur = lambda bi, i: (bi, i, 0)
    whole = lambda bi, i: (bi, 0, 0)
    const3 = lambda bi, i: (0, 0, 0)
    return pl.pallas_call(
        functools.partial(_mla_kernel, causal=causal, qb=qb, kb=kb, n_keys=n_keys, hpc=hpc),
        out_shape=jax.ShapeDtypeStruct((b, t, MLA_HEADS * V_DIM), jnp.float32),
        grid=(b, t // qb),
        in_specs=[pl.BlockSpec((1, qb, MLA_QNOPE), cur), pl.BlockSpec((1, qb, MLA_QPE), cur),
                  pl.BlockSpec((1, tk, KV_LORA), whole), pl.BlockSpec((1, tk, ROPE_DIM), whole),
                  pl.BlockSpec((MLA_HEADS, NOPE_DIM, KV_LORA), const3), pl.BlockSpec((MLA_HEADS, KV_LORA, V_DIM), const3)],
        out_specs=pl.BlockSpec((1, qb, MLA_HEADS * V_DIM), cur),
        scratch_shapes=([pltpu.VMEM((rows, KV_LORA), jnp.bfloat16)] * n_chains
                        + [pltpu.VMEM((rows, ROPE_DIM), jnp.bfloat16)] * n_chains
                        + [pltpu.VMEM((rows, 1), jnp.float32)] * (2 * n_chains)
                        + [pltpu.VMEM((rows, KV_LORA), jnp.float32)] * n_chains),
        compiler_params=_cparams(2), name="mla_attention",
    )(q_nope, q_pe, c_keys_bf16, kp_keys_bf16, wuk_h, wuv_h)


def _pool_kernel(u_ref, prev_ref, w_ref, scale_ref, o_ref, new_ref, ext, *, tm, pos0):
    j = pl.program_id(1)

    @pl.when(j == 0)
    def _():
        ext[0:1, :] = jnp.zeros((1, POOL_WIDTH), jnp.float32)
        ext[1:POOL_MAX, :] = prev_ref[0]

    @pl.when(j > 0)
    def _():
        ext[0:POOL_MAX, :] = ext[tm:tm + POOL_MAX, :]

    u = u_ref[0]
    ext[POOL_MAX:POOL_MAX + tm, :] = u
    pos = pos0 + j * tm + lax.broadcasted_iota(jnp.int32, (tm, POOL_GROUP), 0)
    outs = []
    for gi, w in enumerate(POOL_WINDOWS):
        cols = slice(gi * POOL_GROUP, (gi + 1) * POOL_GROUP)
        tot = u[:, cols]
        for d in range(1, w):
            tot = tot + ext[POOL_MAX - d:POOL_MAX - d + tm, cols]
        cnt = jnp.minimum(pos + 1, w).astype(jnp.float32)
        m = tot / cnt - u[:, cols]
        outs.append(jnp.dot(m.astype(jnp.bfloat16), w_ref[gi], preferred_element_type=jnp.float32))
    o_ref[0] = jnp.concatenate(outs, axis=-1) * scale_ref[...]
    new_ref[0] = ext[tm + 1:tm + POOL_MAX, :]


def pool_mixer(u, prev, pool_w_bf16, pool_scale, pos0):
    b, t, _ = u.shape
    tm = min(ROW_BLOCK, t)
    cur = lambda bi, j: (bi, j, 0)
    per_b = lambda bi, j: (bi, 0, 0)
    return pl.pallas_call(
        functools.partial(_pool_kernel, tm=tm, pos0=pos0),
        out_shape=(jax.ShapeDtypeStruct((b, t, POOL_WIDTH), jnp.float32),
                   jax.ShapeDtypeStruct((b, POOL_BUF, POOL_WIDTH), jnp.float32)),
        grid=(b, t // tm),
        in_specs=[pl.BlockSpec((1, tm, POOL_WIDTH), cur), pl.BlockSpec((1, POOL_BUF, POOL_WIDTH), per_b),
                  pl.BlockSpec((len(POOL_WINDOWS), POOL_GROUP, POOL_GROUP), lambda bi, j: (0, 0, 0)),
                  pl.BlockSpec((1, POOL_WIDTH), lambda bi, j: (0, 0))],
        out_specs=(pl.BlockSpec((1, tm, POOL_WIDTH), cur), pl.BlockSpec((1, POOL_BUF, POOL_WIDTH), per_b)),
        scratch_shapes=[pltpu.VMEM((tm + POOL_MAX, POOL_WIDTH), jnp.float32)],
        compiler_params=_cparams(2), name="pool_mixer",
    )(u, prev, pool_w_bf16, pool_scale.reshape(1, POOL_WIDTH))


def _top16_rows(s, n_rows):
    iota = lax.broadcasted_iota(jnp.int32, s.shape, 0).astype(jnp.float32)
    vals, idxs = [], []
    for _ in range(PEER_TOPK):
        m = jnp.max(s, axis=0, keepdims=True)
        idx = jnp.min(jnp.where(s == m, iota, float(n_rows)), axis=0, keepdims=True)
        vals.append(m)
        idxs.append(idx)
        s = jnp.where(iota == idx, -jnp.inf, s)
    return jnp.concatenate(vals, axis=0), jnp.concatenate(idxs, axis=0)


def _route_kernel(x_ref, g_ref, wq_ref, keys_ref, xn_ref, pair_ref, gate_ref, code_t, gate_t):
    xn = _rms(x_ref[...], g_ref[...])
    xn_ref[...] = xn
    xb = xn.astype(jnp.bfloat16)

    def head_step(hs, carry):
        for hh in range(ROUTE_HEADS_PER_STEP):
            one_head(hs * ROUTE_HEADS_PER_STEP + hh)
        return carry

    def one_head(h):
        qb = jnp.dot(xb, wq_ref[h], preferred_element_type=jnp.float32).astype(jnp.bfloat16)
        sv, si = [], []
        for p in range(2):
            s = lax.dot_general(keys_ref[h * 2 + p], qb[:, p * D_HALF:(p + 1) * D_HALF],
                                (((1,), (1,)), ((), ())), preferred_element_type=jnp.float32)
            v, i = _top16_rows(s, N_KEYS)
            sv.append(v)
            si.append(i)
        cand = jnp.concatenate([sv[0][a:a + 1] + sv[1][:nb] for a, nb in enumerate(PAIR_COLS)]
                               + [jnp.full((PAIR_PAD, s.shape[1]), -jnp.inf, jnp.float32)], axis=0)
        eid = jnp.concatenate([si[0][a:a + 1] * float(N_KEYS) + si[1][:nb] for a, nb in enumerate(PAIR_COLS)]
                              + [jnp.zeros((PAIR_PAD, s.shape[1]), jnp.float32)], axis=0)
        iota = lax.broadcasted_iota(jnp.int32, cand.shape, 0).astype(jnp.float32)
        cv, ce = [], []
        for _ in range(PEER_TOPK):
            m = jnp.max(cand, axis=0, keepdims=True)
            idx = jnp.min(jnp.where(cand == m, iota, float(PAIR_ROWS)), axis=0, keepdims=True)
            hit = iota == idx
            cv.append(m)
            ce.append(jnp.max(jnp.where(hit, eid, -1.0), axis=0, keepdims=True))
            cand = jnp.where(hit, -jnp.inf, cand)
        cv = jnp.concatenate(cv, axis=0)
        ce = jnp.concatenate(ce, axis=0).astype(jnp.int32)
        e = jnp.exp(cv - cv[0:1])
        rows = pl.ds(pl.multiple_of(h * PEER_TOPK, PEER_TOPK), PEER_TOPK)
        gate_t[rows, :] = e / jnp.sum(e, axis=0, keepdims=True)
        code_t[rows, :] = ((ce & (HALF_EXPERTS - 1)) << 3) | (ce >> 13)

    lax.fori_loop(0, PEER_HEADS // ROUTE_HEADS_PER_STEP, head_step, 0)
    ca, cb = code_t[:NSEL // 2, :], code_t[NSEL // 2:, :]
    pair_ref[...] = (ca | (((cb & 1) ^ 1) << 1) | ((cb & -8) << 16)).T
    gate_ref[...] = gate_t[...].T


def peer_route(x2d, g, wq_heads, keys_bf16, tb):
    n = x2d.shape[0]
    return pl.pallas_call(
        _route_kernel,
        out_shape=(jax.ShapeDtypeStruct((n, D_MODEL), jnp.float32),
                   jax.ShapeDtypeStruct((n, NSEL // 2), jnp.int32),
                   jax.ShapeDtypeStruct((n, NSEL), jnp.float32)),
        grid=(n // tb,),
        in_specs=[pl.BlockSpec((tb, D_MODEL), lambda i: (i, 0)),
                  pl.BlockSpec((1, D_MODEL), lambda i: (0, 0)),
                  pl.BlockSpec((PEER_HEADS, D_MODEL, D_KEY), lambda i: (0, 0, 0)),
                  pl.BlockSpec((PEER_HEADS * 2, N_KEYS, D_HALF), lambda i: (0, 0, 0))],
        out_specs=(pl.BlockSpec((tb, D_MODEL), lambda i: (i, 0)),
                   pl.BlockSpec((tb, NSEL // 2), lambda i: (i, 0)),
                   pl.BlockSpec((tb, NSEL), lambda i: (i, 0))),
        scratch_shapes=[pltpu.VMEM((NSEL, tb), jnp.int32), pltpu.VMEM((NSEL, tb), jnp.float32)],
        compiler_params=_cparams(), name="peer_route",
    )(x2d, g.reshape(1, D_MODEL), wq_heads, keys_bf16)


def pack_table(tab):
    b = lax.bitcast_convert_type(tab.astype(jnp.bfloat16), jnp.uint16).astype(jnp.uint32)
    w = b[:HALF_EXPERTS] | (b[HALF_EXPERTS:] << 16)
    return lax.bitcast_convert_type(w, jnp.int32).reshape(HALF_EXPERTS * ROW_TILE, LANES)


def _gather_pair(tab_ref, cab):
    wa = tab_ref[pl.ds(pl.multiple_of(cab & 0xFFF8, SUBLANES), SUBLANES), :]
    wb = tab_ref[pl.ds(pl.multiple_of(lax.shift_right_logical(cab, 16), SUBLANES), SUBLANES), :]
    cv = jnp.full((SUBLANES, LANES), cab, jnp.int32)
    sha = (cv & 1) << 4
    shb = (cv & 2) << 3
    return (lax.shift_right_logical(wa, sha) & 0xFFFF) | (lax.shift_left(wb, shb) & HI16)


def _gather_token(tab_ref, pair_ref, t, buf):
    for i in range(NSEL // 2):
        buf[i * ROW_TILE:(i + 1) * ROW_TILE, :] = _gather_pair(tab_ref, pair_ref[t, i])


def _grouped_token_loop(tb, gather, finish, group_a, group_b):
    for buf in group_b:
        buf[...] = jnp.zeros(buf.shape, jnp.int32)

    def body(i, carry):
        t0 = 2 * UPASS_GROUP * i
        for q in range(UPASS_GROUP):
            gather(t0 + q, group_a[q])
        finish(jnp.maximum(t0 - UPASS_GROUP, 0), group_b)
        for q in range(UPASS_GROUP):
            gather(t0 + UPASS_GROUP + q, group_b[q])
        finish(t0, group_a)
        return carry

    lax.fori_loop(0, tb // (2 * UPASS_GROUP), body, 0)
    finish(tb - UPASS_GROUP, group_b)


def _upass_kernel(pair_ref, gate_ref, xn_ref, tab_ref, w_ref, *bufs, tb):
    col = lax.broadcasted_iota(jnp.int32, (2 * ROW_TILE, SLOT_ROWS), 1)
    row = lax.broadcasted_iota(jnp.int32, (2 * ROW_TILE, SLOT_ROWS), 0)
    chunk_mask = (((col & 15) >> 1) == (row & 7)).astype(jnp.float32)
    c2 = lax.broadcasted_iota(jnp.int32, (SLOT_ROWS, NSEL), 0)
    j2 = lax.broadcasted_iota(jnp.int32, (SLOT_ROWS, NSEL), 1)
    fold = (j2 == (c2 >> 4) + ((c2 & 1) << 6)).astype(jnp.bfloat16)

    def gather(t, buf):
        _gather_token(tab_ref, pair_ref, t, buf)

    def finish(t0, group):
        zs = []
        for q, buf in enumerate(group):
            xt = xn_ref[pl.ds(t0 + q, 1), :].reshape(ROW_TILE, LANES)
            xhi = xt.astype(jnp.bfloat16)
            xlo = (xt - xhi.astype(jnp.float32)).astype(jnp.bfloat16)
            x16 = jnp.concatenate([xhi, xlo], axis=0)
            us = pltpu.bitcast(buf[...], jnp.bfloat16)
            r = lax.dot_general(x16, us, (((1,), (1,)), ((), ())), preferred_element_type=jnp.float32)
            zs.append(jnp.sum(r * chunk_mask, axis=0, keepdims=True))
        z = jnp.concatenate(zs, axis=0)
        zhi = z.astype(jnp.bfloat16)
        zlo = (z - zhi.astype(jnp.float32)).astype(jnp.bfloat16)
        act = (jnp.dot(zhi, fold, preferred_element_type=jnp.float32)
               + jnp.dot(zlo, fold, preferred_element_type=jnp.float32))
        gelu = 0.5 * act * (1.0 + lax.erf(act * (1.0 / math.sqrt(2.0))))
        rows = pl.ds(pl.multiple_of(t0, UPASS_GROUP), UPASS_GROUP)
        w_ref[rows, :] = (gate_ref[rows, :] * gelu).astype(jnp.bfloat16)

    _grouped_token_loop(tb, gather, finish, bufs[:UPASS_GROUP], bufs[UPASS_GROUP:])


def _peer_pass_call(kernel_fn, out_shape, out_block, pairs, per_token, rows, tab_packed, tb, name):
    n = pairs.shape[0]
    return pl.pallas_call(
        functools.partial(kernel_fn, tb=tb),
        out_shape=out_shape,
        grid=(n // tb,),
        in_specs=[pl.BlockSpec((tb, NSEL // 2), lambda i: (i, 0), memory_space=pltpu.SMEM),
                  pl.BlockSpec((tb, NSEL), lambda i: (i, 0)),
                  pl.BlockSpec((tb, D_MODEL), lambda i: (i, 0)),
                  pl.BlockSpec((HALF_EXPERTS * ROW_TILE, LANES), lambda i: (0, 0), pipeline_mode=pl.Buffered(1))],
        out_specs=pl.BlockSpec(out_block, lambda i: (i, 0)),
        scratch_shapes=[pltpu.VMEM((SLOT_ROWS // 2, LANES), jnp.int32)] * (2 * UPASS_GROUP),
        compiler_params=_cparams(), name=name,
    )(pairs, per_token, rows, tab_packed)


def peer_upass(pairs, gate, xn_rows, tab_packed, tb):
    n = pairs.shape[0]
    return _peer_pass_call(_upass_kernel, jax.ShapeDtypeStruct((n, NSEL), jnp.bfloat16), (tb, NSEL),
                           pairs, gate, xn_rows, tab_packed, tb, "peer_upass")


def _vpass_kernel(pair_ref, w_ref, x_ref, tab_ref, o_ref, *bufs, tb):
    col = lax.broadcasted_iota(jnp.int32, (ROW_TILE, SLOT_ROWS), 1)
    row = lax.broadcasted_iota(jnp.int32, (ROW_TILE, SLOT_ROWS), 0)
    chunk_mask = (((col & 15) >> 1) == row).astype(jnp.float32)
    j2 = lax.broadcasted_iota(jnp.int32, (NSEL, SLOT_ROWS), 0)
    c2 = lax.broadcasted_iota(jnp.int32, (NSEL, SLOT_ROWS), 1)
    spread = (j2 == (c2 >> 4) + ((c2 & 1) << 6)).astype(jnp.bfloat16)

    def gather(t, buf):
        _gather_token(tab_ref, pair_ref, t, buf)

    def finish(t0, group):
        rows = pl.ds(pl.multiple_of(t0, UPASS_GROUP), UPASS_GROUP)
        wexp = jnp.dot(w_ref[rows, :], spread, preferred_element_type=jnp.float32)
        for q, buf in enumerate(group):
            wsel = (jnp.broadcast_to(wexp[q:q + 1], (ROW_TILE, SLOT_ROWS)) * chunk_mask).astype(jnp.bfloat16)
            vs = pltpu.bitcast(buf[...], jnp.bfloat16)
            tile = jnp.dot(wsel, vs, preferred_element_type=jnp.float32)
            o_ref[pl.ds(t0 + q, 1), :] = x_ref[pl.ds(t0 + q, 1), :] + tile.reshape(1, D_MODEL)

    _grouped_token_loop(tb, gather, finish, bufs[:UPASS_GROUP], bufs[UPASS_GROUP:])


def peer_vpass(pairs, w_bf16, x_rows, tab_packed, tb):
    n = pairs.shape[0]
    return _peer_pass_call(_vpass_kernel, jax.ShapeDtypeStruct((n, D_MODEL), jnp.float32),
                           (tb, D_MODEL), pairs, w_bf16, x_rows, tab_packed, tb, "peer_vpass")


def peer_block(x, g, wq_heads, keys_bf16, u_packed, v_packed):
    shp = x.shape
    x2d = x.reshape(-1, D_MODEL)
    n = x2d.shape[0]
    xn, pairs, gate = peer_route(x2d, g, wq_heads, keys_bf16, min(PEER_ROUTE_BLOCK, n))
    w = peer_upass(pairs, gate, xn, u_packed, min(PEER_PASS_BLOCK, n))
    out = peer_vpass(pairs, w, x2d, v_packed, min(PEER_PASS_BLOCK, n))
    return out.reshape(shp)


def _rms_kernel(x_ref, g_ref, o_ref):
    o_ref[...] = _rms(x_ref[...], g_ref[...])


def rmsnorm_pallas(x, g):
    shp = x.shape
    xt = x.reshape(-1, shp[-1])
    n = xt.shape[0]
    tm = min(ROW_BLOCK, n)
    out = pl.pallas_call(
        _rms_kernel,
        out_shape=jax.ShapeDtypeStruct(xt.shape, xt.dtype),
        grid=(n // tm,),
        in_specs=[pl.BlockSpec((tm, shp[-1]), lambda i: (i, 0)),
                  pl.BlockSpec((1, shp[-1]), lambda i: (0, 0))],
        out_specs=pl.BlockSpec((tm, shp[-1]), lambda i: (i, 0)),
        compiler_params=_cparams(), name="final_rmsnorm",
    )(xt, g.reshape(1, -1))
    return out.reshape(shp)


def even_layer(x, pos0, k_prev, v_prev, h0r, h0i, norm_g, w_in, w_out, sink, s5_params, d_skip, w_glu, b_glu):
    b, t, _ = x.shape
    n = b * t
    x2d = x.reshape(n, D_MODEL)
    tabs = rope_tables(pos0 + jnp.arange(t), ROT_DIM, HEAD_DIM, SWA_Q, max(1, min(ROW_BLOCK, n) // t))
    q, k, v, u = in_even(x2d, norm_g, w_in.astype(jnp.bfloat16), tabs)
    q3, k3, v3 = q.reshape(b, t, SWA_Q), k.reshape(b, t, SWA_KV), v.reshape(b, t, SWA_KV)
    if k_prev is None:
        att = swa_attention(q3, k3, k3, v3, v3, sink, True)
        k_all, v_all = k3, v3
        h0r = jnp.zeros((b, S5_FLAT), jnp.float32)
        h0i = jnp.zeros((b, S5_FLAT), jnp.float32)
    else:
        kp, vp = k_prev.reshape(b, WINDOW, SWA_KV), v_prev.reshape(b, WINDOW, SWA_KV)
        att = swa_attention(q3, kp, k3, vp, v3, sink, False)
        k_all, v_all = jnp.concatenate([kp, k3], axis=1), jnp.concatenate([vp, v3], axis=1)
        h0r, h0i = h0r.reshape(b, S5_FLAT), h0i.reshape(b, S5_FLAT)
    lam, wb, wc = s5_discretize(*s5_params)
    u_tm = u.reshape(b, t, S5_WIDTH).transpose(1, 0, 2).reshape(n, S5_WIDTH)
    s5o_tm, hre, him = s5_mixer(u_tm, b, lam, wb, wc, d_skip, w_glu.astype(jnp.bfloat16), b_glu, h0r, h0i,
                                min(t, S5_STEPS))
    s5o = s5o_tm.reshape(t, b, S5_WIDTH).transpose(1, 0, 2).reshape(n, S5_WIDTH)
    out = out_proj(x2d, att.reshape(n, SWA_Q), s5o, w_out.astype(jnp.bfloat16))
    return (out.reshape(b, t, D_MODEL),
            k_all[:, -WINDOW:].reshape(b, WINDOW, SWA_KV_HEADS, HEAD_DIM),
            v_all[:, -WINDOW:].reshape(b, WINDOW, SWA_KV_HEADS, HEAD_DIM),
            hre.reshape(b, S5_GROUPS, S5_STATE), him.reshape(b, S5_GROUPS, S5_STATE))


def odd_layer(x, pos0, pool_prev, ckv_prev, kpe_prev, norm_g, w_in, w_out, pool_w, pool_scale,
              q_norm, kv_norm, w_uq, w_uk, w_uv):
    b, t, _ = x.shape
    n = b * t
    x2d = x.reshape(n, D_MODEL)
    reps = max(1, min(ROW_BLOCK, n) // t)
    pos = pos0 + jnp.arange(t)
    ktabs = rope_tables(pos, ROPE_DIM, KPE_PAD, KPE_PAD, reps)
    qtabs = rope_tables(pos, ROPE_DIM, ROPE_DIM, MLA_QPE, reps)
    w_perm, wuq_perm = permute_odd_weights(w_in, w_uq)
    qnope, qpe, c, kp, u = in_odd(x2d, norm_g, w_perm, q_norm, kv_norm, wuq_perm, ktabs, qtabs)
    c3, kp3 = c.reshape(b, t, KV_LORA), kp.reshape(b, t, ROPE_DIM)
    wuk_h = w_uk.transpose(1, 2, 0).astype(jnp.bfloat16)
    wuv_h = w_uv.transpose(1, 0, 2).astype(jnp.bfloat16)
    if ckv_prev is None:
        ck, kk, causal, n_keys = c3, kp3, True, t
        pool_prev = jnp.zeros((b, POOL_BUF, POOL_WIDTH), jnp.float32)
    else:
        ck, kk, causal = jnp.concatenate([ckv_prev, c3], axis=1), jnp.concatenate([kpe_prev, kp3], axis=1), False
        n_keys = ck.shape[1]
        pad = -n_keys % min(MLA_KBLOCK, n_keys)
        ck, kk = jnp.pad(ck, ((0, 0), (0, pad), (0, 0))), jnp.pad(kk, ((0, 0), (0, pad), (0, 0)))
    mla = mla_attention(qnope.reshape(b, t, MLA_QNOPE), qpe.reshape(b, t, MLA_QPE), ck.astype(jnp.bfloat16),
                        kk.astype(jnp.bfloat16), wuk_h, wuv_h, causal, n_keys)
    pool_out, pool_new = pool_mixer(u.reshape(b, t, POOL_WIDTH), pool_prev, pool_w.astype(jnp.bfloat16), pool_scale, pos0)
    out = out_proj(x2d, pool_out.reshape(n, POOL_WIDTH), mla.reshape(n, MLA_HEADS * V_DIM), w_out.astype(jnp.bfloat16))
    return out.reshape(b, t, D_MODEL), pool_new, c3, kp3


def kernel(x_prompt, x_sample, cache_swa_k, cache_swa_v, state_ssm_re, state_ssm_im, state_pool,
           cache_mla_ckv, cache_mla_kpe, norm_mix, norm_ffn, norm_final, w_in_even, w_out_even,
           swa_sink, s5_lam_re, s5_lam_im, s5_log_dt, s5_b_re, s5_b_im, s5_c_re, s5_c_im, s5_d,
           s5_w_glu, s5_b_glu, w_in_odd, w_out_odd, pool_w, pool_scale, mla_q_norm, mla_kv_norm,
           mla_w_uq, mla_w_uk, mla_w_uv, peer_w_q, peer_keys, peer_u, peer_v):
    xp, xs = x_prompt, x_sample
    kp_l, vp_l, rp_l, ip_l, poolp_l, cp_l, ep_l = [], [], [], [], [], [], []
    ks_l, vs_l, rs_l, is_l, pools_l, cs_l, es_l = [], [], [], [], [], [], []
    for layer in range(DEPTH):
        i = layer // 2
        if layer % 2 == 0:
            s5_params = (s5_lam_re[i], s5_lam_im[i], s5_log_dt[i], s5_b_re[i], s5_b_im[i], s5_c_re[i], s5_c_im[i])
            ew = (norm_mix[layer], w_in_even[i], w_out_even[i], swa_sink[i], s5_params, s5_d[i], s5_w_glu[i], s5_b_glu[i])
            xp, k1, v1, r1, i1 = even_layer(xp, 0, None, None, None, None, *ew)
            xs, k2, v2, r2, i2 = even_layer(xs, PAST_LEN, cache_swa_k[i], cache_swa_v[i],
                                            state_ssm_re[i], state_ssm_im[i], *ew)
            kp_l.append(k1); vp_l.append(v1); rp_l.append(r1); ip_l.append(i1)
            ks_l.append(k2); vs_l.append(v2); rs_l.append(r2); is_l.append(i2)
        else:
            ow = (norm_mix[layer], w_in_odd[i], w_out_odd[i], pool_w[i], pool_scale[i], mla_q_norm[i], mla_kv_norm[i],
                  mla_w_uq[i], mla_w_uk[i], mla_w_uv[i])
            xp, p1, c1, e1 = odd_layer(xp, 0, None, None, None, *ow)
            xs, p2, c2, e2 = odd_layer(xs, PAST_LEN, state_pool[i], cache_mla_ckv[i], cache_mla_kpe[i], *ow)
            poolp_l.append(p1); cp_l.append(c1); ep_l.append(e1)
            pools_l.append(p2); cs_l.append(c2); es_l.append(e2)
        wq_heads = peer_w_q[layer].reshape(D_MODEL, PEER_HEADS, D_KEY).transpose(1, 0, 2).astype(jnp.bfloat16)
        keys_bf16 = peer_keys[layer].reshape(PEER_HEADS * 2, N_KEYS, D_HALF).astype(jnp.bfloat16)
        u_packed, v_packed = pack_table(peer_u[layer]), pack_table(peer_v[layer])
        xp = peer_block(xp, norm_ffn[layer], wq_heads, keys_bf16, u_packed, v_packed)
        xs = peer_block(xs, norm_ffn[layer], wq_heads, keys_bf16, u_packed, v_packed)
    y_prompt = rmsnorm_pallas(xp, norm_final)
    y_sample = rmsnorm_pallas(xs, norm_final)
    return (y_prompt, y_sample,
            jnp.stack(kp_l), jnp.stack(vp_l), jnp.stack(rp_l), jnp.stack(ip_l),
            jnp.stack(poolp_l), jnp.stack(cp_l), jnp.stack(ep_l),
            jnp.stack(ks_l), jnp.stack(vs_l), jnp.stack(rs_l), jnp.stack(is_l),
            jnp.stack(pools_l), jnp.stack(cs_l), jnp.stack(es_l))
```

```python
import functools
import math
import jax
import jax.numpy as jnp
from jax import lax
from jax.experimental import pallas as pl
from jax.experimental.pallas import tpu as pltpu

D_MODEL = 1024
DEPTH = 2
PAST_LEN = 2048

CHUNK = 64
RMS_EPS = 1e-6
ROPE_THETA = 500000.0
NEG_INF = -1e30

SWA_HEADS = 8
SWA_KV_HEADS = 2
SWA_GROUP = SWA_HEADS // SWA_KV_HEADS
HEAD_DIM = 64
ROT_DIM = HEAD_DIM // 4
WINDOW = 128
SWA_Q = SWA_HEADS * HEAD_DIM
SWA_KV = SWA_KV_HEADS * HEAD_DIM
SWA_SCALE = HEAD_DIM ** -0.5

S5_WIDTH = 512
S5_GROUP = 16
S5_GROUPS = S5_WIDTH // S5_GROUP
S5_STATE = 64
S5_FLAT = S5_GROUPS * S5_STATE

POOL_WIDTH = 512
POOL_WINDOWS = (2, 4, 8, 16)
POOL_GROUP = POOL_WIDTH // len(POOL_WINDOWS)
POOL_MAX = 16
POOL_BUF = POOL_MAX - 1

MLA_HEADS = 8
Q_LORA = 512
KV_LORA = 256
NOPE_DIM = 64
ROPE_DIM = 32
V_DIM = 64
MLA_SCALE = (NOPE_DIM + ROPE_DIM) ** -0.5
MLA_QNOPE = MLA_HEADS * NOPE_DIM
MLA_QPE = MLA_HEADS * ROPE_DIM

EVEN_IN = SWA_Q + 2 * SWA_KV + S5_WIDTH

PEER_HEADS = 8
N_KEYS = 128
N_EXPERTS = N_KEYS * N_KEYS
D_KEY = 128
D_HALF = D_KEY // 2
PEER_TOPK = 16
NSEL = PEER_HEADS * PEER_TOPK
HALF_EXPERTS = N_EXPERTS // 2

SUBLANES = 8
LANES = 128
VMEM_LIMIT = 56 * 1024 * 1024
ROW_TILE = D_MODEL // LANES

ROW_BLOCK = 512
SWA_QBLOCK = 256
MLA_QBLOCK = 256
MLA_KBLOCK = 512
MLA_CHAIN_ROWS = 256
KPE_PAD = LANES
S5_STEPS = 128
S5_CARRY_VREGS = 16
PEER_ROUTE_BLOCK = 256
ROUTE_HEADS_PER_STEP = 4
PEER_PASS_BLOCK = 256
UPASS_GROUP = 16
SLOT_ROWS = NSEL * ROW_TILE
HI16 = -65536
PAIR_COLS = tuple(PEER_TOPK // (a + 1) for a in range(PEER_TOPK))
PAIR_ROWS = -(-sum(PAIR_COLS) // SUBLANES) * SUBLANES
PAIR_PAD = PAIR_ROWS - sum(PAIR_COLS)

ODD_U0 = Q_LORA + KV_LORA
ODD_KPE0 = ODD_U0 + POOL_WIDTH
ODD_IN_PAD = ODD_KPE0 + KPE_PAD


def _cparams(n_axes=1):
    return pltpu.CompilerParams(dimension_semantics=("arbitrary",) * n_axes, vmem_limit_bytes=VMEM_LIMIT)


def _rms(x, g):
    return x * lax.rsqrt(jnp.mean(x * x, axis=-1, keepdims=True) + RMS_EPS) * g


def _rope_lanes(x, cos, sin_lo, sin_hi, half):
    n = x.shape[-1]
    return x * cos + pltpu.roll(x, n - half, 1) * sin_lo + pltpu.roll(x, half, 1) * sin_hi


def rope_tables(pos, rot, period, width, reps):
    inv = ROPE_THETA ** (-jnp.arange(0, rot, 2, dtype=jnp.float32) / rot)
    ang = pos.astype(jnp.float32)[:, None] * inv[None, :]
    cos, sin = jnp.cos(ang), jnp.sin(ang)
    lane = jnp.arange(width) % period
    idx = lane % (rot // 2)
    in_lo = lane < rot // 2
    in_hi = (lane >= rot // 2) & (lane < rot)
    c = jnp.where((in_lo | in_hi)[None, :], cos[:, idx], 1.0)
    s_lo = jnp.where(in_lo[None, :], -sin[:, idx], 0.0)
    s_hi = jnp.where(in_hi[None, :], sin[:, idx], 0.0)
    return tuple(jnp.tile(t, (reps, 1)) for t in (c, s_lo, s_hi))


def _in_even_kernel(x_ref, g_ref, w_ref, cos_ref, slo_ref, shi_ref, q_ref, k_ref, v_ref, u_ref):
    xn = _rms(x_ref[...], g_ref[...])
    proj = jnp.dot(xn.astype(jnp.bfloat16), w_ref[...], preferred_element_type=jnp.float32)
    cos, slo, shi = cos_ref[...], slo_ref[...], shi_ref[...]
    q_ref[...] = _rope_lanes(proj[:, :SWA_Q], cos, slo, shi, ROT_DIM // 2)
    k_ref[...] = _rope_lanes(proj[:, SWA_Q:SWA_Q + SWA_KV], cos[:, :SWA_KV], slo[:, :SWA_KV], shi[:, :SWA_KV],
                             ROT_DIM // 2)
    v_ref[...] = proj[:, SWA_Q + SWA_KV:SWA_Q + 2 * SWA_KV]
    u_ref[...] = proj[:, SWA_Q + 2 * SWA_KV:]


def in_even(x2d, g, w_bf16, tabs):
    n = x2d.shape[0]
    tm = min(ROW_BLOCK, n)
    nt = tabs[0].shape[0] // tm
    row = lambda i: (i, 0)
    const = lambda i: (0, 0)
    tab = lambda i: (i % nt, 0)
    return pl.pallas_call(
        _in_even_kernel,
        out_shape=(jax.ShapeDtypeStruct((n, SWA_Q), jnp.float32), jax.ShapeDtypeStruct((n, SWA_KV), jnp.float32),
                   jax.ShapeDtypeStruct((n, SWA_KV), jnp.float32), jax.ShapeDtypeStruct((n, S5_WIDTH), jnp.float32)),
        grid=(n // tm,),
        in_specs=[pl.BlockSpec((tm, D_MODEL), row), pl.BlockSpec((1, D_MODEL), const),
                  pl.BlockSpec((D_MODEL, EVEN_IN), const),
                  pl.BlockSpec((tm, SWA_Q), tab), pl.BlockSpec((tm, SWA_Q), tab), pl.BlockSpec((tm, SWA_Q), tab)],
        out_specs=(pl.BlockSpec((tm, SWA_Q), row), pl.BlockSpec((tm, SWA_KV), row),
                   pl.BlockSpec((tm, SWA_KV), row), pl.BlockSpec((tm, S5_WIDTH), row)),
        compiler_params=_cparams(), name="in_even",
    )(x2d, g.reshape(1, D_MODEL), w_bf16, *tabs)


def _swa_kernel(sink_ref, q_ref, kp_ref, kc_ref, vp_ref, vc_ref, o_ref, *, banded, qb):
    i = pl.program_id(1)
    q = q_ref[0]
    k = jnp.concatenate([kp_ref[0], kc_ref[0]], axis=0).astype(jnp.bfloat16)
    v = jnp.concatenate([vp_ref[0], vc_ref[0]], axis=0).astype(jnp.bfloat16)
    nk = WINDOW + qb
    wc = WINDOW // CHUNK
    if banded:
        qc = lax.broadcasted_iota(jnp.int32, (qb, nk), 0) // CHUNK + wc
        kc = lax.broadcasted_iota(jnp.int32, (qb, nk), 1) // CHUNK
        visible = (kc <= qc) & (kc >= qc - wc) & ((kc >= wc) | (i > 0))
    ss = []
    for h in range(SWA_HEADS):
        hk = h // SWA_GROUP
        qh = q[:, h * HEAD_DIM:(h + 1) * HEAD_DIM].astype(jnp.bfloat16)
        ss.append(lax.dot_general(qh, k[:, hk * HEAD_DIM:(hk + 1) * HEAD_DIM], (((1,), (1,)), ((), ())),
                                  preferred_element_type=jnp.float32) * SWA_SCALE)
    ps, dens = [], []
    for h in range(SWA_HEADS):
        s = jnp.where(visible, ss[h], NEG_INF) if banded else ss[h]
        sk = sink_ref[h]
        m = jnp.maximum(jnp.max(s, axis=-1, keepdims=True), sk)
        p = jnp.exp(s - m)
        dens.append(jnp.sum(p, axis=-1, keepdims=True) + jnp.exp(sk - m))
        ps.append(p.astype(jnp.bfloat16))
    outs = [jnp.dot(ps[h], v[:, (h // SWA_GROUP) * HEAD_DIM:(h // SWA_GROUP + 1) * HEAD_DIM],
                    preferred_element_type=jnp.float32) / dens[h] for h in range(SWA_HEADS)]
    o_ref[0] = jnp.concatenate(outs, axis=-1)


def swa_attention(q, k_prev, k_cur, v_prev, v_cur, sink, banded):
    b, t, _ = q.shape
    qb = min(SWA_QBLOCK, t)
    per = qb // WINDOW
    prev_map = (lambda bi, i: (bi, jnp.maximum(i * per - 1, 0), 0)) if banded else (lambda bi, i: (bi, 0, 0))
    cur = lambda bi, i: (bi, i, 0)
    return pl.pallas_call(
        functools.partial(_swa_kernel, banded=banded, qb=qb),
        out_shape=jax.ShapeDtypeStruct((b, t, SWA_Q), jnp.float32),
        grid=(b, t // qb),
        in_specs=[pl.BlockSpec(memory_space=pltpu.SMEM),
                  pl.BlockSpec((1, qb, SWA_Q), cur),
                  pl.BlockSpec((1, WINDOW, SWA_KV), prev_map), pl.BlockSpec((1, qb, SWA_KV), cur),
                  pl.BlockSpec((1, WINDOW, SWA_KV), prev_map), pl.BlockSpec((1, qb, SWA_KV), cur)],
        out_specs=pl.BlockSpec((1, qb, SWA_Q), cur),
        compiler_params=_cparams(2), name="swa_attention",
    )(sink, q, k_prev, k_cur, v_prev, v_cur)


def _out_kernel(x_ref, a_ref, b_ref, w_ref, o_ref):
    ka = a_ref.shape[-1]
    o_ref[...] = (x_ref[...]
                  + jnp.dot(a_ref[...].astype(jnp.bfloat16), w_ref[:ka, :], preferred_element_type=jnp.float32)
                  + jnp.dot(b_ref[...].astype(jnp.bfloat16), w_ref[ka:, :], preferred_element_type=jnp.float32))


def out_proj(x2d, a, b, w_bf16):
    n = x2d.shape[0]
    tm = min(ROW_BLOCK, n)
    row = lambda i: (i, 0)
    return pl.pallas_call(
        _out_kernel,
        out_shape=jax.ShapeDtypeStruct((n, D_MODEL), jnp.float32),
        grid=(n // tm,),
        in_specs=[pl.BlockSpec((tm, D_MODEL), row), pl.BlockSpec((tm, a.shape[1]), row),
                  pl.BlockSpec((tm, b.shape[1]), row), pl.BlockSpec(w_bf16.shape, lambda i: (0, 0))],
        out_specs=pl.BlockSpec((tm, D_MODEL), row),
        compiler_params=_cparams(), name="out_proj",
    )(x2d, a, b, w_bf16)


def s5_discretize(lam_re, lam_im, log_dt, b_re, b_im, c_re, c_im):
    lr = jnp.minimum(lam_re, -1e-4)
    li = lam_im
    dt = jnp.exp(log_dt)[:, None]
    mag = jnp.exp(lr * dt)
    ang = li * dt
    ab_re, ab_im = mag * jnp.cos(ang), mag * jnp.sin(ang)
    den = lr * lr + li * li
    nr, ni = ab_re - 1.0, ab_im
    f_re = (nr * lr + ni * li) / den
    f_im = (ni * lr - nr * li) / den
    bb_re = f_re[..., None] * b_re - f_im[..., None] * b_im
    bb_im = f_re[..., None] * b_im + f_im[..., None] * b_re
    eye = jnp.eye(S5_GROUPS, dtype=jnp.float32)

    def embed_b(bb):
        return jnp.einsum('gnc,gh->gchn', bb, eye).reshape(S5_WIDTH, S5_FLAT)

    def embed_c(c):
        return jnp.einsum('gcn,gh->gnhc', c, eye).reshape(S5_FLAT, S5_WIDTH)

    wb = jnp.concatenate([embed_b(bb_re), embed_b(bb_im)], axis=1)
    wc = jnp.concatenate([embed_c(c_re), -embed_c(c_im)], axis=0)
    lam = jnp.stack([ab_re.reshape(S5_FLAT), ab_im.reshape(S5_FLAT)])
    return lam, wb.astype(jnp.bfloat16), wc.astype(jnp.bfloat16)


def _s5_kernel(u_ref, lam_ref, wb_ref, wc_ref, d_ref, wglu_ref, bglu_ref, h0r_ref, h0i_ref,
               o_ref, hr_ref, hi_ref, hbuf, *, nb, steps, width):
    @pl.when(pl.program_id(0) == 0)
    def _():
        hr_ref[...] = h0r_ref[...]
        hi_ref[...] = h0i_ref[...]

    u = u_ref[...]
    hbuf[...] = jnp.dot(u.astype(jnp.bfloat16), wb_ref[...], preferred_element_type=jnp.float32)
    for c0 in range(0, S5_FLAT, width):
        cre = slice(c0, c0 + width)
        cim = slice(S5_FLAT + c0, S5_FLAT + c0 + width)
        lr = jnp.broadcast_to(lam_ref[0:1, cre], (nb, width))
        li = jnp.broadcast_to(lam_ref[1:2, cre], (nb, width))

        def step(t, carry):
            hr, hi = carry
            rows = pl.ds(pl.multiple_of(t * nb, nb), nb)
            nhr = lr * hr - li * hi + hbuf[rows, cre]
            nhi = lr * hi + li * hr + hbuf[rows, cim]
            hbuf[rows, cre] = nhr
            hbuf[rows, cim] = nhi
            return nhr, nhi

        hr, hi = lax.fori_loop(0, steps, step, (hr_ref[:, cre], hi_ref[:, cre]))
        hr_ref[:, cre] = hr
        hi_ref[:, cre] = hi
    y = jnp.dot(hbuf[...].astype(jnp.bfloat16), wc_ref[...], preferred_element_type=jnp.float32) + d_ref[...] * u
    z = 0.5 * y * (1.0 + jnp.tanh(math.sqrt(2.0 / math.pi) * (y + 0.044715 * (y * y * y))))
    gate = jnp.dot(z.astype(jnp.bfloat16), wglu_ref[...], preferred_element_type=jnp.float32) + bglu_ref[...]
    o_ref[...] = z * (1.0 / (1.0 + jnp.exp(-gate)))


def s5_mixer(u_tm, nb, lam, wb, wc, d_skip, w_glu_bf16, b_glu, h0r, h0i, steps):
    rows = u_tm.shape[0]
    t_total = rows // nb
    width = min(S5_FLAT, max(LANES, (S5_CARRY_VREGS * SUBLANES * LANES) // (2 * nb)))
    blk = steps * nb
    const = lambda i: (0, 0)
    return pl.pallas_call(
        functools.partial(_s5_kernel, nb=nb, steps=steps, width=width),
        out_shape=(jax.ShapeDtypeStruct((rows, S5_WIDTH), jnp.float32),
                   jax.ShapeDtypeStruct((nb, S5_FLAT), jnp.float32),
                   jax.ShapeDtypeStruct((nb, S5_FLAT), jnp.float32)),
        grid=(t_total // steps,),
        in_specs=[pl.BlockSpec((blk, S5_WIDTH), lambda i: (i, 0)),
                  pl.BlockSpec((2, S5_FLAT), const),
                  pl.BlockSpec((S5_WIDTH, 2 * S5_FLAT), const),
                  pl.BlockSpec((2 * S5_FLAT, S5_WIDTH), const),
                  pl.BlockSpec((1, S5_WIDTH), const),
                  pl.BlockSpec((S5_WIDTH, S5_WIDTH), const),
                  pl.BlockSpec((1, S5_WIDTH), const),
                  pl.BlockSpec((nb, S5_FLAT), const),
                  pl.BlockSpec((nb, S5_FLAT), const)],
        out_specs=(pl.BlockSpec((blk, S5_WIDTH), lambda i: (i, 0)),
                   pl.BlockSpec((nb, S5_FLAT), const),
                   pl.BlockSpec((nb, S5_FLAT), const)),
        scratch_shapes=[pltpu.VMEM((blk, 2 * S5_FLAT), jnp.float32)],
        compiler_params=_cparams(), name="s5_mixer",
    )(u_tm, lam, wb, wc, d_skip.reshape(1, S5_WIDTH), w_glu_bf16, b_glu.reshape(1, S5_WIDTH), h0r, h0i)


def _in_odd_kernel(x_ref, g_ref, w_ref, qn_ref, kvn_ref, wuq_ref, kcos_ref, kslo_ref, kshi_ref,
                   qcos_ref, qslo_ref, qshi_ref, qnope_ref, qpe_ref, c_ref, kp_ref, u_ref):
    xn = _rms(x_ref[...], g_ref[...])
    proj = jnp.dot(xn.astype(jnp.bfloat16), w_ref[...], preferred_element_type=jnp.float32)
    cqn = _rms(proj[:, :Q_LORA], qn_ref[...])
    q = jnp.dot(cqn.astype(jnp.bfloat16), wuq_ref[...], preferred_element_type=jnp.float32)
    qnope_ref[...] = q[:, :MLA_QNOPE]
    qpe_ref[...] = _rope_lanes(q[:, MLA_QNOPE:], qcos_ref[...], qslo_ref[...], qshi_ref[...], ROPE_DIM // 2)
    c_ref[...] = _rms(proj[:, Q_LORA:ODD_U0], kvn_ref[...])
    kp = _rope_lanes(proj[:, ODD_KPE0:], kcos_ref[...], kslo_ref[...], kshi_ref[...], ROPE_DIM // 2)
    kp_ref[...] = kp[:, :ROPE_DIM]
    u_ref[...] = proj[:, ODD_U0:ODD_KPE0]


def in_odd(x2d, g, w_perm_bf16, q_norm, kv_norm, wuq_perm_bf16, ktabs, qtabs):
    n = x2d.shape[0]
    tm = min(ROW_BLOCK, n)
    nt = ktabs[0].shape[0] // tm
    row = lambda i: (i, 0)
    const = lambda i: (0, 0)
    tab = lambda i: (i % nt, 0)
    return pl.pallas_call(
        _in_odd_kernel,
        out_shape=(jax.ShapeDtypeStruct((n, MLA_QNOPE), jnp.float32), jax.ShapeDtypeStruct((n, MLA_QPE), jnp.float32),
                   jax.ShapeDtypeStruct((n, KV_LORA), jnp.float32), jax.ShapeDtypeStruct((n, ROPE_DIM), jnp.float32),
                   jax.ShapeDtypeStruct((n, POOL_WIDTH), jnp.float32)),
        grid=(n // tm,),
        in_specs=[pl.BlockSpec((tm, D_MODEL), row), pl.BlockSpec((1, D_MODEL), const),
                  pl.BlockSpec((D_MODEL, ODD_IN_PAD), const),
                  pl.BlockSpec((1, Q_LORA), const), pl.BlockSpec((1, KV_LORA), const),
                  pl.BlockSpec((Q_LORA, MLA_QNOPE + MLA_QPE), const),
                  pl.BlockSpec((tm, KPE_PAD), tab), pl.BlockSpec((tm, KPE_PAD), tab), pl.BlockSpec((tm, KPE_PAD), tab),
                  pl.BlockSpec((tm, MLA_QPE), tab), pl.BlockSpec((tm, MLA_QPE), tab), pl.BlockSpec((tm, MLA_QPE), tab)],
        out_specs=(pl.BlockSpec((tm, MLA_QNOPE), row), pl.BlockSpec((tm, MLA_QPE), row), pl.BlockSpec((tm, KV_LORA), row),
                   pl.BlockSpec((tm, ROPE_DIM), row), pl.BlockSpec((tm, POOL_WIDTH), row)),
        compiler_params=_cparams(), name="in_odd",
    )(x2d, g.reshape(1, D_MODEL), w_perm_bf16, q_norm.reshape(1, Q_LORA), kv_norm.reshape(1, KV_LORA), wuq_perm_bf16,
      *ktabs, *qtabs)


def permute_odd_weights(w_in, w_uq):
    o1, o2 = Q_LORA + KV_LORA, Q_LORA + KV_LORA + ROPE_DIM
    kpe = jnp.pad(w_in[:, o1:o2], ((0, 0), (0, KPE_PAD - ROPE_DIM)))
    w_perm = jnp.concatenate([w_in[:, :o1], w_in[:, o2:], kpe], axis=1)
    wq = w_uq.reshape(Q_LORA, MLA_HEADS, NOPE_DIM + ROPE_DIM)
    wq_perm = jnp.concatenate([wq[:, :, :NOPE_DIM].reshape(Q_LORA, MLA_QNOPE),
                               wq[:, :, NOPE_DIM:].reshape(Q_LORA, MLA_QPE)], axis=1)
    return w_perm.astype(jnp.bfloat16), wq_perm.astype(jnp.bfloat16)


def _mla_kernel(qn_ref, qp_ref, c_ref, kp_ref, wuk_ref, wuv_ref, o_ref, *scratch, causal, qb, kb, n_keys, hpc):
    i = pl.program_id(1)
    n_chains = MLA_HEADS // hpc
    qa_s, qp_s, m_s, l_s, acc_s = (scratch[k * n_chains:(k + 1) * n_chains] for k in range(5))
    qn = qn_ref[0].astype(jnp.bfloat16)
    qp = qp_ref[0].astype(jnp.bfloat16)
    for g in range(n_chains):
        for hh in range(hpc):
            h = g * hpc + hh
            rows = slice(hh * qb, (hh + 1) * qb)
            qa_s[g][rows, :] = jnp.dot(qn[:, h * NOPE_DIM:(h + 1) * NOPE_DIM], wuk_ref[h],
                                       preferred_element_type=jnp.float32).astype(jnp.bfloat16)
            qp_s[g][rows, :] = qp[:, h * ROPE_DIM:(h + 1) * ROPE_DIM]
        m_s[g][...] = jnp.full(m_s[g].shape, NEG_INF, jnp.float32)
        l_s[g][...] = jnp.zeros(l_s[g].shape, jnp.float32)
        acc_s[g][...] = jnp.zeros(acc_s[g].shape, jnp.float32)
    if causal:
        qpos = i * qb + lax.broadcasted_iota(jnp.int32, (hpc * qb, kb), 0) % qb
        limit = (qpos // CHUNK + 1) * CHUNK
        nblk = ((i + 1) * qb + kb - 1) // kb
    else:
        limit = n_keys
        nblk = (n_keys + kb - 1) // kb
    kidx0 = lax.broadcasted_iota(jnp.int32, (hpc * qb, kb), 1)

    def body(j, carry):
        rows = pl.ds(pl.multiple_of(j * kb, kb), kb)
        cb = c_ref[0, rows, :]
        kpb = kp_ref[0, rows, :]
        visible = kidx0 + j * kb < limit
        ss = [(lax.dot_general(qa_s[g][...], cb, (((1,), (1,)), ((), ())), preferred_element_type=jnp.float32)
               + lax.dot_general(qp_s[g][...], kpb, (((1,), (1,)), ((), ())),
                                 preferred_element_type=jnp.float32)) * MLA_SCALE for g in range(n_chains)]
        ps, alphas = [], []
        for g in range(n_chains):
            s = jnp.where(visible, ss[g], NEG_INF)
            m_old = m_s[g][...]
            m_new = jnp.maximum(m_old, jnp.max(s, axis=-1, keepdims=True))
            alpha = jnp.exp(m_old - m_new)
            p = jnp.exp(s - m_new)
            m_s[g][...] = m_new
            l_s[g][...] = alpha * l_s[g][...] + jnp.sum(p, axis=-1, keepdims=True)
            ps.append(p.astype(jnp.bfloat16))
            alphas.append(alpha)
        for g in range(n_chains):
            acc_s[g][...] = alphas[g] * acc_s[g][...] + jnp.dot(ps[g], cb, preferred_element_type=jnp.float32)
        return carry

    lax.fori_loop(0, nblk, body, 0)
    outs = []
    for g in range(n_chains):
        o_lat = (acc_s[g][...] / l_s[g][...]).astype(jnp.bfloat16)
        for hh in range(hpc):
            outs.append(jnp.dot(o_lat[hh * qb:(hh + 1) * qb, :], wuv_ref[g * hpc + hh],
                                preferred_element_type=jnp.float32))
    o_ref[0] = jnp.concatenate(outs, axis=-1)


def mla_attention(q_nope, q_pe, c_keys_bf16, kp_keys_bf16, wuk_h, wuv_h, causal, n_keys):
    b, t, _ = q_nope.shape
    tk = c_keys_bf16.shape[1]
    qb = min(MLA_QBLOCK, t)
    kb = min(MLA_KBLOCK, tk)
    hpc = max(1, min(MLA_HEADS, MLA_CHAIN_ROWS // qb))
    n_chains, rows = MLA_HEADS // hpc, hpc * qb
    cur = lambda bi, i: (bi, i, 0)
    whole = lambda bi, i: (bi, 0, 0)
    const3 = lambda bi, i: (0, 0, 0)
    return pl.pallas_call(
        functools.partial(_mla_kernel, causal=causal, qb=qb, kb=kb, n_keys=n_keys, hpc=hpc),
        out_shape=jax.ShapeDtypeStruct((b, t, MLA_HEADS * V_DIM), jnp.float32),
        grid=(b, t // qb),
        in_specs=[pl.BlockSpec((1, qb, MLA_QNOPE), cur), pl.BlockSpec((1, qb, MLA_QPE), cur),
                  pl.BlockSpec((1, tk, KV_LORA), whole), pl.BlockSpec((1, tk, ROPE_DIM), whole),
                  pl.BlockSpec((MLA_HEADS, NOPE_DIM, KV_LORA), const3), pl.BlockSpec((MLA_HEADS, KV_LORA, V_DIM), const3)],
        out_specs=pl.BlockSpec((1, qb, MLA_HEADS * V_DIM), cur),
        scratch_shapes=([pltpu.VMEM((rows, KV_LORA), jnp.bfloat16)] * n_chains
                        + [pltpu.VMEM((rows, ROPE_DIM), jnp.bfloat16)] * n_chains
                        + [pltpu.VMEM((rows, 1), jnp.float32)] * (2 * n_chains)
                        + [pltpu.VMEM((rows, KV_LORA), jnp.float32)] * n_chains),
        compiler_params=_cparams(2), name="mla_attention",
    )(q_nope, q_pe, c_keys_bf16, kp_keys_bf16, wuk_h, wuv_h)


def _pool_kernel(u_ref, prev_ref, w_ref, scale_ref, o_ref, new_ref, ext, *, tm, pos0):
    j = pl.program_id(1)

    @pl.when(j == 0)
    def _():
        ext[0:1, :] = jnp.zeros((1, POOL_WIDTH), jnp.float32)
        ext[1:POOL_MAX, :] = prev_ref[0]

    @pl.when(j > 0)
    def _():
        ext[0:POOL_MAX, :] = ext[tm:tm + POOL_MAX, :]

    u = u_ref[0]
    ext[POOL_MAX:POOL_MAX + tm, :] = u
    pos = pos0 + j * tm + lax.broadcasted_iota(jnp.int32, (tm, POOL_GROUP), 0)
    outs = []
    for gi, w in enumerate(POOL_WINDOWS):
        cols = slice(gi * POOL_GROUP, (gi + 1) * POOL_GROUP)
        tot = u[:, cols]
        for d in range(1, w):
            tot = tot + ext[POOL_MAX - d:POOL_MAX - d + tm, cols]
        cnt = jnp.minimum(pos + 1, w).astype(jnp.float32)
        m = tot / cnt - u[:, cols]
        outs.append(jnp.dot(m.astype(jnp.bfloat16), w_ref[gi], preferred_element_type=jnp.float32))
    o_ref[0] = jnp.concatenate(outs, axis=-1) * scale_ref[...]
    new_ref[0] = ext[tm + 1:tm + POOL_MAX, :]


def pool_mixer(u, prev, pool_w_bf16, pool_scale, pos0):
    b, t, _ = u.shape
    tm = min(ROW_BLOCK, t)
    cur = lambda bi, j: (bi, j, 0)
    per_b = lambda bi, j: (bi, 0, 0)
    return pl.pallas_call(
        functools.partial(_pool_kernel, tm=tm, pos0=pos0),
        out_shape=(jax.ShapeDtypeStruct((b, t, POOL_WIDTH), jnp.float32),
                   jax.ShapeDtypeStruct((b, POOL_BUF, POOL_WIDTH), jnp.float32)),
        grid=(b, t // tm),
        in_specs=[pl.BlockSpec((1, tm, POOL_WIDTH), cur), pl.BlockSpec((1, POOL_BUF, POOL_WIDTH), per_b),
                  pl.BlockSpec((len(POOL_WINDOWS), POOL_GROUP, POOL_GROUP), lambda bi, j: (0, 0, 0)),
                  pl.BlockSpec((1, POOL_WIDTH), lambda bi, j: (0, 0))],
        out_specs=(pl.BlockSpec((1, tm, POOL_WIDTH), cur), pl.BlockSpec((1, POOL_BUF, POOL_WIDTH), per_b)),
        scratch_shapes=[pltpu.VMEM((tm + POOL_MAX, POOL_WIDTH), jnp.float32)],
        compiler_params=_cparams(2), name="pool_mixer",
    )(u, prev, pool_w_bf16, pool_scale.reshape(1, POOL_WIDTH))


def _top16_rows(s, n_rows):
    iota = lax.broadcasted_iota(jnp.int32, s.shape, 0).astype(jnp.float32)
    vals, idxs = [], []
    for _ in range(PEER_TOPK):
        m = jnp.max(s, axis=0, keepdims=True)
        idx = jnp.min(jnp.where(s == m, iota, float(n_rows)), axis=0, keepdims=True)
        vals.append(m)
        idxs.append(idx)
        s = jnp.where(iota == idx, -jnp.inf, s)
    return jnp.concatenate(vals, axis=0), jnp.concatenate(idxs, axis=0)


def _route_kernel(x_ref, g_ref, wq_ref, keys_ref, xn_ref, pair_ref, gate_ref, code_t, gate_t):
    xn = _rms(x_ref[...], g_ref[...])
    xn_ref[...] = xn
    xb = xn.astype(jnp.bfloat16)

    def head_step(hs, carry):
        for hh in range(ROUTE_HEADS_PER_STEP):
            one_head(hs * ROUTE_HEADS_PER_STEP + hh)
        return carry

    def one_head(h):
        qb = jnp.dot(xb, wq_ref[h], preferred_element_type=jnp.float32).astype(jnp.bfloat16)
        sv, si = [], []
        for p in range(2):
            s = lax.dot_general(keys_ref[h * 2 + p], qb[:, p * D_HALF:(p + 1) * D_HALF],
                                (((1,), (1,)), ((), ())), preferred_element_type=jnp.float32)
            v, i = _top16_rows(s, N_KEYS)
            sv.append(v)
            si.append(i)
        cand = jnp.concatenate([sv[0][a:a + 1] + sv[1][:nb] for a, nb in enumerate(PAIR_COLS)]
                               + [jnp.full((PAIR_PAD, s.shape[1]), -jnp.inf, jnp.float32)], axis=0)
        eid = jnp.concatenate([si[0][a:a + 1] * float(N_KEYS) + si[1][:nb] for a, nb in enumerate(PAIR_COLS)]
                              + [jnp.zeros((PAIR_PAD, s.shape[1]), jnp.float32)], axis=0)
        iota = lax.broadcasted_iota(jnp.int32, cand.shape, 0).astype(jnp.float32)
        cv, ce = [], []
        for _ in range(PEER_TOPK):
            m = jnp.max(cand, axis=0, keepdims=True)
            idx = jnp.min(jnp.where(cand == m, iota, float(PAIR_ROWS)), axis=0, keepdims=True)
            hit = iota == idx
            cv.append(m)
            ce.append(jnp.max(jnp.where(hit, eid, -1.0), axis=0, keepdims=True))
            cand = jnp.where(hit, -jnp.inf, cand)
        cv = jnp.concatenate(cv, axis=0)
        ce = jnp.concatenate(ce, axis=0).astype(jnp.int32)
        e = jnp.exp(cv - cv[0:1])
        rows = pl.ds(pl.multiple_of(h * PEER_TOPK, PEER_TOPK), PEER_TOPK)
        gate_t[rows, :] = e / jnp.sum(e, axis=0, keepdims=True)
        code_t[rows, :] = ((ce & (HALF_EXPERTS - 1)) << 3) | (ce >> 13)

    lax.fori_loop(0, PEER_HEADS // ROUTE_HEADS_PER_STEP, head_step, 0)
    ca, cb = code_t[:NSEL // 2, :], code_t[NSEL // 2:, :]
    pair_ref[...] = (ca | (((cb & 1) ^ 1) << 1) | ((cb & -8) << 16)).T
    gate_ref[...] = gate_t[...].T


def peer_route(x2d, g, wq_heads, keys_bf16, tb):
    n = x2d.shape[0]
    return pl.pallas_call(
        _route_kernel,
        out_shape=(jax.ShapeDtypeStruct((n, D_MODEL), jnp.float32),
                   jax.ShapeDtypeStruct((n, NSEL // 2), jnp.int32),
                   jax.ShapeDtypeStruct((n, NSEL), jnp.float32)),
        grid=(n // tb,),
        in_specs=[pl.BlockSpec((tb, D_MODEL), lambda i: (i, 0)),
                  pl.BlockSpec((1, D_MODEL), lambda i: (0, 0)),
                  pl.BlockSpec((PEER_HEADS, D_MODEL, D_KEY), lambda i: (0, 0, 0)),
                  pl.BlockSpec((PEER_HEADS * 2, N_KEYS, D_HALF), lambda i: (0, 0, 0))],
        out_specs=(pl.BlockSpec((tb, D_MODEL), lambda i: (i, 0)),
                   pl.BlockSpec((tb, NSEL // 2), lambda i: (i, 0)),
                   pl.BlockSpec((tb, NSEL), lambda i: (i, 0))),
        scratch_shapes=[pltpu.VMEM((NSEL, tb), jnp.int32), pltpu.VMEM((NSEL, tb), jnp.float32)],
        compiler_params=_cparams(), name="peer_route",
    )(x2d, g.reshape(1, D_MODEL), wq_heads, keys_bf16)


def pack_table(tab):
    b = lax.bitcast_convert_type(tab.astype(jnp.bfloat16), jnp.uint16).astype(jnp.uint32)
    w = b[:HALF_EXPERTS] | (b[HALF_EXPERTS:] << 16)
    return lax.bitcast_convert_type(w, jnp.int32).reshape(HALF_EXPERTS * ROW_TILE, LANES)


def _gather_pair(tab_ref, cab):
    wa = tab_ref[pl.ds(pl.multiple_of(cab & 0xFFF8, SUBLANES), SUBLANES), :]
    wb = tab_ref[pl.ds(pl.multiple_of(lax.shift_right_logical(cab, 16), SUBLANES), SUBLANES), :]
    cv = jnp.full((SUBLANES, LANES), cab, jnp.int32)
    sha = (cv & 1) << 4
    shb = (cv & 2) << 3
    return (lax.shift_right_logical(wa, sha) & 0xFFFF) | (lax.shift_left(wb, shb) & HI16)


def _gather_token(tab_ref, pair_ref, t, buf):
    for i in range(NSEL // 2):
        buf[i * ROW_TILE:(i + 1) * ROW_TILE, :] = _gather_pair(tab_ref, pair_ref[t, i])


def _grouped_token_loop(tb, gather, finish, group_a, group_b):
    for buf in group_b:
        buf[...] = jnp.zeros(buf.shape, jnp.int32)

    def body(i, carry):
        t0 = 2 * UPASS_GROUP * i
        for q in range(UPASS_GROUP):
            gather(t0 + q, group_a[q])
        finish(jnp.maximum(t0 - UPASS_GROUP, 0), group_b)
        for q in range(UPASS_GROUP):
            gather(t0 + UPASS_GROUP + q, group_b[q])
        finish(t0, group_a)
        return carry

    lax.fori_loop(0, tb // (2 * UPASS_GROUP), body, 0)
    finish(tb - UPASS_GROUP, group_b)


def _upass_kernel(pair_ref, gate_ref, xn_ref, tab_ref, w_ref, *bufs, tb):
    col = lax.broadcasted_iota(jnp.int32, (2 * ROW_TILE, SLOT_ROWS), 1)
    row = lax.broadcasted_iota(jnp.int32, (2 * ROW_TILE, SLOT_ROWS), 0)
    chunk_mask = (((col & 15) >> 1) == (row & 7)).astype(jnp.float32)
    c2 = lax.broadcasted_iota(jnp.int32, (SLOT_ROWS, NSEL), 0)
    j2 = lax.broadcasted_iota(jnp.int32, (SLOT_ROWS, NSEL), 1)
    fold = (j2 == (c2 >> 4) + ((c2 & 1) << 6)).astype(jnp.bfloat16)

    def gather(t, buf):
        _gather_token(tab_ref, pair_ref, t, buf)

    def finish(t0, group):
        zs = []
        for q, buf in enumerate(group):
            xt = xn_ref[pl.ds(t0 + q, 1), :].reshape(ROW_TILE, LANES)
            xhi = xt.astype(jnp.bfloat16)
            xlo = (xt - xhi.astype(jnp.float32)).astype(jnp.bfloat16)
            x16 = jnp.concatenate([xhi, xlo], axis=0)
            us = pltpu.bitcast(buf[...], jnp.bfloat16)
            r = lax.dot_general(x16, us, (((1,), (1,)), ((), ())), preferred_element_type=jnp.float32)
            zs.append(jnp.sum(r * chunk_mask, axis=0, keepdims=True))
        z = jnp.concatenate(zs, axis=0)
        zhi = z.astype(jnp.bfloat16)
        zlo = (z - zhi.astype(jnp.float32)).astype(jnp.bfloat16)
        act = (jnp.dot(zhi, fold, preferred_element_type=jnp.float32)
               + jnp.dot(zlo, fold, preferred_element_type=jnp.float32))
        gelu = 0.5 * act * (1.0 + lax.erf(act * (1.0 / math.sqrt(2.0))))
        rows = pl.ds(pl.multiple_of(t0, UPASS_GROUP), UPASS_GROUP)
        w_ref[rows, :] = (gate_ref[rows, :] * gelu).astype(jnp.bfloat16)

    _grouped_token_loop(tb, gather, finish, bufs[:UPASS_GROUP], bufs[UPASS_GROUP:])


def _peer_pass_call(kernel_fn, out_shape, out_block, pairs, per_token, rows, tab_packed, tb, name):
    n = pairs.shape[0]
    return pl.pallas_call(
        functools.partial(kernel_fn, tb=tb),
        out_shape=out_shape,
        grid=(n // tb,),
        in_specs=[pl.BlockSpec((tb, NSEL // 2), lambda i: (i, 0), memory_space=pltpu.SMEM),
                  pl.BlockSpec((tb, NSEL), lambda i: (i, 0)),
                  pl.BlockSpec((tb, D_MODEL), lambda i: (i, 0)),
                  pl.BlockSpec((HALF_EXPERTS * ROW_TILE, LANES), lambda i: (0, 0), pipeline_mode=pl.Buffered(1))],
        out_specs=pl.BlockSpec(out_block, lambda i: (i, 0)),
        scratch_shapes=[pltpu.VMEM((SLOT_ROWS // 2, LANES), jnp.int32)] * (2 * UPASS_GROUP),
        compiler_params=_cparams(), name=name,
    )(pairs, per_token, rows, tab_packed)


def peer_upass(pairs, gate, xn_rows, tab_packed, tb):
    n = pairs.shape[0]
    return _peer_pass_call(_upass_kernel, jax.ShapeDtypeStruct((n, NSEL), jnp.bfloat16), (tb, NSEL),
                           pairs, gate, xn_rows, tab_packed, tb, "peer_upass")


def _vpass_kernel(pair_ref, w_ref, x_ref, tab_ref, o_ref, *bufs, tb):
    col = lax.broadcasted_iota(jnp.int32, (ROW_TILE, SLOT_ROWS), 1)
    row = lax.broadcasted_iota(jnp.int32, (ROW_TILE, SLOT_ROWS), 0)
    chunk_mask = (((col & 15) >> 1) == row).astype(jnp.float32)
    j2 = lax.broadcasted_iota(jnp.int32, (NSEL, SLOT_ROWS), 0)
    c2 = lax.broadcasted_iota(jnp.int32, (NSEL, SLOT_ROWS), 1)
    spread = (j2 == (c2 >> 4) + ((c2 & 1) << 6)).astype(jnp.bfloat16)

    def gather(t, buf):
        _gather_token(tab_ref, pair_ref, t, buf)

    def finish(t0, group):
        rows = pl.ds(pl.multiple_of(t0, UPASS_GROUP), UPASS_GROUP)
        wexp = jnp.dot(w_ref[rows, :], spread, preferred_element_type=jnp.float32)
        for q, buf in enumerate(group):
            wsel = (jnp.broadcast_to(wexp[q:q + 1], (ROW_TILE, SLOT_ROWS)) * chunk_mask).astype(jnp.bfloat16)
            vs = pltpu.bitcast(buf[...], jnp.bfloat16)
            tile = jnp.dot(wsel, vs, preferred_element_type=jnp.float32)
            o_ref[pl.ds(t0 + q, 1), :] = x_ref[pl.ds(t0 + q, 1), :] + tile.reshape(1, D_MODEL)

    _grouped_token_loop(tb, gather, finish, bufs[:UPASS_GROUP], bufs[UPASS_GROUP:])


def peer_vpass(pairs, w_bf16, x_rows, tab_packed, tb):
    n = pairs.shape[0]
    return _peer_pass_call(_vpass_kernel, jax.ShapeDtypeStruct((n, D_MODEL), jnp.float32),
                           (tb, D_MODEL), pairs, w_bf16, x_rows, tab_packed, tb, "peer_vpass")


def peer_block(x, g, wq_heads, keys_bf16, u_packed, v_packed):
    shp = x.shape
    x2d = x.reshape(-1, D_MODEL)
    n = x2d.shape[0]
    xn, pairs, gate = peer_route(x2d, g, wq_heads, keys_bf16, min(PEER_ROUTE_BLOCK, n))
    w = peer_upass(pairs, gate, xn, u_packed, min(PEER_PASS_BLOCK, n))
    out = peer_vpass(pairs, w, x2d, v_packed, min(PEER_PASS_BLOCK, n))
    return out.reshape(shp)


def _rms_kernel(x_ref, g_ref, o_ref):
    o_ref[...] = _rms(x_ref[...], g_ref[...])


def rmsnorm_pallas(x, g):
    shp = x.shape
    xt = x.reshape(-1, shp[-1])
    n = xt.shape[0]
    tm = min(ROW_BLOCK, n)
    out = pl.pallas_call(
        _rms_kernel,
        out_shape=jax.ShapeDtypeStruct(xt.shape, xt.dtype),
        grid=(n // tm,),
        in_specs=[pl.BlockSpec((tm, shp[-1]), lambda i: (i, 0)),
                  pl.BlockSpec((1, shp[-1]), lambda i: (0, 0))],
        out_specs=pl.BlockSpec((tm, shp[-1]), lambda i: (i, 0)),
        compiler_params=_cparams(), name="final_rmsnorm",
    )(xt, g.reshape(1, -1))
    return out.reshape(shp)


def even_layer(x, pos0, k_prev, v_prev, h0r, h0i, norm_g, w_in, w_out, sink, s5_params, d_skip, w_glu, b_glu):
    b, t, _ = x.shape
    n = b * t
    x2d = x.reshape(n, D_MODEL)
    tabs = rope_tables(pos0 + jnp.arange(t), ROT_DIM, HEAD_DIM, SWA_Q, max(1, min(ROW_BLOCK, n) // t))
    q, k, v, u = in_even(x2d, norm_g, w_in.astype(jnp.bfloat16), tabs)
    q3, k3, v3 = q.reshape(b, t, SWA_Q), k.reshape(b, t, SWA_KV), v.reshape(b, t, SWA_KV)
    if k_prev is None:
        att = swa_attention(q3, k3, k3, v3, v3, sink, True)
        k_all, v_all = k3, v3
        h0r = jnp.zeros((b, S5_FLAT), jnp.float32)
        h0i = jnp.zeros((b, S5_FLAT), jnp.float32)
    else:
        kp, vp = k_prev.reshape(b, WINDOW, SWA_KV), v_prev.reshape(b, WINDOW, SWA_KV)
        att = swa_attention(q3, kp, k3, vp, v3, sink, False)
        k_all, v_all = jnp.concatenate([kp, k3], axis=1), jnp.concatenate([vp, v3], axis=1)
        h0r, h0i = h0r.reshape(b, S5_FLAT), h0i.reshape(b, S5_FLAT)
    lam, wb, wc = s5_discretize(*s5_params)
    u_tm = u.reshape(b, t, S5_WIDTH).transpose(1, 0, 2).reshape(n, S5_WIDTH)
    s5o_tm, hre, him = s5_mixer(u_tm, b, lam, wb, wc, d_skip, w_glu.astype(jnp.bfloat16), b_glu, h0r, h0i,
                                min(t, S5_STEPS))
    s5o = s5o_tm.reshape(t, b, S5_WIDTH).transpose(1, 0, 2).reshape(n, S5_WIDTH)
    out = out_proj(x2d, att.reshape(n, SWA_Q), s5o, w_out.astype(jnp.bfloat16))
    return (out.reshape(b, t, D_MODEL),
            k_all[:, -WINDOW:].reshape(b, WINDOW, SWA_KV_HEADS, HEAD_DIM),
            v_all[:, -WINDOW:].reshape(b, WINDOW, SWA_KV_HEADS, HEAD_DIM),
            hre.reshape(b, S5_GROUPS, S5_STATE), him.reshape(b, S5_GROUPS, S5_STATE))


def odd_layer(x, pos0, pool_prev, ckv_prev, kpe_prev, norm_g, w_in, w_out, pool_w, pool_scale,
              q_norm, kv_norm, w_uq, w_uk, w_uv):
    b, t, _ = x.shape
    n = b * t
    x2d = x.reshape(n, D_MODEL)
    reps = max(1, min(ROW_BLOCK, n) // t)
    pos = pos0 + jnp.arange(t)
    ktabs = rope_tables(pos, ROPE_DIM, KPE_PAD, KPE_PAD, reps)
    qtabs = rope_tables(pos, ROPE_DIM, ROPE_DIM, MLA_QPE, reps)
    w_perm, wuq_perm = permute_odd_weights(w_in, w_uq)
    qnope, qpe, c, kp, u = in_odd(x2d, norm_g, w_perm, q_norm, kv_norm, wuq_perm, ktabs, qtabs)
    c3, kp3 = c.reshape(b, t, KV_LORA), kp.reshape(b, t, ROPE_DIM)
    wuk_h = w_uk.transpose(1, 2, 0).astype(jnp.bfloat16)
    wuv_h = w_uv.transpose(1, 0, 2).astype(jnp.bfloat16)
    if ckv_prev is None:
        ck, kk, causal, n_keys = c3, kp3, True, t
        pool_prev = jnp.zeros((b, POOL_BUF, POOL_WIDTH), jnp.float32)
    else:
        ck, kk, causal = jnp.concatenate([ckv_prev, c3], axis=1), jnp.concatenate([kpe_prev, kp3], axis=1), False
        n_keys = ck.shape[1]
        pad = -n_keys % min(MLA_KBLOCK, n_keys)
        ck, kk = jnp.pad(ck, ((0, 0), (0, pad), (0, 0))), jnp.pad(kk, ((0, 0), (0, pad), (0, 0)))
    mla = mla_attention(qnope.reshape(b, t, MLA_QNOPE), qpe.reshape(b, t, MLA_QPE), ck.astype(jnp.bfloat16),
                        kk.astype(jnp.bfloat16), wuk_h, wuv_h, causal, n_keys)
    pool_out, pool_new = pool_mixer(u.reshape(b, t, POOL_WIDTH), pool_prev, pool_w.astype(jnp.bfloat16), pool_scale, pos0)
    out = out_proj(x2d, pool_out.reshape(n, POOL_WIDTH), mla.reshape(n, MLA_HEADS * V_DIM), w_out.astype(jnp.bfloat16))
    return out.reshape(b, t, D_MODEL), pool_new, c3, kp3


def kernel(x_prompt, x_sample, cache_swa_k, cache_swa_v, state_ssm_re, state_ssm_im, state_pool,
           cache_mla_ckv, cache_mla_kpe, norm_mix, norm_ffn, norm_final, w_in_even, w_out_even,
           swa_sink, s5_lam_re, s5_lam_im, s5_log_dt, s5_b_re, s5_b_im, s5_c_re, s5_c_im, s5_d,
           s5_w_glu, s5_b_glu, w_in_odd, w_out_odd, pool_w, pool_scale, mla_q_norm, mla_kv_norm,
           mla_w_uq, mla_w_uk, mla_w_uv, peer_w_q, peer_keys, peer_u, peer_v):
    xp, xs = x_prompt, x_sample
    kp_l, vp_l, rp_l, ip_l, poolp_l, cp_l, ep_l = [], [], [], [], [], [], []
    ks_l, vs_l, rs_l, is_l, pools_l, cs_l, es_l = [], [], [], [], [], [], []
    for layer in range(DEPTH):
        i = layer // 2
        if layer % 2 == 0:
            s5_params = (s5_lam_re[i], s5_lam_im[i], s5_log_dt[i], s5_b_re[i], s5_b_im[i], s5_c_re[i], s5_c_im[i])
            ew = (norm_mix[layer], w_in_even[i], w_out_even[i], swa_sink[i], s5_params, s5_d[i], s5_w_glu[i], s5_b_glu[i])
            xp, k1, v1, r1, i1 = even_layer(xp, 0, None, None, None, None, *ew)
            xs, k2, v2, r2, i2 = even_layer(xs, PAST_LEN, cache_swa_k[i], cache_swa_v[i],
                                            state_ssm_re[i], state_ssm_im[i], *ew)
            kp_l.append(k1); vp_l.append(v1); rp_l.append(r1); ip_l.append(i1)
            ks_l.append(k2); vs_l.append(v2); rs_l.append(r2); is_l.append(i2)
        else:
            ow = (norm_mix[layer], w_in_odd[i], w_out_odd[i], pool_w[i], pool_scale[i], mla_q_norm[i], mla_kv_norm[i],
                  mla_w_uq[i], mla_w_uk[i], mla_w_uv[i])
            xp, p1, c1, e1 = odd_layer(xp, 0, None, None, None, *ow)
            xs, p2, c2, e2 = odd_layer(xs, PAST_LEN, state_pool[i], cache_mla_ckv[i], cache_mla_kpe[i], *ow)
            poolp_l.append(p1); cp_l.append(c1); ep_l.append(e1)
            pools_l.append(p2); cs_l.append(c2); es_l.append(e2)
        wq_heads = peer_w_q[layer].reshape(D_MODEL, PEER_HEADS, D_KEY).transpose(1, 0, 2).astype(jnp.bfloat16)
        keys_bf16 = peer_keys[layer].reshape(PEER_HEADS * 2, N_KEYS, D_HALF).astype(jnp.bfloat16)
        u_packed, v_packed = pack_table(peer_u[layer]), pack_table(peer_v[layer])
        xp = peer_block(xp, norm_ffn[layer], wq_heads, keys_bf16, u_packed, v_packed)
        xs = peer_block(xs, norm_ffn[layer], wq_heads, keys_bf16, u_packed, v_packed)
    y_prompt = rmsnorm_pallas(xp, norm_final)
    y_sample = rmsnorm_pallas(xs, norm_final)
    return (y_prompt, y_sample,
            jnp.stack(kp_l), jnp.stack(vp_l), jnp.stack(rp_l), jnp.stack(ip_l),
            jnp.stack(poolp_l), jnp.stack(cp_l), jnp.stack(ep_l),
            jnp.stack(ks_l), jnp.stack(vs_l), jnp.stack(rs_l), jnp.stack(is_l),
            jnp.stack(pools_l), jnp.stack(cs_l), jnp.stack(es_l))
```

```python
import functools
import math
import jax
import jax.numpy as jnp
from jax import lax
from jax.experimental import pallas as pl
from jax.experimental.pallas import tpu as pltpu

D_MODEL = 1024
DEPTH = 2
PAST_LEN = 2048

CHUNK = 64
RMS_EPS = 1e-6
ROPE_THETA = 500000.0
NEG_INF = -1e30

SWA_HEADS = 8
SWA_KV_HEADS = 2
SWA_GROUP = SWA_HEADS // SWA_KV_HEADS
HEAD_DIM = 64
ROT_DIM = HEAD_DIM // 4
WINDOW = 128
SWA_Q = SWA_HEADS * HEAD_DIM
SWA_KV = SWA_KV_HEADS * HEAD_DIM
SWA_SCALE = HEAD_DIM ** -0.5

S5_WIDTH = 512
S5_GROUP = 16
S5_GROUPS = S5_WIDTH // S5_GROUP
S5_STATE = 64
S5_FLAT = S5_GROUPS * S5_STATE

POOL_WIDTH = 512
POOL_WINDOWS = (2, 4, 8, 16)
POOL_GROUP = POOL_WIDTH // len(POOL_WINDOWS)
POOL_MAX = 16
POOL_BUF = POOL_MAX - 1

MLA_HEADS = 8
Q_LORA = 512
KV_LORA = 256
NOPE_DIM = 64
ROPE_DIM = 32
V_DIM = 64
MLA_SCALE = (NOPE_DIM + ROPE_DIM) ** -0.5
MLA_QNOPE = MLA_HEADS * NOPE_DIM
MLA_QPE = MLA_HEADS * ROPE_DIM

EVEN_IN = SWA_Q + 2 * SWA_KV + S5_WIDTH

PEER_HEADS = 8
N_KEYS = 128
N_EXPERTS = N_KEYS * N_KEYS
D_KEY = 128
D_HALF = D_KEY // 2
PEER_TOPK = 16
NSEL = PEER_HEADS * PEER_TOPK
HALF_EXPERTS = N_EXPERTS // 2

SUBLANES = 8
LANES = 128
VMEM_LIMIT = 56 * 1024 * 1024
ROW_TILE = D_MODEL // LANES

ROW_BLOCK = 512
SWA_QBLOCK = 256
MLA_QBLOCK = 256
MLA_KBLOCK = 512
MLA_CHAIN_ROWS = 256
KPE_PAD = LANES
S5_STEPS = 128
S5_CARRY_VREGS = 16
PEER_ROUTE_BLOCK = 256
ROUTE_HEADS_PER_STEP = 4
PEER_PASS_BLOCK = 256
PACK_BLOCK = 256
UPASS_GROUP = 16
SLOT_ROWS = NSEL * ROW_TILE
HI16 = -65536
PAIR_COLS = tuple(PEER_TOPK // (a + 1) for a in range(PEER_TOPK))
PAIR_ROWS = -(-sum(PAIR_COLS) // SUBLANES) * SUBLANES
PAIR_PAD = PAIR_ROWS - sum(PAIR_COLS)

ODD_U0 = Q_LORA + KV_LORA
ODD_KPE0 = ODD_U0 + POOL_WIDTH
ODD_IN_PAD = ODD_KPE0 + KPE_PAD


def _cparams(n_axes=1):
    return pltpu.CompilerParams(dimension_semantics=("arbitrary",) * n_axes, vmem_limit_bytes=VMEM_LIMIT)


def _rms(x, g):
    return x * lax.rsqrt(jnp.mean(x * x, axis=-1, keepdims=True) + RMS_EPS) * g


def _rope_lanes(x, cos, sin_lo, sin_hi, half):
    n = x.shape[-1]
    return x * cos + pltpu.roll(x, n - half, 1) * sin_lo + pltpu.roll(x, half, 1) * sin_hi


def rope_tables(pos, rot, period, width, reps):
    inv = ROPE_THETA ** (-jnp.arange(0, rot, 2, dtype=jnp.float32) / rot)
    ang = pos.astype(jnp.float32)[:, None] * inv[None, :]
    cos, sin = jnp.cos(ang), jnp.sin(ang)
    lane = jnp.arange(width) % period
    idx = lane % (rot // 2)
    in_lo = lane < rot // 2
    in_hi = (lane >= rot // 2) & (lane < rot)
    c = jnp.where((in_lo | in_hi)[None, :], cos[:, idx], 1.0)
    s_lo = jnp.where(in_lo[None, :], -sin[:, idx], 0.0)
    s_hi = jnp.where(in_hi[None, :], sin[:, idx], 0.0)
    return tuple(jnp.tile(t, (reps, 1)) for t in (c, s_lo, s_hi))


def _in_even_kernel(x_ref, g_ref, w_ref, cos_ref, slo_ref, shi_ref, q_ref, k_ref, v_ref, u_ref):
    xn = _rms(x_ref[...], g_ref[...])
    proj = jnp.dot(xn.astype(jnp.bfloat16), w_ref[...], preferred_element_type=jnp.float32)
    cos, slo, shi = cos_ref[...], slo_ref[...], shi_ref[...]
    q_ref[...] = _rope_lanes(proj[:, :SWA_Q], cos, slo, shi, ROT_DIM // 2)
    k_ref[...] = _rope_lanes(proj[:, SWA_Q:SWA_Q + SWA_KV], cos[:, :SWA_KV], slo[:, :SWA_KV], shi[:, :SWA_KV],
                             ROT_DIM // 2)
    v_ref[...] = proj[:, SWA_Q + SWA_KV:SWA_Q + 2 * SWA_KV]
    u_ref[...] = proj[:, SWA_Q + 2 * SWA_KV:]


def in_even(x2d, g, w_bf16, tabs):
    n = x2d.shape[0]
    tm = min(ROW_BLOCK, n)
    nt = tabs[0].shape[0] // tm
    row = lambda i: (i, 0)
    const = lambda i: (0, 0)
    tab = lambda i: (i % nt, 0)
    return pl.pallas_call(
        _in_even_kernel,
        out_shape=(jax.ShapeDtypeStruct((n, SWA_Q), jnp.float32), jax.ShapeDtypeStruct((n, SWA_KV), jnp.float32),
                   jax.ShapeDtypeStruct((n, SWA_KV), jnp.float32), jax.ShapeDtypeStruct((n, S5_WIDTH), jnp.float32)),
        grid=(n // tm,),
        in_specs=[pl.BlockSpec((tm, D_MODEL), row), pl.BlockSpec((1, D_MODEL), const),
                  pl.BlockSpec((D_MODEL, EVEN_IN), const),
                  pl.BlockSpec((tm, SWA_Q), tab), pl.BlockSpec((tm, SWA_Q), tab), pl.BlockSpec((tm, SWA_Q), tab)],
        out_specs=(pl.BlockSpec((tm, SWA_Q), row), pl.BlockSpec((tm, SWA_KV), row),
                   pl.BlockSpec((tm, SWA_KV), row), pl.BlockSpec((tm, S5_WIDTH), row)),
        compiler_params=_cparams(), name="in_even",
    )(x2d, g.reshape(1, D_MODEL), w_bf16, *tabs)


def _swa_kernel(sink_ref, q_ref, kp_ref, kc_ref, vp_ref, vc_ref, o_ref, *, banded, qb):
    i = pl.program_id(1)
    q = q_ref[0]
    k = jnp.concatenate([kp_ref[0], kc_ref[0]], axis=0).astype(jnp.bfloat16)
    v = jnp.concatenate([vp_ref[0], vc_ref[0]], axis=0).astype(jnp.bfloat16)
    nk = WINDOW + qb
    wc = WINDOW // CHUNK
    if banded:
        qc = lax.broadcasted_iota(jnp.int32, (qb, nk), 0) // CHUNK + wc
        kc = lax.broadcasted_iota(jnp.int32, (qb, nk), 1) // CHUNK
        visible = (kc <= qc) & (kc >= qc - wc) & ((kc >= wc) | (i > 0))
    ss = []
    for h in range(SWA_HEADS):
        hk = h // SWA_GROUP
        qh = q[:, h * HEAD_DIM:(h + 1) * HEAD_DIM].astype(jnp.bfloat16)
        ss.append(lax.dot_general(qh, k[:, hk * HEAD_DIM:(hk + 1) * HEAD_DIM], (((1,), (1,)), ((), ())),
                                  preferred_element_type=jnp.float32) * SWA_SCALE)
    ps, dens = [], []
    for h in range(SWA_HEADS):
        s = jnp.where(visible, ss[h], NEG_INF) if banded else ss[h]
        sk = sink_ref[h]
        m = jnp.maximum(jnp.max(s, axis=-1, keepdims=True), sk)
        p = jnp.exp(s - m)
        dens.append(jnp.sum(p, axis=-1, keepdims=True) + jnp.exp(sk - m))
        ps.append(p.astype(jnp.bfloat16))
    outs = [jnp.dot(ps[h], v[:, (h // SWA_GROUP) * HEAD_DIM:(h // SWA_GROUP + 1) * HEAD_DIM],
                    preferred_element_type=jnp.float32) / dens[h] for h in range(SWA_HEADS)]
    o_ref[0] = jnp.concatenate(outs, axis=-1)


def swa_attention(q, k_prev, k_cur, v_prev, v_cur, sink, banded):
    b, t, _ = q.shape
    qb = min(SWA_QBLOCK, t)
    per = qb // WINDOW
    prev_map = (lambda bi, i: (bi, jnp.maximum(i * per - 1, 0), 0)) if banded else (lambda bi, i: (bi, 0, 0))
    cur = lambda bi, i: (bi, i, 0)
    return pl.pallas_call(
        functools.partial(_swa_kernel, banded=banded, qb=qb),
        out_shape=jax.ShapeDtypeStruct((b, t, SWA_Q), jnp.float32),
        grid=(b, t // qb),
        in_specs=[pl.BlockSpec(memory_space=pltpu.SMEM),
                  pl.BlockSpec((1, qb, SWA_Q), cur),
                  pl.BlockSpec((1, WINDOW, SWA_KV), prev_map), pl.BlockSpec((1, qb, SWA_KV), cur),
                  pl.BlockSpec((1, WINDOW, SWA_KV), prev_map), pl.BlockSpec((1, qb, SWA_KV), cur)],
        out_specs=pl.BlockSpec((1, qb, SWA_Q), cur),
        compiler_params=_cparams(2), name="swa_attention",
    )(sink, q, k_prev, k_cur, v_prev, v_cur)


def _out_kernel(x_ref, a_ref, b_ref, w_ref, o_ref):
    ka = a_ref.shape[-1]
    o_ref[...] = (x_ref[...]
                  + jnp.dot(a_ref[...].astype(jnp.bfloat16), w_ref[:ka, :], preferred_element_type=jnp.float32)
                  + jnp.dot(b_ref[...].astype(jnp.bfloat16), w_ref[ka:, :], preferred_element_type=jnp.float32))


def out_proj(x2d, a, b, w_bf16):
    n = x2d.shape[0]
    tm = min(ROW_BLOCK, n)
    row = lambda i: (i, 0)
    return pl.pallas_call(
        _out_kernel,
        out_shape=jax.ShapeDtypeStruct((n, D_MODEL), jnp.float32),
        grid=(n // tm,),
        in_specs=[pl.BlockSpec((tm, D_MODEL), row), pl.BlockSpec((tm, a.shape[1]), row),
                  pl.BlockSpec((tm, b.shape[1]), row), pl.BlockSpec(w_bf16.shape, lambda i: (0, 0))],
        out_specs=pl.BlockSpec((tm, D_MODEL), row),
        compiler_params=_cparams(), name="out_proj",
    )(x2d, a, b, w_bf16)


def s5_discretize(lam_re, lam_im, log_dt, b_re, b_im, c_re, c_im):
    lr = jnp.minimum(lam_re, -1e-4)
    li = lam_im
    dt = jnp.exp(log_dt)[:, None]
    mag = jnp.exp(lr * dt)
    ang = li * dt
    ab_re, ab_im = mag * jnp.cos(ang), mag * jnp.sin(ang)
    den = lr * lr + li * li
    nr, ni = ab_re - 1.0, ab_im
    f_re = (nr * lr + ni * li) / den
    f_im = (ni * lr - nr * li) / den
    bb_re = f_re[..., None] * b_re - f_im[..., None] * b_im
    bb_im = f_re[..., None] * b_im + f_im[..., None] * b_re
    eye = jnp.eye(S5_GROUPS, dtype=jnp.float32)

    def embed_b(bb):
        return jnp.einsum('gnc,gh->gchn', bb, eye).reshape(S5_WIDTH, S5_FLAT)

    def embed_c(c):
        return jnp.einsum('gcn,gh->gnhc', c, eye).reshape(S5_FLAT, S5_WIDTH)

    wb = jnp.concatenate([embed_b(bb_re), embed_b(bb_im)], axis=1)
    wc = jnp.concatenate([embed_c(c_re), -embed_c(c_im)], axis=0)
    lam = jnp.stack([ab_re.reshape(S5_FLAT), ab_im.reshape(S5_FLAT)])
    return lam, wb.astype(jnp.bfloat16), wc.astype(jnp.bfloat16)


def _s5_kernel(u_ref, lam_ref, wb_ref, wc_ref, d_ref, wglu_ref, bglu_ref, h0r_ref, h0i_ref,
               o_ref, hr_ref, hi_ref, hbuf, *, nb, steps, width):
    @pl.when(pl.program_id(0) == 0)
    def _():
        hr_ref[...] = h0r_ref[...]
        hi_ref[...] = h0i_ref[...]

    u = u_ref[...]
    hbuf[...] = jnp.dot(u.astype(jnp.bfloat16), wb_ref[...], preferred_element_type=jnp.float32)
    for c0 in range(0, S5_FLAT, width):
        cre = slice(c0, c0 + width)
        cim = slice(S5_FLAT + c0, S5_FLAT + c0 + width)
        lr = jnp.broadcast_to(lam_ref[0:1, cre], (nb, width))
        li = jnp.broadcast_to(lam_ref[1:2, cre], (nb, width))

        def step(t, carry):
            hr, hi = carry
            rows = pl.ds(pl.multiple_of(t * nb, nb), nb)
            nhr = lr * hr - li * hi + hbuf[rows, cre]
            nhi = lr * hi + li * hr + hbuf[rows, cim]
            hbuf[rows, cre] = nhr
            hbuf[rows, cim] = nhi
            return nhr, nhi

        hr, hi = lax.fori_loop(0, steps, step, (hr_ref[:, cre], hi_ref[:, cre]))
        hr_ref[:, cre] = hr
        hi_ref[:, cre] = hi
    y = jnp.dot(hbuf[...].astype(jnp.bfloat16), wc_ref[...], preferred_element_type=jnp.float32) + d_ref[...] * u
    z = 0.5 * y * (1.0 + jnp.tanh(math.sqrt(2.0 / math.pi) * (y + 0.044715 * (y * y * y))))
    gate = jnp.dot(z.astype(jnp.bfloat16), wglu_ref[...], preferred_element_type=jnp.float32) + bglu_ref[...]
    o_ref[...] = z * (1.0 / (1.0 + jnp.exp(-gate)))


def s5_mixer(u_tm, nb, lam, wb, wc, d_skip, w_glu_bf16, b_glu, h0r, h0i, steps):
    rows = u_tm.shape[0]
    t_total = rows // nb
    width = min(S5_FLAT, max(LANES, (S5_CARRY_VREGS * SUBLANES * LANES) // (2 * nb)))
    blk = steps * nb
    const = lambda i: (0, 0)
    return pl.pallas_call(
        functools.partial(_s5_kernel, nb=nb, steps=steps, width=width),
        out_shape=(jax.ShapeDtypeStruct((rows, S5_WIDTH), jnp.float32),
                   jax.ShapeDtypeStruct((nb, S5_FLAT), jnp.float32),
                   jax.ShapeDtypeStruct((nb, S5_FLAT), jnp.float32)),
        grid=(t_total // steps,),
        in_specs=[pl.BlockSpec((blk, S5_WIDTH), lambda i: (i, 0)),
                  pl.BlockSpec((2, S5_FLAT), const),
                  pl.BlockSpec((S5_WIDTH, 2 * S5_FLAT), const),
                  pl.BlockSpec((2 * S5_FLAT, S5_WIDTH), const),
                  pl.BlockSpec((1, S5_WIDTH), const),
                  pl.BlockSpec((S5_WIDTH, S5_WIDTH), const),
                  pl.BlockSpec((1, S5_WIDTH), const),
                  pl.BlockSpec((nb, S5_FLAT), const),
                  pl.BlockSpec((nb, S5_FLAT), const)],
        out_specs=(pl.BlockSpec((blk, S5_WIDTH), lambda i: (i, 0)),
                   pl.BlockSpec((nb, S5_FLAT), const),
                   pl.BlockSpec((nb, S5_FLAT), const)),
        scratch_shapes=[pltpu.VMEM((blk, 2 * S5_FLAT), jnp.float32)],
        compiler_params=_cparams(), name="s5_mixer",
    )(u_tm, lam, wb, wc, d_skip.reshape(1, S5_WIDTH), w_glu_bf16, b_glu.reshape(1, S5_WIDTH), h0r, h0i)


def _in_odd_kernel(x_ref, g_ref, w_ref, qn_ref, kvn_ref, wuq_ref, kcos_ref, kslo_ref, kshi_ref,
                   qcos_ref, qslo_ref, qshi_ref, qnope_ref, qpe_ref, c_ref, kp_ref, u_ref):
    xn = _rms(x_ref[...], g_ref[...])
    proj = jnp.dot(xn.astype(jnp.bfloat16), w_ref[...], preferred_element_type=jnp.float32)
    cqn = _rms(proj[:, :Q_LORA], qn_ref[...])
    q = jnp.dot(cqn.astype(jnp.bfloat16), wuq_ref[...], preferred_element_type=jnp.float32)
    qnope_ref[...] = q[:, :MLA_QNOPE]
    qpe_ref[...] = _rope_lanes(q[:, MLA_QNOPE:], qcos_ref[...], qslo_ref[...], qshi_ref[...], ROPE_DIM // 2)
    c_ref[...] = _rms(proj[:, Q_LORA:ODD_U0], kvn_ref[...])
    kp = _rope_lanes(proj[:, ODD_KPE0:], kcos_ref[...], kslo_ref[...], kshi_ref[...], ROPE_DIM // 2)
    kp_ref[...] = kp[:, :ROPE_DIM]
    u_ref[...] = proj[:, ODD_U0:ODD_KPE0]


def in_odd(x2d, g, w_perm_bf16, q_norm, kv_norm, wuq_perm_bf16, ktabs, qtabs):
    n = x2d.shape[0]
    tm = min(ROW_BLOCK, n)
    nt = ktabs[0].shape[0] // tm
    row = lambda i: (i, 0)
    const = lambda i: (0, 0)
    tab = lambda i: (i % nt, 0)
    return pl.pallas_call(
        _in_odd_kernel,
        out_shape=(jax.ShapeDtypeStruct((n, MLA_QNOPE), jnp.float32), jax.ShapeDtypeStruct((n, MLA_QPE), jnp.float32),
                   jax.ShapeDtypeStruct((n, KV_LORA), jnp.float32), jax.ShapeDtypeStruct((n, ROPE_DIM), jnp.float32),
                   jax.ShapeDtypeStruct((n, POOL_WIDTH), jnp.float32)),
        grid=(n // tm,),
        in_specs=[pl.BlockSpec((tm, D_MODEL), row), pl.BlockSpec((1, D_MODEL), const),
                  pl.BlockSpec((D_MODEL, ODD_IN_PAD), const),
                  pl.BlockSpec((1, Q_LORA), const), pl.BlockSpec((1, KV_LORA), const),
                  pl.BlockSpec((Q_LORA, MLA_QNOPE + MLA_QPE), const),
                  pl.BlockSpec((tm, KPE_PAD), tab), pl.BlockSpec((tm, KPE_PAD), tab), pl.BlockSpec((tm, KPE_PAD), tab),
                  pl.BlockSpec((tm, MLA_QPE), tab), pl.BlockSpec((tm, MLA_QPE), tab), pl.BlockSpec((tm, MLA_QPE), tab)],
        out_specs=(pl.BlockSpec((tm, MLA_QNOPE), row), pl.BlockSpec((tm, MLA_QPE), row), pl.BlockSpec((tm, KV_LORA), row),
                   pl.BlockSpec((tm, ROPE_DIM), row), pl.BlockSpec((tm, POOL_WIDTH), row)),
        compiler_params=_cparams(), name="in_odd",
    )(x2d, g.reshape(1, D_MODEL), w_perm_bf16, q_norm.reshape(1, Q_LORA), kv_norm.reshape(1, KV_LORA), wuq_perm_bf16,
      *ktabs, *qtabs)


def permute_odd_weights(w_in, w_uq):
    o1, o2 = Q_LORA + KV_LORA, Q_LORA + KV_LORA + ROPE_DIM
    kpe = jnp.pad(w_in[:, o1:o2], ((0, 0), (0, KPE_PAD - ROPE_DIM)))
    w_perm = jnp.concatenate([w_in[:, :o1], w_in[:, o2:], kpe], axis=1)
    wq = w_uq.reshape(Q_LORA, MLA_HEADS, NOPE_DIM + ROPE_DIM)
    wq_perm = jnp.concatenate([wq[:, :, :NOPE_DIM].reshape(Q_LORA, MLA_QNOPE),
                               wq[:, :, NOPE_DIM:].reshape(Q_LORA, MLA_QPE)], axis=1)
    return w_perm.astype(jnp.bfloat16), wq_perm.astype(jnp.bfloat16)


def _mla_kernel(qn_ref, qp_ref, c_ref, kp_ref, wuk_ref, wuv_ref, o_ref, *scratch, causal, qb, kb, n_keys, hpc):
    i = pl.program_id(1)
    n_chains = MLA_HEADS // hpc
    qa_s, qp_s, m_s, l_s, acc_s = (scratch[k * n_chains:(k + 1) * n_chains] for k in range(5))
    qn = qn_ref[0].astype(jnp.bfloat16)
    qp = qp_ref[0].astype(jnp.bfloat16)
    for g in range(n_chains):
        for hh in range(hpc):
            h = g * hpc + hh
            rows = slice(hh * qb, (hh + 1) * qb)
            qa_s[g][rows, :] = jnp.dot(qn[:, h * NOPE_DIM:(h + 1) * NOPE_DIM], wuk_ref[h],
                                       preferred_element_type=jnp.float32).astype(jnp.bfloat16)
            qp_s[g][rows, :] = qp[:, h * ROPE_DIM:(h + 1) * ROPE_DIM]
        m_s[g][...] = jnp.full(m_s[g].shape, NEG_INF, jnp.float32)
        l_s[g][...] = jnp.zeros(l_s[g].shape, jnp.float32)
        acc_s[g][...] = jnp.zeros(acc_s[g].shape, jnp.float32)
    if causal:
        qpos = i * qb + lax.broadcasted_iota(jnp.int32, (hpc * qb, kb), 0) % qb
        limit = (qpos // CHUNK + 1) * CHUNK
        nblk = ((i + 1) * qb + kb - 1) // kb
    else:
        limit = n_keys
        nblk = (n_keys + kb - 1) // kb
    kidx0 = lax.broadcasted_iota(jnp.int32, (hpc * qb, kb), 1)

    def body(j, carry):
        rows = pl.ds(pl.multiple_of(j * kb, kb), kb)
        cb = c_ref[0, rows, :]
        kpb = kp_ref[0, rows, :]
        visible = kidx0 + j * kb < limit
        ss = [(lax.dot_general(qa_s[g][...], cb, (((1,), (1,)), ((), ())), preferred_element_type=jnp.float32)
               + lax.dot_general(qp_s[g][...], kpb, (((1,), (1,)), ((), ())),
                                 preferred_element_type=jnp.float32)) * MLA_SCALE for g in range(n_chains)]
        ps, alphas = [], []
        for g in range(n_chains):
            s = jnp.where(visible, ss[g], NEG_INF)
            m_old = m_s[g][...]
            m_new = jnp.maximum(m_old, jnp.max(s, axis=-1, keepdims=True))
            alpha = jnp.exp(m_old - m_new)
            p = jnp.exp(s - m_new)
            m_s[g][...] = m_new
            l_s[g][...] = alpha * l_s[g][...] + jnp.sum(p, axis=-1, keepdims=True)
            ps.append(p.astype(jnp.bfloat16))
            alphas.append(alpha)
        for g in range(n_chains):
            acc_s[g][...] = alphas[g] * acc_s[g][...] + jnp.dot(ps[g], cb, preferred_element_type=jnp.float32)
        return carry

    lax.fori_loop(0, nblk, body, 0)
    outs = []
    for g in range(n_chains):
        o_lat = (acc_s[g][...] / l_s[g][...]).astype(jnp.bfloat16)
        for hh in range(hpc):
            outs.append(jnp.dot(o_lat[hh * qb:(hh + 1) * qb, :], wuv_ref[g * hpc + hh],
                                preferred_element_type=jnp.float32))
    o_ref[0] = jnp.concatenate(outs, axis=-1)


def mla_attention(q_nope, q_pe, c_keys_bf16, kp_keys_bf16, wuk_h, wuv_h, causal, n_keys):
    b, t, _ = q_nope.shape
    tk = c_keys_bf16.shape[1]
    qb = min(MLA_QBLOCK, t)
    kb = min(MLA_KBLOCK, tk)
    hpc = max(1, min(MLA_HEADS, MLA_CHAIN_ROWS // qb))
    n_chains, rows = MLA_HEADS // hpc, hpc * qb
    cur = lambda bi, i: (bi, i, 0)
    whole = lambda bi, i: (bi, 0, 0)
    const3 = lambda bi, i: (0, 0, 0)
    return pl.pallas_call(
        functools.partial(_mla_kernel, causal=causal, qb=qb, kb=kb, n_keys=n_keys, hpc=hpc),
        out_shape=jax.ShapeDtypeStruct((b, t, MLA_HEADS * V_DIM), jnp.float32),
        grid=(b, t // qb),
        in_specs=[pl.BlockSpec((1, qb, MLA_QNOPE), cur), pl.BlockSpec((1, qb, MLA_QPE), cur),
                  pl.BlockSpec((1, tk, KV_LORA), whole), pl.BlockSpec((1, tk, ROPE_DIM), whole),
                  pl.BlockSpec((MLA_HEADS, NOPE_DIM, KV_LORA), const3), pl.BlockSpec((MLA_HEADS, KV_LORA, V_DIM), const3)],
        out_specs=pl.BlockSpec((1, qb, MLA_HEADS * V_DIM), cur),
        scratch_shapes=([pltpu.VMEM((rows, KV_LORA), jnp.bfloat16)] * n_chains
                        + [pltpu.VMEM((rows, ROPE_DIM), jnp.bfloat16)] * n_chains
                        + [pltpu.VMEM((rows, 1), jnp.float32)] * (2 * n_chains)
                        + [pltpu.VMEM((rows, KV_LORA), jnp.float32)] * n_chains),
        compiler_params=_cparams(2), name="mla_attention",
    )(q_nope, q_pe, c_keys_bf16, kp_keys_bf16, wuk_h, wuv_h)


def _pool_kernel(u_ref, prev_ref, w_ref, scale_ref, o_ref, new_ref, ext, *, tm, pos0):
    j = pl.program_id(1)

    @pl.when(j == 0)
    def _():
        ext[0:1, :] = jnp.zeros((1, POOL_WIDTH), jnp.float32)
        ext[1:POOL_MAX, :] = prev_ref[0]

    @pl.when(j > 0)
    def _():
        ext[0:POOL_MAX, :] = ext[tm:tm + POOL_MAX, :]

    u = u_ref[0]
    ext[POOL_MAX:POOL_MAX + tm, :] = u
    pos = pos0 + j * tm + lax.broadcasted_iota(jnp.int32, (tm, POOL_GROUP), 0)
    outs = []
    for gi, w in enumerate(POOL_WINDOWS):
        cols = slice(gi * POOL_GROUP, (gi + 1) * POOL_GROUP)
        tot = u[:, cols]
        for d in range(1, w):
            tot = tot + ext[POOL_MAX - d:POOL_MAX - d + tm, cols]
        cnt = jnp.minimum(pos + 1, w).astype(jnp.float32)
        m = tot / cnt - u[:, cols]
        outs.append(jnp.dot(m.astype(jnp.bfloat16), w_ref[gi], preferred_element_type=jnp.float32))
    o_ref[0] = jnp.concatenate(outs, axis=-1) * scale_ref[...]
    new_ref[0] = ext[tm + 1:tm + POOL_MAX, :]


def pool_mixer(u, prev, pool_w_bf16, pool_scale, pos0):
    b, t, _ = u.shape
    tm = min(ROW_BLOCK, t)
    cur = lambda bi, j: (bi, j, 0)
    per_b = lambda bi, j: (bi, 0, 0)
    return pl.pallas_call(
        functools.partial(_pool_kernel, tm=tm, pos0=pos0),
        out_shape=(jax.ShapeDtypeStruct((b, t, POOL_WIDTH), jnp.float32),
                   jax.ShapeDtypeStruct((b, POOL_BUF, POOL_WIDTH), jnp.float32)),
        grid=(b, t // tm),
        in_specs=[pl.BlockSpec((1, tm, POOL_WIDTH), cur), pl.BlockSpec((1, POOL_BUF, POOL_WIDTH), per_b),
                  pl.BlockSpec((len(POOL_WINDOWS), POOL_GROUP, POOL_GROUP), lambda bi, j: (0, 0, 0)),
                  pl.BlockSpec((1, POOL_WIDTH), lambda bi, j: (0, 0))],
        out_specs=(pl.BlockSpec((1, tm, POOL_WIDTH), cur), pl.BlockSpec((1, POOL_BUF, POOL_WIDTH), per_b)),
        scratch_shapes=[pltpu.VMEM((tm + POOL_MAX, POOL_WIDTH), jnp.float32)],
        compiler_params=_cparams(2), name="pool_mixer",
    )(u, prev, pool_w_bf16, pool_scale.reshape(1, POOL_WIDTH))


def _top16_rows(s, n_rows):
    iota = lax.broadcasted_iota(jnp.int32, s.shape, 0).astype(jnp.float32)
    vals, idxs = [], []
    for _ in range(PEER_TOPK):
        m = jnp.max(s, axis=0, keepdims=True)
        idx = jnp.min(jnp.where(s == m, iota, float(n_rows)), axis=0, keepdims=True)
        vals.append(m)
        idxs.append(idx)
        s = jnp.where(iota == idx, -jnp.inf, s)
    return jnp.concatenate(vals, axis=0), jnp.concatenate(idxs, axis=0)


def _route_kernel(x_ref, g_ref, wq_ref, keys_ref, xn_ref, pair_ref, gate_ref, code_t, gate_t):
    xn = _rms(x_ref[...], g_ref[...])
    xn_ref[...] = xn
    xb = xn.astype(jnp.bfloat16)

    def head_step(hs, carry):
        for hh in range(ROUTE_HEADS_PER_STEP):
            one_head(hs * ROUTE_HEADS_PER_STEP + hh)
        return carry

    def one_head(h):
        qb = jnp.dot(xb, wq_ref[h], preferred_element_type=jnp.float32).astype(jnp.bfloat16)
        sv, si = [], []
        for p in range(2):
            s = lax.dot_general(keys_ref[h * 2 + p], qb[:, p * D_HALF:(p + 1) * D_HALF],
                                (((1,), (1,)), ((), ())), preferred_element_type=jnp.float32)
            v, i = _top16_rows(s, N_KEYS)
            sv.append(v)
            si.append(i)
        cand = jnp.concatenate([sv[0][a:a + 1] + sv[1][:nb] for a, nb in enumerate(PAIR_COLS)]
                               + [jnp.full((PAIR_PAD, s.shape[1]), -jnp.inf, jnp.float32)], axis=0)
        eid = jnp.concatenate([si[0][a:a + 1] * float(N_KEYS) + si[1][:nb] for a, nb in enumerate(PAIR_COLS)]
                              + [jnp.zeros((PAIR_PAD, s.shape[1]), jnp.float32)], axis=0)
        iota = lax.broadcasted_iota(jnp.int32, cand.shape, 0).astype(jnp.float32)
        cv, ce = [], []
        for _ in range(PEER_TOPK):
            m = jnp.max(cand, axis=0, keepdims=True)
            idx = jnp.min(jnp.where(cand == m, iota, float(PAIR_ROWS)), axis=0, keepdims=True)
            hit = iota == idx
            cv.append(m)
            ce.append(jnp.max(jnp.where(hit, eid, -1.0), axis=0, keepdims=True))
            cand = jnp.where(hit, -jnp.inf, cand)
        cv = jnp.concatenate(cv, axis=0)
        ce = jnp.concatenate(ce, axis=0).astype(jnp.int32)
        e = jnp.exp(cv - cv[0:1])
        rows = pl.ds(pl.multiple_of(h * PEER_TOPK, PEER_TOPK), PEER_TOPK)
        gate_t[rows, :] = e / jnp.sum(e, axis=0, keepdims=True)
        code_t[rows, :] = ((ce & (HALF_EXPERTS - 1)) << 3) | (ce >> 13)

    lax.fori_loop(0, PEER_HEADS // ROUTE_HEADS_PER_STEP, head_step, 0)
    ca, cb = code_t[:NSEL // 2, :], code_t[NSEL // 2:, :]
    pair_ref[...] = (ca | (((cb & 1) ^ 1) << 1) | ((cb & -8) << 16)).T
    gate_ref[...] = gate_t[...].T


def peer_route(x2d, g, wq_heads, keys_bf16, tb):
    n = x2d.shape[0]
    return pl.pallas_call(
        _route_kernel,
        out_shape=(jax.ShapeDtypeStruct((n, D_MODEL), jnp.float32),
                   jax.ShapeDtypeStruct((n, NSEL // 2), jnp.int32),
                   jax.ShapeDtypeStruct((n, NSEL), jnp.float32)),
        grid=(n // tb,),
        in_specs=[pl.BlockSpec((tb, D_MODEL), lambda i: (i, 0)),
                  pl.BlockSpec((1, D_MODEL), lambda i: (0, 0)),
                  pl.BlockSpec((PEER_HEADS, D_MODEL, D_KEY), lambda i: (0, 0, 0)),
                  pl.BlockSpec((PEER_HEADS * 2, N_KEYS, D_HALF), lambda i: (0, 0, 0))],
        out_specs=(pl.BlockSpec((tb, D_MODEL), lambda i: (i, 0)),
                   pl.BlockSpec((tb, NSEL // 2), lambda i: (i, 0)),
                   pl.BlockSpec((tb, NSEL), lambda i: (i, 0))),
        scratch_shapes=[pltpu.VMEM((NSEL, tb), jnp.int32), pltpu.VMEM((NSEL, tb), jnp.float32)],
        compiler_params=_cparams(), name="peer_route",
    )(x2d, g.reshape(1, D_MODEL), wq_heads, keys_bf16)


def _pack_kernel(lo_ref, hi_ref, o_ref):
    def bf16_bits(x):
        return lax.bitcast_convert_type(x.astype(jnp.bfloat16).astype(jnp.float32), jnp.int32)

    word = lax.shift_right_logical(bf16_bits(lo_ref[...]), 16) | (bf16_bits(hi_ref[...]) & HI16)
    for k in range(ROW_TILE):
        o_ref[:, k, :] = word[:, k * LANES:(k + 1) * LANES]


def pack_table(tab):
    nb = HALF_EXPERTS // PACK_BLOCK
    out = pl.pallas_call(
        _pack_kernel,
        out_shape=jax.ShapeDtypeStruct((HALF_EXPERTS, ROW_TILE, LANES), jnp.int32),
        grid=(nb,),
        in_specs=[pl.BlockSpec((PACK_BLOCK, D_MODEL), lambda i: (i, 0)),
                  pl.BlockSpec((PACK_BLOCK, D_MODEL), lambda i: (i + nb, 0))],
        out_specs=pl.BlockSpec((PACK_BLOCK, ROW_TILE, LANES), lambda i: (i, 0, 0)),
        compiler_params=_cparams(), name="pack_table",
    )(tab, tab)
    return out.reshape(HALF_EXPERTS * ROW_TILE, LANES)


def _gather_pair(tab_ref, cab):
    wa = tab_ref[pl.ds(pl.multiple_of(cab & 0xFFF8, SUBLANES), SUBLANES), :]
    wb = tab_ref[pl.ds(pl.multiple_of(lax.shift_right_logical(cab, 16), SUBLANES), SUBLANES), :]
    cv = jnp.full((SUBLANES, LANES), cab, jnp.int32)
    sha = (cv & 1) << 4
    shb = (cv & 2) << 3
    return (lax.shift_right_logical(wa, sha) & 0xFFFF) | (lax.shift_left(wb, shb) & HI16)


def _gather_token(tab_ref, pair_ref, t, buf):
    for i in range(NSEL // 2):
        buf[i * ROW_TILE:(i + 1) * ROW_TILE, :] = _gather_pair(tab_ref, pair_ref[t, i])


def _grouped_token_loop(tb, gather, finish, group_a, group_b):
    @pl.when(pl.program_id(0) == 0)
    def _():
        for buf in group_b:
            buf[...] = jnp.zeros(buf.shape, jnp.int32)

    def body(i, carry):
        t0 = 2 * UPASS_GROUP * i
        for q in range(UPASS_GROUP):
            gather(t0 + q, group_a[q])
        finish(jnp.maximum(t0 - UPASS_GROUP, 0), group_b)
        for q in range(UPASS_GROUP):
            gather(t0 + UPASS_GROUP + q, group_b[q])
        finish(t0, group_a)
        return carry

    lax.fori_loop(0, tb // (2 * UPASS_GROUP), body, 0)
    finish(tb - UPASS_GROUP, group_b)


def _upass_kernel(pair_ref, gate_ref, xn_ref, tab_ref, w_ref, *bufs, tb):
    col = lax.broadcasted_iota(jnp.int32, (2 * ROW_TILE, SLOT_ROWS), 1)
    row = lax.broadcasted_iota(jnp.int32, (2 * ROW_TILE, SLOT_ROWS), 0)
    chunk_mask = (((col & 15) >> 1) == (row & 7)).astype(jnp.float32)
    c2 = lax.broadcasted_iota(jnp.int32, (SLOT_ROWS, NSEL), 0)
    j2 = lax.broadcasted_iota(jnp.int32, (SLOT_ROWS, NSEL), 1)
    fold = (j2 == (c2 >> 4) + ((c2 & 1) << 6)).astype(jnp.bfloat16)

    def gather(t, buf):
        _gather_token(tab_ref, pair_ref, t, buf)

    def finish(t0, group):
        zs = []
        for q, buf in enumerate(group):
            xt = xn_ref[pl.ds(t0 + q, 1), :].reshape(ROW_TILE, LANES)
            xhi = xt.astype(jnp.bfloat16)
            xlo = (xt - xhi.astype(jnp.float32)).astype(jnp.bfloat16)
            x16 = jnp.concatenate([xhi, xlo], axis=0)
            us = pltpu.bitcast(buf[...], jnp.bfloat16)
            r = lax.dot_general(x16, us, (((1,), (1,)), ((), ())), preferred_element_type=jnp.float32)
            zs.append(jnp.sum(r * chunk_mask, axis=0, keepdims=True))
        z = jnp.concatenate(zs, axis=0)
        zhi = z.astype(jnp.bfloat16)
        zlo = (z - zhi.astype(jnp.float32)).astype(jnp.bfloat16)
        act = (jnp.dot(zhi, fold, preferred_element_type=jnp.float32)
               + jnp.dot(zlo, fold, preferred_element_type=jnp.float32))
        gelu = 0.5 * act * (1.0 + lax.erf(act * (1.0 / math.sqrt(2.0))))
        rows = pl.ds(pl.multiple_of(t0, UPASS_GROUP), UPASS_GROUP)
        w_ref[rows, :] = (gate_ref[rows, :] * gelu).astype(jnp.bfloat16)

    _grouped_token_loop(tb, gather, finish, bufs[:UPASS_GROUP], bufs[UPASS_GROUP:])


def _peer_pass_call(kernel_fn, out_shape, out_block, pairs, per_token, rows, tab_packed, tb, name):
    n = pairs.shape[0]
    return pl.pallas_call(
        functools.partial(kernel_fn, tb=tb),
        out_shape=out_shape,
        grid=(n // tb,),
        in_specs=[pl.BlockSpec((tb, NSEL // 2), lambda i: (i, 0), memory_space=pltpu.SMEM),
                  pl.BlockSpec((tb, NSEL), lambda i: (i, 0)),
                  pl.BlockSpec((tb, D_MODEL), lambda i: (i, 0)),
                  pl.BlockSpec((HALF_EXPERTS * ROW_TILE, LANES), lambda i: (0, 0), pipeline_mode=pl.Buffered(1))],
        out_specs=pl.BlockSpec(out_block, lambda i: (i, 0)),
        scratch_shapes=[pltpu.VMEM((SLOT_ROWS // 2, LANES), jnp.int32)] * (2 * UPASS_GROUP),
        compiler_params=_cparams(), name=name,
    )(pairs, per_token, rows, tab_packed)


def peer_upass(pairs, gate, xn_rows, tab_packed, tb):
    n = pairs.shape[0]
    return _peer_pass_call(_upass_kernel, jax.ShapeDtypeStruct((n, NSEL), jnp.bfloat16), (tb, NSEL),
                           pairs, gate, xn_rows, tab_packed, tb, "peer_upass")


def _vpass_kernel(pair_ref, w_ref, x_ref, tab_ref, o_ref, *bufs, tb):
    col = lax.broadcasted_iota(jnp.int32, (ROW_TILE, SLOT_ROWS), 1)
    row = lax.broadcasted_iota(jnp.int32, (ROW_TILE, SLOT_ROWS), 0)
    chunk_mask = (((col & 15) >> 1) == row).astype(jnp.float32)
    j2 = lax.broadcasted_iota(jnp.int32, (NSEL, SLOT_ROWS), 0)
    c2 = lax.broadcasted_iota(jnp.int32, (NSEL, SLOT_ROWS), 1)
    spread = (j2 == (c2 >> 4) + ((c2 & 1) << 6)).astype(jnp.bfloat16)

    def gather(t, buf):
        _gather_token(tab_ref, pair_ref, t, buf)

    def finish(t0, group):
        rows = pl.ds(pl.multiple_of(t0, UPASS_GROUP), UPASS_GROUP)
        wexp = jnp.dot(w_ref[rows, :], spread, preferred_element_type=jnp.float32)
        for q, buf in enumerate(group):
            wsel = (jnp.broadcast_to(wexp[q:q + 1], (ROW_TILE, SLOT_ROWS)) * chunk_mask).astype(jnp.bfloat16)
            vs = pltpu.bitcast(buf[...], jnp.bfloat16)
            tile = jnp.dot(wsel, vs, preferred_element_type=jnp.float32)
            o_ref[pl.ds(t0 + q, 1), :] = x_ref[pl.ds(t0 + q, 1), :] + tile.reshape(1, D_MODEL)

    _grouped_token_loop(tb, gather, finish, bufs[:UPASS_GROUP], bufs[UPASS_GROUP:])


def peer_vpass(pairs, w_bf16, x_rows, tab_packed, tb):
    n = pairs.shape[0]
    return _peer_pass_call(_vpass_kernel, jax.ShapeDtypeStruct((n, D_MODEL), jnp.float32),
                           (tb, D_MODEL), pairs, w_bf16, x_rows, tab_packed, tb, "peer_vpass")


def peer_block(x, g, wq_heads, keys_bf16, u_packed, v_packed):
    shp = x.shape
    x2d = x.reshape(-1, D_MODEL)
    n = x2d.shape[0]
    xn, pairs, gate = peer_route(x2d, g, wq_heads, keys_bf16, min(PEER_ROUTE_BLOCK, n))
    w = peer_upass(pairs, gate, xn, u_packed, min(PEER_PASS_BLOCK, n))
    out = peer_vpass(pairs, w, x2d, v_packed, min(PEER_PASS_BLOCK, n))
    return out.reshape(shp)


def _rms_kernel(x_ref, g_ref, o_ref):
    o_ref[...] = _rms(x_ref[...], g_ref[...])


def rmsnorm_pallas(x, g):
    shp = x.shape
    xt = x.reshape(-1, shp[-1])
    n = xt.shape[0]
    tm = min(ROW_BLOCK, n)
    out = pl.pallas_call(
        _rms_kernel,
        out_shape=jax.ShapeDtypeStruct(xt.shape, xt.dtype),
        grid=(n // tm,),
        in_specs=[pl.BlockSpec((tm, shp[-1]), lambda i: (i, 0)),
                  pl.BlockSpec((1, shp[-1]), lambda i: (0, 0))],
        out_specs=pl.BlockSpec((tm, shp[-1]), lambda i: (i, 0)),
        compiler_params=_cparams(), name="final_rmsnorm",
    )(xt, g.reshape(1, -1))
    return out.reshape(shp)


def even_layer(x, pos0, k_prev, v_prev, h0r, h0i, norm_g, w_in, w_out, sink, s5_params, d_skip, w_glu, b_glu):
    b, t, _ = x.shape
    n = b * t
    x2d = x.reshape(n, D_MODEL)
    tabs = rope_tables(pos0 + jnp.arange(t), ROT_DIM, HEAD_DIM, SWA_Q, max(1, min(ROW_BLOCK, n) // t))
    q, k, v, u = in_even(x2d, norm_g, w_in.astype(jnp.bfloat16), tabs)
    q3, k3, v3 = q.reshape(b, t, SWA_Q), k.reshape(b, t, SWA_KV), v.reshape(b, t, SWA_KV)
    if k_prev is None:
        att = swa_attention(q3, k3, k3, v3, v3, sink, True)
        k_all, v_all = k3, v3
        h0r = jnp.zeros((b, S5_FLAT), jnp.float32)
        h0i = jnp.zeros((b, S5_FLAT), jnp.float32)
    else:
        kp, vp = k_prev.reshape(b, WINDOW, SWA_KV), v_prev.reshape(b, WINDOW, SWA_KV)
        att = swa_attention(q3, kp, k3, vp, v3, sink, False)
        k_all, v_all = jnp.concatenate([kp, k3], axis=1), jnp.concatenate([vp, v3], axis=1)
        h0r, h0i = h0r.reshape(b, S5_FLAT), h0i.reshape(b, S5_FLAT)
    lam, wb, wc = s5_discretize(*s5_params)
    u_tm = u.reshape(b, t, S5_WIDTH).transpose(1, 0, 2).reshape(n, S5_WIDTH)
    s5o_tm, hre, him = s5_mixer(u_tm, b, lam, wb, wc, d_skip, w_glu.astype(jnp.bfloat16), b_glu, h0r, h0i,
                                min(t, S5_STEPS))
    s5o = s5o_tm.reshape(t, b, S5_WIDTH).transpose(1, 0, 2).reshape(n, S5_WIDTH)
    out = out_proj(x2d, att.reshape(n, SWA_Q), s5o, w_out.astype(jnp.bfloat16))
    return (out.reshape(b, t, D_MODEL),
            k_all[:, -WINDOW:].reshape(b, WINDOW, SWA_KV_HEADS, HEAD_DIM),
            v_all[:, -WINDOW:].reshape(b, WINDOW, SWA_KV_HEADS, HEAD_DIM),
            hre.reshape(b, S5_GROUPS, S5_STATE), him.reshape(b, S5_GROUPS, S5_STATE))


def odd_layer(x, pos0, pool_prev, ckv_prev, kpe_prev, norm_g, w_in, w_out, pool_w, pool_scale,
              q_norm, kv_norm, w_uq, w_uk, w_uv):
    b, t, _ = x.shape
    n = b * t
    x2d = x.reshape(n, D_MODEL)
    reps = max(1, min(ROW_BLOCK, n) // t)
    pos = pos0 + jnp.arange(t)
    ktabs = rope_tables(pos, ROPE_DIM, KPE_PAD, KPE_PAD, reps)
    qtabs = rope_tables(pos, ROPE_DIM, ROPE_DIM, MLA_QPE, reps)
    w_perm, wuq_perm = permute_odd_weights(w_in, w_uq)
    qnope, qpe, c, kp, u = in_odd(x2d, norm_g, w_perm, q_norm, kv_norm, wuq_perm, ktabs, qtabs)
    c3, kp3 = c.reshape(b, t, KV_LORA), kp.reshape(b, t, ROPE_DIM)
    wuk_h = w_uk.transpose(1, 2, 0).astype(jnp.bfloat16)
    wuv_h = w_uv.transpose(1, 0, 2).astype(jnp.bfloat16)
    if ckv_prev is None:
        ck, kk, causal, n_keys = c3, kp3, True, t
        pool_prev = jnp.zeros((b, POOL_BUF, POOL_WIDTH), jnp.float32)
    else:
        ck, kk, causal = jnp.concatenate([ckv_prev, c3], axis=1), jnp.concatenate([kpe_prev, kp3], axis=1), False
        n_keys = ck.shape[1]
        pad = -n_keys % min(MLA_KBLOCK, n_keys)
        ck, kk = jnp.pad(ck, ((0, 0), (0, pad), (0, 0))), jnp.pad(kk, ((0, 0), (0, pad), (0, 0)))
    mla = mla_attention(qnope.reshape(b, t, MLA_QNOPE), qpe.reshape(b, t, MLA_QPE), ck.astype(jnp.bfloat16),
                        kk.astype(jnp.bfloat16), wuk_h, wuv_h, causal, n_keys)
    pool_out, pool_new = pool_mixer(u.reshape(b, t, POOL_WIDTH), pool_prev, pool_w.astype(jnp.bfloat16), pool_scale, pos0)
    out = out_proj(x2d, pool_out.reshape(n, POOL_WIDTH), mla.reshape(n, MLA_HEADS * V_DIM), w_out.astype(jnp.bfloat16))
    return out.reshape(b, t, D_MODEL), pool_new, c3, kp3


def kernel(x_prompt, x_sample, cache_swa_k, cache_swa_v, state_ssm_re, state_ssm_im, state_pool,
           cache_mla_ckv, cache_mla_kpe, norm_mix, norm_ffn, norm_final, w_in_even, w_out_even,
           swa_sink, s5_lam_re, s5_lam_im, s5_log_dt, s5_b_re, s5_b_im, s5_c_re, s5_c_im, s5_d,
           s5_w_glu, s5_b_glu, w_in_odd, w_out_odd, pool_w, pool_scale, mla_q_norm, mla_kv_norm,
           mla_w_uq, mla_w_uk, mla_w_uv, peer_w_q, peer_keys, peer_u, peer_v):
    xp, xs = x_prompt, x_sample
    kp_l, vp_l, rp_l, ip_l, poolp_l, cp_l, ep_l = [], [], [], [], [], [], []
    ks_l, vs_l, rs_l, is_l, pools_l, cs_l, es_l = [], [], [], [], [], [], []
    for layer in range(DEPTH):
        i = layer // 2
        if layer % 2 == 0:
            s5_params = (s5_lam_re[i], s5_lam_im[i], s5_log_dt[i], s5_b_re[i], s5_b_im[i], s5_c_re[i], s5_c_im[i])
            ew = (norm_mix[layer], w_in_even[i], w_out_even[i], swa_sink[i], s5_params, s5_d[i], s5_w_glu[i], s5_b_glu[i])
            xp, k1, v1, r1, i1 = even_layer(xp, 0, None, None, None, None, *ew)
            xs, k2, v2, r2, i2 = even_layer(xs, PAST_LEN, cache_swa_k[i], cache_swa_v[i],
                                            state_ssm_re[i], state_ssm_im[i], *ew)
            kp_l.append(k1); vp_l.append(v1); rp_l.append(r1); ip_l.append(i1)
            ks_l.append(k2); vs_l.append(v2); rs_l.append(r2); is_l.append(i2)
        else:
            ow = (norm_mix[layer], w_in_odd[i], w_out_odd[i], pool_w[i], pool_scale[i], mla_q_norm[i], mla_kv_norm[i],
                  mla_w_uq[i], mla_w_uk[i], mla_w_uv[i])
            xp, p1, c1, e1 = odd_layer(xp, 0, None, None, None, *ow)
            xs, p2, c2, e2 = odd_layer(xs, PAST_LEN, state_pool[i], cache_mla_ckv[i], cache_mla_kpe[i], *ow)
            poolp_l.append(p1); cp_l.append(c1); ep_l.append(e1)
            pools_l.append(p2); cs_l.append(c2); es_l.append(e2)
        wq_heads = peer_w_q[layer].reshape(D_MODEL, PEER_HEADS, D_KEY).transpose(1, 0, 2).astype(jnp.bfloat16)
        keys_bf16 = peer_keys[layer].reshape(PEER_HEADS * 2, N_KEYS, D_HALF).astype(jnp.bfloat16)
        u_packed, v_packed = pack_table(peer_u[layer]), pack_table(peer_v[layer])
        xp = peer_block(xp, norm_ffn[layer], wq_heads, keys_bf16, u_packed, v_packed)
        xs = peer_block(xs, norm_ffn[layer], wq_heads, keys_bf16, u_packed, v_packed)
    y_prompt = rmsnorm_pallas(xp, norm_final)
    y_sample = rmsnorm_pallas(xs, norm_final)
    return (y_prompt, y_sample,
            jnp.stack(kp_l), jnp.stack(vp_l), jnp.stack(rp_l), jnp.stack(ip_l),
            jnp.stack(poolp_l), jnp.stack(cp_l), jnp.stack(ep_l),
            jnp.stack(ks_l), jnp.stack(vs_l), jnp.stack(rs_l), jnp.stack(is_l),
            jnp.stack(pools_l), jnp.stack(cs_l), jnp.stack(es_l))
```

```python
import functools
import math
import jax
import jax.numpy as jnp
from jax import lax
from jax.experimental import pallas as pl
from jax.experimental.pallas import tpu as pltpu

D_MODEL = 1024
DEPTH = 2
PAST_LEN = 2048

CHUNK = 64
RMS_EPS = 1e-6
ROPE_THETA = 500000.0
NEG_INF = -1e30

SWA_HEADS = 8
SWA_KV_HEADS = 2
SWA_GROUP = SWA_HEADS // SWA_KV_HEADS
HEAD_DIM = 64
ROT_DIM = HEAD_DIM // 4
WINDOW = 128
SWA_Q = SWA_HEADS * HEAD_DIM
SWA_KV = SWA_KV_HEADS * HEAD_DIM
SWA_SCALE = HEAD_DIM ** -0.5

S5_WIDTH = 512
S5_GROUP = 16
S5_GROUPS = S5_WIDTH // S5_GROUP
S5_STATE = 64
S5_FLAT = S5_GROUPS * S5_STATE

POOL_WIDTH = 512
POOL_WINDOWS = (2, 4, 8, 16)
POOL_GROUP = POOL_WIDTH // len(POOL_WINDOWS)
POOL_MAX = 16
POOL_BUF = POOL_MAX - 1

MLA_HEADS = 8
Q_LORA = 512
KV_LORA = 256
NOPE_DIM = 64
ROPE_DIM = 32
V_DIM = 64
MLA_SCALE = (NOPE_DIM + ROPE_DIM) ** -0.5
MLA_QNOPE = MLA_HEADS * NOPE_DIM
MLA_QPE = MLA_HEADS * ROPE_DIM

EVEN_IN = SWA_Q + 2 * SWA_KV + S5_WIDTH

PEER_HEADS = 8
N_KEYS = 128
N_EXPERTS = N_KEYS * N_KEYS
D_KEY = 128
D_HALF = D_KEY // 2
PEER_TOPK = 16
NSEL = PEER_HEADS * PEER_TOPK
HALF_EXPERTS = N_EXPERTS // 2

SUBLANES = 8
LANES = 128
VMEM_LIMIT = 56 * 1024 * 1024
ROW_TILE = D_MODEL // LANES

ROW_BLOCK = 512
SWA_QBLOCK = 256
MLA_QBLOCK = 256
MLA_KBLOCK = 512
MLA_CHAIN_ROWS = 256
KPE_PAD = LANES
S5_STEPS = 128
S5_CARRY_VREGS = 16
PEER_ROUTE_BLOCK = 256
ROUTE_HEADS_PER_STEP = 4
PEER_PASS_BLOCK = 512
PACK_BLOCK = 256
UPASS_GROUP = 16
SLOT_ROWS = NSEL * ROW_TILE
HI16 = -65536
PAIR_COLS = tuple(PEER_TOPK // (a + 1) for a in range(PEER_TOPK))
PAIR_ROWS = -(-sum(PAIR_COLS) // SUBLANES) * SUBLANES
PAIR_PAD = PAIR_ROWS - sum(PAIR_COLS)

ODD_U0 = Q_LORA + KV_LORA
ODD_KPE0 = ODD_U0 + POOL_WIDTH
ODD_IN_PAD = ODD_KPE0 + KPE_PAD


def _cparams(n_axes=1):
    return pltpu.CompilerParams(dimension_semantics=("arbitrary",) * n_axes, vmem_limit_bytes=VMEM_LIMIT)


def _rms(x, g):
    return x * lax.rsqrt(jnp.mean(x * x, axis=-1, keepdims=True) + RMS_EPS) * g


def _rope_lanes(x, cos, sin_lo, sin_hi, half):
    n = x.shape[-1]
    return x * cos + pltpu.roll(x, n - half, 1) * sin_lo + pltpu.roll(x, half, 1) * sin_hi


def rope_tables(pos, rot, period, width, reps):
    inv = ROPE_THETA ** (-jnp.arange(0, rot, 2, dtype=jnp.float32) / rot)
    ang = pos.astype(jnp.float32)[:, None] * inv[None, :]
    cos, sin = jnp.cos(ang), jnp.sin(ang)
    lane = jnp.arange(width) % period
    idx = lane % (rot // 2)
    in_lo = lane < rot // 2
    in_hi = (lane >= rot // 2) & (lane < rot)
    c = jnp.where((in_lo | in_hi)[None, :], cos[:, idx], 1.0)
    s_lo = jnp.where(in_lo[None, :], -sin[:, idx], 0.0)
    s_hi = jnp.where(in_hi[None, :], sin[:, idx], 0.0)
    return tuple(jnp.tile(t, (reps, 1)) for t in (c, s_lo, s_hi))


def _in_even_kernel(x_ref, g_ref, w_ref, cos_ref, slo_ref, shi_ref, q_ref, k_ref, v_ref, u_ref):
    xn = _rms(x_ref[...], g_ref[...])
    proj = jnp.dot(xn.astype(jnp.bfloat16), w_ref[...], preferred_element_type=jnp.float32)
    cos, slo, shi = cos_ref[...], slo_ref[...], shi_ref[...]
    q_ref[...] = _rope_lanes(proj[:, :SWA_Q], cos, slo, shi, ROT_DIM // 2)
    k_ref[...] = _rope_lanes(proj[:, SWA_Q:SWA_Q + SWA_KV], cos[:, :SWA_KV], slo[:, :SWA_KV], shi[:, :SWA_KV],
                             ROT_DIM // 2)
    v_ref[...] = proj[:, SWA_Q + SWA_KV:SWA_Q + 2 * SWA_KV]
    u_ref[...] = proj[:, SWA_Q + 2 * SWA_KV:]


def in_even(x2d, g, w_bf16, tabs):
    n = x2d.shape[0]
    tm = min(ROW_BLOCK, n)
    nt = tabs[0].shape[0] // tm
    row = lambda i: (i, 0)
    const = lambda i: (0, 0)
    tab = lambda i: (i % nt, 0)
    return pl.pallas_call(
        _in_even_kernel,
        out_shape=(jax.ShapeDtypeStruct((n, SWA_Q), jnp.float32), jax.ShapeDtypeStruct((n, SWA_KV), jnp.float32),
                   jax.ShapeDtypeStruct((n, SWA_KV), jnp.float32), jax.ShapeDtypeStruct((n, S5_WIDTH), jnp.float32)),
        grid=(n // tm,),
        in_specs=[pl.BlockSpec((tm, D_MODEL), row), pl.BlockSpec((1, D_MODEL), const),
                  pl.BlockSpec((D_MODEL, EVEN_IN), const),
                  pl.BlockSpec((tm, SWA_Q), tab), pl.BlockSpec((tm, SWA_Q), tab), pl.BlockSpec((tm, SWA_Q), tab)],
        out_specs=(pl.BlockSpec((tm, SWA_Q), row), pl.BlockSpec((tm, SWA_KV), row),
                   pl.BlockSpec((tm, SWA_KV), row), pl.BlockSpec((tm, S5_WIDTH), row)),
        compiler_params=_cparams(), name="in_even",
    )(x2d, g.reshape(1, D_MODEL), w_bf16, *tabs)


def _swa_kernel(sink_ref, q_ref, kp_ref, kc_ref, vp_ref, vc_ref, o_ref, *, banded, qb):
    i = pl.program_id(1)
    q = q_ref[0]
    k = jnp.concatenate([kp_ref[0], kc_ref[0]], axis=0).astype(jnp.bfloat16)
    v = jnp.concatenate([vp_ref[0], vc_ref[0]], axis=0).astype(jnp.bfloat16)
    nk = WINDOW + qb
    wc = WINDOW // CHUNK
    if banded:
        qc = lax.broadcasted_iota(jnp.int32, (qb, nk), 0) // CHUNK + wc
        kc = lax.broadcasted_iota(jnp.int32, (qb, nk), 1) // CHUNK
        visible = (kc <= qc) & (kc >= qc - wc) & ((kc >= wc) | (i > 0))
    ss = []
    for h in range(SWA_HEADS):
        hk = h // SWA_GROUP
        qh = q[:, h * HEAD_DIM:(h + 1) * HEAD_DIM].astype(jnp.bfloat16)
        ss.append(lax.dot_general(qh, k[:, hk * HEAD_DIM:(hk + 1) * HEAD_DIM], (((1,), (1,)), ((), ())),
                                  preferred_element_type=jnp.float32) * SWA_SCALE)
    ps, dens = [], []
    for h in range(SWA_HEADS):
        s = jnp.where(visible, ss[h], NEG_INF) if banded else ss[h]
        sk = sink_ref[h]
        m = jnp.maximum(jnp.max(s, axis=-1, keepdims=True), sk)
        p = jnp.exp(s - m)
        dens.append(jnp.sum(p, axis=-1, keepdims=True) + jnp.exp(sk - m))
        ps.append(p.astype(jnp.bfloat16))
    outs = [jnp.dot(ps[h], v[:, (h // SWA_GROUP) * HEAD_DIM:(h // SWA_GROUP + 1) * HEAD_DIM],
                    preferred_element_type=jnp.float32) / dens[h] for h in range(SWA_HEADS)]
    o_ref[0] = jnp.concatenate(outs, axis=-1)


def swa_attention(q, k_prev, k_cur, v_prev, v_cur, sink, banded):
    b, t, _ = q.shape
    qb = min(SWA_QBLOCK, t)
    per = qb // WINDOW
    prev_map = (lambda bi, i: (bi, jnp.maximum(i * per - 1, 0), 0)) if banded else (lambda bi, i: (bi, 0, 0))
    cur = lambda bi, i: (bi, i, 0)
    return pl.pallas_call(
        functools.partial(_swa_kernel, banded=banded, qb=qb),
        out_shape=jax.ShapeDtypeStruct((b, t, SWA_Q), jnp.float32),
        grid=(b, t // qb),
        in_specs=[pl.BlockSpec(memory_space=pltpu.SMEM),
                  pl.BlockSpec((1, qb, SWA_Q), cur),
                  pl.BlockSpec((1, WINDOW, SWA_KV), prev_map), pl.BlockSpec((1, qb, SWA_KV), cur),
                  pl.BlockSpec((1, WINDOW, SWA_KV), prev_map), pl.BlockSpec((1, qb, SWA_KV), cur)],
        out_specs=pl.BlockSpec((1, qb, SWA_Q), cur),
        compiler_params=_cparams(2), name="swa_attention",
    )(sink, q, k_prev, k_cur, v_prev, v_cur)


def _out_kernel(x_ref, a_ref, b_ref, w_ref, o_ref):
    ka = a_ref.shape[-1]
    o_ref[...] = (x_ref[...]
                  + jnp.dot(a_ref[...].astype(jnp.bfloat16), w_ref[:ka, :], preferred_element_type=jnp.float32)
                  + jnp.dot(b_ref[...].astype(jnp.bfloat16), w_ref[ka:, :], preferred_element_type=jnp.float32))


def out_proj(x2d, a, b, w_bf16):
    n = x2d.shape[0]
    tm = min(ROW_BLOCK, n)
    row = lambda i: (i, 0)
    return pl.pallas_call(
        _out_kernel,
        out_shape=jax.ShapeDtypeStruct((n, D_MODEL), jnp.float32),
        grid=(n // tm,),
        in_specs=[pl.BlockSpec((tm, D_MODEL), row), pl.BlockSpec((tm, a.shape[1]), row),
                  pl.BlockSpec((tm, b.shape[1]), row), pl.BlockSpec(w_bf16.shape, lambda i: (0, 0))],
        out_specs=pl.BlockSpec((tm, D_MODEL), row),
        compiler_params=_cparams(), name="out_proj",
    )(x2d, a, b, w_bf16)


def s5_discretize(lam_re, lam_im, log_dt, b_re, b_im, c_re, c_im):
    lr = jnp.minimum(lam_re, -1e-4)
    li = lam_im
    dt = jnp.exp(log_dt)[:, None]
    mag = jnp.exp(lr * dt)
    ang = li * dt
    ab_re, ab_im = mag * jnp.cos(ang), mag * jnp.sin(ang)
    den = lr * lr + li * li
    nr, ni = ab_re - 1.0, ab_im
    f_re = (nr * lr + ni * li) / den
    f_im = (ni * lr - nr * li) / den
    bb_re = f_re[..., None] * b_re - f_im[..., None] * b_im
    bb_im = f_re[..., None] * b_im + f_im[..., None] * b_re
    eye = jnp.eye(S5_GROUPS, dtype=jnp.float32)

    def embed_b(bb):
        return jnp.einsum('gnc,gh->gchn', bb, eye).reshape(S5_WIDTH, S5_FLAT)

    def embed_c(c):
        return jnp.einsum('gcn,gh->gnhc', c, eye).reshape(S5_FLAT, S5_WIDTH)

    wb = jnp.concatenate([embed_b(bb_re), embed_b(bb_im)], axis=1)
    wc = jnp.concatenate([embed_c(c_re), -embed_c(c_im)], axis=0)
    lam = jnp.stack([ab_re.reshape(S5_FLAT), ab_im.reshape(S5_FLAT)])
    return lam, wb.astype(jnp.bfloat16), wc.astype(jnp.bfloat16)


def _s5_kernel(u_ref, lam_ref, wb_ref, wc_ref, d_ref, wglu_ref, bglu_ref, h0r_ref, h0i_ref,
               o_ref, hr_ref, hi_ref, hbuf, *, nb, steps, width):
    @pl.when(pl.program_id(0) == 0)
    def _():
        hr_ref[...] = h0r_ref[...]
        hi_ref[...] = h0i_ref[...]

    u = u_ref[...]
    hbuf[...] = jnp.dot(u.astype(jnp.bfloat16), wb_ref[...], preferred_element_type=jnp.float32)
    for c0 in range(0, S5_FLAT, width):
        cre = slice(c0, c0 + width)
        cim = slice(S5_FLAT + c0, S5_FLAT + c0 + width)
        lr = jnp.broadcast_to(lam_ref[0:1, cre], (nb, width))
        li = jnp.broadcast_to(lam_ref[1:2, cre], (nb, width))

        def step(t, carry):
            hr, hi = carry
            rows = pl.ds(pl.multiple_of(t * nb, nb), nb)
            nhr = lr * hr - li * hi + hbuf[rows, cre]
            nhi = lr * hi + li * hr + hbuf[rows, cim]
            hbuf[rows, cre] = nhr
            hbuf[rows, cim] = nhi
            return nhr, nhi

        hr, hi = lax.fori_loop(0, steps, step, (hr_ref[:, cre], hi_ref[:, cre]))
        hr_ref[:, cre] = hr
        hi_ref[:, cre] = hi
    y = jnp.dot(hbuf[...].astype(jnp.bfloat16), wc_ref[...], preferred_element_type=jnp.float32) + d_ref[...] * u
    z = 0.5 * y * (1.0 + jnp.tanh(math.sqrt(2.0 / math.pi) * (y + 0.044715 * (y * y * y))))
    gate = jnp.dot(z.astype(jnp.bfloat16), wglu_ref[...], preferred_element_type=jnp.float32) + bglu_ref[...]
    o_ref[...] = z * (1.0 / (1.0 + jnp.exp(-gate)))


def s5_mixer(u_tm, nb, lam, wb, wc, d_skip, w_glu_bf16, b_glu, h0r, h0i, steps):
    rows = u_tm.shape[0]
    t_total = rows // nb
    width = min(S5_FLAT, max(LANES, (S5_CARRY_VREGS * SUBLANES * LANES) // (2 * nb)))
    blk = steps * nb
    const = lambda i: (0, 0)
    return pl.pallas_call(
        functools.partial(_s5_kernel, nb=nb, steps=steps, width=width),
        out_shape=(jax.ShapeDtypeStruct((rows, S5_WIDTH), jnp.float32),
                   jax.ShapeDtypeStruct((nb, S5_FLAT), jnp.float32),
                   jax.ShapeDtypeStruct((nb, S5_FLAT), jnp.float32)),
        grid=(t_total // steps,),
        in_specs=[pl.BlockSpec((blk, S5_WIDTH), lambda i: (i, 0)),
                  pl.BlockSpec((2, S5_FLAT), const),
                  pl.BlockSpec((S5_WIDTH, 2 * S5_FLAT), const),
                  pl.BlockSpec((2 * S5_FLAT, S5_WIDTH), const),
                  pl.BlockSpec((1, S5_WIDTH), const),
                  pl.BlockSpec((S5_WIDTH, S5_WIDTH), const),
                  pl.BlockSpec((1, S5_WIDTH), const),
                  pl.BlockSpec((nb, S5_FLAT), const),
                  pl.BlockSpec((nb, S5_FLAT), const)],
        out_specs=(pl.BlockSpec((blk, S5_WIDTH), lambda i: (i, 0)),
                   pl.BlockSpec((nb, S5_FLAT), const),
                   pl.BlockSpec((nb, S5_FLAT), const)),
        scratch_shapes=[pltpu.VMEM((blk, 2 * S5_FLAT), jnp.float32)],
        compiler_params=_cparams(), name="s5_mixer",
    )(u_tm, lam, wb, wc, d_skip.reshape(1, S5_WIDTH), w_glu_bf16, b_glu.reshape(1, S5_WIDTH), h0r, h0i)


def _in_odd_kernel(x_ref, g_ref, w_ref, qn_ref, kvn_ref, wuq_ref, kcos_ref, kslo_ref, kshi_ref,
                   qcos_ref, qslo_ref, qshi_ref, qnope_ref, qpe_ref, c_ref, kp_ref, u_ref):
    xn = _rms(x_ref[...], g_ref[...])
    proj = jnp.dot(xn.astype(jnp.bfloat16), w_ref[...], preferred_element_type=jnp.float32)
    cqn = _rms(proj[:, :Q_LORA], qn_ref[...])
    q = jnp.dot(cqn.astype(jnp.bfloat16), wuq_ref[...], preferred_element_type=jnp.float32)
    qnope_ref[...] = q[:, :MLA_QNOPE]
    qpe_ref[...] = _rope_lanes(q[:, MLA_QNOPE:], qcos_ref[...], qslo_ref[...], qshi_ref[...], ROPE_DIM // 2)
    c_ref[...] = _rms(proj[:, Q_LORA:ODD_U0], kvn_ref[...])
    kp = _rope_lanes(proj[:, ODD_KPE0:], kcos_ref[...], kslo_ref[...], kshi_ref[...], ROPE_DIM // 2)
    kp_ref[...] = kp[:, :ROPE_DIM]
    u_ref[...] = proj[:, ODD_U0:ODD_KPE0]


def in_odd(x2d, g, w_perm_bf16, q_norm, kv_norm, wuq_perm_bf16, ktabs, qtabs):
    n = x2d.shape[0]
    tm = min(ROW_BLOCK, n)
    nt = ktabs[0].shape[0] // tm
    row = lambda i: (i, 0)
    const = lambda i: (0, 0)
    tab = lambda i: (i % nt, 0)
    return pl.pallas_call(
        _in_odd_kernel,
        out_shape=(jax.ShapeDtypeStruct((n, MLA_QNOPE), jnp.float32), jax.ShapeDtypeStruct((n, MLA_QPE), jnp.float32),
                   jax.ShapeDtypeStruct((n, KV_LORA), jnp.float32), jax.ShapeDtypeStruct((n, ROPE_DIM), jnp.float32),
                   jax.ShapeDtypeStruct((n, POOL_WIDTH), jnp.float32)),
        grid=(n // tm,),
        in_specs=[pl.BlockSpec((tm, D_MODEL), row), pl.BlockSpec((1, D_MODEL), const),
                  pl.BlockSpec((D_MODEL, ODD_IN_PAD), const),
                  pl.BlockSpec((1, Q_LORA), const), pl.BlockSpec((1, KV_LORA), const),
                  pl.BlockSpec((Q_LORA, MLA_QNOPE + MLA_QPE), const),
                  pl.BlockSpec((tm, KPE_PAD), tab), pl.BlockSpec((tm, KPE_PAD), tab), pl.BlockSpec((tm, KPE_PAD), tab),
                  pl.BlockSpec((tm, MLA_QPE), tab), pl.BlockSpec((tm, MLA_QPE), tab), pl.BlockSpec((tm, MLA_QPE), tab)],
        out_specs=(pl.BlockSpec((tm, MLA_QNOPE), row), pl.BlockSpec((tm, MLA_QPE), row), pl.BlockSpec((tm, KV_LORA), row),
                   pl.BlockSpec((tm, ROPE_DIM), row), pl.BlockSpec((tm, POOL_WIDTH), row)),
        compiler_params=_cparams(), name="in_odd",
    )(x2d, g.reshape(1, D_MODEL), w_perm_bf16, q_norm.reshape(1, Q_LORA), kv_norm.reshape(1, KV_LORA), wuq_perm_bf16,
      *ktabs, *qtabs)


def permute_odd_weights(w_in, w_uq):
    o1, o2 = Q_LORA + KV_LORA, Q_LORA + KV_LORA + ROPE_DIM
    kpe = jnp.pad(w_in[:, o1:o2], ((0, 0), (0, KPE_PAD - ROPE_DIM)))
    w_perm = jnp.concatenate([w_in[:, :o1], w_in[:, o2:], kpe], axis=1)
    wq = w_uq.reshape(Q_LORA, MLA_HEADS, NOPE_DIM + ROPE_DIM)
    wq_perm = jnp.concatenate([wq[:, :, :NOPE_DIM].reshape(Q_LORA, MLA_QNOPE),
                               wq[:, :, NOPE_DIM:].reshape(Q_LORA, MLA_QPE)], axis=1)
    return w_perm.astype(jnp.bfloat16), wq_perm.astype(jnp.bfloat16)


def _mla_kernel(qn_ref, qp_ref, c_ref, kp_ref, wuk_ref, wuv_ref, o_ref, *scratch, causal, qb, kb, n_keys, hpc):
    i = pl.program_id(1)
    n_chains = MLA_HEADS // hpc
    qa_s, qp_s, m_s, l_s, acc_s = (scratch[k * n_chains:(k + 1) * n_chains] for k in range(5))
    qn = qn_ref[0].astype(jnp.bfloat16)
    qp = qp_ref[0].astype(jnp.bfloat16)
    for g in range(n_chains):
        for hh in range(hpc):
            h = g * hpc + hh
            rows = slice(hh * qb, (hh + 1) * qb)
            qa_s[g][rows, :] = jnp.dot(qn[:, h * NOPE_DIM:(h + 1) * NOPE_DIM], wuk_ref[h],
                                       preferred_element_type=jnp.float32).astype(jnp.bfloat16)
            qp_s[g][rows, :] = qp[:, h * ROPE_DIM:(h + 1) * ROPE_DIM]
        m_s[g][...] = jnp.full(m_s[g].shape, NEG_INF, jnp.float32)
        l_s[g][...] = jnp.zeros(l_s[g].shape, jnp.float32)
        acc_s[g][...] = jnp.zeros(acc_s[g].shape, jnp.float32)
    if causal:
        qpos = i * qb + lax.broadcasted_iota(jnp.int32, (hpc * qb, kb), 0) % qb
        limit = (qpos // CHUNK + 1) * CHUNK
        nblk = ((i + 1) * qb + kb - 1) // kb
    else:
        limit = n_keys
        nblk = (n_keys + kb - 1) // kb
    kidx0 = lax.broadcasted_iota(jnp.int32, (hpc * qb, kb), 1)

    def body(j, carry):
        rows = pl.ds(pl.multiple_of(j * kb, kb), kb)
        cb = c_ref[0, rows, :]
        kpb = kp_ref[0, rows, :]
        visible = kidx0 + j * kb < limit
        ss = [(lax.dot_general(qa_s[g][...], cb, (((1,), (1,)), ((), ())), preferred_element_type=jnp.float32)
               + lax.dot_general(qp_s[g][...], kpb, (((1,), (1,)), ((), ())),
                                 preferred_element_type=jnp.float32)) * MLA_SCALE for g in range(n_chains)]
        ps, alphas = [], []
        for g in range(n_chains):
            s = jnp.where(visible, ss[g], NEG_INF)
            m_old = m_s[g][...]
            m_new = jnp.maximum(m_old, jnp.max(s, axis=-1, keepdims=True))
            alpha = jnp.exp(m_old - m_new)
            p = jnp.exp(s - m_new)
            m_s[g][...] = m_new
            l_s[g][...] = alpha * l_s[g][...] + jnp.sum(p, axis=-1, keepdims=True)
            ps.append(p.astype(jnp.bfloat16))
            alphas.append(alpha)
        for g in range(n_chains):
            acc_s[g][...] = alphas[g] * acc_s[g][...] + jnp.dot(ps[g], cb, preferred_element_type=jnp.float32)
        return carry

    lax.fori_loop(0, nblk, body, 0)
    outs = []
    for g in range(n_chains):
        o_lat = (acc_s[g][...] / l_s[g][...]).astype(jnp.bfloat16)
        for hh in range(hpc):
            outs.append(jnp.dot(o_lat[hh * qb:(hh + 1) * qb, :], wuv_ref[g * hpc + hh],
                                preferred_element_type=jnp.float32))
    o_ref[0] = jnp.concatenate(outs, axis=-1)


def mla_attention(q_nope, q_pe, c_keys_bf16, kp_keys_bf16, wuk_h, wuv_h, causal, n_keys):
    b, t, _ = q_nope.shape
    tk = c_keys_bf16.shape[1]
    qb = min(MLA_QBLOCK, t)
    kb = min(MLA_KBLOCK, tk)
    hpc = max(1, min(MLA_HEADS, MLA_CHAIN_ROWS // qb))
    n_chains, rows = MLA_HEADS // hpc, hpc * qb
    cur = lambda bi, i: (bi, i, 0)
    whole = lambda bi, i: (bi, 0, 0)
    const3 = lambda bi, i: (0, 0, 0)
    return pl.pallas_call(
        functools.partial(_mla_kernel, causal=causal, qb=qb, kb=kb, n_keys=n_keys, hpc=hpc),
        out_shape=jax.ShapeDtypeStruct((b, t, MLA_HEADS * V_DIM), jnp.float32),
        grid=(b, t // qb),
        in_specs=[pl.BlockSpec((1, qb, MLA_QNOPE), cur), pl.BlockSpec((1, qb, MLA_QPE), cur),
                  pl.BlockSpec((1, tk, KV_LORA), whole), pl.BlockSpec((1, tk, ROPE_DIM), whole),
                  pl.BlockSpec((MLA_HEADS, NOPE_DIM, KV_LORA), const3), pl.BlockSpec((MLA_HEADS, KV_LORA, V_DIM), const3)],
        out_specs=pl.BlockSpec((1, qb, MLA_HEADS * V_DIM), cur),
        scratch_shapes=([pltpu.VMEM((rows, KV_LORA), jnp.bfloat16)] * n_chains
                        + [pltpu.VMEM((rows, ROPE_DIM), jnp.bfloat16)] * n_chains
                        + [pltpu.VMEM((rows, 1), jnp.float32)] * (2 * n_chains)
                        + [pltpu.VMEM((rows, KV_LORA), jnp.float32)] * n_chains),
        compiler_params=_cparams(2), name="mla_attention",
    )(q_nope, q_pe, c_keys_bf16, kp_keys_bf16, wuk_h, wuv_h)


def _pool_kernel(u_ref, prev_ref, w_ref, scale_ref, o_ref, new_ref, ext, *, tm, pos0):
    j = pl.program_id(1)

    @pl.when(j == 0)
    def _():
        ext[0:1, :] = jnp.zeros((1, POOL_WIDTH), jnp.float32)
        ext[1:POOL_MAX, :] = prev_ref[0]

    @pl.when(j > 0)
    def _():
        ext[0:POOL_MAX, :] = ext[tm:tm + POOL_MAX, :]

    u = u_ref[0]
    ext[POOL_MAX:POOL_MAX + tm, :] = u
    pos = pos0 + j * tm + lax.broadcasted_iota(jnp.int32, (tm, POOL_GROUP), 0)
    outs = []
    for gi, w in enumerate(POOL_WINDOWS):
        cols = slice(gi * POOL_GROUP, (gi + 1) * POOL_GROUP)
        tot = u[:, cols]
        for d in range(1, w):
            tot = tot + ext[POOL_MAX - d:POOL_MAX - d + tm, cols]
        cnt = jnp.minimum(pos + 1, w).astype(jnp.float32)
        m = tot / cnt - u[:, cols]
        outs.append(jnp.dot(m.astype(jnp.bfloat16), w_ref[gi], preferred_element_type=jnp.float32))
    o_ref[0] = jnp.concatenate(outs, axis=-1) * scale_ref[...]
    new_ref[0] = ext[tm + 1:tm + POOL_MAX, :]


def pool_mixer(u, prev, pool_w_bf16, pool_scale, pos0):
    b, t, _ = u.shape
    tm = min(ROW_BLOCK, t)
    cur = lambda bi, j: (bi, j, 0)
    per_b = lambda bi, j: (bi, 0, 0)
    return pl.pallas_call(
        functools.partial(_pool_kernel, tm=tm, pos0=pos0),
        out_shape=(jax.ShapeDtypeStruct((b, t, POOL_WIDTH), jnp.float32),
                   jax.ShapeDtypeStruct((b, POOL_BUF, POOL_WIDTH), jnp.float32)),
        grid=(b, t // tm),
        in_specs=[pl.BlockSpec((1, tm, POOL_WIDTH), cur), pl.BlockSpec((1, POOL_BUF, POOL_WIDTH), per_b),
                  pl.BlockSpec((len(POOL_WINDOWS), POOL_GROUP, POOL_GROUP), lambda bi, j: (0, 0, 0)),
                  pl.BlockSpec((1, POOL_WIDTH), lambda bi, j: (0, 0))],
        out_specs=(pl.BlockSpec((1, tm, POOL_WIDTH), cur), pl.BlockSpec((1, POOL_BUF, POOL_WIDTH), per_b)),
        scratch_shapes=[pltpu.VMEM((tm + POOL_MAX, POOL_WIDTH), jnp.float32)],
        compiler_params=_cparams(2), name="pool_mixer",
    )(u, prev, pool_w_bf16, pool_scale.reshape(1, POOL_WIDTH))


def _top16_rows(s, n_rows):
    iota = lax.broadcasted_iota(jnp.int32, s.shape, 0).astype(jnp.float32)
    vals, idxs = [], []
    for _ in range(PEER_TOPK):
        m = jnp.max(s, axis=0, keepdims=True)
        idx = jnp.min(jnp.where(s == m, iota, float(n_rows)), axis=0, keepdims=True)
        vals.append(m)
        idxs.append(idx)
        s = jnp.where(iota == idx, -jnp.inf, s)
    return jnp.concatenate(vals, axis=0), jnp.concatenate(idxs, axis=0)


def _route_kernel(x_ref, g_ref, wq_ref, keys_ref, xn_ref, pair_ref, gate_ref, code_t, gate_t):
    xn = _rms(x_ref[...], g_ref[...])
    xn_ref[...] = xn
    xb = xn.astype(jnp.bfloat16)

    def head_step(hs, carry):
        for hh in range(ROUTE_HEADS_PER_STEP):
            one_head(hs * ROUTE_HEADS_PER_STEP + hh)
        return carry

    def one_head(h):
        qb = jnp.dot(xb, wq_ref[h], preferred_element_type=jnp.float32).astype(jnp.bfloat16)
        sv, si = [], []
        for p in range(2):
            s = lax.dot_general(keys_ref[h * 2 + p], qb[:, p * D_HALF:(p + 1) * D_HALF],
                                (((1,), (1,)), ((), ())), preferred_element_type=jnp.float32)
            v, i = _top16_rows(s, N_KEYS)
            sv.append(v)
            si.append(i)
        cand = jnp.concatenate([sv[0][a:a + 1] + sv[1][:nb] for a, nb in enumerate(PAIR_COLS)]
                               + [jnp.full((PAIR_PAD, s.shape[1]), -jnp.inf, jnp.float32)], axis=0)
        eid = jnp.concatenate([si[0][a:a + 1] * float(N_KEYS) + si[1][:nb] for a, nb in enumerate(PAIR_COLS)]
                              + [jnp.zeros((PAIR_PAD, s.shape[1]), jnp.float32)], axis=0)
        iota = lax.broadcasted_iota(jnp.int32, cand.shape, 0).astype(jnp.float32)
        cv, ce = [], []
        for _ in range(PEER_TOPK):
            m = jnp.max(cand, axis=0, keepdims=True)
            idx = jnp.min(jnp.where(cand == m, iota, float(PAIR_ROWS)), axis=0, keepdims=True)
            hit = iota == idx
            cv.append(m)
            ce.append(jnp.max(jnp.where(hit, eid, -1.0), axis=0, keepdims=True))
            cand = jnp.where(hit, -jnp.inf, cand)
        cv = jnp.concatenate(cv, axis=0)
        ce = jnp.concatenate(ce, axis=0).astype(jnp.int32)
        e = jnp.exp(cv - cv[0:1])
        rows = pl.ds(pl.multiple_of(h * PEER_TOPK, PEER_TOPK), PEER_TOPK)
        gate_t[rows, :] = e / jnp.sum(e, axis=0, keepdims=True)
        code_t[rows, :] = ((ce & (HALF_EXPERTS - 1)) << 3) | (ce >> 13)

    lax.fori_loop(0, PEER_HEADS // ROUTE_HEADS_PER_STEP, head_step, 0)
    ca, cb = code_t[:NSEL // 2, :], code_t[NSEL // 2:, :]
    pair_ref[...] = (ca | (((cb & 1) ^ 1) << 1) | ((cb & -8) << 16)).T
    gate_ref[...] = gate_t[...].T


def peer_route(x2d, g, wq_heads, keys_bf16, tb):
    n = x2d.shape[0]
    return pl.pallas_call(
        _route_kernel,
        out_shape=(jax.ShapeDtypeStruct((n, D_MODEL), jnp.float32),
                   jax.ShapeDtypeStruct((n, NSEL // 2), jnp.int32),
                   jax.ShapeDtypeStruct((n, NSEL), jnp.float32)),
        grid=(n // tb,),
        in_specs=[pl.BlockSpec((tb, D_MODEL), lambda i: (i, 0)),
                  pl.BlockSpec((1, D_MODEL), lambda i: (0, 0)),
                  pl.BlockSpec((PEER_HEADS, D_MODEL, D_KEY), lambda i: (0, 0, 0)),
                  pl.BlockSpec((PEER_HEADS * 2, N_KEYS, D_HALF), lambda i: (0, 0, 0))],
        out_specs=(pl.BlockSpec((tb, D_MODEL), lambda i: (i, 0)),
                   pl.BlockSpec((tb, NSEL // 2), lambda i: (i, 0)),
                   pl.BlockSpec((tb, NSEL), lambda i: (i, 0))),
        scratch_shapes=[pltpu.VMEM((NSEL, tb), jnp.int32), pltpu.VMEM((NSEL, tb), jnp.float32)],
        compiler_params=_cparams(), name="peer_route",
    )(x2d, g.reshape(1, D_MODEL), wq_heads, keys_bf16)


def _pack_kernel(lo_ref, hi_ref, o_ref):
    def bf16_bits(x):
        return lax.bitcast_convert_type(x.astype(jnp.bfloat16).astype(jnp.float32), jnp.int32)

    word = lax.shift_right_logical(bf16_bits(lo_ref[...]), 16) | (bf16_bits(hi_ref[...]) & HI16)
    for k in range(ROW_TILE):
        o_ref[:, k, :] = word[:, k * LANES:(k + 1) * LANES]


def pack_table(tab):
    nb = HALF_EXPERTS // PACK_BLOCK
    out = pl.pallas_call(
        _pack_kernel,
        out_shape=jax.ShapeDtypeStruct((HALF_EXPERTS, ROW_TILE, LANES), jnp.int32),
        grid=(nb,),
        in_specs=[pl.BlockSpec((PACK_BLOCK, D_MODEL), lambda i: (i, 0)),
                  pl.BlockSpec((PACK_BLOCK, D_MODEL), lambda i: (i + nb, 0))],
        out_specs=pl.BlockSpec((PACK_BLOCK, ROW_TILE, LANES), lambda i: (i, 0, 0)),
        compiler_params=_cparams(), name="pack_table",
    )(tab, tab)
    return out.reshape(HALF_EXPERTS * ROW_TILE, LANES)


def _gather_pair(tab_ref, cab):
    wa = tab_ref[pl.ds(pl.multiple_of(cab & 0xFFF8, SUBLANES), SUBLANES), :]
    wb = tab_ref[pl.ds(pl.multiple_of(lax.shift_right_logical(cab, 16), SUBLANES), SUBLANES), :]
    cv = jnp.full((SUBLANES, LANES), cab, jnp.int32)
    sha = (cv & 1) << 4
    shb = (cv & 2) << 3
    return (lax.shift_right_logical(wa, sha) & 0xFFFF) | (lax.shift_left(wb, shb) & HI16)


def _gather_token(tab_ref, pair_ref, t, buf):
    for i in range(NSEL // 2):
        buf[i * ROW_TILE:(i + 1) * ROW_TILE, :] = _gather_pair(tab_ref, pair_ref[t, i])


def _grouped_token_loop(tb, gather, finish, group_a, group_b):
    @pl.when(pl.program_id(0) == 0)
    def _():
        for buf in group_b:
            buf[...] = jnp.zeros(buf.shape, jnp.int32)

    def body(i, carry):
        t0 = 2 * UPASS_GROUP * i
        for q in range(UPASS_GROUP):
            gather(t0 + q, group_a[q])
        finish(jnp.maximum(t0 - UPASS_GROUP, 0), group_b)
        for q in range(UPASS_GROUP):
            gather(t0 + UPASS_GROUP + q, group_b[q])
        finish(t0, group_a)
        return carry

    lax.fori_loop(0, tb // (2 * UPASS_GROUP), body, 0)
    finish(tb - UPASS_GROUP, group_b)


def _upass_kernel(pair_ref, gate_ref, xn_ref, tab_ref, w_ref, *bufs, tb):
    col = lax.broadcasted_iota(jnp.int32, (2 * ROW_TILE, SLOT_ROWS), 1)
    row = lax.broadcasted_iota(jnp.int32, (2 * ROW_TILE, SLOT_ROWS), 0)
    chunk_mask = (((col & 15) >> 1) == (row & 7)).astype(jnp.float32)
    c2 = lax.broadcasted_iota(jnp.int32, (SLOT_ROWS, NSEL), 0)
    j2 = lax.broadcasted_iota(jnp.int32, (SLOT_ROWS, NSEL), 1)
    fold = (j2 == (c2 >> 4) + ((c2 & 1) << 6)).astype(jnp.bfloat16)

    def gather(t, buf):
        _gather_token(tab_ref, pair_ref, t, buf)

    def finish(t0, group):
        zs = []
        for q, buf in enumerate(group):
            xt = xn_ref[pl.ds(t0 + q, 1), :].reshape(ROW_TILE, LANES)
            xhi = xt.astype(jnp.bfloat16)
            xlo = (xt - xhi.astype(jnp.float32)).astype(jnp.bfloat16)
            x16 = jnp.concatenate([xhi, xlo], axis=0)
            us = pltpu.bitcast(buf[...], jnp.bfloat16)
            r = lax.dot_general(x16, us, (((1,), (1,)), ((), ())), preferred_element_type=jnp.float32)
            zs.append(jnp.sum(r * chunk_mask, axis=0, keepdims=True))
        z = jnp.concatenate(zs, axis=0)
        zhi = z.astype(jnp.bfloat16)
        zlo = (z - zhi.astype(jnp.float32)).astype(jnp.bfloat16)
        act = (jnp.dot(zhi, fold, preferred_element_type=jnp.float32)
               + jnp.dot(zlo, fold, preferred_element_type=jnp.float32))
        gelu = 0.5 * act * (1.0 + lax.erf(act * (1.0 / math.sqrt(2.0))))
        rows = pl.ds(pl.multiple_of(t0, UPASS_GROUP), UPASS_GROUP)
        w_ref[rows, :] = (gate_ref[rows, :] * gelu).astype(jnp.bfloat16)

    _grouped_token_loop(tb, gather, finish, bufs[:UPASS_GROUP], bufs[UPASS_GROUP:])


def _peer_pass_call(kernel_fn, out_shape, out_block, pairs, per_token, rows, tab_packed, tb, name):
    n = pairs.shape[0]
    return pl.pallas_call(
        functools.partial(kernel_fn, tb=tb),
        out_shape=out_shape,
        grid=(n // tb,),
        in_specs=[pl.BlockSpec((tb, NSEL // 2), lambda i: (i, 0), memory_space=pltpu.SMEM),
                  pl.BlockSpec((tb, NSEL), lambda i: (i, 0)),
                  pl.BlockSpec((tb, D_MODEL), lambda i: (i, 0)),
                  pl.BlockSpec((HALF_EXPERTS * ROW_TILE, LANES), lambda i: (0, 0), pipeline_mode=pl.Buffered(1))],
        out_specs=pl.BlockSpec(out_block, lambda i: (i, 0)),
        scratch_shapes=[pltpu.VMEM((SLOT_ROWS // 2, LANES), jnp.int32)] * (2 * UPASS_GROUP),
        compiler_params=_cparams(), name=name,
    )(pairs, per_token, rows, tab_packed)


def peer_upass(pairs, gate, xn_rows, tab_packed, tb):
    n = pairs.shape[0]
    return _peer_pass_call(_upass_kernel, jax.ShapeDtypeStruct((n, NSEL), jnp.bfloat16), (tb, NSEL),
                           pairs, gate, xn_rows, tab_packed, tb, "peer_upass")


def _vpass_kernel(pair_ref, w_ref, x_ref, tab_ref, o_ref, *bufs, tb):
    col = lax.broadcasted_iota(jnp.int32, (ROW_TILE, SLOT_ROWS), 1)
    row = lax.broadcasted_iota(jnp.int32, (ROW_TILE, SLOT_ROWS), 0)
    chunk_mask = (((col & 15) >> 1) == row).astype(jnp.float32)
    j2 = lax.broadcasted_iota(jnp.int32, (NSEL, SLOT_ROWS), 0)
    c2 = lax.broadcasted_iota(jnp.int32, (NSEL, SLOT_ROWS), 1)
    spread = (j2 == (c2 >> 4) + ((c2 & 1) << 6)).astype(jnp.bfloat16)

    def gather(t, buf):
        _gather_token(tab_ref, pair_ref, t, buf)

    def finish(t0, group):
        rows = pl.ds(pl.multiple_of(t0, UPASS_GROUP), UPASS_GROUP)
        wexp = jnp.dot(w_ref[rows, :], spread, preferred_element_type=jnp.float32)
        for q, buf in enumerate(group):
            wsel = (jnp.broadcast_to(wexp[q:q + 1], (ROW_TILE, SLOT_ROWS)) * chunk_mask).astype(jnp.bfloat16)
            vs = pltpu.bitcast(buf[...], jnp.bfloat16)
            tile = jnp.dot(wsel, vs, preferred_element_type=jnp.float32)
            o_ref[pl.ds(t0 + q, 1), :] = x_ref[pl.ds(t0 + q, 1), :] + tile.reshape(1, D_MODEL)

    _grouped_token_loop(tb, gather, finish, bufs[:UPASS_GROUP], bufs[UPASS_GROUP:])


def peer_vpass(pairs, w_bf16, x_rows, tab_packed, tb):
    n = pairs.shape[0]
    return _peer_pass_call(_vpass_kernel, jax.ShapeDtypeStruct((n, D_MODEL), jnp.float32),
                           (tb, D_MODEL), pairs, w_bf16, x_rows, tab_packed, tb, "peer_vpass")


def peer_block(x, g, wq_heads, keys_bf16, u_packed, v_packed):
    shp = x.shape
    x2d = x.reshape(-1, D_MODEL)
    n = x2d.shape[0]
    xn, pairs, gate = peer_route(x2d, g, wq_heads, keys_bf16, min(PEER_ROUTE_BLOCK, n))
    w = peer_upass(pairs, gate, xn, u_packed, min(PEER_PASS_BLOCK, n))
    out = peer_vpass(pairs, w, x2d, v_packed, min(PEER_PASS_BLOCK, n))
    return out.reshape(shp)


def _rms_kernel(x_ref, g_ref, o_ref):
    o_ref[...] = _rms(x_ref[...], g_ref[...])


def rmsnorm_pallas(x, g):
    shp = x.shape
    xt = x.reshape(-1, shp[-1])
    n = xt.shape[0]
    tm = min(ROW_BLOCK, n)
    out = pl.pallas_call(
        _rms_kernel,
        out_shape=jax.ShapeDtypeStruct(xt.shape, xt.dtype),
        grid=(n // tm,),
        in_specs=[pl.BlockSpec((tm, shp[-1]), lambda i: (i, 0)),
                  pl.BlockSpec((1, shp[-1]), lambda i: (0, 0))],
        out_specs=pl.BlockSpec((tm, shp[-1]), lambda i: (i, 0)),
        compiler_params=_cparams(), name="final_rmsnorm",
    )(xt, g.reshape(1, -1))
    return out.reshape(shp)


def even_layer(x, pos0, k_prev, v_prev, h0r, h0i, norm_g, w_in, w_out, sink, s5_params, d_skip, w_glu, b_glu):
    b, t, _ = x.shape
    n = b * t
    x2d = x.reshape(n, D_MODEL)
    tabs = rope_tables(pos0 + jnp.arange(t), ROT_DIM, HEAD_DIM, SWA_Q, max(1, min(ROW_BLOCK, n) // t))
    q, k, v, u = in_even(x2d, norm_g, w_in.astype(jnp.bfloat16), tabs)
    q3, k3, v3 = q.reshape(b, t, SWA_Q), k.reshape(b, t, SWA_KV), v.reshape(b, t, SWA_KV)
    if k_prev is None:
        att = swa_attention(q3, k3, k3, v3, v3, sink, True)
        k_all, v_all = k3, v3
        h0r = jnp.zeros((b, S5_FLAT), jnp.float32)
        h0i = jnp.zeros((b, S5_FLAT), jnp.float32)
    else:
        kp, vp = k_prev.reshape(b, WINDOW, SWA_KV), v_prev.reshape(b, WINDOW, SWA_KV)
        att = swa_attention(q3, kp, k3, vp, v3, sink, False)
        k_all, v_all = jnp.concatenate([kp, k3], axis=1), jnp.concatenate([vp, v3], axis=1)
        h0r, h0i = h0r.reshape(b, S5_FLAT), h0i.reshape(b, S5_FLAT)
    lam, wb, wc = s5_discretize(*s5_params)
    u_tm = u.reshape(b, t, S5_WIDTH).transpose(1, 0, 2).reshape(n, S5_WIDTH)
    s5o_tm, hre, him = s5_mixer(u_tm, b, lam, wb, wc, d_skip, w_glu.astype(jnp.bfloat16), b_glu, h0r, h0i,
                                min(t, S5_STEPS))
    s5o = s5o_tm.reshape(t, b, S5_WIDTH).transpose(1, 0, 2).reshape(n, S5_WIDTH)
    out = out_proj(x2d, att.reshape(n, SWA_Q), s5o, w_out.astype(jnp.bfloat16))
    return (out.reshape(b, t, D_MODEL),
            k_all[:, -WINDOW:].reshape(b, WINDOW, SWA_KV_HEADS, HEAD_DIM),
            v_all[:, -WINDOW:].reshape(b, WINDOW, SWA_KV_HEADS, HEAD_DIM),
            hre.reshape(b, S5_GROUPS, S5_STATE), him.reshape(b, S5_GROUPS, S5_STATE))


def odd_layer(x, pos0, pool_prev, ckv_prev, kpe_prev, norm_g, w_in, w_out, pool_w, pool_scale,
              q_norm, kv_norm, w_uq, w_uk, w_uv):
    b, t, _ = x.shape
    n = b * t
    x2d = x.reshape(n, D_MODEL)
    reps = max(1, min(ROW_BLOCK, n) // t)
    pos = pos0 + jnp.arange(t)
    ktabs = rope_tables(pos, ROPE_DIM, KPE_PAD, KPE_PAD, reps)
    qtabs = rope_tables(pos, ROPE_DIM, ROPE_DIM, MLA_QPE, reps)
    w_perm, wuq_perm = permute_odd_weights(w_in, w_uq)
    qnope, qpe, c, kp, u = in_odd(x2d, norm_g, w_perm, q_norm, kv_norm, wuq_perm, ktabs, qtabs)
    c3, kp3 = c.reshape(b, t, KV_LORA), kp.reshape(b, t, ROPE_DIM)
    wuk_h = w_uk.transpose(1, 2, 0).astype(jnp.bfloat16)
    wuv_h = w_uv.transpose(1, 0, 2).astype(jnp.bfloat16)
    if ckv_prev is None:
        ck, kk, causal, n_keys = c3, kp3, True, t
        pool_prev = jnp.zeros((b, POOL_BUF, POOL_WIDTH), jnp.float32)
    else:
        ck, kk, causal = jnp.concatenate([ckv_prev, c3], axis=1), jnp.concatenate([kpe_prev, kp3], axis=1), False
        n_keys = ck.shape[1]
        pad = -n_keys % min(MLA_KBLOCK, n_keys)
        ck, kk = jnp.pad(ck, ((0, 0), (0, pad), (0, 0))), jnp.pad(kk, ((0, 0), (0, pad), (0, 0)))
    mla = mla_attention(qnope.reshape(b, t, MLA_QNOPE), qpe.reshape(b, t, MLA_QPE), ck.astype(jnp.bfloat16),
                        kk.astype(jnp.bfloat16), wuk_h, wuv_h, causal, n_keys)
    pool_out, pool_new = pool_mixer(u.reshape(b, t, POOL_WIDTH), pool_prev, pool_w.astype(jnp.bfloat16), pool_scale, pos0)
    out = out_proj(x2d, pool_out.reshape(n, POOL_WIDTH), mla.reshape(n, MLA_HEADS * V_DIM), w_out.astype(jnp.bfloat16))
    return out.reshape(b, t, D_MODEL), pool_new, c3, kp3


def kernel(x_prompt, x_sample, cache_swa_k, cache_swa_v, state_ssm_re, state_ssm_im, state_pool,
           cache_mla_ckv, cache_mla_kpe, norm_mix, norm_ffn, norm_final, w_in_even, w_out_even,
           swa_sink, s5_lam_re, s5_lam_im, s5_log_dt, s5_b_re, s5_b_im, s5_c_re, s5_c_im, s5_d,
           s5_w_glu, s5_b_glu, w_in_odd, w_out_odd, pool_w, pool_scale, mla_q_norm, mla_kv_norm,
           mla_w_uq, mla_w_uk, mla_w_uv, peer_w_q, peer_keys, peer_u, peer_v):
    xp, xs = x_prompt, x_sample
    kp_l, vp_l, rp_l, ip_l, poolp_l, cp_l, ep_l = [], [], [], [], [], [], []
    ks_l, vs_l, rs_l, is_l, pools_l, cs_l, es_l = [], [], [], [], [], [], []
    for layer in range(DEPTH):
        i = layer // 2
        if layer % 2 == 0:
            s5_params = (s5_lam_re[i], s5_lam_im[i], s5_log_dt[i], s5_b_re[i], s5_b_im[i], s5_c_re[i], s5_c_im[i])
            ew = (norm_mix[layer], w_in_even[i], w_out_even[i], swa_sink[i], s5_params, s5_d[i], s5_w_glu[i], s5_b_glu[i])
            xp, k1, v1, r1, i1 = even_layer(xp, 0, None, None, None, None, *ew)
            xs, k2, v2, r2, i2 = even_layer(xs, PAST_LEN, cache_swa_k[i], cache_swa_v[i],
                                            state_ssm_re[i], state_ssm_im[i], *ew)
            kp_l.append(k1); vp_l.append(v1); rp_l.append(r1); ip_l.append(i1)
            ks_l.append(k2); vs_l.append(v2); rs_l.append(r2); is_l.append(i2)
        else:
            ow = (norm_mix[layer], w_in_odd[i], w_out_odd[i], pool_w[i], pool_scale[i], mla_q_norm[i], mla_kv_norm[i],
                  mla_w_uq[i], mla_w_uk[i], mla_w_uv[i])
            xp, p1, c1, e1 = odd_layer(xp, 0, None, None, None, *ow)
            xs, p2, c2, e2 = odd_layer(xs, PAST_LEN, state_pool[i], cache_mla_ckv[i], cache_mla_kpe[i], *ow)
            poolp_l.append(p1); cp_l.append(c1); ep_l.append(e1)
            pools_l.append(p2); cs_l.append(c2); es_l.append(e2)
        wq_heads = peer_w_q[layer].reshape(D_MODEL, PEER_HEADS, D_KEY).transpose(1, 0, 2).astype(jnp.bfloat16)
        keys_bf16 = peer_keys[layer].reshape(PEER_HEADS * 2, N_KEYS, D_HALF).astype(jnp.bfloat16)
        u_packed, v_packed = pack_table(peer_u[layer]), pack_table(peer_v[layer])
        xp = peer_block(xp, norm_ffn[layer], wq_heads, keys_bf16, u_packed, v_packed)
        xs = peer_block(xs, norm_ffn[layer], wq_heads, keys_bf16, u_packed, v_packed)
    y_prompt = rmsnorm_pallas(xp, norm_final)
    y_sample = rmsnorm_pallas(xs, norm_final)
    return (y_prompt, y_sample,
            jnp.stack(kp_l), jnp.stack(vp_l), jnp.stack(rp_l), jnp.stack(ip_l),
            jnp.stack(poolp_l), jnp.stack(cp_l), jnp.stack(ep_l),
            jnp.stack(ks_l), jnp.stack(vs_l), jnp.stack(rs_l), jnp.stack(is_l),
            jnp.stack(pools_l), jnp.stack(cs_l), jnp.stack(es_l))
```

```python
import functools
import math
import jax
import jax.numpy as jnp
from jax import lax
from jax.experimental import pallas as pl
from jax.experimental.pallas import tpu as pltpu

D_MODEL = 1024
DEPTH = 2
PAST_LEN = 2048

CHUNK = 64
RMS_EPS = 1e-6
ROPE_THETA = 500000.0
NEG_INF = -1e30

SWA_HEADS = 8
SWA_KV_HEADS = 2
SWA_GROUP = SWA_HEADS // SWA_KV_HEADS
HEAD_DIM = 64
ROT_DIM = HEAD_DIM // 4
WINDOW = 128
SWA_Q = SWA_HEADS * HEAD_DIM
SWA_KV = SWA_KV_HEADS * HEAD_DIM
SWA_SCALE = HEAD_DIM ** -0.5

S5_WIDTH = 512
S5_GROUP = 16
S5_GROUPS = S5_WIDTH // S5_GROUP
S5_STATE = 64
S5_FLAT = S5_GROUPS * S5_STATE

POOL_WIDTH = 512
POOL_WINDOWS = (2, 4, 8, 16)
POOL_GROUP = POOL_WIDTH // len(POOL_WINDOWS)
POOL_MAX = 16
POOL_BUF = POOL_MAX - 1

MLA_HEADS = 8
Q_LORA = 512
KV_LORA = 256
NOPE_DIM = 64
ROPE_DIM = 32
V_DIM = 64
MLA_SCALE = (NOPE_DIM + ROPE_DIM) ** -0.5
MLA_QNOPE = MLA_HEADS * NOPE_DIM
MLA_QPE = MLA_HEADS * ROPE_DIM

EVEN_IN = SWA_Q + 2 * SWA_KV + S5_WIDTH

PEER_HEADS = 8
N_KEYS = 128
N_EXPERTS = N_KEYS * N_KEYS
D_KEY = 128
D_HALF = D_KEY // 2
PEER_TOPK = 16
NSEL = PEER_HEADS * PEER_TOPK
HALF_EXPERTS = N_EXPERTS // 2

SUBLANES = 8
LANES = 128
VMEM_LIMIT = 56 * 1024 * 1024
ROW_TILE = D_MODEL // LANES

ROW_BLOCK = 512
SWA_QBLOCK = 256
MLA_QBLOCK = 256
MLA_KBLOCK = 512
MLA_CHAIN_ROWS = 256
KPE_PAD = LANES
S5_STEPS = 128
S5_BLOCK_GROUPS = 8
S5_CARRY_VREGS = 16
PEER_ROUTE_BLOCK = 256
ROUTE_HEADS_PER_STEP = 4
PEER_PASS_BLOCK = 512
PACK_BLOCK = 256
UPASS_GROUP = 16
SLOT_ROWS = NSEL * ROW_TILE
HI16 = -65536
PAIR_COLS = tuple(PEER_TOPK // (a + 1) for a in range(PEER_TOPK))
PAIR_ROWS = -(-sum(PAIR_COLS) // SUBLANES) * SUBLANES
PAIR_PAD = PAIR_ROWS - sum(PAIR_COLS)

ODD_U0 = Q_LORA + KV_LORA
ODD_KPE0 = ODD_U0 + POOL_WIDTH
ODD_IN_PAD = ODD_KPE0 + KPE_PAD


def _cparams(n_axes=1):
    return pltpu.CompilerParams(dimension_semantics=("arbitrary",) * n_axes, vmem_limit_bytes=VMEM_LIMIT)


def _rms(x, g):
    return x * lax.rsqrt(jnp.mean(x * x, axis=-1, keepdims=True) + RMS_EPS) * g


def _rope_lanes(x, cos, sin_lo, sin_hi, half):
    n = x.shape[-1]
    return x * cos + pltpu.roll(x, n - half, 1) * sin_lo + pltpu.roll(x, half, 1) * sin_hi


def rope_tables(pos, rot, period, width, reps):
    inv = ROPE_THETA ** (-jnp.arange(0, rot, 2, dtype=jnp.float32) / rot)
    ang = pos.astype(jnp.float32)[:, None] * inv[None, :]
    cos, sin = jnp.cos(ang), jnp.sin(ang)
    lane = jnp.arange(width) % period
    idx = lane % (rot // 2)
    in_lo = lane < rot // 2
    in_hi = (lane >= rot // 2) & (lane < rot)
    c = jnp.where((in_lo | in_hi)[None, :], cos[:, idx], 1.0)
    s_lo = jnp.where(in_lo[None, :], -sin[:, idx], 0.0)
    s_hi = jnp.where(in_hi[None, :], sin[:, idx], 0.0)
    return tuple(jnp.tile(t, (reps, 1)) for t in (c, s_lo, s_hi))


def _in_even_kernel(x_ref, g_ref, w_ref, cos_ref, slo_ref, shi_ref, q_ref, k_ref, v_ref, u_ref):
    xn = _rms(x_ref[...], g_ref[...])
    proj = jnp.dot(xn.astype(jnp.bfloat16), w_ref[...], preferred_element_type=jnp.float32)
    cos, slo, shi = cos_ref[...], slo_ref[...], shi_ref[...]
    q_ref[...] = _rope_lanes(proj[:, :SWA_Q], cos, slo, shi, ROT_DIM // 2)
    k_ref[...] = _rope_lanes(proj[:, SWA_Q:SWA_Q + SWA_KV], cos[:, :SWA_KV], slo[:, :SWA_KV], shi[:, :SWA_KV],
                             ROT_DIM // 2)
    v_ref[...] = proj[:, SWA_Q + SWA_KV:SWA_Q + 2 * SWA_KV]
    u_ref[...] = proj[:, SWA_Q + 2 * SWA_KV:]


def in_even(x2d, g, w_bf16, tabs):
    n = x2d.shape[0]
    tm = min(ROW_BLOCK, n)
    nt = tabs[0].shape[0] // tm
    row = lambda i: (i, 0)
    const = lambda i: (0, 0)
    tab = lambda i: (i % nt, 0)
    return pl.pallas_call(
        _in_even_kernel,
        out_shape=(jax.ShapeDtypeStruct((n, SWA_Q), jnp.float32), jax.ShapeDtypeStruct((n, SWA_KV), jnp.float32),
                   jax.ShapeDtypeStruct((n, SWA_KV), jnp.float32), jax.ShapeDtypeStruct((n, S5_WIDTH), jnp.float32)),
        grid=(n // tm,),
        in_specs=[pl.BlockSpec((tm, D_MODEL), row), pl.BlockSpec((1, D_MODEL), const),
                  pl.BlockSpec((D_MODEL, EVEN_IN), const),
                  pl.BlockSpec((tm, SWA_Q), tab), pl.BlockSpec((tm, SWA_Q), tab), pl.BlockSpec((tm, SWA_Q), tab)],
        out_specs=(pl.BlockSpec((tm, SWA_Q), row), pl.BlockSpec((tm, SWA_KV), row),
                   pl.BlockSpec((tm, SWA_KV), row), pl.BlockSpec((tm, S5_WIDTH), row)),
        compiler_params=_cparams(), name="in_even",
    )(x2d, g.reshape(1, D_MODEL), w_bf16, *tabs)


def _swa_kernel(sink_ref, q_ref, kp_ref, kc_ref, vp_ref, vc_ref, o_ref, *, banded, qb):
    i = pl.program_id(1)
    q = q_ref[0]
    k = jnp.concatenate([kp_ref[0], kc_ref[0]], axis=0).astype(jnp.bfloat16)
    v = jnp.concatenate([vp_ref[0], vc_ref[0]], axis=0).astype(jnp.bfloat16)
    nk = WINDOW + qb
    wc = WINDOW // CHUNK
    if banded:
        qc = lax.broadcasted_iota(jnp.int32, (qb, nk), 0) // CHUNK + wc
        kc = lax.broadcasted_iota(jnp.int32, (qb, nk), 1) // CHUNK
        visible = (kc <= qc) & (kc >= qc - wc) & ((kc >= wc) | (i > 0))
    ss = []
    for h in range(SWA_HEADS):
        hk = h // SWA_GROUP
        qh = q[:, h * HEAD_DIM:(h + 1) * HEAD_DIM].astype(jnp.bfloat16)
        ss.append(lax.dot_general(qh, k[:, hk * HEAD_DIM:(hk + 1) * HEAD_DIM], (((1,), (1,)), ((), ())),
                                  preferred_element_type=jnp.float32) * SWA_SCALE)
    ps, dens = [], []
    for h in range(SWA_HEADS):
        s = jnp.where(visible, ss[h], NEG_INF) if banded else ss[h]
        sk = sink_ref[h]
        m = jnp.maximum(jnp.max(s, axis=-1, keepdims=True), sk)
        p = jnp.exp(s - m)
        dens.append(jnp.sum(p, axis=-1, keepdims=True) + jnp.exp(sk - m))
        ps.append(p.astype(jnp.bfloat16))
    outs = [jnp.dot(ps[h], v[:, (h // SWA_GROUP) * HEAD_DIM:(h // SWA_GROUP + 1) * HEAD_DIM],
                    preferred_element_type=jnp.float32) / dens[h] for h in range(SWA_HEADS)]
    o_ref[0] = jnp.concatenate(outs, axis=-1)


def swa_attention(q, k_prev, k_cur, v_prev, v_cur, sink, banded):
    b, t, _ = q.shape
    qb = min(SWA_QBLOCK, t)
    per = qb // WINDOW
    prev_map = (lambda bi, i: (bi, jnp.maximum(i * per - 1, 0), 0)) if banded else (lambda bi, i: (bi, 0, 0))
    cur = lambda bi, i: (bi, i, 0)
    return pl.pallas_call(
        functools.partial(_swa_kernel, banded=banded, qb=qb),
        out_shape=jax.ShapeDtypeStruct((b, t, SWA_Q), jnp.float32),
        grid=(b, t // qb),
        in_specs=[pl.BlockSpec(memory_space=pltpu.SMEM),
                  pl.BlockSpec((1, qb, SWA_Q), cur),
                  pl.BlockSpec((1, WINDOW, SWA_KV), prev_map), pl.BlockSpec((1, qb, SWA_KV), cur),
                  pl.BlockSpec((1, WINDOW, SWA_KV), prev_map), pl.BlockSpec((1, qb, SWA_KV), cur)],
        out_specs=pl.BlockSpec((1, qb, SWA_Q), cur),
        compiler_params=_cparams(2), name="swa_attention",
    )(sink, q, k_prev, k_cur, v_prev, v_cur)


def _out_kernel(x_ref, a_ref, b_ref, w_ref, o_ref):
    ka = a_ref.shape[-1]
    o_ref[...] = (x_ref[...]
                  + jnp.dot(a_ref[...].astype(jnp.bfloat16), w_ref[:ka, :], preferred_element_type=jnp.float32)
                  + jnp.dot(b_ref[...].astype(jnp.bfloat16), w_ref[ka:, :], preferred_element_type=jnp.float32))


def out_proj(x2d, a, b, w_bf16):
    n = x2d.shape[0]
    tm = min(ROW_BLOCK, n)
    row = lambda i: (i, 0)
    return pl.pallas_call(
        _out_kernel,
        out_shape=jax.ShapeDtypeStruct((n, D_MODEL), jnp.float32),
        grid=(n // tm,),
        in_specs=[pl.BlockSpec((tm, D_MODEL), row), pl.BlockSpec((tm, a.shape[1]), row),
                  pl.BlockSpec((tm, b.shape[1]), row), pl.BlockSpec(w_bf16.shape, lambda i: (0, 0))],
        out_specs=pl.BlockSpec((tm, D_MODEL), row),
        compiler_params=_cparams(), name="out_proj",
    )(x2d, a, b, w_bf16)


def s5_discretize(lam_re, lam_im, log_dt, b_re, b_im, c_re, c_im):
    lr = jnp.minimum(lam_re, -1e-4)
    li = lam_im
    dt = jnp.exp(log_dt)[:, None]
    mag = jnp.exp(lr * dt)
    ang = li * dt
    ab_re, ab_im = mag * jnp.cos(ang), mag * jnp.sin(ang)
    den = lr * lr + li * li
    nr, ni = ab_re - 1.0, ab_im
    f_re = (nr * lr + ni * li) / den
    f_im = (ni * lr - nr * li) / den
    bb_re = f_re[..., None] * b_re - f_im[..., None] * b_im
    bb_im = f_re[..., None] * b_im + f_im[..., None] * b_re
    eye = jnp.eye(S5_GROUPS, dtype=jnp.float32)

    def embed_b(bb):
        return jnp.einsum('gnc,gh->gchn', bb, eye).reshape(S5_WIDTH, S5_FLAT)

    def embed_c(c):
        return jnp.einsum('gcn,gh->gnhc', c, eye).reshape(S5_FLAT, S5_WIDTH)

    wb = jnp.concatenate([embed_b(bb_re), embed_b(bb_im)], axis=1)
    wc = jnp.concatenate([embed_c(c_re), -embed_c(c_im)], axis=0)
    lam = jnp.stack([ab_re.reshape(S5_FLAT), ab_im.reshape(S5_FLAT)])
    return lam, wb.astype(jnp.bfloat16), wc.astype(jnp.bfloat16)


def _s5_kernel(u_ref, lam_ref, wb_ref, wc_ref, d_ref, wglu_ref, bglu_ref, h0r_ref, h0i_ref,
               o_ref, hr_ref, hi_ref, hbuf, *, nb, steps, width):
    @pl.when(pl.program_id(0) == 0)
    def _():
        hr_ref[...] = h0r_ref[...]
        hi_ref[...] = h0i_ref[...]

    u = u_ref[...]
    ub = u.astype(jnp.bfloat16)
    n_blocks = S5_GROUPS // S5_BLOCK_GROUPS
    uw, sw = S5_BLOCK_GROUPS * S5_GROUP, S5_BLOCK_GROUPS * S5_STATE
    for gb in range(n_blocks):
        ucols = slice(gb * uw, (gb + 1) * uw)
        for part in range(2):
            scols = slice(part * S5_FLAT + gb * sw, part * S5_FLAT + (gb + 1) * sw)
            hbuf[:, scols] = jnp.dot(ub[:, ucols], wb_ref[ucols, scols], preferred_element_type=jnp.float32)
    for c0 in range(0, S5_FLAT, width):
        cre = slice(c0, c0 + width)
        cim = slice(S5_FLAT + c0, S5_FLAT + c0 + width)
        lr = jnp.broadcast_to(lam_ref[0:1, cre], (nb, width))
        li = jnp.broadcast_to(lam_ref[1:2, cre], (nb, width))

        def step(t, carry):
            hr, hi = carry
            rows = pl.ds(pl.multiple_of(t * nb, nb), nb)
            nhr = lr * hr - li * hi + hbuf[rows, cre]
            nhi = lr * hi + li * hr + hbuf[rows, cim]
            hbuf[rows, cre] = nhr
            hbuf[rows, cim] = nhi
            return nhr, nhi

        hr, hi = lax.fori_loop(0, steps, step, (hr_ref[:, cre], hi_ref[:, cre]))
        hr_ref[:, cre] = hr
        hi_ref[:, cre] = hi
    ys = []
    for gb in range(n_blocks):
        ucols = slice(gb * uw, (gb + 1) * uw)
        re = slice(gb * sw, (gb + 1) * sw)
        im = slice(S5_FLAT + gb * sw, S5_FLAT + (gb + 1) * sw)
        ys.append(jnp.dot(hbuf[:, re].astype(jnp.bfloat16), wc_ref[re, ucols], preferred_element_type=jnp.float32)
                  + jnp.dot(hbuf[:, im].astype(jnp.bfloat16), wc_ref[im, ucols], preferred_element_type=jnp.float32))
    y = jnp.concatenate(ys, axis=-1) + d_ref[...] * u
    z = 0.5 * y * (1.0 + jnp.tanh(math.sqrt(2.0 / math.pi) * (y + 0.044715 * (y * y * y))))
    gate = jnp.dot(z.astype(jnp.bfloat16), wglu_ref[...], preferred_element_type=jnp.float32) + bglu_ref[...]
    o_ref[...] = z * (1.0 / (1.0 + jnp.exp(-gate)))


def s5_mixer(u_tm, nb, lam, wb, wc, d_skip, w_glu_bf16, b_glu, h0r, h0i, steps):
    rows = u_tm.shape[0]
    t_total = rows // nb
    width = min(S5_FLAT, max(LANES, (S5_CARRY_VREGS * SUBLANES * LANES) // (2 * nb)))
    blk = steps * nb
    const = lambda i: (0, 0)
    return pl.pallas_call(
        functools.partial(_s5_kernel, nb=nb, steps=steps, width=width),
        out_shape=(jax.ShapeDtypeStruct((rows, S5_WIDTH), jnp.float32),
                   jax.ShapeDtypeStruct((nb, S5_FLAT), jnp.float32),
                   jax.ShapeDtypeStruct((nb, S5_FLAT), jnp.float32)),
        grid=(t_total // steps,),
        in_specs=[pl.BlockSpec((blk, S5_WIDTH), lambda i: (i, 0)),
                  pl.BlockSpec((2, S5_FLAT), const),
                  pl.BlockSpec((S5_WIDTH, 2 * S5_FLAT), const),
                  pl.BlockSpec((2 * S5_FLAT, S5_WIDTH), const),
                  pl.BlockSpec((1, S5_WIDTH), const),
                  pl.BlockSpec((S5_WIDTH, S5_WIDTH), const),
                  pl.BlockSpec((1, S5_WIDTH), const),
                  pl.BlockSpec((nb, S5_FLAT), const),
                  pl.BlockSpec((nb, S5_FLAT), const)],
        out_specs=(pl.BlockSpec((blk, S5_WIDTH), lambda i: (i, 0)),
                   pl.BlockSpec((nb, S5_FLAT), const),
                   pl.BlockSpec((nb, S5_FLAT), const)),
        scratch_shapes=[pltpu.VMEM((blk, 2 * S5_FLAT), jnp.float32)],
        compiler_params=_cparams(), name="s5_mixer",
    )(u_tm, lam, wb, wc, d_skip.reshape(1, S5_WIDTH), w_glu_bf16, b_glu.reshape(1, S5_WIDTH), h0r, h0i)


def _in_odd_kernel(x_ref, g_ref, w_ref, qn_ref, kvn_ref, wuq_ref, kcos_ref, kslo_ref, kshi_ref,
                   qcos_ref, qslo_ref, qshi_ref, qnope_ref, qpe_ref, c_ref, kp_ref, u_ref):
    xn = _rms(x_ref[...], g_ref[...])
    proj = jnp.dot(xn.astype(jnp.bfloat16), w_ref[...], preferred_element_type=jnp.float32)
    cqn = _rms(proj[:, :Q_LORA], qn_ref[...])
    q = jnp.dot(cqn.astype(jnp.bfloat16), wuq_ref[...], preferred_element_type=jnp.float32)
    qnope_ref[...] = q[:, :MLA_QNOPE]
    qpe_ref[...] = _rope_lanes(q[:, MLA_QNOPE:], qcos_ref[...], qslo_ref[...], qshi_ref[...], ROPE_DIM // 2)
    c_ref[...] = _rms(proj[:, Q_LORA:ODD_U0], kvn_ref[...])
    kp = _rope_lanes(proj[:, ODD_KPE0:], kcos_ref[...], kslo_ref[...], kshi_ref[...], ROPE_DIM // 2)
    kp_ref[...] = kp[:, :ROPE_DIM]
    u_ref[...] = proj[:, ODD_U0:ODD_KPE0]


def in_odd(x2d, g, w_perm_bf16, q_norm, kv_norm, wuq_perm_bf16, ktabs, qtabs):
    n = x2d.shape[0]
    tm = min(ROW_BLOCK, n)
    nt = ktabs[0].shape[0] // tm
    row = lambda i: (i, 0)
    const = lambda i: (0, 0)
    tab = lambda i: (i % nt, 0)
    return pl.pallas_call(
        _in_odd_kernel,
        out_shape=(jax.ShapeDtypeStruct((n, MLA_QNOPE), jnp.float32), jax.ShapeDtypeStruct((n, MLA_QPE), jnp.float32),
                   jax.ShapeDtypeStruct((n, KV_LORA), jnp.float32), jax.ShapeDtypeStruct((n, ROPE_DIM), jnp.float32),
                   jax.ShapeDtypeStruct((n, POOL_WIDTH), jnp.float32)),
        grid=(n // tm,),
        in_specs=[pl.BlockSpec((tm, D_MODEL), row), pl.BlockSpec((1, D_MODEL), const),
                  pl.BlockSpec((D_MODEL, ODD_IN_PAD), const),
                  pl.BlockSpec((1, Q_LORA), const), pl.BlockSpec((1, KV_LORA), const),
                  pl.BlockSpec((Q_LORA, MLA_QNOPE + MLA_QPE), const),
                  pl.BlockSpec((tm, KPE_PAD), tab), pl.BlockSpec((tm, KPE_PAD), tab), pl.BlockSpec((tm, KPE_PAD), tab),
                  pl.BlockSpec((tm, MLA_QPE), tab), pl.BlockSpec((tm, MLA_QPE), tab), pl.BlockSpec((tm, MLA_QPE), tab)],
        out_specs=(pl.BlockSpec((tm, MLA_QNOPE), row), pl.BlockSpec((tm, MLA_QPE), row), pl.BlockSpec((tm, KV_LORA), row),
                   pl.BlockSpec((tm, ROPE_DIM), row), pl.BlockSpec((tm, POOL_WIDTH), row)),
        compiler_params=_cparams(), name="in_odd",
    )(x2d, g.reshape(1, D_MODEL), w_perm_bf16, q_norm.reshape(1, Q_LORA), kv_norm.reshape(1, KV_LORA), wuq_perm_bf16,
      *ktabs, *qtabs)


def permute_odd_weights(w_in, w_uq):
    o1, o2 = Q_LORA + KV_LORA, Q_LORA + KV_LORA + ROPE_DIM
    kpe = jnp.pad(w_in[:, o1:o2], ((0, 0), (0, KPE_PAD - ROPE_DIM)))
    w_perm = jnp.concatenate([w_in[:, :o1], w_in[:, o2:], kpe], axis=1)
    wq = w_uq.reshape(Q_LORA, MLA_HEADS, NOPE_DIM + ROPE_DIM)
    wq_perm = jnp.concatenate([wq[:, :, :NOPE_DIM].reshape(Q_LORA, MLA_QNOPE),
                               wq[:, :, NOPE_DIM:].reshape(Q_LORA, MLA_QPE)], axis=1)
    return w_perm.astype(jnp.bfloat16), wq_perm.astype(jnp.bfloat16)


def _mla_kernel(qn_ref, qp_ref, c_ref, kp_ref, wuk_ref, wuv_ref, o_ref, *scratch, causal, qb, kb, n_keys, hpc):
    i = pl.program_id(1)
    n_chains = MLA_HEADS // hpc
    qa_s, qp_s, m_s, l_s, acc_s = (scratch[k * n_chains:(k + 1) * n_chains] for k in range(5))
    qn = qn_ref[0].astype(jnp.bfloat16)
    qp = qp_ref[0].astype(jnp.bfloat16)
    for g in range(n_chains):
        for hh in range(hpc):
            h = g * hpc + hh
            rows = slice(hh * qb, (hh + 1) * qb)
            qa_s[g][rows, :] = jnp.dot(qn[:, h * NOPE_DIM:(h + 1) * NOPE_DIM], wuk_ref[h],
                                       preferred_element_type=jnp.float32).astype(jnp.bfloat16)
            qp_s[g][rows, :] = qp[:, h * ROPE_DIM:(h + 1) * ROPE_DIM]
        m_s[g][...] = jnp.full(m_s[g].shape, NEG_INF, jnp.float32)
        l_s[g][...] = jnp.zeros(l_s[g].shape, jnp.float32)
        acc_s[g][...] = jnp.zeros(acc_s[g].shape, jnp.float32)
    if causal:
        qpos = i * qb + lax.broadcasted_iota(jnp.int32, (hpc * qb, kb), 0) % qb
        limit = (qpos // CHUNK + 1) * CHUNK
        nblk = ((i + 1) * qb + kb - 1) // kb
    else:
        limit = n_keys
        nblk = (n_keys + kb - 1) // kb
    kidx0 = lax.broadcasted_iota(jnp.int32, (hpc * qb, kb), 1)

    def body(j, carry):
        rows = pl.ds(pl.multiple_of(j * kb, kb), kb)
        cb = c_ref[0, rows, :]
        kpb = kp_ref[0, rows, :]
        visible = kidx0 + j * kb < limit
        ss = [(lax.dot_general(qa_s[g][...], cb, (((1,), (1,)), ((), ())), preferred_element_type=jnp.float32)
               + lax.dot_general(qp_s[g][...], kpb, (((1,), (1,)), ((), ())),
                                 preferred_element_type=jnp.float32)) * MLA_SCALE for g in range(n_chains)]
        ps, alphas = [], []
        for g in range(n_chains):
            s = jnp.where(visible, ss[g], NEG_INF)
            m_old = m_s[g][...]
            m_new = jnp.maximum(m_old, jnp.max(s, axis=-1, keepdims=True))
            alpha = jnp.exp(m_old - m_new)
            p = jnp.exp(s - m_new)
            m_s[g][...] = m_new
            l_s[g][...] = alpha * l_s[g][...] + jnp.sum(p, axis=-1, keepdims=True)
            ps.append(p.astype(jnp.bfloat16))
            alphas.append(alpha)
        for g in range(n_chains):
            acc_s[g][...] = alphas[g] * acc_s[g][...] + jnp.dot(ps[g], cb, preferred_element_type=jnp.float32)
        return carry

    lax.fori_loop(0, nblk, body, 0)
    outs = []
    for g in range(n_chains):
        o_lat = (acc_s[g][...] / l_s[g][...]).astype(jnp.bfloat16)
        for hh in range(hpc):
            outs.append(jnp.dot(o_lat[hh * qb:(hh + 1) * qb, :], wuv_ref[g * hpc + hh],
                                preferred_element_type=jnp.float32))
    o_ref[0] = jnp.concatenate(outs, axis=-1)


def mla_attention(q_nope, q_pe, c_keys_bf16, kp_keys_bf16, wuk_h, wuv_h, causal, n_keys):
    b, t, _ = q_nope.shape
    tk = c_keys_bf16.shape[1]
    qb = min(MLA_QBLOCK, t)
    kb = min(MLA_KBLOCK, tk)
    hpc = max(1, min(MLA_HEADS, MLA_CHAIN_ROWS // qb))
    n_chains, rows = MLA_HEADS // hpc, hpc * qb
    cur = lambda bi, i: (bi, i, 0)
    whole = lambda bi, i: (bi, 0, 0)
    const3 = lambda bi, i: (0, 0, 0)
    return pl.pallas_call(
        functools.partial(_mla_kernel, causal=causal, qb=qb, kb=kb, n_keys=n_keys, hpc=hpc),
        out_shape=jax.ShapeDtypeStruct((b, t, MLA_HEADS * V_DIM), jnp.float32),
        grid=(b, t // qb),
        in_specs=[pl.BlockSpec((1, qb, MLA_QNOPE), cur), pl.BlockSpec((1, qb, MLA_QPE), cur),
                  pl.BlockSpec((1, tk, KV_LORA), whole), pl.BlockSpec((1, tk, ROPE_DIM), whole),
                  pl.BlockSpec((MLA_HEADS, NOPE_DIM, KV_LORA), const3), pl.BlockSpec((MLA_HEADS, KV_LORA, V_DIM), const3)],
        out_specs=pl.BlockSpec((1, qb, MLA_HEADS * V_DIM), cur),
        scratch_shapes=([pltpu.VMEM((rows, KV_LORA), jnp.bfloat16)] * n_chains
                        + [pltpu.VMEM((rows, ROPE_DIM), jnp.bfloat16)] * n_chains
                        + [pltpu.VMEM((rows, 1), jnp.float32)] * (2 * n_chains)
                        + [pltpu.VMEM((rows, KV_LORA), jnp.float32)] * n_chains),
        compiler_params=_cparams(2), name="mla_attention",
    )(q_nope, q_pe, c_keys_bf16, kp_keys_bf16, wuk_h, wuv_h)


def _pool_kernel(u_ref, prev_ref, w_ref, scale_ref, o_ref, new_ref, ext, *, tm, pos0):
    j = pl.program_id(1)

    @pl.when(j == 0)
    def _():
        ext[0:1, :] = jnp.zeros((1, POOL_WIDTH), jnp.float32)
        ext[1:POOL_MAX, :] = prev_ref[0]

    @pl.when(j > 0)
    def _():
        ext[0:POOL_MAX, :] = ext[tm:tm + POOL_MAX, :]

    u = u_ref[0]
    ext[POOL_MAX:POOL_MAX + tm, :] = u
    pos = pos0 + j * tm + lax.broadcasted_iota(jnp.int32, (tm, POOL_GROUP), 0)
    outs = []
    for gi, w in enumerate(POOL_WINDOWS):
        cols = slice(gi * POOL_GROUP, (gi + 1) * POOL_GROUP)
        tot = u[:, cols]
        for d in range(1, w):
            tot = tot + ext[POOL_MAX - d:POOL_MAX - d + tm, cols]
        cnt = jnp.minimum(pos + 1, w).astype(jnp.float32)
        m = tot / cnt - u[:, cols]
        outs.append(jnp.dot(m.astype(jnp.bfloat16), w_ref[gi], preferred_element_type=jnp.float32))
    o_ref[0] = jnp.concatenate(outs, axis=-1) * scale_ref[...]
    new_ref[0] = ext[tm + 1:tm + POOL_MAX, :]


def pool_mixer(u, prev, pool_w_bf16, pool_scale, pos0):
    b, t, _ = u.shape
    tm = min(ROW_BLOCK, t)
    cur = lambda bi, j: (bi, j, 0)
    per_b = lambda bi, j: (bi, 0, 0)
    return pl.pallas_call(
        functools.partial(_pool_kernel, tm=tm, pos0=pos0),
        out_shape=(jax.ShapeDtypeStruct((b, t, POOL_WIDTH), jnp.float32),
                   jax.ShapeDtypeStruct((b, POOL_BUF, POOL_WIDTH), jnp.float32)),
        grid=(b, t // tm),
        in_specs=[pl.BlockSpec((1, tm, POOL_WIDTH), cur), pl.BlockSpec((1, POOL_BUF, POOL_WIDTH), per_b),
                  pl.BlockSpec((len(POOL_WINDOWS), POOL_GROUP, POOL_GROUP), lambda bi, j: (0, 0, 0)),
                  pl.BlockSpec((1, POOL_WIDTH), lambda bi, j: (0, 0))],
        out_specs=(pl.BlockSpec((1, tm, POOL_WIDTH), cur), pl.BlockSpec((1, POOL_BUF, POOL_WIDTH), per_b)),
        scratch_shapes=[pltpu.VMEM((tm + POOL_MAX, POOL_WIDTH), jnp.float32)],
        compiler_params=_cparams(2), name="pool_mixer",
    )(u, prev, pool_w_bf16, pool_scale.reshape(1, POOL_WIDTH))


def _top16_rows(s, n_rows):
    iota = lax.broadcasted_iota(jnp.int32, s.shape, 0).astype(jnp.float32)
    vals, idxs = [], []
    for _ in range(PEER_TOPK):
        m = jnp.max(s, axis=0, keepdims=True)
        idx = jnp.min(jnp.where(s == m, iota, float(n_rows)), axis=0, keepdims=True)
        vals.append(m)
        idxs.append(idx)
        s = jnp.where(iota == idx, -jnp.inf, s)
    return jnp.concatenate(vals, axis=0), jnp.concatenate(idxs, axis=0)


def _route_kernel(x_ref, g_ref, wq_ref, keys_ref, xn_ref, pair_ref, gate_ref, code_t, gate_t):
    xn = _rms(x_ref[...], g_ref[...])
    xn_ref[...] = xn
    xb = xn.astype(jnp.bfloat16)

    def head_step(hs, carry):
        for hh in range(ROUTE_HEADS_PER_STEP):
            one_head(hs * ROUTE_HEADS_PER_STEP + hh)
        return carry

    def one_head(h):
        qb = jnp.dot(xb, wq_ref[h], preferred_element_type=jnp.float32).astype(jnp.bfloat16)
        sv, si = [], []
        for p in range(2):
            s = lax.dot_general(keys_ref[h * 2 + p], qb[:, p * D_HALF:(p + 1) * D_HALF],
                                (((1,), (1,)), ((), ())), preferred_element_type=jnp.float32)
            v, i = _top16_rows(s, N_KEYS)
            sv.append(v)
            si.append(i)
        cand = jnp.concatenate([sv[0][a:a + 1] + sv[1][:nb] for a, nb in enumerate(PAIR_COLS)]
                               + [jnp.full((PAIR_PAD, s.shape[1]), -jnp.inf, jnp.float32)], axis=0)
        eid = jnp.concatenate([si[0][a:a + 1] * float(N_KEYS) + si[1][:nb] for a, nb in enumerate(PAIR_COLS)]
                              + [jnp.zeros((PAIR_PAD, s.shape[1]), jnp.float32)], axis=0)
        iota = lax.broadcasted_iota(jnp.int32, cand.shape, 0).astype(jnp.float32)
        cv, ce = [], []
        for _ in range(PEER_TOPK):
            m = jnp.max(cand, axis=0, keepdims=True)
            idx = jnp.min(jnp.where(cand == m, iota, float(PAIR_ROWS)), axis=0, keepdims=True)
            hit = iota == idx
            cv.append(m)
            ce.append(jnp.max(jnp.where(hit, eid, -1.0), axis=0, keepdims=True))
            cand = jnp.where(hit, -jnp.inf, cand)
        cv = jnp.concatenate(cv, axis=0)
        ce = jnp.concatenate(ce, axis=0).astype(jnp.int32)
        e = jnp.exp(cv - cv[0:1])
        rows = pl.ds(pl.multiple_of(h * PEER_TOPK, PEER_TOPK), PEER_TOPK)
        gate_t[rows, :] = e / jnp.sum(e, axis=0, keepdims=True)
        code_t[rows, :] = ((ce & (HALF_EXPERTS - 1)) << 3) | (ce >> 13)

    lax.fori_loop(0, PEER_HEADS // ROUTE_HEADS_PER_STEP, head_step, 0)
    ca, cb = code_t[:NSEL // 2, :], code_t[NSEL // 2:, :]
    pair_ref[...] = (ca | (((cb & 1) ^ 1) << 1) | ((cb & -8) << 16)).T
    gate_ref[...] = gate_t[...].T


def peer_route(x2d, g, wq_heads, keys_bf16, tb):
    n = x2d.shape[0]
    return pl.pallas_call(
        _route_kernel,
        out_shape=(jax.ShapeDtypeStruct((n, D_MODEL), jnp.float32),
                   jax.ShapeDtypeStruct((n, NSEL // 2), jnp.int32),
                   jax.ShapeDtypeStruct((n, NSEL), jnp.float32)),
        grid=(n // tb,),
        in_specs=[pl.BlockSpec((tb, D_MODEL), lambda i: (i, 0)),
                  pl.BlockSpec((1, D_MODEL), lambda i: (0, 0)),
                  pl.BlockSpec((PEER_HEADS, D_MODEL, D_KEY), lambda i: (0, 0, 0)),
                  pl.BlockSpec((PEER_HEADS * 2, N_KEYS, D_HALF), lambda i: (0, 0, 0))],
        out_specs=(pl.BlockSpec((tb, D_MODEL), lambda i: (i, 0)),
                   pl.BlockSpec((tb, NSEL // 2), lambda i: (i, 0)),
                   pl.BlockSpec((tb, NSEL), lambda i: (i, 0))),
        scratch_shapes=[pltpu.VMEM((NSEL, tb), jnp.int32), pltpu.VMEM((NSEL, tb), jnp.float32)],
        compiler_params=_cparams(), name="peer_route",
    )(x2d, g.reshape(1, D_MODEL), wq_heads, keys_bf16)


def _pack_kernel(lo_ref, hi_ref, o_ref):
    def bf16_bits(x):
        return lax.bitcast_convert_type(x.astype(jnp.bfloat16).astype(jnp.float32), jnp.int32)

    word = lax.shift_right_logical(bf16_bits(lo_ref[...]), 16) | (bf16_bits(hi_ref[...]) & HI16)
    for k in range(ROW_TILE):
        o_ref[:, k, :] = word[:, k * LANES:(k + 1) * LANES]


def pack_table(tab):
    nb = HALF_EXPERTS // PACK_BLOCK
    out = pl.pallas_call(
        _pack_kernel,
        out_shape=jax.ShapeDtypeStruct((HALF_EXPERTS, ROW_TILE, LANES), jnp.int32),
        grid=(nb,),
        in_specs=[pl.BlockSpec((PACK_BLOCK, D_MODEL), lambda i: (i, 0)),
                  pl.BlockSpec((PACK_BLOCK, D_MODEL), lambda i: (i + nb, 0))],
        out_specs=pl.BlockSpec((PACK_BLOCK, ROW_TILE, LANES), lambda i: (i, 0, 0)),
        compiler_params=_cparams(), name="pack_table",
    )(tab, tab)
    return out.reshape(HALF_EXPERTS * ROW_TILE, LANES)


def _gather_pair(tab_ref, cab):
    wa = tab_ref[pl.ds(pl.multiple_of(cab & 0xFFF8, SUBLANES), SUBLANES), :]
    wb = tab_ref[pl.ds(pl.multiple_of(lax.shift_right_logical(cab, 16), SUBLANES), SUBLANES), :]
    cv = jnp.full((SUBLANES, LANES), cab, jnp.int32)
    sha = (cv & 1) << 4
    shb = (cv & 2) << 3
    return (lax.shift_right_logical(wa, sha) & 0xFFFF) | (lax.shift_left(wb, shb) & HI16)


def _gather_token(tab_ref, pair_ref, t, buf):
    for i in range(NSEL // 2):
        buf[i * ROW_TILE:(i + 1) * ROW_TILE, :] = _gather_pair(tab_ref, pair_ref[t, i])


def _grouped_token_loop(tb, gather, finish, group_a, group_b):
    @pl.when(pl.program_id(0) == 0)
    def _():
        for buf in group_b:
            buf[...] = jnp.zeros(buf.shape, jnp.int32)

    def body(i, carry):
        t0 = 2 * UPASS_GROUP * i
        for q in range(UPASS_GROUP):
            gather(t0 + q, group_a[q])
        finish(jnp.maximum(t0 - UPASS_GROUP, 0), group_b)
        for q in range(UPASS_GROUP):
            gather(t0 + UPASS_GROUP + q, group_b[q])
        finish(t0, group_a)
        return carry

    lax.fori_loop(0, tb // (2 * UPASS_GROUP), body, 0)
    finish(tb - UPASS_GROUP, group_b)


def _upass_kernel(pair_ref, gate_ref, xn_ref, tab_ref, w_ref, *bufs, tb):
    col = lax.broadcasted_iota(jnp.int32, (2 * ROW_TILE, SLOT_ROWS), 1)
    row = lax.broadcasted_iota(jnp.int32, (2 * ROW_TILE, SLOT_ROWS), 0)
    chunk_mask = (((col & 15) >> 1) == (row & 7)).astype(jnp.float32)
    c2 = lax.broadcasted_iota(jnp.int32, (SLOT_ROWS, NSEL), 0)
    j2 = lax.broadcasted_iota(jnp.int32, (SLOT_ROWS, NSEL), 1)
    fold = (j2 == (c2 >> 4) + ((c2 & 1) << 6)).astype(jnp.bfloat16)

    def gather(t, buf):
        _gather_token(tab_ref, pair_ref, t, buf)

    def finish(t0, group):
        zs = []
        for q, buf in enumerate(group):
            xt = xn_ref[pl.ds(t0 + q, 1), :].reshape(ROW_TILE, LANES)
            xhi = xt.astype(jnp.bfloat16)
            xlo = (xt - xhi.astype(jnp.float32)).astype(jnp.bfloat16)
            x16 = jnp.concatenate([xhi, xlo], axis=0)
            us = pltpu.bitcast(buf[...], jnp.bfloat16)
            r = lax.dot_general(x16, us, (((1,), (1,)), ((), ())), preferred_element_type=jnp.float32)
            zs.append(jnp.sum(r * chunk_mask, axis=0, keepdims=True))
        z = jnp.concatenate(zs, axis=0)
        zhi = z.astype(jnp.bfloat16)
        zlo = (z - zhi.astype(jnp.float32)).astype(jnp.bfloat16)
        act = (jnp.dot(zhi, fold, preferred_element_type=jnp.float32)
               + jnp.dot(zlo, fold, preferred_element_type=jnp.float32))
        gelu = 0.5 * act * (1.0 + lax.erf(act * (1.0 / math.sqrt(2.0))))
        rows = pl.ds(pl.multiple_of(t0, UPASS_GROUP), UPASS_GROUP)
        w_ref[rows, :] = (gate_ref[rows, :] * gelu).astype(jnp.bfloat16)

    _grouped_token_loop(tb, gather, finish, bufs[:UPASS_GROUP], bufs[UPASS_GROUP:])


def _peer_pass_call(kernel_fn, out_shape, out_block, pairs, per_token, rows, tab_packed, tb, name):
    n = pairs.shape[0]
    return pl.pallas_call(
        functools.partial(kernel_fn, tb=tb),
        out_shape=out_shape,
        grid=(n // tb,),
        in_specs=[pl.BlockSpec((tb, NSEL // 2), lambda i: (i, 0), memory_space=pltpu.SMEM),
                  pl.BlockSpec((tb, NSEL), lambda i: (i, 0)),
                  pl.BlockSpec((tb, D_MODEL), lambda i: (i, 0)),
                  pl.BlockSpec((HALF_EXPERTS * ROW_TILE, LANES), lambda i: (0, 0), pipeline_mode=pl.Buffered(1))],
        out_specs=pl.BlockSpec(out_block, lambda i: (i, 0)),
        scratch_shapes=[pltpu.VMEM((SLOT_ROWS // 2, LANES), jnp.int32)] * (2 * UPASS_GROUP),
        compiler_params=_cparams(), name=name,
    )(pairs, per_token, rows, tab_packed)


def peer_upass(pairs, gate, xn_rows, tab_packed, tb):
    n = pairs.shape[0]
    return _peer_pass_call(_upass_kernel, jax.ShapeDtypeStruct((n, NSEL), jnp.bfloat16), (tb, NSEL),
                           pairs, gate, xn_rows, tab_packed, tb, "peer_upass")


def _vpass_kernel(pair_ref, w_ref, x_ref, tab_ref, o_ref, *bufs, tb):
    col = lax.broadcasted_iota(jnp.int32, (ROW_TILE, SLOT_ROWS), 1)
    row = lax.broadcasted_iota(jnp.int32, (ROW_TILE, SLOT_ROWS), 0)
    chunk_mask = (((col & 15) >> 1) == row).astype(jnp.float32)
    j2 = lax.broadcasted_iota(jnp.int32, (NSEL, SLOT_ROWS), 0)
    c2 = lax.broadcasted_iota(jnp.int32, (NSEL, SLOT_ROWS), 1)
    spread = (j2 == (c2 >> 4) + ((c2 & 1) << 6)).astype(jnp.bfloat16)

    def gather(t, buf):
        _gather_token(tab_ref, pair_ref, t, buf)

    def finish(t0, group):
        rows = pl.ds(pl.multiple_of(t0, UPASS_GROUP), UPASS_GROUP)
        wexp = jnp.dot(w_ref[rows, :], spread, preferred_element_type=jnp.float32)
        for q, buf in enumerate(group):
            wsel = (jnp.broadcast_to(wexp[q:q + 1], (ROW_TILE, SLOT_ROWS)) * chunk_mask).astype(jnp.bfloat16)
            vs = pltpu.bitcast(buf[...], jnp.bfloat16)
            tile = jnp.dot(wsel, vs, preferred_element_type=jnp.float32)
            o_ref[pl.ds(t0 + q, 1), :] = x_ref[pl.ds(t0 + q, 1), :] + tile.reshape(1, D_MODEL)

    _grouped_token_loop(tb, gather, finish, bufs[:UPASS_GROUP], bufs[UPASS_GROUP:])


def peer_vpass(pairs, w_bf16, x_rows, tab_packed, tb):
    n = pairs.shape[0]
    return _peer_pass_call(_vpass_kernel, jax.ShapeDtypeStruct((n, D_MODEL), jnp.float32),
                           (tb, D_MODEL), pairs, w_bf16, x_rows, tab_packed, tb, "peer_vpass")


def peer_block(x, g, wq_heads, keys_bf16, u_packed, v_packed):
    shp = x.shape
    x2d = x.reshape(-1, D_MODEL)
    n = x2d.shape[0]
    xn, pairs, gate = peer_route(x2d, g, wq_heads, keys_bf16, min(PEER_ROUTE_BLOCK, n))
    w = peer_upass(pairs, gate, xn, u_packed, min(PEER_PASS_BLOCK, n))
    out = peer_vpass(pairs, w, x2d, v_packed, min(PEER_PASS_BLOCK, n))
    return out.reshape(shp)


def _rms_kernel(x_ref, g_ref, o_ref):
    o_ref[...] = _rms(x_ref[...], g_ref[...])


def rmsnorm_pallas(x, g):
    shp = x.shape
    xt = x.reshape(-1, shp[-1])
    n = xt.shape[0]
    tm = min(ROW_BLOCK, n)
    out = pl.pallas_call(
        _rms_kernel,
        out_shape=jax.ShapeDtypeStruct(xt.shape, xt.dtype),
        grid=(n // tm,),
        in_specs=[pl.BlockSpec((tm, shp[-1]), lambda i: (i, 0)),
                  pl.BlockSpec((1, shp[-1]), lambda i: (0, 0))],
        out_specs=pl.BlockSpec((tm, shp[-1]), lambda i: (i, 0)),
        compiler_params=_cparams(), name="final_rmsnorm",
    )(xt, g.reshape(1, -1))
    return out.reshape(shp)


def even_layer(x, pos0, k_prev, v_prev, h0r, h0i, norm_g, w_in, w_out, sink, s5_params, d_skip, w_glu, b_glu):
    b, t, _ = x.shape
    n = b * t
    x2d = x.reshape(n, D_MODEL)
    tabs = rope_tables(pos0 + jnp.arange(t), ROT_DIM, HEAD_DIM, SWA_Q, max(1, min(ROW_BLOCK, n) // t))
    q, k, v, u = in_even(x2d, norm_g, w_in.astype(jnp.bfloat16), tabs)
    q3, k3, v3 = q.reshape(b, t, SWA_Q), k.reshape(b, t, SWA_KV), v.reshape(b, t, SWA_KV)
    if k_prev is None:
        att = swa_attention(q3, k3, k3, v3, v3, sink, True)
        k_all, v_all = k3, v3
        h0r = jnp.zeros((b, S5_FLAT), jnp.float32)
        h0i = jnp.zeros((b, S5_FLAT), jnp.float32)
    else:
        kp, vp = k_prev.reshape(b, WINDOW, SWA_KV), v_prev.reshape(b, WINDOW, SWA_KV)
        att = swa_attention(q3, kp, k3, vp, v3, sink, False)
        k_all, v_all = jnp.concatenate([kp, k3], axis=1), jnp.concatenate([vp, v3], axis=1)
        h0r, h0i = h0r.reshape(b, S5_FLAT), h0i.reshape(b, S5_FLAT)
    lam, wb, wc = s5_discretize(*s5_params)
    u_tm = u.reshape(b, t, S5_WIDTH).transpose(1, 0, 2).reshape(n, S5_WIDTH)
    s5o_tm, hre, him = s5_mixer(u_tm, b, lam, wb, wc, d_skip, w_glu.astype(jnp.bfloat16), b_glu, h0r, h0i,
                                min(t, S5_STEPS))
    s5o = s5o_tm.reshape(t, b, S5_WIDTH).transpose(1, 0, 2).reshape(n, S5_WIDTH)
    out = out_proj(x2d, att.reshape(n, SWA_Q), s5o, w_out.astype(jnp.bfloat16))
    return (out.reshape(b, t, D_MODEL),
            k_all[:, -WINDOW:].reshape(b, WINDOW, SWA_KV_HEADS, HEAD_DIM),
            v_all[:, -WINDOW:].reshape(b, WINDOW, SWA_KV_HEADS, HEAD_DIM),
            hre.reshape(b, S5_GROUPS, S5_STATE), him.reshape(b, S5_GROUPS, S5_STATE))


def odd_layer(x, pos0, pool_prev, ckv_prev, kpe_prev, norm_g, w_in, w_out, pool_w, pool_scale,
              q_norm, kv_norm, w_uq, w_uk, w_uv):
    b, t, _ = x.shape
    n = b * t
    x2d = x.reshape(n, D_MODEL)
    reps = max(1, min(ROW_BLOCK, n) // t)
    pos = pos0 + jnp.arange(t)
    ktabs = rope_tables(pos, ROPE_DIM, KPE_PAD, KPE_PAD, reps)
    qtabs = rope_tables(pos, ROPE_DIM, ROPE_DIM, MLA_QPE, reps)
    w_perm, wuq_perm = permute_odd_weights(w_in, w_uq)
    qnope, qpe, c, kp, u = in_odd(x2d, norm_g, w_perm, q_norm, kv_norm, wuq_perm, ktabs, qtabs)
    c3, kp3 = c.reshape(b, t, KV_LORA), kp.reshape(b, t, ROPE_DIM)
    wuk_h = w_uk.transpose(1, 2, 0).astype(jnp.bfloat16)
    wuv_h = w_uv.transpose(1, 0, 2).astype(jnp.bfloat16)
    if ckv_prev is None:
        ck, kk, causal, n_keys = c3, kp3, True, t
        pool_prev = jnp.zeros((b, POOL_BUF, POOL_WIDTH), jnp.float32)
    else:
        ck, kk, causal = jnp.concatenate([ckv_prev, c3], axis=1), jnp.concatenate([kpe_prev, kp3], axis=1), False
        n_keys = ck.shape[1]
        pad = -n_keys % min(MLA_KBLOCK, n_keys)
        ck, kk = jnp.pad(ck, ((0, 0), (0, pad), (0, 0))), jnp.pad(kk, ((0, 0), (0, pad), (0, 0)))
    mla = mla_attention(qnope.reshape(b, t, MLA_QNOPE), qpe.reshape(b, t, MLA_QPE), ck.astype(jnp.bfloat16),
                        kk.astype(jnp.bfloat16), wuk_h, wuv_h, causal, n_keys)
    pool_out, pool_new = pool_mixer(u.reshape(b, t, POOL_WIDTH), pool_prev, pool_w.astype(jnp.bfloat16), pool_scale, pos0)
    out = out_proj(x2d, pool_out.reshape(n, POOL_WIDTH), mla.reshape(n, MLA_HEADS * V_DIM), w_out.astype(jnp.bfloat16))
    return out.reshape(b, t, D_MODEL), pool_new, c3, kp3


def kernel(x_prompt, x_sample, cache_swa_k, cache_swa_v, state_ssm_re, state_ssm_im, state_pool,
           cache_mla_ckv, cache_mla_kpe, norm_mix, norm_ffn, norm_final, w_in_even, w_out_even,
           swa_sink, s5_lam_re, s5_lam_im, s5_log_dt, s5_b_re, s5_b_im, s5_c_re, s5_c_im, s5_d,
           s5_w_glu, s5_b_glu, w_in_odd, w_out_odd, pool_w, pool_scale, mla_q_norm, mla_kv_norm,
           mla_w_uq, mla_w_uk, mla_w_uv, peer_w_q, peer_keys, peer_u, peer_v):
    xp, xs = x_prompt, x_sample
    kp_l, vp_l, rp_l, ip_l, poolp_l, cp_l, ep_l = [], [], [], [], [], [], []
    ks_l, vs_l, rs_l, is_l, pools_l, cs_l, es_l = [], [], [], [], [], [], []
    for layer in range(DEPTH):
        i = layer // 2
        if layer % 2 == 0:
            s5_params = (s5_lam_re[i], s5_lam_im[i], s5_log_dt[i], s5_b_re[i], s5_b_im[i], s5_c_re[i], s5_c_im[i])
            ew = (norm_mix[layer], w_in_even[i], w_out_even[i], swa_sink[i], s5_params, s5_d[i], s5_w_glu[i], s5_b_glu[i])
            xp, k1, v1, r1, i1 = even_layer(xp, 0, None, None, None, None, *ew)
            xs, k2, v2, r2, i2 = even_layer(xs, PAST_LEN, cache_swa_k[i], cache_swa_v[i],
                                            state_ssm_re[i], state_ssm_im[i], *ew)
            kp_l.append(k1); vp_l.append(v1); rp_l.append(r1); ip_l.append(i1)
            ks_l.append(k2); vs_l.append(v2); rs_l.append(r2); is_l.append(i2)
        else:
            ow = (norm_mix[layer], w_in_odd[i], w_out_odd[i], pool_w[i], pool_scale[i], mla_q_norm[i], mla_kv_norm[i],
                  mla_w_uq[i], mla_w_uk[i], mla_w_uv[i])
            xp, p1, c1, e1 = odd_layer(xp, 0, None, None, None, *ow)
            xs, p2, c2, e2 = odd_layer(xs, PAST_LEN, state_pool[i], cache_mla_ckv[i], cache_mla_kpe[i], *ow)
            poolp_l.append(p1); cp_l.append(c1); ep_l.append(e1)
            pools_l.append(p2); cs_l.append(c2); es_l.append(e2)
        wq_heads = peer_w_q[layer].reshape(D_MODEL, PEER_HEADS, D_KEY).transpose(1, 0, 2).astype(jnp.bfloat16)
        keys_bf16 = peer_keys[layer].reshape(PEER_HEADS * 2, N_KEYS, D_HALF).astype(jnp.bfloat16)
        u_packed, v_packed = pack_table(peer_u[layer]), pack_table(peer_v[layer])
        xp = peer_block(xp, norm_ffn[layer], wq_heads, keys_bf16, u_packed, v_packed)
        xs = peer_block(xs, norm_ffn[layer], wq_heads, keys_bf16, u_packed, v_packed)
    y_prompt = rmsnorm_pallas(xp, norm_final)
    y_sample = rmsnorm_pallas(xs, norm_final)
    return (y_prompt, y_sample,
            jnp.stack(kp_l), jnp.stack(vp_l), jnp.stack(rp_l), jnp.stack(ip_l),
            jnp.stack(poolp_l), jnp.stack(cp_l), jnp.stack(ep_l),
            jnp.stack(ks_l), jnp.stack(vs_l), jnp.stack(rs_l), jnp.stack(is_l),
            jnp.stack(pools_l), jnp.stack(cs_l), jnp.stack(es_l))
```

```python
import functools
import math
import jax
import jax.numpy as jnp
from jax import lax
from jax.experimental import pallas as pl
from jax.experimental.pallas import tpu as pltpu

D_MODEL = 1024
DEPTH = 2
PAST_LEN = 2048

CHUNK = 64
RMS_EPS = 1e-6
ROPE_THETA = 500000.0
NEG_INF = -1e30

SWA_HEADS = 8
SWA_KV_HEADS = 2
SWA_GROUP = SWA_HEADS // SWA_KV_HEADS
HEAD_DIM = 64
ROT_DIM = HEAD_DIM // 4
WINDOW = 128
SWA_Q = SWA_HEADS * HEAD_DIM
SWA_KV = SWA_KV_HEADS * HEAD_DIM
SWA_SCALE = HEAD_DIM ** -0.5

S5_WIDTH = 512
S5_GROUP = 16
S5_GROUPS = S5_WIDTH // S5_GROUP
S5_STATE = 64
S5_FLAT = S5_GROUPS * S5_STATE

POOL_WIDTH = 512
POOL_WINDOWS = (2, 4, 8, 16)
POOL_GROUP = POOL_WIDTH // len(POOL_WINDOWS)
POOL_MAX = 16
POOL_BUF = POOL_MAX - 1

MLA_HEADS = 8
Q_LORA = 512
KV_LORA = 256
NOPE_DIM = 64
ROPE_DIM = 32
V_DIM = 64
MLA_SCALE = (NOPE_DIM + ROPE_DIM) ** -0.5
MLA_QNOPE = MLA_HEADS * NOPE_DIM
MLA_QPE = MLA_HEADS * ROPE_DIM

EVEN_IN = SWA_Q + 2 * SWA_KV + S5_WIDTH

PEER_HEADS = 8
N_KEYS = 128
N_EXPERTS = N_KEYS * N_KEYS
D_KEY = 128
D_HALF = D_KEY // 2
PEER_TOPK = 16
NSEL = PEER_HEADS * PEER_TOPK
HALF_EXPERTS = N_EXPERTS // 2

SUBLANES = 8
LANES = 128
VMEM_LIMIT = 56 * 1024 * 1024
ROW_TILE = D_MODEL // LANES

ROW_BLOCK = 512
SWA_QBLOCK = 256
MLA_QBLOCK = 256
MLA_KBLOCK = 512
MLA_CHAIN_ROWS = 256
KPE_PAD = LANES
S5_STEPS = 128
S5_BLOCK_GROUPS = 8
S5_CARRY_VREGS = 16
PEER_ROUTE_BLOCK = 256
ROUTE_HEADS_PER_STEP = 4
PEER_PASS_BLOCK = 512
PACK_BLOCK = 256
UPASS_GROUP = 32
SLOT_ROWS = NSEL * ROW_TILE
HI16 = -65536
PAIR_COLS = tuple(PEER_TOPK // (a + 1) for a in range(PEER_TOPK))
PAIR_ROWS = -(-sum(PAIR_COLS) // SUBLANES) * SUBLANES
PAIR_PAD = PAIR_ROWS - sum(PAIR_COLS)

ODD_U0 = Q_LORA + KV_LORA
ODD_KPE0 = ODD_U0 + POOL_WIDTH
ODD_IN_PAD = ODD_KPE0 + KPE_PAD


def _cparams(n_axes=1):
    return pltpu.CompilerParams(dimension_semantics=("arbitrary",) * n_axes, vmem_limit_bytes=VMEM_LIMIT)


def _rms(x, g):
    return x * lax.rsqrt(jnp.mean(x * x, axis=-1, keepdims=True) + RMS_EPS) * g


def _rope_lanes(x, cos, sin_lo, sin_hi, half):
    n = x.shape[-1]
    return x * cos + pltpu.roll(x, n - half, 1) * sin_lo + pltpu.roll(x, half, 1) * sin_hi


def rope_tables(pos, rot, period, width, reps):
    inv = ROPE_THETA ** (-jnp.arange(0, rot, 2, dtype=jnp.float32) / rot)
    ang = pos.astype(jnp.float32)[:, None] * inv[None, :]
    cos, sin = jnp.cos(ang), jnp.sin(ang)
    lane = jnp.arange(width) % period
    idx = lane % (rot // 2)
    in_lo = lane < rot // 2
    in_hi = (lane >= rot // 2) & (lane < rot)
    c = jnp.where((in_lo | in_hi)[None, :], cos[:, idx], 1.0)
    s_lo = jnp.where(in_lo[None, :], -sin[:, idx], 0.0)
    s_hi = jnp.where(in_hi[None, :], sin[:, idx], 0.0)
    return tuple(jnp.tile(t, (reps, 1)) for t in (c, s_lo, s_hi))


def _in_even_kernel(x_ref, g_ref, w_ref, cos_ref, slo_ref, shi_ref, q_ref, k_ref, v_ref, u_ref):
    xn = _rms(x_ref[...], g_ref[...])
    proj = jnp.dot(xn.astype(jnp.bfloat16), w_ref[...], preferred_element_type=jnp.float32)
    cos, slo, shi = cos_ref[...], slo_ref[...], shi_ref[...]
    q_ref[...] = _rope_lanes(proj[:, :SWA_Q], cos, slo, shi, ROT_DIM // 2)
    k_ref[...] = _rope_lanes(proj[:, SWA_Q:SWA_Q + SWA_KV], cos[:, :SWA_KV], slo[:, :SWA_KV], shi[:, :SWA_KV],
                             ROT_DIM // 2)
    v_ref[...] = proj[:, SWA_Q + SWA_KV:SWA_Q + 2 * SWA_KV]
    u_ref[...] = proj[:, SWA_Q + 2 * SWA_KV:]


def in_even(x2d, g, w_bf16, tabs):
    n = x2d.shape[0]
    tm = min(ROW_BLOCK, n)
    nt = tabs[0].shape[0] // tm
    row = lambda i: (i, 0)
    const = lambda i: (0, 0)
    tab = lambda i: (i % nt, 0)
    return pl.pallas_call(
        _in_even_kernel,
        out_shape=(jax.ShapeDtypeStruct((n, SWA_Q), jnp.float32), jax.ShapeDtypeStruct((n, SWA_KV), jnp.float32),
                   jax.ShapeDtypeStruct((n, SWA_KV), jnp.float32), jax.ShapeDtypeStruct((n, S5_WIDTH), jnp.float32)),
        grid=(n // tm,),
        in_specs=[pl.BlockSpec((tm, D_MODEL), row), pl.BlockSpec((1, D_MODEL), const),
                  pl.BlockSpec((D_MODEL, EVEN_IN), const),
                  pl.BlockSpec((tm, SWA_Q), tab), pl.BlockSpec((tm, SWA_Q), tab), pl.BlockSpec((tm, SWA_Q), tab)],
        out_specs=(pl.BlockSpec((tm, SWA_Q), row), pl.BlockSpec((tm, SWA_KV), row),
                   pl.BlockSpec((tm, SWA_KV), row), pl.BlockSpec((tm, S5_WIDTH), row)),
        compiler_params=_cparams(), name="in_even",
    )(x2d, g.reshape(1, D_MODEL), w_bf16, *tabs)


def _swa_kernel(sink_ref, q_ref, kp_ref, kc_ref, vp_ref, vc_ref, o_ref, *, banded, qb):
    i = pl.program_id(1)
    q = q_ref[0]
    k = jnp.concatenate([kp_ref[0], kc_ref[0]], axis=0).astype(jnp.bfloat16)
    v = jnp.concatenate([vp_ref[0], vc_ref[0]], axis=0).astype(jnp.bfloat16)
    nk = WINDOW + qb
    wc = WINDOW // CHUNK
    if banded:
        qc = lax.broadcasted_iota(jnp.int32, (qb, nk), 0) // CHUNK + wc
        kc = lax.broadcasted_iota(jnp.int32, (qb, nk), 1) // CHUNK
        visible = (kc <= qc) & (kc >= qc - wc) & ((kc >= wc) | (i > 0))
    ss = []
    for h in range(SWA_HEADS):
        hk = h // SWA_GROUP
        qh = q[:, h * HEAD_DIM:(h + 1) * HEAD_DIM].astype(jnp.bfloat16)
        ss.append(lax.dot_general(qh, k[:, hk * HEAD_DIM:(hk + 1) * HEAD_DIM], (((1,), (1,)), ((), ())),
                                  preferred_element_type=jnp.float32) * SWA_SCALE)
    ps, dens = [], []
    for h in range(SWA_HEADS):
        s = jnp.where(visible, ss[h], NEG_INF) if banded else ss[h]
        sk = sink_ref[h]
        m = jnp.maximum(jnp.max(s, axis=-1, keepdims=True), sk)
        p = jnp.exp(s - m)
        dens.append(jnp.sum(p, axis=-1, keepdims=True) + jnp.exp(sk - m))
        ps.append(p.astype(jnp.bfloat16))
    outs = [jnp.dot(ps[h], v[:, (h // SWA_GROUP) * HEAD_DIM:(h // SWA_GROUP + 1) * HEAD_DIM],
                    preferred_element_type=jnp.float32) / dens[h] for h in range(SWA_HEADS)]
    o_ref[0] = jnp.concatenate(outs, axis=-1)


def swa_attention(q, k_prev, k_cur, v_prev, v_cur, sink, banded):
    b, t, _ = q.shape
    qb = min(SWA_QBLOCK, t)
    per = qb // WINDOW
    prev_map = (lambda bi, i: (bi, jnp.maximum(i * per - 1, 0), 0)) if banded else (lambda bi, i: (bi, 0, 0))
    cur = lambda bi, i: (bi, i, 0)
    return pl.pallas_call(
        functools.partial(_swa_kernel, banded=banded, qb=qb),
        out_shape=jax.ShapeDtypeStruct((b, t, SWA_Q), jnp.float32),
        grid=(b, t // qb),
        in_specs=[pl.BlockSpec(memory_space=pltpu.SMEM),
                  pl.BlockSpec((1, qb, SWA_Q), cur),
                  pl.BlockSpec((1, WINDOW, SWA_KV), prev_map), pl.BlockSpec((1, qb, SWA_KV), cur),
                  pl.BlockSpec((1, WINDOW, SWA_KV), prev_map), pl.BlockSpec((1, qb, SWA_KV), cur)],
        out_specs=pl.BlockSpec((1, qb, SWA_Q), cur),
        compiler_params=_cparams(2), name="swa_attention",
    )(sink, q, k_prev, k_cur, v_prev, v_cur)


def _out_kernel(x_ref, a_ref, b_ref, w_ref, o_ref):
    ka = a_ref.shape[-1]
    o_ref[...] = (x_ref[...]
                  + jnp.dot(a_ref[...].astype(jnp.bfloat16), w_ref[:ka, :], preferred_element_type=jnp.float32)
                  + jnp.dot(b_ref[...].astype(jnp.bfloat16), w_ref[ka:, :], preferred_element_type=jnp.float32))


def out_proj(x2d, a, b, w_bf16):
    n = x2d.shape[0]
    tm = min(ROW_BLOCK, n)
    row = lambda i: (i, 0)
    return pl.pallas_call(
        _out_kernel,
        out_shape=jax.ShapeDtypeStruct((n, D_MODEL), jnp.float32),
        grid=(n // tm,),
        in_specs=[pl.BlockSpec((tm, D_MODEL), row), pl.BlockSpec((tm, a.shape[1]), row),
                  pl.BlockSpec((tm, b.shape[1]), row), pl.BlockSpec(w_bf16.shape, lambda i: (0, 0))],
        out_specs=pl.BlockSpec((tm, D_MODEL), row),
        compiler_params=_cparams(), name="out_proj",
    )(x2d, a, b, w_bf16)


def s5_discretize(lam_re, lam_im, log_dt, b_re, b_im, c_re, c_im):
    lr = jnp.minimum(lam_re, -1e-4)
    li = lam_im
    dt = jnp.exp(log_dt)[:, None]
    mag = jnp.exp(lr * dt)
    ang = li * dt
    ab_re, ab_im = mag * jnp.cos(ang), mag * jnp.sin(ang)
    den = lr * lr + li * li
    nr, ni = ab_re - 1.0, ab_im
    f_re = (nr * lr + ni * li) / den
    f_im = (ni * lr - nr * li) / den
    bb_re = f_re[..., None] * b_re - f_im[..., None] * b_im
    bb_im = f_re[..., None] * b_im + f_im[..., None] * b_re
    eye = jnp.eye(S5_GROUPS, dtype=jnp.float32)

    def embed_b(bb):
        return jnp.einsum('gnc,gh->gchn', bb, eye).reshape(S5_WIDTH, S5_FLAT)

    def embed_c(c):
        return jnp.einsum('gcn,gh->gnhc', c, eye).reshape(S5_FLAT, S5_WIDTH)

    wb = jnp.concatenate([embed_b(bb_re), embed_b(bb_im)], axis=1)
    wc = jnp.concatenate([embed_c(c_re), -embed_c(c_im)], axis=0)
    lam = jnp.stack([ab_re.reshape(S5_FLAT), ab_im.reshape(S5_FLAT)])
    return lam, wb.astype(jnp.bfloat16), wc.astype(jnp.bfloat16)


def _s5_kernel(u_ref, lam_ref, wb_ref, wc_ref, d_ref, wglu_ref, bglu_ref, h0r_ref, h0i_ref,
               o_ref, hr_ref, hi_ref, hbuf, *, nb, steps, width):
    @pl.when(pl.program_id(0) == 0)
    def _():
        hr_ref[...] = h0r_ref[...]
        hi_ref[...] = h0i_ref[...]

    u = u_ref[...]
    ub = u.astype(jnp.bfloat16)
    n_blocks = S5_GROUPS // S5_BLOCK_GROUPS
    uw, sw = S5_BLOCK_GROUPS * S5_GROUP, S5_BLOCK_GROUPS * S5_STATE
    for gb in range(n_blocks):
        ucols = slice(gb * uw, (gb + 1) * uw)
        for part in range(2):
            scols = slice(part * S5_FLAT + gb * sw, part * S5_FLAT + (gb + 1) * sw)
            hbuf[:, scols] = jnp.dot(ub[:, ucols], wb_ref[ucols, scols], preferred_element_type=jnp.float32)
    for c0 in range(0, S5_FLAT, width):
        cre = slice(c0, c0 + width)
        cim = slice(S5_FLAT + c0, S5_FLAT + c0 + width)
        lr = jnp.broadcast_to(lam_ref[0:1, cre], (nb, width))
        li = jnp.broadcast_to(lam_ref[1:2, cre], (nb, width))

        def step(t, carry):
            hr, hi = carry
            rows = pl.ds(pl.multiple_of(t * nb, nb), nb)
            nhr = lr * hr - li * hi + hbuf[rows, cre]
            nhi = lr * hi + li * hr + hbuf[rows, cim]
            hbuf[rows, cre] = nhr
            hbuf[rows, cim] = nhi
            return nhr, nhi

        hr, hi = lax.fori_loop(0, steps, step, (hr_ref[:, cre], hi_ref[:, cre]))
        hr_ref[:, cre] = hr
        hi_ref[:, cre] = hi
    ys = []
    for gb in range(n_blocks):
        ucols = slice(gb * uw, (gb + 1) * uw)
        re = slice(gb * sw, (gb + 1) * sw)
        im = slice(S5_FLAT + gb * sw, S5_FLAT + (gb + 1) * sw)
        ys.append(jnp.dot(hbuf[:, re].astype(jnp.bfloat16), wc_ref[re, ucols], preferred_element_type=jnp.float32)
                  + jnp.dot(hbuf[:, im].astype(jnp.bfloat16), wc_ref[im, ucols], preferred_element_type=jnp.float32))
    y = jnp.concatenate(ys, axis=-1) + d_ref[...] * u
    z = 0.5 * y * (1.0 + jnp.tanh(math.sqrt(2.0 / math.pi) * (y + 0.044715 * (y * y * y))))
    gate = jnp.dot(z.astype(jnp.bfloat16), wglu_ref[...], preferred_element_type=jnp.float32) + bglu_ref[...]
    o_ref[...] = z * (1.0 / (1.0 + jnp.exp(-gate)))


def s5_mixer(u_tm, nb, lam, wb, wc, d_skip, w_glu_bf16, b_glu, h0r, h0i, steps):
    rows = u_tm.shape[0]
    t_total = rows // nb
    width = min(S5_FLAT, max(LANES, (S5_CARRY_VREGS * SUBLANES * LANES) // (2 * nb)))
    blk = steps * nb
    const = lambda i: (0, 0)
    return pl.pallas_call(
        functools.partial(_s5_kernel, nb=nb, steps=steps, width=width),
        out_shape=(jax.ShapeDtypeStruct((rows, S5_WIDTH), jnp.float32),
                   jax.ShapeDtypeStruct((nb, S5_FLAT), jnp.float32),
                   jax.ShapeDtypeStruct((nb, S5_FLAT), jnp.float32)),
        grid=(t_total // steps,),
        in_specs=[pl.BlockSpec((blk, S5_WIDTH), lambda i: (i, 0)),
                  pl.BlockSpec((2, S5_FLAT), const),
                  pl.BlockSpec((S5_WIDTH, 2 * S5_FLAT), const),
                  pl.BlockSpec((2 * S5_FLAT, S5_WIDTH), const),
                  pl.BlockSpec((1, S5_WIDTH), const),
                  pl.BlockSpec((S5_WIDTH, S5_WIDTH), const),
                  pl.BlockSpec((1, S5_WIDTH), const),
                  pl.BlockSpec((nb, S5_FLAT), const),
                  pl.BlockSpec((nb, S5_FLAT), const)],
        out_specs=(pl.BlockSpec((blk, S5_WIDTH), lambda i: (i, 0)),
                   pl.BlockSpec((nb, S5_FLAT), const),
                   pl.BlockSpec((nb, S5_FLAT), const)),
        scratch_shapes=[pltpu.VMEM((blk, 2 * S5_FLAT), jnp.float32)],
        compiler_params=_cparams(), name="s5_mixer",
    )(u_tm, lam, wb, wc, d_skip.reshape(1, S5_WIDTH), w_glu_bf16, b_glu.reshape(1, S5_WIDTH), h0r, h0i)


def _in_odd_kernel(x_ref, g_ref, w_ref, qn_ref, kvn_ref, wuq_ref, kcos_ref, kslo_ref, kshi_ref,
                   qcos_ref, qslo_ref, qshi_ref, qnope_ref, qpe_ref, c_ref, kp_ref, u_ref):
    xn = _rms(x_ref[...], g_ref[...])
    proj = jnp.dot(xn.astype(jnp.bfloat16), w_ref[...], preferred_element_type=jnp.float32)
    cqn = _rms(proj[:, :Q_LORA], qn_ref[...])
    q = jnp.dot(cqn.astype(jnp.bfloat16), wuq_ref[...], preferred_element_type=jnp.float32)
    qnope_ref[...] = q[:, :MLA_QNOPE]
    qpe_ref[...] = _rope_lanes(q[:, MLA_QNOPE:], qcos_ref[...], qslo_ref[...], qshi_ref[...], ROPE_DIM // 2)
    c_ref[...] = _rms(proj[:, Q_LORA:ODD_U0], kvn_ref[...])
    kp = _rope_lanes(proj[:, ODD_KPE0:], kcos_ref[...], kslo_ref[...], kshi_ref[...], ROPE_DIM // 2)
    kp_ref[...] = kp[:, :ROPE_DIM]
    u_ref[...] = proj[:, ODD_U0:ODD_KPE0]


def in_odd(x2d, g, w_perm_bf16, q_norm, kv_norm, wuq_perm_bf16, ktabs, qtabs):
    n = x2d.shape[0]
    tm = min(ROW_BLOCK, n)
    nt = ktabs[0].shape[0] // tm
    row = lambda i: (i, 0)
    const = lambda i: (0, 0)
    tab = lambda i: (i % nt, 0)
    return pl.pallas_call(
        _in_odd_kernel,
        out_shape=(jax.ShapeDtypeStruct((n, MLA_QNOPE), jnp.float32), jax.ShapeDtypeStruct((n, MLA_QPE), jnp.float32),
                   jax.ShapeDtypeStruct((n, KV_LORA), jnp.float32), jax.ShapeDtypeStruct((n, ROPE_DIM), jnp.float32),
                   jax.ShapeDtypeStruct((n, POOL_WIDTH), jnp.float32)),
        grid=(n // tm,),
        in_specs=[pl.BlockSpec((tm, D_MODEL), row), pl.BlockSpec((1, D_MODEL), const),
                  pl.BlockSpec((D_MODEL, ODD_IN_PAD), const),
                  pl.BlockSpec((1, Q_LORA), const), pl.BlockSpec((1, KV_LORA), const),
                  pl.BlockSpec((Q_LORA, MLA_QNOPE + MLA_QPE), const),
                  pl.BlockSpec((tm, KPE_PAD), tab), pl.BlockSpec((tm, KPE_PAD), tab), pl.BlockSpec((tm, KPE_PAD), tab),
                  pl.BlockSpec((tm, MLA_QPE), tab), pl.BlockSpec((tm, MLA_QPE), tab), pl.BlockSpec((tm, MLA_QPE), tab)],
        out_specs=(pl.BlockSpec((tm, MLA_QNOPE), row), pl.BlockSpec((tm, MLA_QPE), row), pl.BlockSpec((tm, KV_LORA), row),
                   pl.BlockSpec((tm, ROPE_DIM), row), pl.BlockSpec((tm, POOL_WIDTH), row)),
        compiler_params=_cparams(), name="in_odd",
    )(x2d, g.reshape(1, D_MODEL), w_perm_bf16, q_norm.reshape(1, Q_LORA), kv_norm.reshape(1, KV_LORA), wuq_perm_bf16,
      *ktabs, *qtabs)


def permute_odd_weights(w_in, w_uq):
    o1, o2 = Q_LORA + KV_LORA, Q_LORA + KV_LORA + ROPE_DIM
    kpe = jnp.pad(w_in[:, o1:o2], ((0, 0), (0, KPE_PAD - ROPE_DIM)))
    w_perm = jnp.concatenate([w_in[:, :o1], w_in[:, o2:], kpe], axis=1)
    wq = w_uq.reshape(Q_LORA, MLA_HEADS, NOPE_DIM + ROPE_DIM)
    wq_perm = jnp.concatenate([wq[:, :, :NOPE_DIM].reshape(Q_LORA, MLA_QNOPE),
                               wq[:, :, NOPE_DIM:].reshape(Q_LORA, MLA_QPE)], axis=1)
    return w_perm.astype(jnp.bfloat16), wq_perm.astype(jnp.bfloat16)


def _mla_kernel(qn_ref, qp_ref, c_ref, kp_ref, wuk_ref, wuv_ref, o_ref, *scratch, causal, qb, kb, n_keys, hpc):
    i = pl.program_id(1)
    n_chains = MLA_HEADS // hpc
    qa_s, qp_s, m_s, l_s, acc_s = (scratch[k * n_chains:(k + 1) * n_chains] for k in range(5))
    qn = qn_ref[0].astype(jnp.bfloat16)
    qp = qp_ref[0].astype(jnp.bfloat16)
    for g in range(n_chains):
        for hh in range(hpc):
            h = g * hpc + hh
            rows = slice(hh * qb, (hh + 1) * qb)
            qa_s[g][rows, :] = jnp.dot(qn[:, h * NOPE_DIM:(h + 1) * NOPE_DIM], wuk_ref[h],
                                       preferred_element_type=jnp.float32).astype(jnp.bfloat16)
            qp_s[g][rows, :] = qp[:, h * ROPE_DIM:(h + 1) * ROPE_DIM]
        m_s[g][...] = jnp.full(m_s[g].shape, NEG_INF, jnp.float32)
        l_s[g][...] = jnp.zeros(l_s[g].shape, jnp.float32)
        acc_s[g][...] = jnp.zeros(acc_s[g].shape, jnp.float32)
    if causal:
        qpos = i * qb + lax.broadcasted_iota(jnp.int32, (hpc * qb, kb), 0) % qb
        limit = (qpos // CHUNK + 1) * CHUNK
        nblk = ((i + 1) * qb + kb - 1) // kb
    else:
        limit = n_keys
        nblk = (n_keys + kb - 1) // kb
    kidx0 = lax.broadcasted_iota(jnp.int32, (hpc * qb, kb), 1)

    def body(j, carry):
        rows = pl.ds(pl.multiple_of(j * kb, kb), kb)
        cb = c_ref[0, rows, :]
        kpb = kp_ref[0, rows, :]
        visible = kidx0 + j * kb < limit
        ss = [(lax.dot_general(qa_s[g][...], cb, (((1,), (1,)), ((), ())), preferred_element_type=jnp.float32)
               + lax.dot_general(qp_s[g][...], kpb, (((1,), (1,)), ((), ())),
                                 preferred_element_type=jnp.float32)) * MLA_SCALE for g in range(n_chains)]
        ps, alphas = [], []
        for g in range(n_chains):
            s = jnp.where(visible, ss[g], NEG_INF)
            m_old = m_s[g][...]
            m_new = jnp.maximum(m_old, jnp.max(s, axis=-1, keepdims=True))
            alpha = jnp.exp(m_old - m_new)
            p = jnp.exp(s - m_new)
            m_s[g][...] = m_new
            l_s[g][...] = alpha * l_s[g][...] + jnp.sum(p, axis=-1, keepdims=True)
            ps.append(p.astype(jnp.bfloat16))
            alphas.append(alpha)
        for g in range(n_chains):
            acc_s[g][...] = alphas[g] * acc_s[g][...] + jnp.dot(ps[g], cb, preferred_element_type=jnp.float32)
        return carry

    lax.fori_loop(0, nblk, body, 0)
    outs = []
    for g in range(n_chains):
        o_lat = (acc_s[g][...] / l_s[g][...]).astype(jnp.bfloat16)
        for hh in range(hpc):
            outs.append(jnp.dot(o_lat[hh * qb:(hh + 1) * qb, :], wuv_ref[g * hpc + hh],
                                preferred_element_type=jnp.float32))
    o_ref[0] = jnp.concatenate(outs, axis=-1)


def mla_attention(q_nope, q_pe, c_keys_bf16, kp_keys_bf16, wuk_h, wuv_h, causal, n_keys):
    b, t, _ = q_nope.shape
    tk = c_keys_bf16.shape[1]
    qb = min(MLA_QBLOCK, t)
    kb = min(MLA_KBLOCK, tk)
    hpc = max(1, min(MLA_HEADS, MLA_CHAIN_ROWS // qb))
    n_chains, rows = MLA_HEADS // hpc, hpc * qb
    cur = lambda bi, i: (bi, i, 0)
    whole = lambda bi, i: (bi, 0, 0)
    const3 = lambda bi, i: (0, 0, 0)
    return pl.pallas_call(
        functools.partial(_mla_kernel, causal=causal, qb=qb, kb=kb, n_keys=n_keys, hpc=hpc),
        out_shape=jax.ShapeDtypeStruct((b, t, MLA_HEADS * V_DIM), jnp.float32),
        grid=(b, t // qb),
        in_specs=[pl.BlockSpec((1, qb, MLA_QNOPE), cur), pl.BlockSpec((1, qb, MLA_QPE), cur),
                  pl.BlockSpec((1, tk, KV_LORA), whole), pl.BlockSpec((1, tk, ROPE_DIM), whole),
                  pl.BlockSpec((MLA_HEADS, NOPE_DIM, KV_LORA), const3), pl.BlockSpec((MLA_HEADS, KV_LORA, V_DIM), const3)],
        out_specs=pl.BlockSpec((1, qb, MLA_HEADS * V_DIM), cur),
        scratch_shapes=([pltpu.VMEM((rows, KV_LORA), jnp.bfloat16)] * n_chains
                        + [pltpu.VMEM((rows, ROPE_DIM), jnp.bfloat16)] * n_chains
                        + [pltpu.VMEM((rows, 1), jnp.float32)] * (2 * n_chains)
                        + [pltpu.VMEM((rows, KV_LORA), jnp.float32)] * n_chains),
        compiler_params=_cparams(2), name="mla_attention",
    )(q_nope, q_pe, c_keys_bf16, kp_keys_bf16, wuk_h, wuv_h)


def _pool_kernel(u_ref, prev_ref, w_ref, scale_ref, o_ref, new_ref, ext, *, tm, pos0):
    j = pl.program_id(1)

    @pl.when(j == 0)
    def _():
        ext[0:1, :] = jnp.zeros((1, POOL_WIDTH), jnp.float32)
        ext[1:POOL_MAX, :] = prev_ref[0]

    @pl.when(j > 0)
    def _():
        ext[0:POOL_MAX, :] = ext[tm:tm + POOL_MAX, :]

    u = u_ref[0]
    ext[POOL_MAX:POOL_MAX + tm, :] = u
    pos = pos0 + j * tm + lax.broadcasted_iota(jnp.int32, (tm, POOL_GROUP), 0)
    outs = []
    for gi, w in enumerate(POOL_WINDOWS):
        cols = slice(gi * POOL_GROUP, (gi + 1) * POOL_GROUP)
        tot = u[:, cols]
        for d in range(1, w):
            tot = tot + ext[POOL_MAX - d:POOL_MAX - d + tm, cols]
        cnt = jnp.minimum(pos + 1, w).astype(jnp.float32)
        m = tot / cnt - u[:, cols]
        outs.append(jnp.dot(m.astype(jnp.bfloat16), w_ref[gi], preferred_element_type=jnp.float32))
    o_ref[0] = jnp.concatenate(outs, axis=-1) * scale_ref[...]
    new_ref[0] = ext[tm + 1:tm + POOL_MAX, :]


def pool_mixer(u, prev, pool_w_bf16, pool_scale, pos0):
    b, t, _ = u.shape
    tm = min(ROW_BLOCK, t)
    cur = lambda bi, j: (bi, j, 0)
    per_b = lambda bi, j: (bi, 0, 0)
    return pl.pallas_call(
        functools.partial(_pool_kernel, tm=tm, pos0=pos0),
        out_shape=(jax.ShapeDtypeStruct((b, t, POOL_WIDTH), jnp.float32),
                   jax.ShapeDtypeStruct((b, POOL_BUF, POOL_WIDTH), jnp.float32)),
        grid=(b, t // tm),
        in_specs=[pl.BlockSpec((1, tm, POOL_WIDTH), cur), pl.BlockSpec((1, POOL_BUF, POOL_WIDTH), per_b),
                  pl.BlockSpec((len(POOL_WINDOWS), POOL_GROUP, POOL_GROUP), lambda bi, j: (0, 0, 0)),
                  pl.BlockSpec((1, POOL_WIDTH), lambda bi, j: (0, 0))],
        out_specs=(pl.BlockSpec((1, tm, POOL_WIDTH), cur), pl.BlockSpec((1, POOL_BUF, POOL_WIDTH), per_b)),
        scratch_shapes=[pltpu.VMEM((tm + POOL_MAX, POOL_WIDTH), jnp.float32)],
        compiler_params=_cparams(2), name="pool_mixer",
    )(u, prev, pool_w_bf16, pool_scale.reshape(1, POOL_WIDTH))


def _top16_rows(s, n_rows):
    iota = lax.broadcasted_iota(jnp.int32, s.shape, 0).astype(jnp.float32)
    vals, idxs = [], []
    for _ in range(PEER_TOPK):
        m = jnp.max(s, axis=0, keepdims=True)
        idx = jnp.min(jnp.where(s == m, iota, float(n_rows)), axis=0, keepdims=True)
        vals.append(m)
        idxs.append(idx)
        s = jnp.where(iota == idx, -jnp.inf, s)
    return jnp.concatenate(vals, axis=0), jnp.concatenate(idxs, axis=0)


def _route_kernel(x_ref, g_ref, wq_ref, keys_ref, xn_ref, pair_ref, gate_ref, code_t, gate_t):
    xn = _rms(x_ref[...], g_ref[...])
    xn_ref[...] = xn
    xb = xn.astype(jnp.bfloat16)

    def head_step(hs, carry):
        for hh in range(ROUTE_HEADS_PER_STEP):
            one_head(hs * ROUTE_HEADS_PER_STEP + hh)
        return carry

    def one_head(h):
        qb = jnp.dot(xb, wq_ref[h], preferred_element_type=jnp.float32).astype(jnp.bfloat16)
        sv, si = [], []
        for p in range(2):
            s = lax.dot_general(keys_ref[h * 2 + p], qb[:, p * D_HALF:(p + 1) * D_HALF],
                                (((1,), (1,)), ((), ())), preferred_element_type=jnp.float32)
            v, i = _top16_rows(s, N_KEYS)
            sv.append(v)
            si.append(i)
        cand = jnp.concatenate([sv[0][a:a + 1] + sv[1][:nb] for a, nb in enumerate(PAIR_COLS)]
                               + [jnp.full((PAIR_PAD, s.shape[1]), -jnp.inf, jnp.float32)], axis=0)
        eid = jnp.concatenate([si[0][a:a + 1] * float(N_KEYS) + si[1][:nb] for a, nb in enumerate(PAIR_COLS)]
                              + [jnp.zeros((PAIR_PAD, s.shape[1]), jnp.float32)], axis=0)
        iota = lax.broadcasted_iota(jnp.int32, cand.shape, 0).astype(jnp.float32)
        cv, ce = [], []
        for _ in range(PEER_TOPK):
            m = jnp.max(cand, axis=0, keepdims=True)
            idx = jnp.min(jnp.where(cand == m, iota, float(PAIR_ROWS)), axis=0, keepdims=True)
            hit = iota == idx
            cv.append(m)
            ce.append(jnp.max(jnp.where(hit, eid, -1.0), axis=0, keepdims=True))
            cand = jnp.where(hit, -jnp.inf, cand)
        cv = jnp.concatenate(cv, axis=0)
        ce = jnp.concatenate(ce, axis=0).astype(jnp.int32)
        e = jnp.exp(cv - cv[0:1])
        rows = pl.ds(pl.multiple_of(h * PEER_TOPK, PEER_TOPK), PEER_TOPK)
        gate_t[rows, :] = e / jnp.sum(e, axis=0, keepdims=True)
        code_t[rows, :] = ((ce & (HALF_EXPERTS - 1)) << 3) | (ce >> 13)

    lax.fori_loop(0, PEER_HEADS // ROUTE_HEADS_PER_STEP, head_step, 0)
    ca, cb = code_t[:NSEL // 2, :], code_t[NSEL // 2:, :]
    pair_ref[...] = (ca | (((cb & 1) ^ 1) << 1) | ((cb & -8) << 16)).T
    gate_ref[...] = gate_t[...].T


def peer_route(x2d, g, wq_heads, keys_bf16, tb):
    n = x2d.shape[0]
    return pl.pallas_call(
        _route_kernel,
        out_shape=(jax.ShapeDtypeStruct((n, D_MODEL), jnp.float32),
                   jax.ShapeDtypeStruct((n, NSEL // 2), jnp.int32),
                   jax.ShapeDtypeStruct((n, NSEL), jnp.float32)),
        grid=(n // tb,),
        in_specs=[pl.BlockSpec((tb, D_MODEL), lambda i: (i, 0)),
                  pl.BlockSpec((1, D_MODEL), lambda i: (0, 0)),
                  pl.BlockSpec((PEER_HEADS, D_MODEL, D_KEY), lambda i: (0, 0, 0)),
                  pl.BlockSpec((PEER_HEADS * 2, N_KEYS, D_HALF), lambda i: (0, 0, 0))],
        out_specs=(pl.BlockSpec((tb, D_MODEL), lambda i: (i, 0)),
                   pl.BlockSpec((tb, NSEL // 2), lambda i: (i, 0)),
                   pl.BlockSpec((tb, NSEL), lambda i: (i, 0))),
        scratch_shapes=[pltpu.VMEM((NSEL, tb), jnp.int32), pltpu.VMEM((NSEL, tb), jnp.float32)],
        compiler_params=_cparams(), name="peer_route",
    )(x2d, g.reshape(1, D_MODEL), wq_heads, keys_bf16)


def _pack_kernel(lo_ref, hi_ref, o_ref):
    def bf16_bits(x):
        return lax.bitcast_convert_type(x.astype(jnp.bfloat16).astype(jnp.float32), jnp.int32)

    word = lax.shift_right_logical(bf16_bits(lo_ref[...]), 16) | (bf16_bits(hi_ref[...]) & HI16)
    for k in range(ROW_TILE):
        o_ref[:, k, :] = word[:, k * LANES:(k + 1) * LANES]


def pack_table(tab):
    nb = HALF_EXPERTS // PACK_BLOCK
    out = pl.pallas_call(
        _pack_kernel,
        out_shape=jax.ShapeDtypeStruct((HALF_EXPERTS, ROW_TILE, LANES), jnp.int32),
        grid=(nb,),
        in_specs=[pl.BlockSpec((PACK_BLOCK, D_MODEL), lambda i: (i, 0)),
                  pl.BlockSpec((PACK_BLOCK, D_MODEL), lambda i: (i + nb, 0))],
        out_specs=pl.BlockSpec((PACK_BLOCK, ROW_TILE, LANES), lambda i: (i, 0, 0)),
        compiler_params=_cparams(), name="pack_table",
    )(tab, tab)
    return out.reshape(HALF_EXPERTS * ROW_TILE, LANES)


def _gather_pair(tab_ref, cab):
    wa = tab_ref[pl.ds(pl.multiple_of(cab & 0xFFF8, SUBLANES), SUBLANES), :]
    wb = tab_ref[pl.ds(pl.multiple_of(lax.shift_right_logical(cab, 16), SUBLANES), SUBLANES), :]
    cv = jnp.full((SUBLANES, LANES), cab, jnp.int32)
    sha = (cv & 1) << 4
    shb = (cv & 2) << 3
    return (lax.shift_right_logical(wa, sha) & 0xFFFF) | (lax.shift_left(wb, shb) & HI16)


def _gather_token(tab_ref, pair_ref, t, buf):
    for i in range(NSEL // 2):
        buf[i * ROW_TILE:(i + 1) * ROW_TILE, :] = _gather_pair(tab_ref, pair_ref[t, i])


def _grouped_token_loop(tb, gather, finish, group_a, group_b):
    @pl.when(pl.program_id(0) == 0)
    def _():
        for buf in group_b:
            buf[...] = jnp.zeros(buf.shape, jnp.int32)

    def body(i, carry):
        t0 = 2 * UPASS_GROUP * i
        for q in range(UPASS_GROUP):
            gather(t0 + q, group_a[q])
        finish(jnp.maximum(t0 - UPASS_GROUP, 0), group_b)
        for q in range(UPASS_GROUP):
            gather(t0 + UPASS_GROUP + q, group_b[q])
        finish(t0, group_a)
        return carry

    lax.fori_loop(0, tb // (2 * UPASS_GROUP), body, 0)
    finish(tb - UPASS_GROUP, group_b)


def _upass_kernel(pair_ref, gate_ref, xn_ref, tab_ref, w_ref, *bufs, tb):
    col = lax.broadcasted_iota(jnp.int32, (2 * ROW_TILE, SLOT_ROWS), 1)
    row = lax.broadcasted_iota(jnp.int32, (2 * ROW_TILE, SLOT_ROWS), 0)
    chunk_mask = (((col & 15) >> 1) == (row & 7)).astype(jnp.float32)
    c2 = lax.broadcasted_iota(jnp.int32, (SLOT_ROWS, NSEL), 0)
    j2 = lax.broadcasted_iota(jnp.int32, (SLOT_ROWS, NSEL), 1)
    fold = (j2 == (c2 >> 4) + ((c2 & 1) << 6)).astype(jnp.bfloat16)

    def gather(t, buf):
        _gather_token(tab_ref, pair_ref, t, buf)

    def finish(t0, group):
        zs = []
        for q, buf in enumerate(group):
            xt = xn_ref[pl.ds(t0 + q, 1), :].reshape(ROW_TILE, LANES)
            xhi = xt.astype(jnp.bfloat16)
            xlo = (xt - xhi.astype(jnp.float32)).astype(jnp.bfloat16)
            x16 = jnp.concatenate([xhi, xlo], axis=0)
            us = pltpu.bitcast(buf[...], jnp.bfloat16)
            r = lax.dot_general(x16, us, (((1,), (1,)), ((), ())), preferred_element_type=jnp.float32)
            zs.append(jnp.sum(r * chunk_mask, axis=0, keepdims=True))
        z = jnp.concatenate(zs, axis=0)
        zhi = z.astype(jnp.bfloat16)
        zlo = (z - zhi.astype(jnp.float32)).astype(jnp.bfloat16)
        act = (jnp.dot(zhi, fold, preferred_element_type=jnp.float32)
               + jnp.dot(zlo, fold, preferred_element_type=jnp.float32))
        gelu = 0.5 * act * (1.0 + lax.erf(act * (1.0 / math.sqrt(2.0))))
        rows = pl.ds(pl.multiple_of(t0, UPASS_GROUP), UPASS_GROUP)
        w_ref[rows, :] = (gate_ref[rows, :] * gelu).astype(jnp.bfloat16)

    _grouped_token_loop(tb, gather, finish, bufs[:UPASS_GROUP], bufs[UPASS_GROUP:])


def _peer_pass_call(kernel_fn, out_shape, out_block, pairs, per_token, rows, tab_packed, tb, name):
    n = pairs.shape[0]
    return pl.pallas_call(
        functools.partial(kernel_fn, tb=tb),
        out_shape=out_shape,
        grid=(n // tb,),
        in_specs=[pl.BlockSpec((tb, NSEL // 2), lambda i: (i, 0), memory_space=pltpu.SMEM),
                  pl.BlockSpec((tb, NSEL), lambda i: (i, 0)),
                  pl.BlockSpec((tb, D_MODEL), lambda i: (i, 0)),
                  pl.BlockSpec((HALF_EXPERTS * ROW_TILE, LANES), lambda i: (0, 0), pipeline_mode=pl.Buffered(1))],
        out_specs=pl.BlockSpec(out_block, lambda i: (i, 0)),
        scratch_shapes=[pltpu.VMEM((SLOT_ROWS // 2, LANES), jnp.int32)] * (2 * UPASS_GROUP),
        compiler_params=_cparams(), name=name,
    )(pairs, per_token, rows, tab_packed)


def peer_upass(pairs, gate, xn_rows, tab_packed, tb):
    n = pairs.shape[0]
    return _peer_pass_call(_upass_kernel, jax.ShapeDtypeStruct((n, NSEL), jnp.bfloat16), (tb, NSEL),
                           pairs, gate, xn_rows, tab_packed, tb, "peer_upass")


def _vpass_kernel(pair_ref, w_ref, x_ref, tab_ref, o_ref, *bufs, tb):
    col = lax.broadcasted_iota(jnp.int32, (ROW_TILE, SLOT_ROWS), 1)
    row = lax.broadcasted_iota(jnp.int32, (ROW_TILE, SLOT_ROWS), 0)
    chunk_mask = (((col & 15) >> 1) == row).astype(jnp.float32)
    j2 = lax.broadcasted_iota(jnp.int32, (NSEL, SLOT_ROWS), 0)
    c2 = lax.broadcasted_iota(jnp.int32, (NSEL, SLOT_ROWS), 1)
    spread = (j2 == (c2 >> 4) + ((c2 & 1) << 6)).astype(jnp.bfloat16)

    def gather(t, buf):
        _gather_token(tab_ref, pair_ref, t, buf)

    def finish(t0, group):
        rows = pl.ds(pl.multiple_of(t0, UPASS_GROUP), UPASS_GROUP)
        wexp = jnp.dot(w_ref[rows, :], spread, preferred_element_type=jnp.float32)
        for q, buf in enumerate(group):
            wsel = (jnp.broadcast_to(wexp[q:q + 1], (ROW_TILE, SLOT_ROWS)) * chunk_mask).astype(jnp.bfloat16)
            vs = pltpu.bitcast(buf[...], jnp.bfloat16)
            tile = jnp.dot(wsel, vs, preferred_element_type=jnp.float32)
            o_ref[pl.ds(t0 + q, 1), :] = x_ref[pl.ds(t0 + q, 1), :] + tile.reshape(1, D_MODEL)

    _grouped_token_loop(tb, gather, finish, bufs[:UPASS_GROUP], bufs[UPASS_GROUP:])


def peer_vpass(pairs, w_bf16, x_rows, tab_packed, tb):
    n = pairs.shape[0]
    return _peer_pass_call(_vpass_kernel, jax.ShapeDtypeStruct((n, D_MODEL), jnp.float32),
                           (tb, D_MODEL), pairs, w_bf16, x_rows, tab_packed, tb, "peer_vpass")


def peer_block(x, g, wq_heads, keys_bf16, u_packed, v_packed):
    shp = x.shape
    x2d = x.reshape(-1, D_MODEL)
    n = x2d.shape[0]
    xn, pairs, gate = peer_route(x2d, g, wq_heads, keys_bf16, min(PEER_ROUTE_BLOCK, n))
    w = peer_upass(pairs, gate, xn, u_packed, min(PEER_PASS_BLOCK, n))
    out = peer_vpass(pairs, w, x2d, v_packed, min(PEER_PASS_BLOCK, n))
    return out.reshape(shp)


def _rms_kernel(x_ref, g_ref, o_ref):
    o_ref[...] = _rms(x_ref[...], g_ref[...])


def rmsnorm_pallas(x, g):
    shp = x.shape
    xt = x.reshape(-1, shp[-1])
    n = xt.shape[0]
    tm = min(ROW_BLOCK, n)
    out = pl.pallas_call(
        _rms_kernel,
        out_shape=jax.ShapeDtypeStruct(xt.shape, xt.dtype),
        grid=(n // tm,),
        in_specs=[pl.BlockSpec((tm, shp[-1]), lambda i: (i, 0)),
                  pl.BlockSpec((1, shp[-1]), lambda i: (0, 0))],
        out_specs=pl.BlockSpec((tm, shp[-1]), lambda i: (i, 0)),
        compiler_params=_cparams(), name="final_rmsnorm",
    )(xt, g.reshape(1, -1))
    return out.reshape(shp)


def even_layer(x, pos0, k_prev, v_prev, h0r, h0i, norm_g, w_in, w_out, sink, s5_params, d_skip, w_glu, b_glu):
    b, t, _ = x.shape
    n = b * t
    x2d = x.reshape(n, D_MODEL)
    tabs = rope_tables(pos0 + jnp.arange(t), ROT_DIM, HEAD_DIM, SWA_Q, max(1, min(ROW_BLOCK, n) // t))
    q, k, v, u = in_even(x2d, norm_g, w_in.astype(jnp.bfloat16), tabs)
    q3, k3, v3 = q.reshape(b, t, SWA_Q), k.reshape(b, t, SWA_KV), v.reshape(b, t, SWA_KV)
    if k_prev is None:
        att = swa_attention(q3, k3, k3, v3, v3, sink, True)
        k_all, v_all = k3, v3
        h0r = jnp.zeros((b, S5_FLAT), jnp.float32)
        h0i = jnp.zeros((b, S5_FLAT), jnp.float32)
    else:
        kp, vp = k_prev.reshape(b, WINDOW, SWA_KV), v_prev.reshape(b, WINDOW, SWA_KV)
        att = swa_attention(q3, kp, k3, vp, v3, sink, False)
        k_all, v_all = jnp.concatenate([kp, k3], axis=1), jnp.concatenate([vp, v3], axis=1)
        h0r, h0i = h0r.reshape(b, S5_FLAT), h0i.reshape(b, S5_FLAT)
    lam, wb, wc = s5_discretize(*s5_params)
    u_tm = u.reshape(b, t, S5_WIDTH).transpose(1, 0, 2).reshape(n, S5_WIDTH)
    s5o_tm, hre, him = s5_mixer(u_tm, b, lam, wb, wc, d_skip, w_glu.astype(jnp.bfloat16), b_glu, h0r, h0i,
                                min(t, S5_STEPS))
    s5o = s5o_tm.reshape(t, b, S5_WIDTH).transpose(1, 0, 2).reshape(n, S5_WIDTH)
    out = out_proj(x2d, att.reshape(n, SWA_Q), s5o, w_out.astype(jnp.bfloat16))
    return (out.reshape(b, t, D_MODEL),
            k_all[:, -WINDOW:].reshape(b, WINDOW, SWA_KV_HEADS, HEAD_DIM),
            v_all[:, -WINDOW:].reshape(b, WINDOW, SWA_KV_HEADS, HEAD_DIM),
            hre.reshape(b, S5_GROUPS, S5_STATE), him.reshape(b, S5_GROUPS, S5_STATE))


def odd_layer(x, pos0, pool_prev, ckv_prev, kpe_prev, norm_g, w_in, w_out, pool_w, pool_scale,
              q_norm, kv_norm, w_uq, w_uk, w_uv):
    b, t, _ = x.shape
    n = b * t
    x2d = x.reshape(n, D_MODEL)
    reps = max(1, min(ROW_BLOCK, n) // t)
    pos = pos0 + jnp.arange(t)
    ktabs = rope_tables(pos, ROPE_DIM, KPE_PAD, KPE_PAD, reps)
    qtabs = rope_tables(pos, ROPE_DIM, ROPE_DIM, MLA_QPE, reps)
    w_perm, wuq_perm = permute_odd_weights(w_in, w_uq)
    qnope, qpe, c, kp, u = in_odd(x2d, norm_g, w_perm, q_norm, kv_norm, wuq_perm, ktabs, qtabs)
    c3, kp3 = c.reshape(b, t, KV_LORA), kp.reshape(b, t, ROPE_DIM)
    wuk_h = w_uk.transpose(1, 2, 0).astype(jnp.bfloat16)
    wuv_h = w_uv.transpose(1, 0, 2).astype(jnp.bfloat16)
    if ckv_prev is None:
        ck, kk, causal, n_keys = c3, kp3, True, t
        pool_prev = jnp.zeros((b, POOL_BUF, POOL_WIDTH), jnp.float32)
    else:
        ck, kk, causal = jnp.concatenate([ckv_prev, c3], axis=1), jnp.concatenate([kpe_prev, kp3], axis=1), False
        n_keys = ck.shape[1]
        pad = -n_keys % min(MLA_KBLOCK, n_keys)
        ck, kk = jnp.pad(ck, ((0, 0), (0, pad), (0, 0))), jnp.pad(kk, ((0, 0), (0, pad), (0, 0)))
    mla = mla_attention(qnope.reshape(b, t, MLA_QNOPE), qpe.reshape(b, t, MLA_QPE), ck.astype(jnp.bfloat16),
                        kk.astype(jnp.bfloat16), wuk_h, wuv_h, causal, n_keys)
    pool_out, pool_new = pool_mixer(u.reshape(b, t, POOL_WIDTH), pool_prev, pool_w.astype(jnp.bfloat16), pool_scale, pos0)
    out = out_proj(x2d, pool_out.reshape(n, POOL_WIDTH), mla.reshape(n, MLA_HEADS * V_DIM), w_out.astype(jnp.bfloat16))
    return out.reshape(b, t, D_MODEL), pool_new, c3, kp3


def kernel(x_prompt, x_sample, cache_swa_k, cache_swa_v, state_ssm_re, state_ssm_im, state_pool,
           cache_mla_ckv, cache_mla_kpe, norm_mix, norm_ffn, norm_final, w_in_even, w_out_even,
           swa_sink, s5_lam_re, s5_lam_im, s5_log_dt, s5_b_re, s5_b_im, s5_c_re, s5_c_im, s5_d,
           s5_w_glu, s5_b_glu, w_in_odd, w_out_odd, pool_w, pool_scale, mla_q_norm, mla_kv_norm,
           mla_w_uq, mla_w_uk, mla_w_uv, peer_w_q, peer_keys, peer_u, peer_v):
    xp, xs = x_prompt, x_sample
    kp_l, vp_l, rp_l, ip_l, poolp_l, cp_l, ep_l = [], [], [], [], [], [], []
    ks_l, vs_l, rs_l, is_l, pools_l, cs_l, es_l = [], [], [], [], [], [], []
    for layer in range(DEPTH):
        i = layer // 2
        if layer % 2 == 0:
            s5_params = (s5_lam_re[i], s5_lam_im[i], s5_log_dt[i], s5_b_re[i], s5_b_im[i], s5_c_re[i], s5_c_im[i])
            ew = (norm_mix[layer], w_in_even[i], w_out_even[i], swa_sink[i], s5_params, s5_d[i], s5_w_glu[i], s5_b_glu[i])
            xp, k1, v1, r1, i1 = even_layer(xp, 0, None, None, None, None, *ew)
            xs, k2, v2, r2, i2 = even_layer(xs, PAST_LEN, cache_swa_k[i], cache_swa_v[i],
                                            state_ssm_re[i], state_ssm_im[i], *ew)
            kp_l.append(k1); vp_l.append(v1); rp_l.append(r1); ip_l.append(i1)
            ks_l.append(k2); vs_l.append(v2); rs_l.append(r2); is_l.append(i2)
        else:
            ow = (norm_mix[layer], w_in_odd[i], w_out_odd[i], pool_w[i], pool_scale[i], mla_q_norm[i], mla_kv_norm[i],
                  mla_w_uq[i], mla_w_uk[i], mla_w_uv[i])
            xp, p1, c1, e1 = odd_layer(xp, 0, None, None, None, *ow)
            xs, p2, c2, e2 = odd_layer(xs, PAST_LEN, state_pool[i], cache_mla_ckv[i], cache_mla_kpe[i], *ow)
            poolp_l.append(p1); cp_l.append(c1); ep_l.append(e1)
            pools_l.append(p2); cs_l.append(c2); es_l.append(e2)
        wq_heads = peer_w_q[layer].reshape(D_MODEL, PEER_HEADS, D_KEY).transpose(1, 0, 2).astype(jnp.bfloat16)
        keys_bf16 = peer_keys[layer].reshape(PEER_HEADS * 2, N_KEYS, D_HALF).astype(jnp.bfloat16)
        u_packed, v_packed = pack_table(peer_u[layer]), pack_table(peer_v[layer])
        xp = peer_block(xp, norm_ffn[layer], wq_heads, keys_bf16, u_packed, v_packed)
        xs = peer_block(xs, norm_ffn[layer], wq_heads, keys_bf16, u_packed, v_packed)
    y_prompt = rmsnorm_pallas(xp, norm_final)
    y_sample = rmsnorm_pallas(xs, norm_final)
    return (y_prompt, y_sample,
            jnp.stack(kp_l), jnp.stack(vp_l), jnp.stack(rp_l), jnp.stack(ip_l),
            jnp.stack(poolp_l), jnp.stack(cp_l), jnp.stack(ep_l),
            jnp.stack(ks_l), jnp.stack(vs_l), jnp.stack(rs_l), jnp.stack(is_l),
            jnp.stack(pools_l), jnp.stack(cs_l), jnp.stack(es_l))
```

```python
import functools
import math
import jax
import jax.numpy as jnp
from jax import lax
from jax.experimental import pallas as pl
from jax.experimental.pallas import tpu as pltpu

D_MODEL = 1024
DEPTH = 2
PAST_LEN = 2048

CHUNK = 64
RMS_EPS = 1e-6
ROPE_THETA = 500000.0
NEG_INF = -1e30

SWA_HEADS = 8
SWA_KV_HEADS = 2
SWA_GROUP = SWA_HEADS // SWA_KV_HEADS
HEAD_DIM = 64
ROT_DIM = HEAD_DIM // 4
WINDOW = 128
SWA_Q = SWA_HEADS * HEAD_DIM
SWA_KV = SWA_KV_HEADS * HEAD_DIM
SWA_SCALE = HEAD_DIM ** -0.5

S5_WIDTH = 512
S5_GROUP = 16
S5_GROUPS = S5_WIDTH // S5_GROUP
S5_STATE = 64
S5_FLAT = S5_GROUPS * S5_STATE

POOL_WIDTH = 512
POOL_WINDOWS = (2, 4, 8, 16)
POOL_GROUP = POOL_WIDTH // len(POOL_WINDOWS)
POOL_MAX = 16
POOL_BUF = POOL_MAX - 1

MLA_HEADS = 8
Q_LORA = 512
KV_LORA = 256
NOPE_DIM = 64
ROPE_DIM = 32
V_DIM = 64
MLA_SCALE = (NOPE_DIM + ROPE_DIM) ** -0.5
MLA_QNOPE = MLA_HEADS * NOPE_DIM
MLA_QPE = MLA_HEADS * ROPE_DIM

EVEN_IN = SWA_Q + 2 * SWA_KV + S5_WIDTH

PEER_HEADS = 8
N_KEYS = 128
N_EXPERTS = N_KEYS * N_KEYS
D_KEY = 128
D_HALF = D_KEY // 2
PEER_TOPK = 16
NSEL = PEER_HEADS * PEER_TOPK
HALF_EXPERTS = N_EXPERTS // 2

SUBLANES = 8
LANES = 128
VMEM_LIMIT = 56 * 1024 * 1024
ROW_TILE = D_MODEL // LANES

ROW_BLOCK = 512
SWA_QBLOCK = 256
MLA_QBLOCK = 256
MLA_KBLOCK = 512
MLA_CHAIN_ROWS = 256
KPE_PAD = LANES
S5_STEPS = 128
S5_BLOCK_GROUPS = 8
S5_CARRY_VREGS = 16
PEER_ROUTE_BLOCK = 256
ROUTE_HEADS_PER_STEP = 8
PEER_PASS_BLOCK = 512
PACK_BLOCK = 256
UPASS_GROUP = 32
SLOT_ROWS = NSEL * ROW_TILE
HI16 = -65536
PAIR_COLS = tuple(PEER_TOPK // (a + 1) for a in range(PEER_TOPK))
PAIR_ROWS = -(-sum(PAIR_COLS) // SUBLANES) * SUBLANES
PAIR_PAD = PAIR_ROWS - sum(PAIR_COLS)

ODD_U0 = Q_LORA + KV_LORA
ODD_KPE0 = ODD_U0 + POOL_WIDTH
ODD_IN_PAD = ODD_KPE0 + KPE_PAD


def _cparams(n_axes=1):
    return pltpu.CompilerParams(dimension_semantics=("arbitrary",) * n_axes, vmem_limit_bytes=VMEM_LIMIT)


def _rms(x, g):
    return x * lax.rsqrt(jnp.mean(x * x, axis=-1, keepdims=True) + RMS_EPS) * g


def _rope_lanes(x, cos, sin_lo, sin_hi, half):
    n = x.shape[-1]
    return x * cos + pltpu.roll(x, n - half, 1) * sin_lo + pltpu.roll(x, half, 1) * sin_hi


def rope_tables(pos, rot, period, width, reps):
    inv = ROPE_THETA ** (-jnp.arange(0, rot, 2, dtype=jnp.float32) / rot)
    ang = pos.astype(jnp.float32)[:, None] * inv[None, :]
    cos, sin = jnp.cos(ang), jnp.sin(ang)
    lane = jnp.arange(width) % period
    idx = lane % (rot // 2)
    in_lo = lane < rot // 2
    in_hi = (lane >= rot // 2) & (lane < rot)
    c = jnp.where((in_lo | in_hi)[None, :], cos[:, idx], 1.0)
    s_lo = jnp.where(in_lo[None, :], -sin[:, idx], 0.0)
    s_hi = jnp.where(in_hi[None, :], sin[:, idx], 0.0)
    return tuple(jnp.tile(t, (reps, 1)) for t in (c, s_lo, s_hi))


def _in_even_kernel(x_ref, g_ref, w_ref, cos_ref, slo_ref, shi_ref, q_ref, k_ref, v_ref, u_ref):
    xn = _rms(x_ref[...], g_ref[...])
    proj = jnp.dot(xn.astype(jnp.bfloat16), w_ref[...], preferred_element_type=jnp.float32)
    cos, slo, shi = cos_ref[...], slo_ref[...], shi_ref[...]
    q_ref[...] = _rope_lanes(proj[:, :SWA_Q], cos, slo, shi, ROT_DIM // 2)
    k_ref[...] = _rope_lanes(proj[:, SWA_Q:SWA_Q + SWA_KV], cos[:, :SWA_KV], slo[:, :SWA_KV], shi[:, :SWA_KV],
                             ROT_DIM // 2)
    v_ref[...] = proj[:, SWA_Q + SWA_KV:SWA_Q + 2 * SWA_KV]
    u_ref[...] = proj[:, SWA_Q + 2 * SWA_KV:]


def in_even(x2d, g, w_bf16, tabs):
    n = x2d.shape[0]
    tm = min(ROW_BLOCK, n)
    nt = tabs[0].shape[0] // tm
    row = lambda i: (i, 0)
    const = lambda i: (0, 0)
    tab = lambda i: (i % nt, 0)
    return pl.pallas_call(
        _in_even_kernel,
        out_shape=(jax.ShapeDtypeStruct((n, SWA_Q), jnp.float32), jax.ShapeDtypeStruct((n, SWA_KV), jnp.float32),
                   jax.ShapeDtypeStruct((n, SWA_KV), jnp.float32), jax.ShapeDtypeStruct((n, S5_WIDTH), jnp.float32)),
        grid=(n // tm,),
        in_specs=[pl.BlockSpec((tm, D_MODEL), row), pl.BlockSpec((1, D_MODEL), const),
                  pl.BlockSpec((D_MODEL, EVEN_IN), const),
                  pl.BlockSpec((tm, SWA_Q), tab), pl.BlockSpec((tm, SWA_Q), tab), pl.BlockSpec((tm, SWA_Q), tab)],
        out_specs=(pl.BlockSpec((tm, SWA_Q), row), pl.BlockSpec((tm, SWA_KV), row),
                   pl.BlockSpec((tm, SWA_KV), row), pl.BlockSpec((tm, S5_WIDTH), row)),
        compiler_params=_cparams(), name="in_even",
    )(x2d, g.reshape(1, D_MODEL), w_bf16, *tabs)


def _swa_kernel(sink_ref, q_ref, kp_ref, kc_ref, vp_ref, vc_ref, o_ref, *, banded, qb):
    i = pl.program_id(1)
    q = q_ref[0]
    k = jnp.concatenate([kp_ref[0], kc_ref[0]], axis=0).astype(jnp.bfloat16)
    v = jnp.concatenate([vp_ref[0], vc_ref[0]], axis=0).astype(jnp.bfloat16)
    nk = WINDOW + qb
    wc = WINDOW // CHUNK
    if banded:
        qc = lax.broadcasted_iota(jnp.int32, (qb, nk), 0) // CHUNK + wc
        kc = lax.broadcasted_iota(jnp.int32, (qb, nk), 1) // CHUNK
        visible = (kc <= qc) & (kc >= qc - wc) & ((kc >= wc) | (i > 0))
    ss = []
    for h in range(SWA_HEADS):
        hk = h // SWA_GROUP
        qh = q[:, h * HEAD_DIM:(h + 1) * HEAD_DIM].astype(jnp.bfloat16)
        ss.append(lax.dot_general(qh, k[:, hk * HEAD_DIM:(hk + 1) * HEAD_DIM], (((1,), (1,)), ((), ())),
                                  preferred_element_type=jnp.float32) * SWA_SCALE)
    ps, dens = [], []
    for h in range(SWA_HEADS):
        s = jnp.where(visible, ss[h], NEG_INF) if banded else ss[h]
        sk = sink_ref[h]
        m = jnp.maximum(jnp.max(s, axis=-1, keepdims=True), sk)
        p = jnp.exp(s - m)
        dens.append(jnp.sum(p, axis=-1, keepdims=True) + jnp.exp(sk - m))
        ps.append(p.astype(jnp.bfloat16))
    outs = [jnp.dot(ps[h], v[:, (h // SWA_GROUP) * HEAD_DIM:(h // SWA_GROUP + 1) * HEAD_DIM],
                    preferred_element_type=jnp.float32) / dens[h] for h in range(SWA_HEADS)]
    o_ref[0] = jnp.concatenate(outs, axis=-1)


def swa_attention(q, k_prev, k_cur, v_prev, v_cur, sink, banded):
    b, t, _ = q.shape
    qb = min(SWA_QBLOCK, t)
    per = qb // WINDOW
    prev_map = (lambda bi, i: (bi, jnp.maximum(i * per - 1, 0), 0)) if banded else (lambda bi, i: (bi, 0, 0))
    cur = lambda bi, i: (bi, i, 0)
    return pl.pallas_call(
        functools.partial(_swa_kernel, banded=banded, qb=qb),
        out_shape=jax.ShapeDtypeStruct((b, t, SWA_Q), jnp.float32),
        grid=(b, t // qb),
        in_specs=[pl.BlockSpec(memory_space=pltpu.SMEM),
                  pl.BlockSpec((1, qb, SWA_Q), cur),
                  pl.BlockSpec((1, WINDOW, SWA_KV), prev_map), pl.BlockSpec((1, qb, SWA_KV), cur),
                  pl.BlockSpec((1, WINDOW, SWA_KV), prev_map), pl.BlockSpec((1, qb, SWA_KV), cur)],
        out_specs=pl.BlockSpec((1, qb, SWA_Q), cur),
        compiler_params=_cparams(2), name="swa_attention",
    )(sink, q, k_prev, k_cur, v_prev, v_cur)


def _out_kernel(x_ref, a_ref, b_ref, w_ref, o_ref):
    ka = a_ref.shape[-1]
    o_ref[...] = (x_ref[...]
                  + jnp.dot(a_ref[...].astype(jnp.bfloat16), w_ref[:ka, :], preferred_element_type=jnp.float32)
                  + jnp.dot(b_ref[...].astype(jnp.bfloat16), w_ref[ka:, :], preferred_element_type=jnp.float32))


def out_proj(x2d, a, b, w_bf16):
    n = x2d.shape[0]
    tm = min(ROW_BLOCK, n)
    row = lambda i: (i, 0)
    return pl.pallas_call(
        _out_kernel,
        out_shape=jax.ShapeDtypeStruct((n, D_MODEL), jnp.float32),
        grid=(n // tm,),
        in_specs=[pl.BlockSpec((tm, D_MODEL), row), pl.BlockSpec((tm, a.shape[1]), row),
                  pl.BlockSpec((tm, b.shape[1]), row), pl.BlockSpec(w_bf16.shape, lambda i: (0, 0))],
        out_specs=pl.BlockSpec((tm, D_MODEL), row),
        compiler_params=_cparams(), name="out_proj",
    )(x2d, a, b, w_bf16)


def s5_discretize(lam_re, lam_im, log_dt, b_re, b_im, c_re, c_im):
    lr = jnp.minimum(lam_re, -1e-4)
    li = lam_im
    dt = jnp.exp(log_dt)[:, None]
    mag = jnp.exp(lr * dt)
    ang = li * dt
    ab_re, ab_im = mag * jnp.cos(ang), mag * jnp.sin(ang)
    den = lr * lr + li * li
    nr, ni = ab_re - 1.0, ab_im
    f_re = (nr * lr + ni * li) / den
    f_im = (ni * lr - nr * li) / den
    bb_re = f_re[..., None] * b_re - f_im[..., None] * b_im
    bb_im = f_re[..., None] * b_im + f_im[..., None] * b_re
    eye = jnp.eye(S5_GROUPS, dtype=jnp.float32)

    def embed_b(bb):
        return jnp.einsum('gnc,gh->gchn', bb, eye).reshape(S5_WIDTH, S5_FLAT)

    def embed_c(c):
        return jnp.einsum('gcn,gh->gnhc', c, eye).reshape(S5_FLAT, S5_WIDTH)

    wb = jnp.concatenate([embed_b(bb_re), embed_b(bb_im)], axis=1)
    wc = jnp.concatenate([embed_c(c_re), -embed_c(c_im)], axis=0)
    lam = jnp.stack([ab_re.reshape(S5_FLAT), ab_im.reshape(S5_FLAT)])
    return lam, wb.astype(jnp.bfloat16), wc.astype(jnp.bfloat16)


def _s5_kernel(u_ref, lam_ref, wb_ref, wc_ref, d_ref, wglu_ref, bglu_ref, h0r_ref, h0i_ref,
               o_ref, hr_ref, hi_ref, hbuf, *, nb, steps, width):
    @pl.when(pl.program_id(0) == 0)
    def _():
        hr_ref[...] = h0r_ref[...]
        hi_ref[...] = h0i_ref[...]

    u = u_ref[...]
    ub = u.astype(jnp.bfloat16)
    n_blocks = S5_GROUPS // S5_BLOCK_GROUPS
    uw, sw = S5_BLOCK_GROUPS * S5_GROUP, S5_BLOCK_GROUPS * S5_STATE
    for gb in range(n_blocks):
        ucols = slice(gb * uw, (gb + 1) * uw)
        for part in range(2):
            scols = slice(part * S5_FLAT + gb * sw, part * S5_FLAT + (gb + 1) * sw)
            hbuf[:, scols] = jnp.dot(ub[:, ucols], wb_ref[ucols, scols], preferred_element_type=jnp.float32)
    for c0 in range(0, S5_FLAT, width):
        cre = slice(c0, c0 + width)
        cim = slice(S5_FLAT + c0, S5_FLAT + c0 + width)
        lr = jnp.broadcast_to(lam_ref[0:1, cre], (nb, width))
        li = jnp.broadcast_to(lam_ref[1:2, cre], (nb, width))

        def step(t, carry):
            hr, hi = carry
            rows = pl.ds(pl.multiple_of(t * nb, nb), nb)
            nhr = lr * hr - li * hi + hbuf[rows, cre]
            nhi = lr * hi + li * hr + hbuf[rows, cim]
            hbuf[rows, cre] = nhr
            hbuf[rows, cim] = nhi
            return nhr, nhi

        hr, hi = lax.fori_loop(0, steps, step, (hr_ref[:, cre], hi_ref[:, cre]))
        hr_ref[:, cre] = hr
        hi_ref[:, cre] = hi
    ys = []
    for gb in range(n_blocks):
        ucols = slice(gb * uw, (gb + 1) * uw)
        re = slice(gb * sw, (gb + 1) * sw)
        im = slice(S5_FLAT + gb * sw, S5_FLAT + (gb + 1) * sw)
        ys.append(jnp.dot(hbuf[:, re].astype(jnp.bfloat16), wc_ref[re, ucols], preferred_element_type=jnp.float32)
                  + jnp.dot(hbuf[:, im].astype(jnp.bfloat16), wc_ref[im, ucols], preferred_element_type=jnp.float32))
    y = jnp.concatenate(ys, axis=-1) + d_ref[...] * u
    z = 0.5 * y * (1.0 + jnp.tanh(math.sqrt(2.0 / math.pi) * (y + 0.044715 * (y * y * y))))
    gate = jnp.dot(z.astype(jnp.bfloat16), wglu_ref[...], preferred_element_type=jnp.float32) + bglu_ref[...]
    o_ref[...] = z * (1.0 / (1.0 + jnp.exp(-gate)))


def s5_mixer(u_tm, nb, lam, wb, wc, d_skip, w_glu_bf16, b_glu, h0r, h0i, steps):
    rows = u_tm.shape[0]
    t_total = rows // nb
    width = min(S5_FLAT, max(LANES, (S5_CARRY_VREGS * SUBLANES * LANES) // (2 * nb)))
    blk = steps * nb
    const = lambda i: (0, 0)
    return pl.pallas_call(
        functools.partial(_s5_kernel, nb=nb, steps=steps, width=width),
        out_shape=(jax.ShapeDtypeStruct((rows, S5_WIDTH), jnp.float32),
                   jax.ShapeDtypeStruct((nb, S5_FLAT), jnp.float32),
                   jax.ShapeDtypeStruct((nb, S5_FLAT), jnp.float32)),
        grid=(t_total // steps,),
        in_specs=[pl.BlockSpec((blk, S5_WIDTH), lambda i: (i, 0)),
                  pl.BlockSpec((2, S5_FLAT), const),
                  pl.BlockSpec((S5_WIDTH, 2 * S5_FLAT), const),
                  pl.BlockSpec((2 * S5_FLAT, S5_WIDTH), const),
                  pl.BlockSpec((1, S5_WIDTH), const),
                  pl.BlockSpec((S5_WIDTH, S5_WIDTH), const),
                  pl.BlockSpec((1, S5_WIDTH), const),
                  pl.BlockSpec((nb, S5_FLAT), const),
                  pl.BlockSpec((nb, S5_FLAT), const)],
        out_specs=(pl.BlockSpec((blk, S5_WIDTH), lambda i: (i, 0)),
                   pl.BlockSpec((nb, S5_FLAT), const),
                   pl.BlockSpec((nb, S5_FLAT), const)),
        scratch_shapes=[pltpu.VMEM((blk, 2 * S5_FLAT), jnp.float32)],
        compiler_params=_cparams(), name="s5_mixer",
    )(u_tm, lam, wb, wc, d_skip.reshape(1, S5_WIDTH), w_glu_bf16, b_glu.reshape(1, S5_WIDTH), h0r, h0i)


def _in_odd_kernel(x_ref, g_ref, w_ref, qn_ref, kvn_ref, wuq_ref, kcos_ref, kslo_ref, kshi_ref,
                   qcos_ref, qslo_ref, qshi_ref, qnope_ref, qpe_ref, c_ref, kp_ref, u_ref):
    xn = _rms(x_ref[...], g_ref[...])
    proj = jnp.dot(xn.astype(jnp.bfloat16), w_ref[...], preferred_element_type=jnp.float32)
    cqn = _rms(proj[:, :Q_LORA], qn_ref[...])
    q = jnp.dot(cqn.astype(jnp.bfloat16), wuq_ref[...], preferred_element_type=jnp.float32)
    qnope_ref[...] = q[:, :MLA_QNOPE]
    qpe_ref[...] = _rope_lanes(q[:, MLA_QNOPE:], qcos_ref[...], qslo_ref[...], qshi_ref[...], ROPE_DIM // 2)
    c_ref[...] = _rms(proj[:, Q_LORA:ODD_U0], kvn_ref[...])
    kp = _rope_lanes(proj[:, ODD_KPE0:], kcos_ref[...], kslo_ref[...], kshi_ref[...], ROPE_DIM // 2)
    kp_ref[...] = kp[:, :ROPE_DIM]
    u_ref[...] = proj[:, ODD_U0:ODD_KPE0]


def in_odd(x2d, g, w_perm_bf16, q_norm, kv_norm, wuq_perm_bf16, ktabs, qtabs):
    n = x2d.shape[0]
    tm = min(ROW_BLOCK, n)
    nt = ktabs[0].shape[0] // tm
    row = lambda i: (i, 0)
    const = lambda i: (0, 0)
    tab = lambda i: (i % nt, 0)
    return pl.pallas_call(
        _in_odd_kernel,
        out_shape=(jax.ShapeDtypeStruct((n, MLA_QNOPE), jnp.float32), jax.ShapeDtypeStruct((n, MLA_QPE), jnp.float32),
                   jax.ShapeDtypeStruct((n, KV_LORA), jnp.float32), jax.ShapeDtypeStruct((n, ROPE_DIM), jnp.float32),
                   jax.ShapeDtypeStruct((n, POOL_WIDTH), jnp.float32)),
        grid=(n // tm,),
        in_specs=[pl.BlockSpec((tm, D_MODEL), row), pl.BlockSpec((1, D_MODEL), const),
                  pl.BlockSpec((D_MODEL, ODD_IN_PAD), const),
                  pl.BlockSpec((1, Q_LORA), const), pl.BlockSpec((1, KV_LORA), const),
                  pl.BlockSpec((Q_LORA, MLA_QNOPE + MLA_QPE), const),
                  pl.BlockSpec((tm, KPE_PAD), tab), pl.BlockSpec((tm, KPE_PAD), tab), pl.BlockSpec((tm, KPE_PAD), tab),
                  pl.BlockSpec((tm, MLA_QPE), tab), pl.BlockSpec((tm, MLA_QPE), tab), pl.BlockSpec((tm, MLA_QPE), tab)],
        out_specs=(pl.BlockSpec((tm, MLA_QNOPE), row), pl.BlockSpec((tm, MLA_QPE), row), pl.BlockSpec((tm, KV_LORA), row),
                   pl.BlockSpec((tm, ROPE_DIM), row), pl.BlockSpec((tm, POOL_WIDTH), row)),
        compiler_params=_cparams(), name="in_odd",
    )(x2d, g.reshape(1, D_MODEL), w_perm_bf16, q_norm.reshape(1, Q_LORA), kv_norm.reshape(1, KV_LORA), wuq_perm_bf16,
      *ktabs, *qtabs)


def permute_odd_weights(w_in, w_uq):
    o1, o2 = Q_LORA + KV_LORA, Q_LORA + KV_LORA + ROPE_DIM
    kpe = jnp.pad(w_in[:, o1:o2], ((0, 0), (0, KPE_PAD - ROPE_DIM)))
    w_perm = jnp.concatenate([w_in[:, :o1], w_in[:, o2:], kpe], axis=1)
    wq = w_uq.reshape(Q_LORA, MLA_HEADS, NOPE_DIM + ROPE_DIM)
    wq_perm = jnp.concatenate([wq[:, :, :NOPE_DIM].reshape(Q_LORA, MLA_QNOPE),
                               wq[:, :, NOPE_DIM:].reshape(Q_LORA, MLA_QPE)], axis=1)
    return w_perm.astype(jnp.bfloat16), wq_perm.astype(jnp.bfloat16)


def _mla_kernel(qn_ref, qp_ref, c_ref, kp_ref, wuk_ref, wuv_ref, o_ref, *scratch, causal, qb, kb, n_keys, hpc):
    i = pl.program_id(1)
    n_chains = MLA_HEADS // hpc
    qa_s, qp_s, m_s, l_s, acc_s = (scratch[k * n_chains:(k + 1) * n_chains] for k in range(5))
    qn = qn_ref[0].astype(jnp.bfloat16)
    qp = qp_ref[0].astype(jnp.bfloat16)
    for g in range(n_chains):
        for hh in range(hpc):
            h = g * hpc + hh
            rows = slice(hh * qb, (hh + 1) * qb)
            qa_s[g][rows, :] = jnp.dot(qn[:, h * NOPE_DIM:(h + 1) * NOPE_DIM], wuk_ref[h],
                                       preferred_element_type=jnp.float32).astype(jnp.bfloat16)
            qp_s[g][rows, :] = qp[:, h * ROPE_DIM:(h + 1) * ROPE_DIM]
        m_s[g][...] = jnp.full(m_s[g].shape, NEG_INF, jnp.float32)
        l_s[g][...] = jnp.zeros(l_s[g].shape, jnp.float32)
        acc_s[g][...] = jnp.zeros(acc_s[g].shape, jnp.float32)
    if causal:
        qpos = i * qb + lax.broadcasted_iota(jnp.int32, (hpc * qb, kb), 0) % qb
        limit = (qpos // CHUNK + 1) * CHUNK
        nblk = ((i + 1) * qb + kb - 1) // kb
    else:
        limit = n_keys
        nblk = (n_keys + kb - 1) // kb
    kidx0 = lax.broadcasted_iota(jnp.int32, (hpc * qb, kb), 1)

    def body(j, carry):
        rows = pl.ds(pl.multiple_of(j * kb, kb), kb)
        cb = c_ref[0, rows, :]
        kpb = kp_ref[0, rows, :]
        visible = kidx0 + j * kb < limit
        ss = [(lax.dot_general(qa_s[g][...], cb, (((1,), (1,)), ((), ())), preferred_element_type=jnp.float32)
               + lax.dot_general(qp_s[g][...], kpb, (((1,), (1,)), ((), ())),
                                 preferred_element_type=jnp.float32)) * MLA_SCALE for g in range(n_chains)]
        ps, alphas = [], []
        for g in range(n_chains):
            s = jnp.where(visible, ss[g], NEG_INF)
            m_old = m_s[g][...]
            m_new = jnp.maximum(m_old, jnp.max(s, axis=-1, keepdims=True))
            alpha = jnp.exp(m_old - m_new)
            p = jnp.exp(s - m_new)
            m_s[g][...] = m_new
            l_s[g][...] = alpha * l_s[g][...] + jnp.sum(p, axis=-1, keepdims=True)
            ps.append(p.astype(jnp.bfloat16))
            alphas.append(alpha)
        for g in range(n_chains):
            acc_s[g][...] = alphas[g] * acc_s[g][...] + jnp.dot(ps[g], cb, preferred_element_type=jnp.float32)
        return carry

    lax.fori_loop(0, nblk, body, 0)
    outs = []
    for g in range(n_chains):
        o_lat = (acc_s[g][...] / l_s[g][...]).astype(jnp.bfloat16)
        for hh in range(hpc):
            outs.append(jnp.dot(o_lat[hh * qb:(hh + 1) * qb, :], wuv_ref[g * hpc + hh],
                                preferred_element_type=jnp.float32))
    o_ref[0] = jnp.concatenate(outs, axis=-1)


def mla_attention(q_nope, q_pe, c_keys_bf16, kp_keys_bf16, wuk_h, wuv_h, causal, n_keys):
    b, t, _ = q_nope.shape
    tk = c_keys_bf16.shape[1]
    qb = min(MLA_QBLOCK, t)
    kb = min(MLA_KBLOCK, tk)
    hpc = max(1, min(MLA_HEADS, MLA_CHAIN_ROWS // qb))
    n_chains, rows = MLA_HEADS // hpc, hpc * qb
    cur = lambda bi, i: (bi, i, 0)
    whole = lambda bi, i: (bi, 0, 0)
    const3 = lambda bi, i: (0, 0, 0)
    return pl.pallas_call(
        functools.partial(_mla_kernel, causal=causal, qb=qb, kb=kb, n_keys=n_keys, hpc=hpc),
        out_shape=jax.ShapeDtypeStruct((b, t, MLA_HEADS * V_DIM), jnp.float32),
        grid=(b, t // qb),
        in_specs=[pl.BlockSpec((1, qb, MLA_QNOPE), cur), pl.BlockSpec((1, qb, MLA_QPE), cur),
                  pl.BlockSpec((1, tk, KV_LORA), whole), pl.BlockSpec((1, tk, ROPE_DIM), whole),
                  pl.BlockSpec((MLA_HEADS, NOPE_DIM, KV_LORA), const3), pl.BlockSpec((MLA_HEADS, KV_LORA, V_DIM), const3)],
        out_specs=pl.BlockSpec((1, qb, MLA_HEADS * V_DIM), cur),
        scratch_shapes=([pltpu.VMEM((rows, KV_LORA), jnp.bfloat16)] * n_chains
                        + [pltpu.VMEM((rows, ROPE_DIM), jnp.bfloat16)] * n_chains
                        + [pltpu.VMEM((rows, 1), jnp.float32)] * (2 * n_chains)
                        + [pltpu.VMEM((rows, KV_LORA), jnp.float32)] * n_chains),
        compiler_params=_cparams(2), name="mla_attention",
    )(q_nope, q_pe, c_keys_bf16, kp_keys_bf16, wuk_h, wuv_h)


def _pool_kernel(u_ref, prev_ref, w_ref, scale_ref, o_ref, new_ref, ext, *, tm, pos0):
    j = pl.program_id(1)

    @pl.when(j == 0)
    def _():
        ext[0:1, :] = jnp.zeros((1, POOL_WIDTH), jnp.float32)
        ext[1:POOL_MAX, :] = prev_ref[0]

    @pl.when(j > 0)
    def _():
        ext[0:POOL_MAX, :] = ext[tm:tm + POOL_MAX, :]

    u = u_ref[0]
    ext[POOL_MAX:POOL_MAX + tm, :] = u
    pos = pos0 + j * tm + lax.broadcasted_iota(jnp.int32, (tm, POOL_GROUP), 0)
    outs = []
    for gi, w in enumerate(POOL_WINDOWS):
        cols = slice(gi * POOL_GROUP, (gi + 1) * POOL_GROUP)
        tot = u[:, cols]
        for d in range(1, w):
            tot = tot + ext[POOL_MAX - d:POOL_MAX - d + tm, cols]
        cnt = jnp.minimum(pos + 1, w).astype(jnp.float32)
        m = tot / cnt - u[:, cols]
        outs.append(jnp.dot(m.astype(jnp.bfloat16), w_ref[gi], preferred_element_type=jnp.float32))
    o_ref[0] = jnp.concatenate(outs, axis=-1) * scale_ref[...]
    new_ref[0] = ext[tm + 1:tm + POOL_MAX, :]


def pool_mixer(u, prev, pool_w_bf16, pool_scale, pos0):
    b, t, _ = u.shape
    tm = min(ROW_BLOCK, t)
    cur = lambda bi, j: (bi, j, 0)
    per_b = lambda bi, j: (bi, 0, 0)
    return pl.pallas_call(
        functools.partial(_pool_kernel, tm=tm, pos0=pos0),
        out_shape=(jax.ShapeDtypeStruct((b, t, POOL_WIDTH), jnp.float32),
                   jax.ShapeDtypeStruct((b, POOL_BUF, POOL_WIDTH), jnp.float32)),
        grid=(b, t // tm),
        in_specs=[pl.BlockSpec((1, tm, POOL_WIDTH), cur), pl.BlockSpec((1, POOL_BUF, POOL_WIDTH), per_b),
                  pl.BlockSpec((len(POOL_WINDOWS), POOL_GROUP, POOL_GROUP), lambda bi, j: (0, 0, 0)),
                  pl.BlockSpec((1, POOL_WIDTH), lambda bi, j: (0, 0))],
        out_specs=(pl.BlockSpec((1, tm, POOL_WIDTH), cur), pl.BlockSpec((1, POOL_BUF, POOL_WIDTH), per_b)),
        scratch_shapes=[pltpu.VMEM((tm + POOL_MAX, POOL_WIDTH), jnp.float32)],
        compiler_params=_cparams(2), name="pool_mixer",
    )(u, prev, pool_w_bf16, pool_scale.reshape(1, POOL_WIDTH))


def _top16_rows(s, n_rows):
    iota = lax.broadcasted_iota(jnp.int32, s.shape, 0).astype(jnp.float32)
    vals, idxs = [], []
    for _ in range(PEER_TOPK):
        m = jnp.max(s, axis=0, keepdims=True)
        idx = jnp.min(jnp.where(s == m, iota, float(n_rows)), axis=0, keepdims=True)
        vals.append(m)
        idxs.append(idx)
        s = jnp.where(iota == idx, -jnp.inf, s)
    return jnp.concatenate(vals, axis=0), jnp.concatenate(idxs, axis=0)


def _route_kernel(x_ref, g_ref, wq_ref, keys_ref, xn_ref, pair_ref, gate_ref, code_t, gate_t):
    xn = _rms(x_ref[...], g_ref[...])
    xn_ref[...] = xn
    xb = xn.astype(jnp.bfloat16)

    def head_step(hs, carry):
        for hh in range(ROUTE_HEADS_PER_STEP):
            one_head(hs * ROUTE_HEADS_PER_STEP + hh)
        return carry

    def one_head(h):
        qb = jnp.dot(xb, wq_ref[h], preferred_element_type=jnp.float32).astype(jnp.bfloat16)
        sv, si = [], []
        for p in range(2):
            s = lax.dot_general(keys_ref[h * 2 + p], qb[:, p * D_HALF:(p + 1) * D_HALF],
                                (((1,), (1,)), ((), ())), preferred_element_type=jnp.float32)
            v, i = _top16_rows(s, N_KEYS)
            sv.append(v)
            si.append(i)
        cand = jnp.concatenate([sv[0][a:a + 1] + sv[1][:nb] for a, nb in enumerate(PAIR_COLS)]
                               + [jnp.full((PAIR_PAD, s.shape[1]), -jnp.inf, jnp.float32)], axis=0)
        eid = jnp.concatenate([si[0][a:a + 1] * float(N_KEYS) + si[1][:nb] for a, nb in enumerate(PAIR_COLS)]
                              + [jnp.zeros((PAIR_PAD, s.shape[1]), jnp.float32)], axis=0)
        iota = lax.broadcasted_iota(jnp.int32, cand.shape, 0).astype(jnp.float32)
        cv, ce = [], []
        for _ in range(PEER_TOPK):
            m = jnp.max(cand, axis=0, keepdims=True)
            idx = jnp.min(jnp.where(cand == m, iota, float(PAIR_ROWS)), axis=0, keepdims=True)
            hit = iota == idx
            cv.append(m)
            ce.append(jnp.max(jnp.where(hit, eid, -1.0), axis=0, keepdims=True))
            cand = jnp.where(hit, -jnp.inf, cand)
        cv = jnp.concatenate(cv, axis=0)
        ce = jnp.concatenate(ce, axis=0).astype(jnp.int32)
        e = jnp.exp(cv - cv[0:1])
        rows = pl.ds(pl.multiple_of(h * PEER_TOPK, PEER_TOPK), PEER_TOPK)
        gate_t[rows, :] = e / jnp.sum(e, axis=0, keepdims=True)
        code_t[rows, :] = ((ce & (HALF_EXPERTS - 1)) << 3) | (ce >> 13)

    lax.fori_loop(0, PEER_HEADS // ROUTE_HEADS_PER_STEP, head_step, 0)
    ca, cb = code_t[:NSEL // 2, :], code_t[NSEL // 2:, :]
    pair_ref[...] = (ca | (((cb & 1) ^ 1) << 1) | ((cb & -8) << 16)).T
    gate_ref[...] = gate_t[...].T


def peer_route(x2d, g, wq_heads, keys_bf16, tb):
    n = x2d.shape[0]
    return pl.pallas_call(
        _route_kernel,
        out_shape=(jax.ShapeDtypeStruct((n, D_MODEL), jnp.float32),
                   jax.ShapeDtypeStruct((n, NSEL // 2), jnp.int32),
                   jax.ShapeDtypeStruct((n, NSEL), jnp.float32)),
        grid=(n // tb,),
        in_specs=[pl.BlockSpec((tb, D_MODEL), lambda i: (i, 0)),
                  pl.BlockSpec((1, D_MODEL), lambda i: (0, 0)),
                  pl.BlockSpec((PEER_HEADS, D_MODEL, D_KEY), lambda i: (0, 0, 0)),
                  pl.BlockSpec((PEER_HEADS * 2, N_KEYS, D_HALF), lambda i: (0, 0, 0))],
        out_specs=(pl.BlockSpec((tb, D_MODEL), lambda i: (i, 0)),
                   pl.BlockSpec((tb, NSEL // 2), lambda i: (i, 0)),
                   pl.BlockSpec((tb, NSEL), lambda i: (i, 0))),
        scratch_shapes=[pltpu.VMEM((NSEL, tb), jnp.int32), pltpu.VMEM((NSEL, tb), jnp.float32)],
        compiler_params=_cparams(), name="peer_route",
    )(x2d, g.reshape(1, D_MODEL), wq_heads, keys_bf16)


def _pack_kernel(lo_ref, hi_ref, o_ref):
    def bf16_bits(x):
        return lax.bitcast_convert_type(x.astype(jnp.bfloat16).astype(jnp.float32), jnp.int32)

    word = lax.shift_right_logical(bf16_bits(lo_ref[...]), 16) | (bf16_bits(hi_ref[...]) & HI16)
    for k in range(ROW_TILE):
        o_ref[:, k, :] = word[:, k * LANES:(k + 1) * LANES]


def pack_table(tab):
    nb = HALF_EXPERTS // PACK_BLOCK
    out = pl.pallas_call(
        _pack_kernel,
        out_shape=jax.ShapeDtypeStruct((HALF_EXPERTS, ROW_TILE, LANES), jnp.int32),
        grid=(nb,),
        in_specs=[pl.BlockSpec((PACK_BLOCK, D_MODEL), lambda i: (i, 0)),
                  pl.BlockSpec((PACK_BLOCK, D_MODEL), lambda i: (i + nb, 0))],
        out_specs=pl.BlockSpec((PACK_BLOCK, ROW_TILE, LANES), lambda i: (i, 0, 0)),
        compiler_params=_cparams(), name="pack_table",
    )(tab, tab)
    return out.reshape(HALF_EXPERTS * ROW_TILE, LANES)


def _gather_pair(tab_ref, cab):
    wa = tab_ref[pl.ds(pl.multiple_of(cab & 0xFFF8, SUBLANES), SUBLANES), :]
    wb = tab_ref[pl.ds(pl.multiple_of(lax.shift_right_logical(cab, 16), SUBLANES), SUBLANES), :]
    cv = jnp.full((SUBLANES, LANES), cab, jnp.int32)
    sha = (cv & 1) << 4
    shb = (cv & 2) << 3
    return (lax.shift_right_logical(wa, sha) & 0xFFFF) | (lax.shift_left(wb, shb) & HI16)


def _gather_token(tab_ref, pair_ref, t, buf):
    for i in range(NSEL // 2):
        buf[i * ROW_TILE:(i + 1) * ROW_TILE, :] = _gather_pair(tab_ref, pair_ref[t, i])


def _grouped_token_loop(tb, gather, finish, group_a, group_b):
    @pl.when(pl.program_id(0) == 0)
    def _():
        for buf in group_b:
            buf[...] = jnp.zeros(buf.shape, jnp.int32)

    def body(i, carry):
        t0 = 2 * UPASS_GROUP * i
        for q in range(UPASS_GROUP):
            gather(t0 + q, group_a[q])
        finish(jnp.maximum(t0 - UPASS_GROUP, 0), group_b)
        for q in range(UPASS_GROUP):
            gather(t0 + UPASS_GROUP + q, group_b[q])
        finish(t0, group_a)
        return carry

    lax.fori_loop(0, tb // (2 * UPASS_GROUP), body, 0)
    finish(tb - UPASS_GROUP, group_b)


def _upass_kernel(pair_ref, gate_ref, xn_ref, tab_ref, w_ref, *bufs, tb):
    col = lax.broadcasted_iota(jnp.int32, (2 * ROW_TILE, SLOT_ROWS), 1)
    row = lax.broadcasted_iota(jnp.int32, (2 * ROW_TILE, SLOT_ROWS), 0)
    chunk_mask = (((col & 15) >> 1) == (row & 7)).astype(jnp.float32)
    c2 = lax.broadcasted_iota(jnp.int32, (SLOT_ROWS, NSEL), 0)
    j2 = lax.broadcasted_iota(jnp.int32, (SLOT_ROWS, NSEL), 1)
    fold = (j2 == (c2 >> 4) + ((c2 & 1) << 6)).astype(jnp.bfloat16)

    def gather(t, buf):
        _gather_token(tab_ref, pair_ref, t, buf)

    def finish(t0, group):
        zs = []
        for q, buf in enumerate(group):
            xt = xn_ref[pl.ds(t0 + q, 1), :].reshape(ROW_TILE, LANES)
            xhi = xt.astype(jnp.bfloat16)
            xlo = (xt - xhi.astype(jnp.float32)).astype(jnp.bfloat16)
            x16 = jnp.concatenate([xhi, xlo], axis=0)
            us = pltpu.bitcast(buf[...], jnp.bfloat16)
            r = lax.dot_general(x16, us, (((1,), (1,)), ((), ())), preferred_element_type=jnp.float32)
            zs.append(jnp.sum(r * chunk_mask, axis=0, keepdims=True))
        z = jnp.concatenate(zs, axis=0)
        zhi = z.astype(jnp.bfloat16)
        zlo = (z - zhi.astype(jnp.float32)).astype(jnp.bfloat16)
        act = (jnp.dot(zhi, fold, preferred_element_type=jnp.float32)
               + jnp.dot(zlo, fold, preferred_element_type=jnp.float32))
        gelu = 0.5 * act * (1.0 + lax.erf(act * (1.0 / math.sqrt(2.0))))
        rows = pl.ds(pl.multiple_of(t0, UPASS_GROUP), UPASS_GROUP)
        w_ref[rows, :] = (gate_ref[rows, :] * gelu).astype(jnp.bfloat16)

    _grouped_token_loop(tb, gather, finish, bufs[:UPASS_GROUP], bufs[UPASS_GROUP:])


def _peer_pass_call(kernel_fn, out_shape, out_block, pairs, per_token, rows, tab_packed, tb, name):
    n = pairs.shape[0]
    return pl.pallas_call(
        functools.partial(kernel_fn, tb=tb),
        out_shape=out_shape,
        grid=(n // tb,),
        in_specs=[pl.BlockSpec((tb, NSEL // 2), lambda i: (i, 0), memory_space=pltpu.SMEM),
                  pl.BlockSpec((tb, NSEL), lambda i: (i, 0)),
                  pl.BlockSpec((tb, D_MODEL), lambda i: (i, 0)),
                  pl.BlockSpec((HALF_EXPERTS * ROW_TILE, LANES), lambda i: (0, 0), pipeline_mode=pl.Buffered(1))],
        out_specs=pl.BlockSpec(out_block, lambda i: (i, 0)),
        scratch_shapes=[pltpu.VMEM((SLOT_ROWS // 2, LANES), jnp.int32)] * (2 * UPASS_GROUP),
        compiler_params=_cparams(), name=name,
    )(pairs, per_token, rows, tab_packed)


def peer_upass(pairs, gate, xn_rows, tab_packed, tb):
    n = pairs.shape[0]
    return _peer_pass_call(_upass_kernel, jax.ShapeDtypeStruct((n, NSEL), jnp.bfloat16), (tb, NSEL),
                           pairs, gate, xn_rows, tab_packed, tb, "peer_upass")


def _vpass_kernel(pair_ref, w_ref, x_ref, tab_ref, o_ref, *bufs, tb):
    col = lax.broadcasted_iota(jnp.int32, (ROW_TILE, SLOT_ROWS), 1)
    row = lax.broadcasted_iota(jnp.int32, (ROW_TILE, SLOT_ROWS), 0)
    chunk_mask = (((col & 15) >> 1) == row).astype(jnp.float32)
    j2 = lax.broadcasted_iota(jnp.int32, (NSEL, SLOT_ROWS), 0)
    c2 = lax.broadcasted_iota(jnp.int32, (NSEL, SLOT_ROWS), 1)
    spread = (j2 == (c2 >> 4) + ((c2 & 1) << 6)).astype(jnp.bfloat16)

    def gather(t, buf):
        _gather_token(tab_ref, pair_ref, t, buf)

    def finish(t0, group):
        rows = pl.ds(pl.multiple_of(t0, UPASS_GROUP), UPASS_GROUP)
        wexp = jnp.dot(w_ref[rows, :], spread, preferred_element_type=jnp.float32)
        for q, buf in enumerate(group):
            wsel = (jnp.broadcast_to(wexp[q:q + 1], (ROW_TILE, SLOT_ROWS)) * chunk_mask).astype(jnp.bfloat16)
            vs = pltpu.bitcast(buf[...], jnp.bfloat16)
            tile = jnp.dot(wsel, vs, preferred_element_type=jnp.float32)
            o_ref[pl.ds(t0 + q, 1), :] = x_ref[pl.ds(t0 + q, 1), :] + tile.reshape(1, D_MODEL)

    _grouped_token_loop(tb, gather, finish, bufs[:UPASS_GROUP], bufs[UPASS_GROUP:])


def peer_vpass(pairs, w_bf16, x_rows, tab_packed, tb):
    n = pairs.shape[0]
    return _peer_pass_call(_vpass_kernel, jax.ShapeDtypeStruct((n, D_MODEL), jnp.float32),
                           (tb, D_MODEL), pairs, w_bf16, x_rows, tab_packed, tb, "peer_vpass")


def peer_block(x, g, wq_heads, keys_bf16, u_packed, v_packed):
    shp = x.shape
    x2d = x.reshape(-1, D_MODEL)
    n = x2d.shape[0]
    xn, pairs, gate = peer_route(x2d, g, wq_heads, keys_bf16, min(PEER_ROUTE_BLOCK, n))
    w = peer_upass(pairs, gate, xn, u_packed, min(PEER_PASS_BLOCK, n))
    out = peer_vpass(pairs, w, x2d, v_packed, min(PEER_PASS_BLOCK, n))
    return out.reshape(shp)


def _rms_kernel(x_ref, g_ref, o_ref):
    o_ref[...] = _rms(x_ref[...], g_ref[...])


def rmsnorm_pallas(x, g):
    shp = x.shape
    xt = x.reshape(-1, shp[-1])
    n = xt.shape[0]
    tm = min(ROW_BLOCK, n)
    out = pl.pallas_call(
        _rms_kernel,
        out_shape=jax.ShapeDtypeStruct(xt.shape, xt.dtype),
        grid=(n // tm,),
        in_specs=[pl.BlockSpec((tm, shp[-1]), lambda i: (i, 0)),
                  pl.BlockSpec((1, shp[-1]), lambda i: (0, 0))],
        out_specs=pl.BlockSpec((tm, shp[-1]), lambda i: (i, 0)),
        compiler_params=_cparams(), name="final_rmsnorm",
    )(xt, g.reshape(1, -1))
    return out.reshape(shp)


def even_layer(x, pos0, k_prev, v_prev, h0r, h0i, norm_g, w_in, w_out, sink, s5_params, d_skip, w_glu, b_glu):
    b, t, _ = x.shape
    n = b * t
    x2d = x.reshape(n, D_MODEL)
    tabs = rope_tables(pos0 + jnp.arange(t), ROT_DIM, HEAD_DIM, SWA_Q, max(1, min(ROW_BLOCK, n) // t))
    q, k, v, u = in_even(x2d, norm_g, w_in.astype(jnp.bfloat16), tabs)
    q3, k3, v3 = q.reshape(b, t, SWA_Q), k.reshape(b, t, SWA_KV), v.reshape(b, t, SWA_KV)
    if k_prev is None:
        att = swa_attention(q3, k3, k3, v3, v3, sink, True)
        k_all, v_all = k3, v3
        h0r = jnp.zeros((b, S5_FLAT), jnp.float32)
        h0i = jnp.zeros((b, S5_FLAT), jnp.float32)
    else:
        kp, vp = k_prev.reshape(b, WINDOW, SWA_KV), v_prev.reshape(b, WINDOW, SWA_KV)
        att = swa_attention(q3, kp, k3, vp, v3, sink, False)
        k_all, v_all = jnp.concatenate([kp, k3], axis=1), jnp.concatenate([vp, v3], axis=1)
        h0r, h0i = h0r.reshape(b, S5_FLAT), h0i.reshape(b, S5_FLAT)
    lam, wb, wc = s5_discretize(*s5_params)
    u_tm = u.reshape(b, t, S5_WIDTH).transpose(1, 0, 2).reshape(n, S5_WIDTH)
    s5o_tm, hre, him = s5_mixer(u_tm, b, lam, wb, wc, d_skip, w_glu.astype(jnp.bfloat16), b_glu, h0r, h0i,
                                min(t, S5_STEPS))
    s5o = s5o_tm.reshape(t, b, S5_WIDTH).transpose(1, 0, 2).reshape(n, S5_WIDTH)
    out = out_proj(x2d, att.reshape(n, SWA_Q), s5o, w_out.astype(jnp.bfloat16))
    return (out.reshape(b, t, D_MODEL),
            k_all[:, -WINDOW:].reshape(b, WINDOW, SWA_KV_HEADS, HEAD_DIM),
            v_all[:, -WINDOW:].reshape(b, WINDOW, SWA_KV_HEADS, HEAD_DIM),
            hre.reshape(b, S5_GROUPS, S5_STATE), him.reshape(b, S5_GROUPS, S5_STATE))


def odd_layer(x, pos0, pool_prev, ckv_prev, kpe_prev, norm_g, w_in, w_out, pool_w, pool_scale,
              q_norm, kv_norm, w_uq, w_uk, w_uv):
    b, t, _ = x.shape
    n = b * t
    x2d = x.reshape(n, D_MODEL)
    reps = max(1, min(ROW_BLOCK, n) // t)
    pos = pos0 + jnp.arange(t)
    ktabs = rope_tables(pos, ROPE_DIM, KPE_PAD, KPE_PAD, reps)
    qtabs = rope_tables(pos, ROPE_DIM, ROPE_DIM, MLA_QPE, reps)
    w_perm, wuq_perm = permute_odd_weights(w_in, w_uq)
    qnope, qpe, c, kp, u = in_odd(x2d, norm_g, w_perm, q_norm, kv_norm, wuq_perm, ktabs, qtabs)
    c3, kp3 = c.reshape(b, t, KV_LORA), kp.reshape(b, t, ROPE_DIM)
    wuk_h = w_uk.transpose(1, 2, 0).astype(jnp.bfloat16)
    wuv_h = w_uv.transpose(1, 0, 2).astype(jnp.bfloat16)
    if ckv_prev is None:
        ck, kk, causal, n_keys = c3, kp3, True, t
        pool_prev = jnp.zeros((b, POOL_BUF, POOL_WIDTH), jnp.float32)
    else:
        ck, kk, causal = jnp.concatenate([ckv_prev, c3], axis=1), jnp.concatenate([kpe_prev, kp3], axis=1), False
        n_keys = ck.shape[1]
        pad = -n_keys % min(MLA_KBLOCK, n_keys)
        ck, kk = jnp.pad(ck, ((0, 0), (0, pad), (0, 0))), jnp.pad(kk, ((0, 0), (0, pad), (0, 0)))
    mla = mla_attention(qnope.reshape(b, t, MLA_QNOPE), qpe.reshape(b, t, MLA_QPE), ck.astype(jnp.bfloat16),
                        kk.astype(jnp.bfloat16), wuk_h, wuv_h, causal, n_keys)
    pool_out, pool_new = pool_mixer(u.reshape(b, t, POOL_WIDTH), pool_prev, pool_w.astype(jnp.bfloat16), pool_scale, pos0)
    out = out_proj(x2d, pool_out.reshape(n, POOL_WIDTH), mla.reshape(n, MLA_HEADS * V_DIM), w_out.astype(jnp.bfloat16))
    return out.reshape(b, t, D_MODEL), pool_new, c3, kp3


def kernel(x_prompt, x_sample, cache_swa_k, cache_swa_v, state_ssm_re, state_ssm_im, state_pool,
           cache_mla_ckv, cache_mla_kpe, norm_mix, norm_ffn, norm_final, w_in_even, w_out_even,
           swa_sink, s5_lam_re, s5_lam_im, s5_log_dt, s5_b_re, s5_b_im, s5_c_re, s5_c_im, s5_d,
           s5_w_glu, s5_b_glu, w_in_odd, w_out_odd, pool_w, pool_scale, mla_q_norm, mla_kv_norm,
           mla_w_uq, mla_w_uk, mla_w_uv, peer_w_q, peer_keys, peer_u, peer_v):
    xp, xs = x_prompt, x_sample
    kp_l, vp_l, rp_l, ip_l, poolp_l, cp_l, ep_l = [], [], [], [], [], [], []
    ks_l, vs_l, rs_l, is_l, pools_l, cs_l, es_l = [], [], [], [], [], [], []
    for layer in range(DEPTH):
        i = layer // 2
        if layer % 2 == 0:
            s5_params = (s5_lam_re[i], s5_lam_im[i], s5_log_dt[i], s5_b_re[i], s5_b_im[i], s5_c_re[i], s5_c_im[i])
            ew = (norm_mix[layer], w_in_even[i], w_out_even[i], swa_sink[i], s5_params, s5_d[i], s5_w_glu[i], s5_b_glu[i])
            xp, k1, v1, r1, i1 = even_layer(xp, 0, None, None, None, None, *ew)
            xs, k2, v2, r2, i2 = even_layer(xs, PAST_LEN, cache_swa_k[i], cache_swa_v[i],
                                            state_ssm_re[i], state_ssm_im[i], *ew)
            kp_l.append(k1); vp_l.append(v1); rp_l.append(r1); ip_l.append(i1)
            ks_l.append(k2); vs_l.append(v2); rs_l.append(r2); is_l.append(i2)
        else:
            ow = (norm_mix[layer], w_in_odd[i], w_out_odd[i], pool_w[i], pool_scale[i], mla_q_norm[i], mla_kv_norm[i],
                  mla_w_uq[i], mla_w_uk[i], mla_w_uv[i])
            xp, p1, c1, e1 = odd_layer(xp, 0, None, None, None, *ow)
            xs, p2, c2, e2 = odd_layer(xs, PAST_LEN, state_pool[i], cache_mla_ckv[i], cache_mla_kpe[i], *ow)
            poolp_l.append(p1); cp_l.append(c1); ep_l.append(e1)
            pools_l.append(p2); cs_l.append(c2); es_l.append(e2)
        wq_heads = peer_w_q[layer].reshape(D_MODEL, PEER_HEADS, D_KEY).transpose(1, 0, 2).astype(jnp.bfloat16)
        keys_bf16 = peer_keys[layer].reshape(PEER_HEADS * 2, N_KEYS, D_HALF).astype(jnp.bfloat16)
        u_packed, v_packed = pack_table(peer_u[layer]), pack_table(peer_v[layer])
        xp = peer_block(xp, norm_ffn[layer], wq_heads, keys_bf16, u_packed, v_packed)
        xs = peer_block(xs, norm_ffn[layer], wq_heads, keys_bf16, u_packed, v_packed)
    y_prompt = rmsnorm_pallas(xp, norm_final)
    y_sample = rmsnorm_pallas(xs, norm_final)
    return (y_prompt, y_sample,
            jnp.stack(kp_l), jnp.stack(vp_l), jnp.stack(rp_l), jnp.stack(ip_l),
            jnp.stack(poolp_l), jnp.stack(cp_l), jnp.stack(ep_l),
            jnp.stack(ks_l), jnp.stack(vs_l), jnp.stack(rs_l), jnp.stack(is_l),
            jnp.stack(pools_l), jnp.stack(cs_l), jnp.stack(es_l))
```
